```python
import jax, jax.numpy as jnp
from jax import lax
import numpy as np

D_MODEL = 1024
BATCH = 2
SEQ = 16384
DEPTH = 2

CHUNK = 64
MIX_WIDTH = D_MODEL
A_HEADS = 4
A_DK = 128
A_DV = 128
B_HEADS = 4
B_DK = 64
B_DV = 128
GLA_RANK = 16
GLA_TAU = 16.0
C_HEADS = 4
C_DK = 64
C_DV = 128
ROPE_BASE = 10000.0
D_WIDTH = 512
D_BLOCKS = 8
D_BLOCK_DIM = D_WIDTH // D_BLOCKS
D_CONV = 4
RGLRU_C = 8.0
MOE_GROUPS = 4
MOE_EXPERTS_PER_GROUP = 8
MOE_N_EXPERTS = MOE_GROUPS * MOE_EXPERTS_PER_GROUP
MOE_TOPK = 2
MOE_DFF = 512
N_EVEN = (DEPTH + 1) // 2
N_ODD = DEPTH // 2
EPS = 1e-6

EVEN_SPLITS = (A_HEADS * A_DK, A_HEADS * A_DK, A_HEADS * A_DV, A_HEADS * A_DV,
               B_HEADS * B_DK, B_HEADS * B_DK, B_HEADS * B_DV, GLA_RANK, B_HEADS * B_DV)
EVEN_IN = sum(EVEN_SPLITS)
ODD_SPLITS = (C_HEADS * C_DK, C_HEADS * C_DK, C_HEADS * C_DV, C_HEADS * C_DV, D_WIDTH, D_WIDTH)
ODD_IN = sum(ODD_SPLITS)

kernel_name = 'hybrid_hgrn2_gla_retnet_rglru_hmoe'


def rms_norm(x, w):
    xf = x.astype(jnp.float32)
    y = xf * lax.rsqrt(jnp.mean(xf * xf, axis=-1, keepdims=True) + EPS)
    return (y * w.astype(jnp.float32)).astype(x.dtype)


def split_cols(t, sizes):
    return jnp.split(t, [int(c) for c in np.cumsum(sizes)[:-1]], axis=-1)


def split_heads(t, n_heads):
    B, L, _ = t.shape
    return t.reshape(B, L, n_heads, -1).transpose(0, 2, 1, 3)


def merge_heads(t):
    B, H, L, d = t.shape
    return t.transpose(0, 2, 1, 3).reshape(B, L, H * d)


def head_norm(o, w, center):
    if center:
        o = o - jnp.mean(o, axis=-1, keepdims=True)
    o = o * lax.rsqrt(jnp.mean(o * o, axis=-1, keepdims=True) + EPS)
    H, dv = o.shape[1], o.shape[3]
    return o * w.astype(jnp.float32).reshape(1, H, 1, dv)


def rope(t, pos):
    half = t.shape[-1] // 2
    inv = ROPE_BASE ** (-jnp.arange(half, dtype=jnp.float32) / half)
    ang = pos[:, None] * inv[None, :]
    cos, sin = jnp.cos(ang), jnp.sin(ang)
    t1, t2 = t[..., :half], t[..., half:]
    return jnp.concatenate([t1 * cos - t2 * sin, t1 * sin + t2 * cos], axis=-1)


def chunked_gla(q, k, v, log_g):
    B, H, L, dk = q.shape
    dv = v.shape[-1]
    n = L // CHUNK

    def to_chunks(t):
        return jnp.moveaxis(t.astype(jnp.float32).reshape(B, H, n, CHUNK, t.shape[-1]), 2, 0)

    causal = jnp.tril(jnp.ones((CHUNK, CHUNK), dtype=bool))[:, :, None]

    def step(S, blk):
        qc, kc, vc, gc = blk
        b = jnp.cumsum(gc, axis=2)
        b_last = b[:, :, -1:, :]
        o_inter = jnp.einsum('bhtd,bhdv->bhtv', qc * jnp.exp(b), S)
        rel = jnp.where(causal, b[:, :, :, None, :] - b[:, :, None, :, :], -jnp.inf)
        scores = jnp.einsum('bhtd,bhsd,bhtsd->bhts', qc, kc, jnp.exp(rel))
        o_intra = jnp.einsum('bhts,bhsv->bhtv', scores, vc)
        S = jnp.exp(b_last)[:, :, 0, :, None] * S + jnp.einsum(
            'bhsd,bhsv->bhdv', kc * jnp.exp(b_last - b), vc)
        return S, o_inter + o_intra

    S0 = jnp.zeros((B, H, dk, dv), jnp.float32)
    _, o = lax.scan(step, S0, (to_chunks(q), to_chunks(k), to_chunks(v), to_chunks(log_g)))
    return jnp.moveaxis(o, 0, 2).reshape(B, H, L, dv)


def even_mixer(h, w_in, lb, gla_w_up, gla_b_up, hgrn_norm_w, gla_norm_w, w_out):
    proj = (h @ w_in).astype(jnp.float32)
    a_q, a_f, a_i, a_g, b_q, b_k, b_v, b_low, b_r = split_cols(proj, EVEN_SPLITS)
    f = lb + (1.0 - lb) * jax.nn.sigmoid(a_f)
    o_a = chunked_gla(split_heads(a_q, A_HEADS), split_heads(1.0 - f, A_HEADS),
                      split_heads(a_i, A_HEADS), split_heads(jnp.log(f), A_HEADS))
    y_a = merge_heads(head_norm(o_a, hgrn_norm_w, False)) * jax.nn.silu(a_g)
    z = b_low @ gla_w_up + gla_b_up
    log_alpha = jax.nn.log_sigmoid(z) / GLA_TAU
    o_b = chunked_gla(split_heads(b_q * B_DK ** -0.5, B_HEADS), split_heads(b_k, B_HEADS),
                      split_heads(b_v, B_HEADS), split_heads(log_alpha, B_HEADS))
    y_b = merge_heads(head_norm(o_b, gla_norm_w, False)) * jax.nn.silu(b_r)
    y = jnp.concatenate([y_a, y_b], axis=-1).astype(h.dtype)
    return y @ w_out


def odd_mixer(h, w_in, ret_norm_w, conv_w, conv_b, w_r, b_r, w_i, b_i, lam, w_out):
    B, L, _ = h.shape
    proj = (h @ w_in).astype(jnp.float32)
    c_q, c_k, c_v, c_g, d_x, d_gate = split_cols(proj, ODD_SPLITS)
    pos = jnp.arange(L, dtype=jnp.float32)
    q = rope(split_heads(c_q, C_HEADS), pos)
    k = rope(split_heads(c_k, C_HEADS), pos) * C_DK ** -0.5
    log_gamma = jnp.log1p(-jnp.exp2(-5.0 - jnp.arange(C_HEADS, dtype=jnp.float32)))
    log_g = jnp.broadcast_to(log_gamma[None, :, None, None], (B, C_HEADS, L, C_DK))
    o_c = chunked_gla(q, k, split_heads(c_v, C_HEADS), log_g)
    y_c = merge_heads(head_norm(o_c, ret_norm_w, True)) * jax.nn.silu(c_g)
    xp = jnp.pad(d_x, ((0, 0), (D_CONV - 1, 0), (0, 0)))
    xc = conv_b
    for j in range(D_CONV):
        xc = xc + xp[:, j:j + L, :] * conv_w[j]
    xb = xc.reshape(B, L, D_BLOCKS, D_BLOCK_DIM)
    r = jax.nn.sigmoid(jnp.einsum('blnd,nde->blne', xb, w_r).reshape(B, L, D_WIDTH) + b_r)
    i = jax.nn.sigmoid(jnp.einsum('blnd,nde->blne', xb, w_i).reshape(B, L, D_WIDTH) + b_i)
    log_a = -RGLRU_C * r * jax.nn.softplus(-lam)
    a = jnp.exp(log_a)
    u = jnp.sqrt(-jnp.expm1(2.0 * log_a)) * (i * xc)

    def combine(left, right):
        a1, b1 = left
        a2, b2 = right
        return a1 * a2, a2 * b1 + b2

    _, hs = lax.associative_scan(combine, (a, u), axis=1)
    y_d = hs * jax.nn.gelu(d_gate)
    y = jnp.concatenate([y_c, y_d], axis=-1).astype(h.dtype)
    return y @ w_out


def hier_moe(h, w_group, b_group, w_expert, b_expert, w1, w3, w2):
    B, L, D = h.shape
    xt = h.reshape(B * L, D)
    g_logits = (xt @ w_group).astype(jnp.float32) + b_group
    g_prob = jax.nn.softmax(g_logits, axis=-1)
    g_idx = jnp.argmax(g_logits, axis=-1)
    g_w = jnp.take_along_axis(g_prob, g_idx[:, None], axis=-1)
    e_logits = ((xt @ w_expert).astype(jnp.float32) + b_expert).reshape(-1, MOE_GROUPS, MOE_EXPERTS_PER_GROUP)
    e_sel = jnp.take_along_axis(e_logits, g_idx[:, None, None], axis=1)[:, 0]
    top_v, top_i = lax.top_k(e_sel, MOE_TOPK)
    top_w = jax.nn.softmax(top_v, axis=-1)
    e_gate = jnp.sum(jax.nn.one_hot(top_i, MOE_EXPERTS_PER_GROUP) * top_w[..., None], axis=1)
    gate = (jax.nn.one_hot(g_idx, MOE_GROUPS)[:, :, None] * e_gate[:, None, :]
            * g_w[:, :, None]).reshape(-1, MOE_N_EXPERTS)

    def expert_step(acc, p):
        w1e, w3e, w2e, ge = p
        hid = jax.nn.silu(xt @ w1e) * (xt @ w3e)
        return acc + ge[:, None].astype(xt.dtype) * (hid @ w2e), None

    out, _ = lax.scan(expert_step, jnp.zeros_like(xt), (w1, w3, w2, gate.T))
    return out.reshape(B, L, D)


def setup_inputs(seed: int = 0) -> dict:
    key = jax.random.key(seed)
    ks = jax.random.split(key, 32)
    f32 = jnp.float32

    def nrm(k, shape, scale):
        return jax.random.normal(k, shape, f32) * scale

    def gain(k, shape):
        return 1.0 + 0.02 * jax.random.normal(k, shape, f32)

    u = jax.random.uniform(ks[18], (N_ODD, D_WIDTH), f32, minval=0.9, maxval=0.999)
    s = u ** (1.0 / RGLRU_C)
    return {
        'x': nrm(ks[0], (BATCH, SEQ, D_MODEL), 1.0),
        'norm_mix': gain(ks[1], (DEPTH, D_MODEL)),
        'norm_ffn': gain(ks[2], (DEPTH, D_MODEL)),
        'norm_final': gain(ks[3], (D_MODEL,)),
        'even_w_in': nrm(ks[4], (N_EVEN, D_MODEL, EVEN_IN), D_MODEL ** -0.5),
        'hgrn_lb_logits': nrm(ks[5], (DEPTH + 1, A_HEADS * A_DK), 0.5),
        'hgrn_norm_w': gain(ks[6], (N_EVEN, A_HEADS * A_DV)),
        'gla_w_up': nrm(ks[7], (N_EVEN, GLA_RANK, B_HEADS * B_DK), GLA_RANK ** -0.5),
        'gla_b_up': nrm(ks[8], (N_EVEN, B_HEADS * B_DK), 0.02),
        'gla_norm_w': gain(ks[9], (N_EVEN, B_HEADS * B_DV)),
        'even_w_out': nrm(ks[10], (N_EVEN, MIX_WIDTH, D_MODEL), MIX_WIDTH ** -0.5),
        'odd_w_in': nrm(ks[11], (N_ODD, D_MODEL, ODD_IN), D_MODEL ** -0.5),
        'ret_norm_w': gain(ks[12], (N_ODD, C_HEADS * C_DV)),
        'conv_w': nrm(ks[13], (N_ODD, D_CONV, D_WIDTH), D_CONV ** -0.5),
        'conv_b': nrm(ks[14], (N_ODD, D_WIDTH), 0.02),
        'rglru_w_r': nrm(ks[15], (N_ODD, D_BLOCKS, D_BLOCK_DIM, D_BLOCK_DIM), D_BLOCK_DIM ** -0.5),
        'rglru_b_r': nrm(ks[16], (N_ODD, D_WIDTH), 0.02),
        'rglru_w_i': nrm(ks[17], (N_ODD, D_BLOCKS, D_BLOCK_DIM, D_BLOCK_DIM), D_BLOCK_DIM ** -0.5),
        'rglru_b_i': nrm(ks[19], (N_ODD, D_WIDTH), 0.02),
        'rglru_lambda': jnp.log(s) - jnp.log1p(-s),
        'odd_w_out': nrm(ks[20], (N_ODD, MIX_WIDTH, D_MODEL), MIX_WIDTH ** -0.5),
        'moe_w_group': nrm(ks[21], (DEPTH, D_MODEL, MOE_GROUPS), D_MODEL ** -0.5),
        'moe_b_group': nrm(ks[22], (DEPTH, MOE_GROUPS), 0.01),
        'moe_w_expert': nrm(ks[23], (DEPTH, D_MODEL, MOE_N_EXPERTS), D_MODEL ** -0.5),
        'moe_b_expert': nrm(ks[24], (DEPTH, MOE_N_EXPERTS), 0.01),
        'moe_w1': nrm(ks[25], (DEPTH, MOE_N_EXPERTS, D_MODEL, MOE_DFF), D_MODEL ** -0.5),
        'moe_w3': nrm(ks[26], (DEPTH, MOE_N_EXPERTS, D_MODEL, MOE_DFF), D_MODEL ** -0.5),
        'moe_w2': nrm(ks[27], (DEPTH, MOE_N_EXPERTS, MOE_DFF, D_MODEL), MOE_DFF ** -0.5),
    }


def reference(x, norm_mix, norm_ffn, norm_final, even_w_in, hgrn_lb_logits, hgrn_norm_w,
              gla_w_up, gla_b_up, gla_norm_w, even_w_out, odd_w_in, ret_norm_w, conv_w, conv_b,
              rglru_w_r, rglru_b_r, rglru_w_i, rglru_b_i, rglru_lambda, odd_w_out,
              moe_w_group, moe_b_group, moe_w_expert, moe_b_expert, moe_w1, moe_w3, moe_w2):
    lb_table = jnp.cumsum(jax.nn.softmax(hgrn_lb_logits.astype(jnp.float32), axis=0), axis=0)
    for l in range(DEPTH):
        j = l // 2
        hn = rms_norm(x, norm_mix[l])
        if l % 2 == 0:
            mix = even_mixer(hn, even_w_in[j], lb_table[l], gla_w_up[j], gla_b_up[j],
                             hgrn_norm_w[j], gla_norm_w[j], even_w_out[j])
        else:
            mix = odd_mixer(hn, odd_w_in[j], ret_norm_w[j], conv_w[j], conv_b[j],
                            rglru_w_r[j], rglru_b_r[j], rglru_w_i[j], rglru_b_i[j],
                            rglru_lambda[j], odd_w_out[j])
        x = x + mix
        x = x + hier_moe(rms_norm(x, norm_ffn[l]), moe_w_group[l], moe_b_group[l],
                         moe_w_expert[l], moe_b_expert[l], moe_w1[l], moe_w3[l], moe_w2[l])
    return rms_norm(x, norm_final)
```

```python
import functools

import numpy as np
import jax
import jax.numpy as jnp
from jax import lax
from jax.experimental import pallas as pl
from jax.experimental.pallas import tpu as pltpu

F32 = jnp.float32
BF16 = jnp.bfloat16

EPS = 1e-6
CHUNK = 64
SUB = 16
LANES = 128
D_MODEL = 1024
MIX_WIDTH = 1024
N_HEADS = 8
GLA_RANK = 16
GLA_TAU = 16.0
RET_HEADS = 4
RET_DK = 64
ROPE_BASE = 10000.0
LRU_WIDTH = 512
LRU_BLOCKS = 8
LRU_CONV = 4
LRU_C = 8.0
MOE_GROUPS = 4
MOE_PER_GROUP = 8
MOE_EXPERTS = MOE_GROUPS * MOE_PER_GROUP
MOE_DFF = 512
NEG_BIG = -1e30

MIX_TILE = 512
TOK_TILE = 512
ROW_TILE = 256
VMEM_LIMIT = 56 * 1024 * 1024


def _dot(a, b):
    return jnp.dot(a.astype(BF16), b.astype(BF16), preferred_element_type=F32)


def _dot_nt(a, b):
    return lax.dot_general(a.astype(BF16), b.astype(BF16), (((1,), (1,)), ((), ())),
                           preferred_element_type=F32)


def _dot_tn(a, b):
    return lax.dot_general(a.astype(BF16), b.astype(BF16), (((0,), (0,)), ((), ())),
                           preferred_element_type=F32)


def _split2(x):
    hi = x.astype(BF16)
    lo = (x - hi.astype(F32)).astype(BF16)
    return hi, lo


def _split3(x):
    hi = x.astype(BF16)
    r = x - hi.astype(F32)
    mid = r.astype(BF16)
    lo = (r - mid.astype(F32)).astype(BF16)
    return hi, mid, lo


def _dot_x3(a, b):
    ah, al = _split2(a)
    bh, bl = _split2(b)
    d = functools.partial(jnp.dot, preferred_element_type=F32)
    return d(ah, bh) + (d(ah, bl) + d(al, bh))


def _sigmoid(x):
    return 1.0 / (1.0 + jnp.exp(-x))


def _softplus(x):
    return jnp.maximum(x, 0.0) + jnp.log(1.0 + jnp.exp(-jnp.abs(x)))


def _rms(x, w):
    return x * lax.rsqrt(jnp.mean(x * x, axis=-1, keepdims=True) + EPS) * w


def _tri(n):
    r = lax.broadcasted_iota(jnp.int32, (n, n), 0)
    c = lax.broadcasted_iota(jnp.int32, (n, n), 1)
    return jnp.where(r >= c, 1.0, 0.0).astype(BF16)


def _cumsum_rows(tri, g):
    hi, mid, lo = _split3(g)
    d = functools.partial(jnp.dot, preferred_element_type=F32)
    return d(tri, hi) + (d(tri, mid) + d(tri, lo))


_E_AQ, _E_AF, _E_AI, _E_AG = 0, 512, 1024, 1536
_E_BQ, _E_BK, _E_BV, _E_BR, _E_LOW = 2048, 2304, 2560, 3072, 3584
EVEN_COLS = 3712
_QK_COLS = 768


def _gla_head_chunk(c, head, unit, half, q_ref, k_ref, v_ref, b_ref, o_ref, st_ref, a_ref):
    ul = slice(unit * LANES, (unit + 1) * LANES)
    vl = slice(head * LANES, (head + 1) * LANES)
    rows = pl.ds(pl.multiple_of(c * CHUNK, CHUNK), CHUNK)
    q = q_ref[rows, ul]
    k = k_ref[rows, ul]
    v = v_ref[rows, vl]
    b = b_ref[:, ul]
    lane = lax.broadcasted_iota(jnp.int32, (1, LANES), 1)
    if half is not None:
        own = (lane < 64) if half == 0 else (lane >= 64)
        q = jnp.where(own, q, 0.0)

    st = st_ref[head]
    o = _dot_nt(q * jnp.exp(b), st)
    b_last = b[CHUNK - 1:CHUNK, :]
    kd = k * jnp.exp(b_last - b)
    st_ref[head] = st * jnp.exp(b_last) + _dot_tn(v, kd)

    def qk(rq, rk, ref_row):
        r = b[ref_row:ref_row + 1, :]
        qq = q[rq[0]:rq[1]] * jnp.exp(b[rq[0]:rq[1]] - r)
        kk = k[rk[0]:rk[1]] * jnp.exp(r - b[rk[0]:rk[1]])
        return qq, kk

    def place(x, r0):
        parts = []
        if r0 > 0:
            parts.append(jnp.zeros((r0, LANES), F32))
        parts.append(x)
        r1 = r0 + x.shape[0]
        if r1 < CHUNK:
            parts.append(jnp.zeros((CHUNK - r1, LANES), F32))
        return jnp.concatenate(parts, axis=0)

    q1, k1 = qk((32, 64), (0, 32), 31)
    q2, k2 = qk((16, 32), (0, 16), 15)
    q3, k3 = qk((48, 64), (32, 48), 47)
    qcat = jnp.concatenate([place(q1, 32), place(q2, 16), place(q3, 48)], axis=1)
    kcat = jnp.concatenate([place(k1, 0), place(k2, 0), place(k3, 32)], axis=1)
    o = o + _dot(_dot_nt(qcat, kcat), v)
    o_ref[rows, vl] = o

    ones = jnp.ones((LANES, LANES), BF16)
    trow = lax.broadcasted_iota(jnp.int32, (SUB, LANES), 0)

    def sub_body(j, carry):
        r0 = pl.multiple_of(c * CHUNK + j * SUB, SUB)
        rb = pl.multiple_of(j * SUB, SUB)
        qs = q_ref[pl.ds(r0, SUB), ul]
        if half is not None:
            qs = jnp.where(own, qs, 0.0)
        bs = b_ref[pl.ds(rb, SUB), ul]
        ks = k_ref[pl.ds(r0, SUB), ul]
        vs = v_ref[pl.ds(r0, SUB), vl]
        for s in range(SUB):
            e = jnp.exp(jnp.where(trow >= s, bs - bs[s:s + 1, :], NEG_BIG))
            a_ref[s * SUB:(s + 1) * SUB, :] = (qs * e * ks[s:s + 1, :]).astype(BF16)
        rs = jnp.dot(a_ref[...], ones, preferred_element_type=F32)
        od = jnp.zeros((SUB, LANES), F32)
        for s in range(SUB):
            od = od + rs[s * SUB:(s + 1) * SUB, :] * vs[s:s + 1, :]
        o_ref[pl.ds(r0, SUB), vl] += od
        return carry

    lax.fori_loop(0, CHUNK // SUB, sub_body, 0)


def _even_kernel(x_ref, nw_ref, win_ref, lb_ref, wup_ref, bup_ref, hw_ref, wout_ref, out_ref,
                 q_ref, k_ref, g_ref, v_ref, gate_ref, o_ref, b_ref, st_ref, a_ref):
    tl = x_ref.shape[1]

    @pl.when(pl.program_id(1) == 0)
    def _():
        st_ref[...] = jnp.zeros_like(st_ref)

    x = x_ref[0]
    hn = _rms(x, nw_ref[...]).astype(BF16)

    def proj(c0, n):
        return jnp.dot(hn, win_ref[:, c0:c0 + n], preferred_element_type=F32)

    q_ref[:, 0:512] = proj(_E_AQ, 512)
    lb = lb_ref[...]
    f = lb + (1.0 - lb) * _sigmoid(proj(_E_AF, 512))
    k_ref[:, 0:512] = 1.0 - f
    g_ref[:, 0:512] = jnp.log(f)
    v_ref[:, 0:512] = proj(_E_AI, 512)
    ag = proj(_E_AG, 512)
    gate_ref[:, 0:512] = ag * _sigmoid(ag)
    q_ref[:, 512:768] = proj(_E_BQ, 256) * (64.0 ** -0.5)
    k_ref[:, 512:768] = proj(_E_BK, 256)
    v_ref[:, 512:1024] = proj(_E_BV, 512)
    br = proj(_E_BR, 512)
    gate_ref[:, 512:1024] = br * _sigmoid(br)
    z = _dot_x3(proj(_E_LOW, LANES), wup_ref[...]) + bup_ref[...]
    g_ref[:, 512:768] = -_softplus(-z) * (1.0 / GLA_TAU)

    tri = _tri(CHUNK)

    def chunk_body(c, carry):
        rows = pl.ds(pl.multiple_of(c * CHUNK, CHUNK), CHUNK)
        b_ref[...] = _cumsum_rows(tri, g_ref[rows, :])
        for head in range(4):
            _gla_head_chunk(c, head, head, None, q_ref, k_ref, v_ref, b_ref, o_ref, st_ref, a_ref)
        for head in range(4, 8):
            _gla_head_chunk(c, head, 4 + (head - 4) // 2, (head - 4) % 2,
                            q_ref, k_ref, v_ref, b_ref, o_ref, st_ref, a_ref)
        return carry

    lax.fori_loop(0, tl // CHUNK, chunk_body, 0)

    hw = hw_ref[...]
    ys = []
    for head in range(N_HEADS):
        sl = slice(head * LANES, (head + 1) * LANES)
        oh = o_ref[:, sl]
        ys.append(oh * lax.rsqrt(jnp.mean(oh * oh, axis=-1, keepdims=True) + EPS)
                  * hw[:, sl] * gate_ref[:, sl])
    y = jnp.concatenate(ys, axis=1).astype(BF16)
    out_ref[0] = x + jnp.dot(y, wout_ref[...], preferred_element_type=F32)


def _const_spec(shape):
    nd = len(shape)
    return pl.BlockSpec(shape, lambda *_: (0,) * nd)


def _even_mixer(x, norm_w, w_in, lb, w_up, b_up, hgrn_norm_w, gla_norm_w, w_out):
    B, L, D = x.shape
    tl = min(MIX_TILE, L)
    s = np.cumsum([0, 512, 512, 512, 512, 256, 256, 512, GLA_RANK, 512])
    cols = [w_in[:, s[i]:s[i + 1]] for i in range(9)]
    low = jnp.pad(cols[7], ((0, 0), (0, LANES - GLA_RANK)))
    w_in_r = jnp.concatenate(cols[:7] + [cols[8], low], axis=1).astype(BF16)
    w_up_p = jnp.pad(w_up, ((0, LANES - GLA_RANK), (0, 0)))
    hw = jnp.concatenate([hgrn_norm_w, gla_norm_w])[None, :]
    return pl.pallas_call(
        _even_kernel,
        grid=(B, L // tl),
        in_specs=[
            pl.BlockSpec((1, tl, D), lambda b, l: (b, l, 0)),
            _const_spec((1, D)),
            _const_spec((D, EVEN_COLS)),
            _const_spec((1, 512)),
            _const_spec((LANES, 256)),
            _const_spec((1, 256)),
            _const_spec((1, MIX_WIDTH)),
            _const_spec((MIX_WIDTH, D)),
        ],
        out_specs=pl.BlockSpec((1, tl, D), lambda b, l: (b, l, 0)),
        out_shape=jax.ShapeDtypeStruct((B, L, D), F32),
        scratch_shapes=[
            pltpu.VMEM((tl, _QK_COLS), F32),
            pltpu.VMEM((tl, _QK_COLS), F32),
            pltpu.VMEM((tl, _QK_COLS), F32),
            pltpu.VMEM((tl, MIX_WIDTH), F32),
            pltpu.VMEM((tl, MIX_WIDTH), F32),
            pltpu.VMEM((tl, MIX_WIDTH), F32),
            pltpu.VMEM((CHUNK, _QK_COLS), F32),
            pltpu.VMEM((N_HEADS, LANES, LANES), F32),
            pltpu.VMEM((SUB * SUB, LANES), BF16),
        ],
        compiler_params=pltpu.CompilerParams(
            dimension_semantics=("arbitrary", "arbitrary"), vmem_limit_bytes=VMEM_LIMIT),
        name="even_mixer",
    )(x, norm_w[None, :], w_in_r, lb[None, :], w_up_p, b_up[None, :], hw, w_out.astype(BF16))


_O_Q, _O_QR, _O_K, _O_KR, _O_V, _O_G, _O_X, _O_GATE = 0, 256, 512, 768, 1024, 1536, 2048, 2560
ODD_COLS = 3072


def _odd_kernel(x_ref, nw_ref, win_ref, cos_ref, sin_ref, dmat_ref, qdec_ref, kdec_ref, sdec_ref,
                rw_ref, cw_ref, cb_ref, wri_ref, bri_ref, lam_ref, wout_ref, out_ref,
                q_ref, k_ref, v_ref, o_ref, xe_ref, a_ref, u_ref, st_ref, h_ref):
    tl = x_ref.shape[1]

    @pl.when(pl.program_id(1) == 0)
    def _():
        st_ref[...] = jnp.zeros_like(st_ref)
        h_ref[...] = jnp.zeros_like(h_ref)
        xe_ref[0:8, :] = jnp.zeros((8, LRU_WIDTH), F32)

    x = x_ref[0]
    hn = _rms(x, nw_ref[...]).astype(BF16)

    def proj(c0, n):
        return jnp.dot(hn, win_ref[:, c0:c0 + n], preferred_element_type=F32)

    cos = cos_ref[...]
    sin = sin_ref[...]
    q_ref[...] = proj(_O_Q, 256) * cos + proj(_O_QR, 256) * sin
    k_ref[...] = (proj(_O_K, 256) * cos + proj(_O_KR, 256) * sin) * (RET_DK ** -0.5)
    v_ref[...] = proj(_O_V, 512)

    xe_ref[8:8 + tl, :] = proj(_O_X, LRU_WIDTH)
    cw = cw_ref[...]
    xc = cb_ref[...]
    for j in range(LRU_CONV):
        xc = xc + xe_ref[5 + j:5 + j + tl, :] * cw[j:j + 1, :]
    xe_ref[0:8, :] = xe_ref[tl:tl + 8, :]
    ri = jnp.dot(xc.astype(BF16), wri_ref[...], preferred_element_type=F32) + bri_ref[...]
    r = _sigmoid(ri[:, :LRU_WIDTH])
    i = _sigmoid(ri[:, LRU_WIDTH:])
    a = jnp.exp(-LRU_C * r * _softplus(-lam_ref[...]))
    a_ref[...] = a
    u_ref[...] = jnp.sqrt(1.0 - a * a) * (i * xc)

    lane = lax.broadcasted_iota(jnp.int32, (1, LANES), 1)
    crow = lax.broadcasted_iota(jnp.int32, (CHUNK, LRU_WIDTH), 0)

    def chunk_body(c, carry):
        rows = pl.ds(pl.multiple_of(c * CHUNK, CHUNK), CHUNK)
        for head in range(RET_HEADS):
            unit, half = head // 2, head % 2
            ul = slice(unit * LANES, (unit + 1) * LANES)
            vl = slice(head * LANES, (head + 1) * LANES)
            own = (lane < 64) if half == 0 else (lane >= 64)
            q = jnp.where(own, q_ref[rows, ul], 0.0)
            k = k_ref[rows, ul]
            v = v_ref[rows, vl]
            st = st_ref[head]
            o = _dot_nt(q, st) * qdec_ref[head]
            o = o + _dot(_dot_nt(q, k) * dmat_ref[head], v)
            st_ref[head] = st * sdec_ref[head] + _dot_tn(v, k * kdec_ref[head])
            o_ref[rows, vl] = o
        ca = a_ref[rows, :]
        ch = u_ref[rows, :]
        d = 1
        while d < CHUNK:
            keep = crow >= d
            sa = jnp.where(keep, pltpu.roll(ca, d, axis=0), 1.0)
            sh = jnp.where(keep, pltpu.roll(ch, d, axis=0), 0.0)
            ch = ca * sh + ch
            ca = ca * sa
            d *= 2
        ch = ch + ca * h_ref[...]
        h_ref[...] = ch[CHUNK - 1:CHUNK, :]
        o_ref[rows, 512:1024] = ch
        return carry

    lax.fori_loop(0, tl // CHUNK, chunk_body, 0)

    rw = rw_ref[...]
    ys = []
    for head in range(RET_HEADS):
        sl = slice(head * LANES, (head + 1) * LANES)
        oh = o_ref[:, sl]
        oh = oh - jnp.mean(oh, axis=-1, keepdims=True)
        ys.append(oh * lax.rsqrt(jnp.mean(oh * oh, axis=-1, keepdims=True) + EPS) * rw[:, sl])
    cg = proj(_O_G, 512)
    y_c = jnp.concatenate(ys, axis=1) * (cg * _sigmoid(cg))
    dg = proj(_O_GATE, LRU_WIDTH)
    gelu = 0.5 * dg * (1.0 + jnp.tanh(np.sqrt(2.0 / np.pi) * (dg + 0.044715 * (dg * dg * dg))))
    y_d = o_ref[:, 512:1024] * gelu
    y = jnp.concatenate([y_c, y_d], axis=1).astype(BF16)
    out_ref[0] = x + jnp.dot(y, wout_ref[...], preferred_element_type=F32)


def _rot_cols(w):
    d = w.shape[0]
    w4 = w.reshape(d, RET_HEADS, 2, RET_DK // 2)
    return jnp.stack([-w4[:, :, 1], w4[:, :, 0]], axis=2).reshape(d, RET_HEADS * RET_DK)


def _odd_mixer(x, norm_w, w_in, ret_norm_w, conv_w, conv_b, w_r, b_r, w_i, b_i, lam, w_out):
    B, L, D = x.shape
    tl = min(MIX_TILE, L)
    s = np.cumsum([0, 256, 256, 512, 512, 512, 512])
    wq, wk, wv, wg, wx, wgate = [w_in[:, s[i]:s[i + 1]] for i in range(6)]
    w_in_r = jnp.concatenate([wq, _rot_cols(wq), wk, _rot_cols(wk), wv, wg, wx, wgate],
                             axis=1).astype(BF16)
    half = RET_DK // 2
    inv = ROPE_BASE ** (-jnp.arange(half, dtype=F32) / half)
    ang = jnp.arange(L, dtype=F32)[:, None] * inv[None, :]
    cos = jnp.tile(jnp.cos(ang), (1, 2 * RET_HEADS))
    sin = jnp.tile(jnp.sin(ang), (1, 2 * RET_HEADS))
    lg = jnp.log1p(-jnp.exp2(-5.0 - jnp.arange(RET_HEADS, dtype=F32)))[:, None, None]
    t = jnp.arange(CHUNK, dtype=F32)
    rel = t[:, None] - t[None, :]
    dmat = jnp.where(rel >= 0, jnp.exp(lg * jnp.maximum(rel, 0.0)), 0.0)
    qdec = jnp.broadcast_to(jnp.exp(lg * (t[None, :, None] + 1.0)), (RET_HEADS, CHUNK, LANES))
    kdec = jnp.broadcast_to(jnp.exp(lg * (CHUNK - 1.0 - t[None, :, None])), (RET_HEADS, CHUNK, LANES))
    sdec = jnp.broadcast_to(jnp.exp(lg * float(CHUNK)), (RET_HEADS, 1, LANES))
    eye = jnp.eye(LRU_BLOCKS, dtype=F32)
    bd = LRU_WIDTH // LRU_BLOCKS

    def blockdiag(w):
        return (eye[:, None, :, None] * w[:, :, None, :]).reshape(LRU_WIDTH, LRU_WIDTH)

    wri = jnp.concatenate([blockdiag(w_r), blockdiag(w_i)], axis=1).astype(BF16)
    bri = jnp.concatenate([b_r, b_i])[None, :]
    return pl.pallas_call(
        _odd_kernel,
        grid=(B, L // tl),
        in_specs=[
            pl.BlockSpec((1, tl, D), lambda b, l: (b, l, 0)),
            _const_spec((1, D)),
            _const_spec((D, ODD_COLS)),
            pl.BlockSpec((tl, 256), lambda b, l: (l, 0)),
            pl.BlockSpec((tl, 256), lambda b, l: (l, 0)),
            _const_spec((RET_HEADS, CHUNK, CHUNK)),
            _const_spec((RET_HEADS, CHUNK, LANES)),
            _const_spec((RET_HEADS, CHUNK, LANES)),
            _const_spec((RET_HEADS, 1, LANES)),
            _const_spec((1, 512)),
            _const_spec((LRU_CONV, LRU_WIDTH)),
            _const_spec((1, LRU_WIDTH)),
            _const_spec((LRU_WIDTH, 2 * LRU_WIDTH)),
            _const_spec((1, 2 * LRU_WIDTH)),
            _const_spec((1, LRU_WIDTH)),
            _const_spec((MIX_WIDTH, D)),
        ],
        out_specs=pl.BlockSpec((1, tl, D), lambda b, l: (b, l, 0)),
        out_shape=jax.ShapeDtypeStruct((B, L, D), F32),
        scratch_shapes=[
            pltpu.VMEM((tl, 256), F32),
            pltpu.VMEM((tl, 256), F32),
            pltpu.VMEM((tl, 512), F32),
            pltpu.VMEM((tl, MIX_WIDTH), F32),
            pltpu.VMEM((tl + 8, LRU_WIDTH), F32),
            pltpu.VMEM((tl, LRU_WIDTH), F32),
            pltpu.VMEM((tl, LRU_WIDTH), F32),
            pltpu.VMEM((RET_HEADS, LANES, LANES), F32),
            pltpu.VMEM((1, LRU_WIDTH), F32),
        ],
        compiler_params=pltpu.CompilerParams(
            dimension_semantics=("arbitrary", "arbitrary"), vmem_limit_bytes=VMEM_LIMIT),
        name="odd_mixer",
    )(x, norm_w[None, :], w_in_r, cos, sin, dmat, qdec, kdec, sdec, ret_norm_w[None, :],
      conv_w, conv_b[None, :], wri, bri, lam[None, :], w_out.astype(BF16))


_I_E1, _I_E2, _I_G1, _I_G2, _I_R1, _I_R2 = 0, 1, 2, 3, 4, 5
_LOGIT_E0 = MOE_GROUPS


def _pack_bf16_pairs(lo, hi):
    lo_b = pltpu.bitcast(lo.astype(BF16).astype(F32), jnp.uint32)
    hi_b = pltpu.bitcast(hi.astype(BF16).astype(F32), jnp.uint32)
    return (lo_b >> 16) | (hi_b & jnp.uint32(0xFFFF0000))


def _unpack_bf16_pairs(w):
    lo = pltpu.bitcast(w << 16, F32).astype(BF16)
    hi = pltpu.bitcast(w & jnp.uint32(0xFFFF0000), F32).astype(BF16)
    return lo, hi


def _router_kernel(x_ref, nw_ref, wr_ref, br_ref, hnp_ref, info_ref, cnt_ref):
    tm = x_ref.shape[0]

    @pl.when(pl.program_id(0) == 0)
    def _():
        cnt_ref[...] = jnp.zeros_like(cnt_ref)

    hn = _rms(x_ref[...], nw_ref[...])
    half = D_MODEL // 2
    hnp_ref[...] = _pack_bf16_pairs(hn[:, :half], hn[:, half:])

    logits = _dot_x3(hn, wr_ref[...]) + br_ref[...]
    lane = lax.broadcasted_iota(jnp.int32, (tm, LANES), 1).astype(F32)

    def first_max(vals):
        m = jnp.max(vals, axis=-1, keepdims=True)
        idx = jnp.min(jnp.where(vals == m, lane, float(LANES)), axis=-1, keepdims=True)
        return m, idx

    gl = jnp.where(lane < MOE_GROUPS, logits, NEG_BIG)
    gmax, gidx = first_max(gl)
    g_w = 1.0 / jnp.sum(jnp.exp(gl - gmax), axis=-1, keepdims=True)
    e0 = _LOGIT_E0 + MOE_PER_GROUP * gidx
    el = jnp.where((lane >= e0) & (lane < e0 + MOE_PER_GROUP), logits, NEG_BIG)
    m1, i1 = first_max(el)
    m2, i2 = first_max(jnp.where(lane == i1, NEG_BIG, el))
    p2 = jnp.exp(m2 - m1)
    w1 = 1.0 / (1.0 + p2)
    e1 = i1 - _LOGIT_E0
    e2 = i2 - _LOGIT_E0

    oh1 = jnp.where(lane == e1, 1.0, 0.0)
    oh2 = jnp.where(lane == e2, 1.0, 0.0)
    tri = _tri(tm)
    pre1 = jnp.dot(tri, oh1.astype(BF16), preferred_element_type=F32)
    pre2 = jnp.dot(tri, oh2.astype(BF16), preferred_element_type=F32)
    carry = cnt_ref[...]
    c1 = pre1[tm - 1:tm, :]
    r1 = jnp.sum(oh1 * (pre1 + carry), axis=-1, keepdims=True) - 1.0
    r2 = jnp.sum(oh2 * (pre2 + (carry + c1)), axis=-1, keepdims=True) - 1.0
    cnt_ref[...] = carry + c1 + pre2[tm - 1:tm, :]

    info = jnp.zeros((tm, LANES), F32)
    for ln, val in ((_I_E1, e1), (_I_E2, e2), (_I_G1, g_w * w1), (_I_G2, g_w * (w1 * p2)),
                    (_I_R1, r1), (_I_R2, r2)):
        info = jnp.where(lane == float(ln), val, info)
    info_ref[...] = info


def _router(x2, norm_w, w_group, b_group, w_expert, b_expert):
    T, D = x2.shape
    tm = min(TOK_TILE, T)
    pad = LANES - MOE_GROUPS - MOE_EXPERTS
    wr = jnp.pad(jnp.concatenate([w_group, w_expert], axis=1), ((0, 0), (0, pad)))
    br = jnp.pad(jnp.concatenate([b_group, b_expert]), (0, pad))[None, :]
    return pl.pallas_call(
        _router_kernel,
        grid=(T // tm,),
        in_specs=[
            pl.BlockSpec((tm, D), lambda i: (i, 0)),
            _const_spec((1, D)),
            _const_spec((D, LANES)),
            _const_spec((1, LANES)),
        ],
        out_specs=[
            pl.BlockSpec((tm, D // 2), lambda i: (i, 0)),
            pl.BlockSpec((tm, LANES), lambda i: (i, 0)),
            _const_spec((1, LANES)),
        ],
        out_shape=[
            jax.ShapeDtypeStruct((T, D // 2), jnp.uint32),
            jax.ShapeDtypeStruct((T, LANES), F32),
            jax.ShapeDtypeStruct((1, LANES), F32),
        ],
        compiler_params=pltpu.CompilerParams(
            dimension_semantics=("arbitrary",), vmem_limit_bytes=VMEM_LIMIT),
        name="moe_router",
    )(x2, norm_w[None, :], wr, br)


def _dispatch_kernel(p1_ref, p2_ref, hnp_ref, xs_in_ref, xs_ref, sem):
    del xs_in_ref
    tm = hnp_ref.shape[0]
    base = pl.program_id(0) * tm

    def row_copy(r, p):
        return pltpu.make_async_copy(hnp_ref.at[pl.ds(r, 1)], xs_ref.at[pl.ds(p, 1)], sem)

    def issue(r, carry):
        row_copy(r, p1_ref[base + r]).start()
        row_copy(r, p2_ref[base + r]).start()
        return carry

    lax.fori_loop(0, tm, issue, 0)

    def drain(r, carry):
        row_copy(0, 0).wait()
        row_copy(0, 0).wait()
        return carry

    lax.fori_loop(0, tm, drain, 0)


def _dispatch(hnp, pos1, pos2, n_rows):
    T, W = hnp.shape
    tm = min(TOK_TILE, T)
    return pl.pallas_call(
        _dispatch_kernel,
        grid_spec=pltpu.PrefetchScalarGridSpec(
            num_scalar_prefetch=2,
            grid=(T // tm,),
            in_specs=[
                pl.BlockSpec((tm, W), lambda i, p1, p2: (i, 0)),
                pl.BlockSpec(memory_space=pl.ANY),
            ],
            out_specs=pl.BlockSpec(memory_space=pl.ANY),
            scratch_shapes=[pltpu.SemaphoreType.DMA(())],
        ),
        out_shape=jax.ShapeDtypeStruct((n_rows, W), jnp.uint32),
        input_output_aliases={3: 0},
        compiler_params=pltpu.CompilerParams(dimension_semantics=("arbitrary",)),
        name="moe_dispatch",
    )(pos1, pos2, hnp, jnp.zeros((n_rows, W), jnp.uint32))


def _experts_kernel(te_ref, nv_ref, xs_ref, w1_ref, w3_ref, w2_ref, ys_ref, w1b, w3b, w2b):
    i = pl.program_id(0)

    @pl.when(i < nv_ref[0])
    def _():
        @pl.when((i == 0) | (te_ref[i] != te_ref[jnp.maximum(i - 1, 0)]))
        def _():
            w1b[...] = w1_ref[0].astype(BF16)
            w3b[...] = w3_ref[0].astype(BF16)
            w2b[...] = w2_ref[0].astype(BF16)

        lo, hi = _unpack_bf16_pairs(xs_ref[...])
        half = D_MODEL // 2
        d = functools.partial(jnp.dot, preferred_element_type=F32)
        h1 = d(lo, w1b[0:half, :]) + d(hi, w1b[half:, :])
        h3 = d(lo, w3b[0:half, :]) + d(hi, w3b[half:, :])
        hid = (h1 * _sigmoid(h1)) * h3
        ys_ref[...] = d(hid.astype(BF16), w2b[...])

    @pl.when(i >= nv_ref[0])
    def _():
        ys_ref[...] = jnp.zeros_like(ys_ref)


def _experts(xs, tile_expert, n_valid, w1, w3, w2):
    n_rows, W = xs.shape
    nt = n_rows // ROW_TILE
    D, F = w1.shape[1], w1.shape[2]

    def row_map(i, te, nv):
        return (i, 0)

    def w_map(i, te, nv):
        return (te[i], 0, 0)

    return pl.pallas_call(
        _experts_kernel,
        grid_spec=pltpu.PrefetchScalarGridSpec(
            num_scalar_prefetch=2,
            grid=(nt,),
            in_specs=[
                pl.BlockSpec((ROW_TILE, W), row_map),
                pl.BlockSpec((1, D, F), w_map),
                pl.BlockSpec((1, D, F), w_map),
                pl.BlockSpec((1, F, D), w_map),
            ],
            out_specs=pl.BlockSpec((ROW_TILE, D), row_map),
            scratch_shapes=[
                pltpu.VMEM((D, F), BF16),
                pltpu.VMEM((D, F), BF16),
                pltpu.VMEM((F, D), BF16),
            ],
        ),
        out_shape=jax.ShapeDtypeStruct((n_rows, D), F32),
        compiler_params=pltpu.CompilerParams(
            dimension_semantics=("arbitrary",), vmem_limit_bytes=VMEM_LIMIT),
        name="moe_experts",
    )(tile_expert, n_valid, xs, w1, w3, w2)


def _combine_kernel(p1_ref, p2_ref, x_ref, info_ref, fw_ref, ys_ref, out_ref, buf, sems, *, final):
    tm = x_ref.shape[0]
    i = pl.program_id(0)
    n = pl.num_programs(0)

    def row_copy(slot, k, r, p):
        return pltpu.make_async_copy(ys_ref.at[pl.ds(p, 1)], buf.at[slot, k, pl.ds(r, 1)],
                                     sems.at[slot])

    def fetch(step, slot):
        base = step * tm

        def issue(r, carry):
            row_copy(slot, 0, r, p1_ref[base + r]).start()
            row_copy(slot, 1, r, p2_ref[base + r]).start()
            return carry

        lax.fori_loop(0, tm, issue, 0)

    @pl.when(i == 0)
    def _():
        fetch(0, 0)

    @pl.when(i + 1 < n)
    def _():
        fetch(i + 1, (i + 1) % 2)

    slot = i % 2

    def drain(r, carry):
        row_copy(slot, 0, 0, 0).wait()
        row_copy(slot, 1, 0, 0).wait()
        return carry

    lax.fori_loop(0, tm, drain, 0)

    info = info_ref[...]
    g1 = info[:, _I_G1:_I_G1 + 1]
    g2 = info[:, _I_G2:_I_G2 + 1]
    y = x_ref[...] + (g1 * buf[slot, 0] + g2 * buf[slot, 1])
    if final:
        y = _rms(y, fw_ref[...])
    out_ref[...] = y


def _combine(x2, info, ys, pos1, pos2, final_w, final):
    T, D = x2.shape
    tm = min(TOK_TILE, T)
    return pl.pallas_call(
        functools.partial(_combine_kernel, final=final),
        grid_spec=pltpu.PrefetchScalarGridSpec(
            num_scalar_prefetch=2,
            grid=(T // tm,),
            in_specs=[
                pl.BlockSpec((tm, D), lambda i, p1, p2: (i, 0)),
                pl.BlockSpec((tm, LANES), lambda i, p1, p2: (i, 0)),
                pl.BlockSpec((1, D), lambda i, p1, p2: (0, 0)),
                pl.BlockSpec(memory_space=pl.ANY),
            ],
            out_specs=pl.BlockSpec((tm, D), lambda i, p1, p2: (i, 0)),
            scratch_shapes=[
                pltpu.VMEM((2, 2, tm, D), F32),
                pltpu.SemaphoreType.DMA((2,)),
            ],
        ),
        out_shape=jax.ShapeDtypeStruct((T, D), F32),
        compiler_params=pltpu.CompilerParams(
            dimension_semantics=("arbitrary",), vmem_limit_bytes=VMEM_LIMIT),
        name="moe_combine",
    )(pos1, pos2, x2, info, final_w[None, :], ys)


def _hier_moe(x2, norm_w, w_group, b_group, w_expert, b_expert, w1, w3, w2, final_w, final):
    T, D = x2.shape
    hnp, info, cnt = _router(x2, norm_w, w_group, b_group, w_expert, b_expert)
    counts = cnt[0, :MOE_EXPERTS].astype(jnp.int32)
    tiles = (counts + ROW_TILE - 1) // ROW_TILE
    tile_end = jnp.cumsum(tiles)
    seg_start = (tile_end - tiles) * ROW_TILE
    nt = (2 * T) // ROW_TILE + MOE_EXPERTS
    n_valid = tile_end[-1:]
    tile_expert = jnp.minimum(jnp.searchsorted(tile_end, jnp.arange(nt, dtype=jnp.int32), side="right"),
                              MOE_EXPERTS - 1).astype(jnp.int32)
    tile_expert = jnp.where(jnp.arange(nt) < n_valid[0], tile_expert,
                            tile_expert[jnp.maximum(n_valid[0] - 1, 0)])
    e1 = info[:, _I_E1].astype(jnp.int32)
    e2 = info[:, _I_E2].astype(jnp.int32)
    pos1 = seg_start[e1] + info[:, _I_R1].astype(jnp.int32)
    pos2 = seg_start[e2] + info[:, _I_R2].astype(jnp.int32)
    xs = _dispatch(hnp, pos1, pos2, nt * ROW_TILE)
    ys = _experts(xs, tile_expert, n_valid.astype(jnp.int32), w1, w3, w2)
    return _combine(x2, info, ys, pos1, pos2, final_w, final)


def kernel(x, norm_mix, norm_ffn, norm_final, even_w_in, hgrn_lb_logits, hgrn_norm_w, gla_w_up,
           gla_b_up, gla_norm_w, even_w_out, odd_w_in, ret_norm_w, conv_w, conv_b, rglru_w_r,
           rglru_b_r, rglru_w_i, rglru_b_i, rglru_lambda, odd_w_out, moe_w_group, moe_b_group,
           moe_w_expert, moe_b_expert, moe_w1, moe_w3, moe_w2):
    B, L, D = x.shape
    depth = norm_mix.shape[0]
    lb_table = jnp.cumsum(jax.nn.softmax(hgrn_lb_logits.astype(F32), axis=0), axis=0)
    for l in range(depth):
        j = l // 2
        if l % 2 == 0:
            x = _even_mixer(x, norm_mix[l], even_w_in[j], lb_table[l], gla_w_up[j], gla_b_up[j],
                            hgrn_norm_w[j], gla_norm_w[j], even_w_out[j])
        else:
            x = _odd_mixer(x, norm_mix[l], odd_w_in[j], ret_norm_w[j], conv_w[j], conv_b[j],
                           rglru_w_r[j], rglru_b_r[j], rglru_w_i[j], rglru_b_i[j], rglru_lambda[j],
                           odd_w_out[j])
        x = _hier_moe(x.reshape(B * L, D), norm_ffn[l], moe_w_group[l], moe_b_group[l],
                      moe_w_expert[l], moe_b_expert[l], moe_w1[l], moe_w3[l], moe_w2[l],
                      norm_final, l == depth - 1).reshape(B, L, D)
    return x
```

```python
import functools

import numpy as np
import jax
import jax.numpy as jnp
from jax import lax
from jax.experimental import pallas as pl
from jax.experimental.pallas import tpu as pltpu

F32 = jnp.float32
BF16 = jnp.bfloat16

EPS = 1e-6
CHUNK = 64
GROUP = 8
LOG2E = 1.4426950408889634
LANES = 128
D_MODEL = 1024
MIX_WIDTH = 1024
N_HEADS = 8
GLA_RANK = 16
GLA_TAU = 16.0
RET_HEADS = 4
RET_DK = 64
ROPE_BASE = 10000.0
LRU_WIDTH = 512
LRU_BLOCKS = 8
LRU_CONV = 4
LRU_C = 8.0
MOE_GROUPS = 4
MOE_PER_GROUP = 8
MOE_EXPERTS = MOE_GROUPS * MOE_PER_GROUP
MOE_DFF = 512
NEG_BIG = -1e30

MIX_TILE = 512
TOK_TILE = 512
ROW_TILE = 256
VMEM_LIMIT = 56 * 1024 * 1024


def _dot(a, b):
    return jnp.dot(a.astype(BF16), b.astype(BF16), preferred_element_type=F32)


def _dot_nt(a, b):
    return lax.dot_general(a.astype(BF16), b.astype(BF16), (((1,), (1,)), ((), ())),
                           preferred_element_type=F32)


def _dot_tn(a, b):
    return lax.dot_general(a.astype(BF16), b.astype(BF16), (((0,), (0,)), ((), ())),
                           preferred_element_type=F32)


def _split2(x):
    hi = x.astype(BF16)
    lo = (x - hi.astype(F32)).astype(BF16)
    return hi, lo


def _split3(x):
    hi = x.astype(BF16)
    r = x - hi.astype(F32)
    mid = r.astype(BF16)
    lo = (r - mid.astype(F32)).astype(BF16)
    return hi, mid, lo


def _dot_x3(a, b):
    ah, al = _split2(a)
    bh, bl = _split2(b)
    d = functools.partial(jnp.dot, preferred_element_type=F32)
    return d(ah, bh) + (d(ah, bl) + d(al, bh))


def _sigmoid(x):
    return 1.0 / (1.0 + jnp.exp(-x))


def _softplus(x):
    return jnp.maximum(x, 0.0) + jnp.log(1.0 + jnp.exp(-jnp.abs(x)))


def _rms(x, w):
    return x * lax.rsqrt(jnp.mean(x * x, axis=-1, keepdims=True) + EPS) * w


def _tri(n):
    r = lax.broadcasted_iota(jnp.int32, (n, n), 0)
    c = lax.broadcasted_iota(jnp.int32, (n, n), 1)
    return jnp.where(r >= c, 1.0, 0.0).astype(BF16)


def _cumsum_rows(tri, g):
    hi, mid, lo = _split3(g)
    d = functools.partial(jnp.dot, preferred_element_type=F32)
    return d(tri, hi) + (d(tri, mid) + d(tri, lo))


_E_AQ, _E_AF, _E_AI, _E_AG = 0, 512, 1024, 1536
_E_BQ, _E_BK, _E_BV, _E_BR, _E_LOW = 2048, 2304, 2560, 3072, 3584
EVEN_COLS = 3712
_QK_COLS = 768


_LEVELS = (((32, 64), (0, 32), 31),
           ((16, 32), (0, 16), 15), ((48, 64), (32, 48), 47),
           ((8, 16), (0, 8), 7), ((24, 32), (16, 24), 23), ((40, 48), (32, 40), 39),
           ((56, 64), (48, 56), 55))


def _place_rows(x, r0):
    parts = []
    if r0 > 0:
        parts.append(jnp.zeros((r0, x.shape[1]), F32))
    parts.append(x)
    r1 = r0 + x.shape[0]
    if r1 < CHUNK:
        parts.append(jnp.zeros((CHUNK - r1, x.shape[1]), F32))
    return jnp.concatenate(parts, axis=0)


def _gla_unit_chunk(c, unit, heads, q_ref, k_ref, v_ref, b2, o_ref, st_ref, a_ref):
    ul = slice(unit * LANES, (unit + 1) * LANES)
    rows = pl.ds(pl.multiple_of(c * CHUNK, CHUNK), CHUNK)
    q = q_ref[rows, ul]
    k = k_ref[rows, ul]
    b = b2[:, ul]
    lane = lax.broadcasted_iota(jnp.int32, (1, LANES), 1)
    paired = len(heads) == 2

    trow = lax.broadcasted_iota(jnp.int32, (GROUP, LANES), 0)
    for j in range(CHUNK // GROUP):
        qj = q[j * GROUP:(j + 1) * GROUP]
        bj = b[j * GROUP:(j + 1) * GROUP]
        for s in range(0, GROUP, 2):
            pair = [qj * jnp.exp2(jnp.where(trow >= s + i, bj - bj[s + i:s + i + 1, :], NEG_BIG))
                    for i in (0, 1)]
            r0 = (j * GROUP + s) * GROUP
            a_ref[unit, r0:r0 + 2 * GROUP, :] = jnp.concatenate(pair, axis=0).astype(BF16)
    if paired:
        keys = jnp.concatenate([jnp.where(lane < 64, k, 0.0), jnp.where(lane >= 64, k, 0.0)], axis=0)
        scol = lane & 63
    else:
        keys = k
        scol = lane[:, :CHUNK]
    sc = _dot_nt(a_ref[unit], keys)
    groups = []
    for j in range(CHUNK // GROUP):
        acc = jnp.zeros((GROUP, sc.shape[1]), F32)
        for s in range(GROUP):
            r0 = (j * GROUP + s) * GROUP
            acc = jnp.where(scol == j * GROUP + s, sc[r0:r0 + GROUP, :], acc)
        groups.append(acc)
    p = jnp.concatenate(groups, axis=0)

    qparts, kparts = [], []
    for rq, rk, ref in _LEVELS:
        r = b[ref:ref + 1, :]
        qparts.append(_place_rows(q[rq[0]:rq[1]] * jnp.exp2(b[rq[0]:rq[1]] - r), rq[0]))
        kparts.append(_place_rows(k[rk[0]:rk[1]] * jnp.exp2(r - b[rk[0]:rk[1]]), rk[0]))
    qcat = jnp.concatenate(qparts, axis=1)
    kcat = jnp.concatenate(kparts, axis=1)

    eb = jnp.exp2(b)
    b_last = b[CHUNK - 1:CHUNK, :]
    kd = k * jnp.exp2(b_last - b)
    dec = jnp.exp2(b_last)
    if paired:
        vstack = jnp.concatenate(
            [v_ref[rows, h * LANES:(h + 1) * LANES] for h, _ in heads], axis=0)
        lane7 = lax.broadcasted_iota(jnp.int32, (1, len(_LEVELS) * LANES), 1)
    for head, lo in heads:
        vl = slice(head * LANES, (head + 1) * LANES)
        v = v_ref[rows, vl]
        if paired:
            own = (lane & 64) == lo
            qm = jnp.where(own, q, 0.0)
            qcm = jnp.where((lane7 & 64) == lo, qcat, 0.0)
        else:
            qm, qcm = q, qcat
        st = st_ref[head]
        o = _dot_nt(qm * eb, st)
        st_ref[head] = st * dec + _dot_tn(v, kd)
        s_off = _dot_nt(qcm, kcat)
        if paired:
            o = o + _dot(s_off, v) + _dot(jnp.where(own, p, 0.0), vstack)
        else:
            o = o + _dot(s_off + p, v)
        o_ref[rows, vl] = o


def _even_kernel(x_ref, nw_ref, win_ref, lb_ref, wup_ref, bup_ref, hw_ref, wout_ref, out_ref,
                 q_ref, k_ref, g_ref, v_ref, gate_ref, o_ref, st_ref, a_ref):
    tl = x_ref.shape[1]

    @pl.when(pl.program_id(1) == 0)
    def _():
        st_ref[...] = jnp.zeros_like(st_ref)

    x = x_ref[0]
    hn = _rms(x, nw_ref[...]).astype(BF16)

    def proj(c0, n):
        return jnp.dot(hn, win_ref[:, c0:c0 + n], preferred_element_type=F32)

    q_ref[:, 0:512] = proj(_E_AQ, 512)
    lb = lb_ref[...]
    f = lb + (1.0 - lb) * _sigmoid(proj(_E_AF, 512))
    k_ref[:, 0:512] = 1.0 - f
    g_ref[:, 0:512] = jnp.log(f)
    v_ref[:, 0:512] = proj(_E_AI, 512)
    ag = proj(_E_AG, 512)
    gate_ref[:, 0:512] = ag * _sigmoid(ag)
    q_ref[:, 512:768] = proj(_E_BQ, 256) * (64.0 ** -0.5)
    k_ref[:, 512:768] = proj(_E_BK, 256)
    v_ref[:, 512:1024] = proj(_E_BV, 512)
    br = proj(_E_BR, 512)
    gate_ref[:, 512:1024] = br * _sigmoid(br)
    z = _dot_x3(proj(_E_LOW, LANES), wup_ref[...]) + bup_ref[...]
    g_ref[:, 512:768] = -_softplus(-z) * (1.0 / GLA_TAU)

    tri = _tri(CHUNK)

    def chunk_body(c, carry):
        rows = pl.ds(pl.multiple_of(c * CHUNK, CHUNK), CHUNK)
        b2 = _cumsum_rows(tri, g_ref[rows, :]) * LOG2E
        for unit in range(4):
            _gla_unit_chunk(c, unit, ((unit, None),), q_ref, k_ref, v_ref, b2, o_ref, st_ref, a_ref)
        for unit in range(4, 6):
            h0 = 4 + 2 * (unit - 4)
            _gla_unit_chunk(c, unit, ((h0, 0), (h0 + 1, 64)), q_ref, k_ref, v_ref, b2, o_ref,
                            st_ref, a_ref)
        return carry

    lax.fori_loop(0, tl // CHUNK, chunk_body, 0)

    hw = hw_ref[...]
    ys = []
    for head in range(N_HEADS):
        sl = slice(head * LANES, (head + 1) * LANES)
        oh = o_ref[:, sl]
        ys.append(oh * lax.rsqrt(jnp.mean(oh * oh, axis=-1, keepdims=True) + EPS)
                  * hw[:, sl] * gate_ref[:, sl])
    y = jnp.concatenate(ys, axis=1).astype(BF16)
    out_ref[0] = x + jnp.dot(y, wout_ref[...], preferred_element_type=F32)


def _const_spec(shape):
    nd = len(shape)
    return pl.BlockSpec(shape, lambda *_: (0,) * nd)


def _even_mixer(x, norm_w, w_in, lb, w_up, b_up, hgrn_norm_w, gla_norm_w, w_out):
    B, L, D = x.shape
    tl = min(MIX_TILE, L)
    s = np.cumsum([0, 512, 512, 512, 512, 256, 256, 512, GLA_RANK, 512])
    cols = [w_in[:, s[i]:s[i + 1]] for i in range(9)]
    low = jnp.pad(cols[7], ((0, 0), (0, LANES - GLA_RANK)))
    w_in_r = jnp.concatenate(cols[:7] + [cols[8], low], axis=1).astype(BF16)
    w_up_p = jnp.pad(w_up, ((0, LANES - GLA_RANK), (0, 0)))
    hw = jnp.concatenate([hgrn_norm_w, gla_norm_w])[None, :]
    return pl.pallas_call(
        _even_kernel,
        grid=(B, L // tl),
        in_specs=[
            pl.BlockSpec((1, tl, D), lambda b, l: (b, l, 0)),
            _const_spec((1, D)),
            _const_spec((D, EVEN_COLS)),
            _const_spec((1, 512)),
            _const_spec((LANES, 256)),
            _const_spec((1, 256)),
            _const_spec((1, MIX_WIDTH)),
            _const_spec((MIX_WIDTH, D)),
        ],
        out_specs=pl.BlockSpec((1, tl, D), lambda b, l: (b, l, 0)),
        out_shape=jax.ShapeDtypeStruct((B, L, D), F32),
        scratch_shapes=[
            pltpu.VMEM((tl, _QK_COLS), F32),
            pltpu.VMEM((tl, _QK_COLS), F32),
            pltpu.VMEM((tl, _QK_COLS), F32),
            pltpu.VMEM((tl, MIX_WIDTH), F32),
            pltpu.VMEM((tl, MIX_WIDTH), F32),
            pltpu.VMEM((tl, MIX_WIDTH), F32),
            pltpu.VMEM((N_HEADS, LANES, LANES), F32),
            pltpu.VMEM((_QK_COLS // LANES, CHUNK * GROUP, LANES), BF16),
        ],
        compiler_params=pltpu.CompilerParams(
            dimension_semantics=("arbitrary", "arbitrary"), vmem_limit_bytes=VMEM_LIMIT),
        name="even_mixer",
    )(x, norm_w[None, :], w_in_r, lb[None, :], w_up_p, b_up[None, :], hw, w_out.astype(BF16))


_O_Q, _O_QR, _O_K, _O_KR, _O_V, _O_G, _O_X, _O_GATE = 0, 256, 512, 768, 1024, 1536, 2048, 2560
ODD_COLS = 3072


def _odd_kernel(x_ref, nw_ref, win_ref, cos_ref, sin_ref, dmat_ref, qdec_ref, kdec_ref, sdec_ref,
                rw_ref, cw_ref, cb_ref, wri_ref, bri_ref, lam_ref, wout_ref, out_ref,
                q_ref, k_ref, v_ref, o_ref, xe_ref, a_ref, u_ref, st_ref, h_ref):
    tl = x_ref.shape[1]

    @pl.when(pl.program_id(1) == 0)
    def _():
        st_ref[...] = jnp.zeros_like(st_ref)
        h_ref[...] = jnp.zeros_like(h_ref)
        xe_ref[0:8, :] = jnp.zeros((8, LRU_WIDTH), F32)

    x = x_ref[0]
    hn = _rms(x, nw_ref[...]).astype(BF16)

    def proj(c0, n):
        return jnp.dot(hn, win_ref[:, c0:c0 + n], preferred_element_type=F32)

    cos = cos_ref[...]
    sin = sin_ref[...]
    q_ref[...] = proj(_O_Q, 256) * cos + proj(_O_QR, 256) * sin
    k_ref[...] = (proj(_O_K, 256) * cos + proj(_O_KR, 256) * sin) * (RET_DK ** -0.5)
    v_ref[...] = proj(_O_V, 512)

    xe_ref[8:8 + tl, :] = proj(_O_X, LRU_WIDTH)
    cw = cw_ref[...]
    xc = cb_ref[...]
    for j in range(LRU_CONV):
        xc = xc + xe_ref[5 + j:5 + j + tl, :] * cw[j:j + 1, :]
    xe_ref[0:8, :] = xe_ref[tl:tl + 8, :]
    ri = jnp.dot(xc.astype(BF16), wri_ref[...], preferred_element_type=F32) + bri_ref[...]
    r = _sigmoid(ri[:, :LRU_WIDTH])
    i = _sigmoid(ri[:, LRU_WIDTH:])
    a = jnp.exp(-LRU_C * r * _softplus(-lam_ref[...]))
    a_ref[...] = a
    u_ref[...] = jnp.sqrt(1.0 - a * a) * (i * xc)

    lane = lax.broadcasted_iota(jnp.int32, (1, LANES), 1)
    crow = lax.broadcasted_iota(jnp.int32, (CHUNK, LRU_WIDTH), 0)

    def chunk_body(c, carry):
        rows = pl.ds(pl.multiple_of(c * CHUNK, CHUNK), CHUNK)
        for head in range(RET_HEADS):
            unit, half = head // 2, head % 2
            ul = slice(unit * LANES, (unit + 1) * LANES)
            vl = slice(head * LANES, (head + 1) * LANES)
            own = (lane < 64) if half == 0 else (lane >= 64)
            q = jnp.where(own, q_ref[rows, ul], 0.0)
            k = k_ref[rows, ul]
            v = v_ref[rows, vl]
            st = st_ref[head]
            o = _dot_nt(q, st) * qdec_ref[head]
            o = o + _dot(_dot_nt(q, k) * dmat_ref[head], v)
            st_ref[head] = st * sdec_ref[head] + _dot_tn(v, k * kdec_ref[head])
            o_ref[rows, vl] = o
        ca = a_ref[rows, :]
        ch = u_ref[rows, :]
        d = 1
        while d < CHUNK:
            keep = crow >= d
            sa = jnp.where(keep, pltpu.roll(ca, d, axis=0), 1.0)
            sh = jnp.where(keep, pltpu.roll(ch, d, axis=0), 0.0)
            ch = ca * sh + ch
            ca = ca * sa
            d *= 2
        ch = ch + ca * h_ref[...]
        h_ref[...] = ch[CHUNK - 1:CHUNK, :]
        o_ref[rows, 512:1024] = ch
        return carry

    lax.fori_loop(0, tl // CHUNK, chunk_body, 0)

    rw = rw_ref[...]
    ys = []
    for head in range(RET_HEADS):
        sl = slice(head * LANES, (head + 1) * LANES)
        oh = o_ref[:, sl]
        oh = oh - jnp.mean(oh, axis=-1, keepdims=True)
        ys.append(oh * lax.rsqrt(jnp.mean(oh * oh, axis=-1, keepdims=True) + EPS) * rw[:, sl])
    cg = proj(_O_G, 512)
    y_c = jnp.concatenate(ys, axis=1) * (cg * _sigmoid(cg))
    dg = proj(_O_GATE, LRU_WIDTH)
    gelu = 0.5 * dg * (1.0 + jnp.tanh(np.sqrt(2.0 / np.pi) * (dg + 0.044715 * (dg * dg * dg))))
    y_d = o_ref[:, 512:1024] * gelu
    y = jnp.concatenate([y_c, y_d], axis=1).astype(BF16)
    out_ref[0] = x + jnp.dot(y, wout_ref[...], preferred_element_type=F32)


def _rot_cols(w):
    d = w.shape[0]
    w4 = w.reshape(d, RET_HEADS, 2, RET_DK // 2)
    return jnp.stack([-w4[:, :, 1], w4[:, :, 0]], axis=2).reshape(d, RET_HEADS * RET_DK)


def _odd_mixer(x, norm_w, w_in, ret_norm_w, conv_w, conv_b, w_r, b_r, w_i, b_i, lam, w_out):
    B, L, D = x.shape
    tl = min(MIX_TILE, L)
    s = np.cumsum([0, 256, 256, 512, 512, 512, 512])
    wq, wk, wv, wg, wx, wgate = [w_in[:, s[i]:s[i + 1]] for i in range(6)]
    w_in_r = jnp.concatenate([wq, _rot_cols(wq), wk, _rot_cols(wk), wv, wg, wx, wgate],
                             axis=1).astype(BF16)
    half = RET_DK // 2
    inv = ROPE_BASE ** (-jnp.arange(half, dtype=F32) / half)
    ang = jnp.arange(L, dtype=F32)[:, None] * inv[None, :]
    cos = jnp.tile(jnp.cos(ang), (1, 2 * RET_HEADS))
    sin = jnp.tile(jnp.sin(ang), (1, 2 * RET_HEADS))
    lg = jnp.log1p(-jnp.exp2(-5.0 - jnp.arange(RET_HEADS, dtype=F32)))[:, None, None]
    t = jnp.arange(CHUNK, dtype=F32)
    rel = t[:, None] - t[None, :]
    dmat = jnp.where(rel >= 0, jnp.exp(lg * jnp.maximum(rel, 0.0)), 0.0)
    qdec = jnp.broadcast_to(jnp.exp(lg * (t[None, :, None] + 1.0)), (RET_HEADS, CHUNK, LANES))
    kdec = jnp.broadcast_to(jnp.exp(lg * (CHUNK - 1.0 - t[None, :, None])), (RET_HEADS, CHUNK, LANES))
    sdec = jnp.broadcast_to(jnp.exp(lg * float(CHUNK)), (RET_HEADS, 1, LANES))
    eye = jnp.eye(LRU_BLOCKS, dtype=F32)
    bd = LRU_WIDTH // LRU_BLOCKS

    def blockdiag(w):
        return (eye[:, None, :, None] * w[:, :, None, :]).reshape(LRU_WIDTH, LRU_WIDTH)

    wri = jnp.concatenate([blockdiag(w_r), blockdiag(w_i)], axis=1).astype(BF16)
    bri = jnp.concatenate([b_r, b_i])[None, :]
    return pl.pallas_call(
        _odd_kernel,
        grid=(B, L // tl),
        in_specs=[
            pl.BlockSpec((1, tl, D), lambda b, l: (b, l, 0)),
            _const_spec((1, D)),
            _const_spec((D, ODD_COLS)),
            pl.BlockSpec((tl, 256), lambda b, l: (l, 0)),
            pl.BlockSpec((tl, 256), lambda b, l: (l, 0)),
            _const_spec((RET_HEADS, CHUNK, CHUNK)),
            _const_spec((RET_HEADS, CHUNK, LANES)),
            _const_spec((RET_HEADS, CHUNK, LANES)),
            _const_spec((RET_HEADS, 1, LANES)),
            _const_spec((1, 512)),
            _const_spec((LRU_CONV, LRU_WIDTH)),
            _const_spec((1, LRU_WIDTH)),
            _const_spec((LRU_WIDTH, 2 * LRU_WIDTH)),
            _const_spec((1, 2 * LRU_WIDTH)),
            _const_spec((1, LRU_WIDTH)),
            _const_spec((MIX_WIDTH, D)),
        ],
        out_specs=pl.BlockSpec((1, tl, D), lambda b, l: (b, l, 0)),
        out_shape=jax.ShapeDtypeStruct((B, L, D), F32),
        scratch_shapes=[
            pltpu.VMEM((tl, 256), F32),
            pltpu.VMEM((tl, 256), F32),
            pltpu.VMEM((tl, 512), F32),
            pltpu.VMEM((tl, MIX_WIDTH), F32),
            pltpu.VMEM((tl + 8, LRU_WIDTH), F32),
            pltpu.VMEM((tl, LRU_WIDTH), F32),
            pltpu.VMEM((tl, LRU_WIDTH), F32),
            pltpu.VMEM((RET_HEADS, LANES, LANES), F32),
            pltpu.VMEM((1, LRU_WIDTH), F32),
        ],
        compiler_params=pltpu.CompilerParams(
            dimension_semantics=("arbitrary", "arbitrary"), vmem_limit_bytes=VMEM_LIMIT),
        name="odd_mixer",
    )(x, norm_w[None, :], w_in_r, cos, sin, dmat, qdec, kdec, sdec, ret_norm_w[None, :],
      conv_w, conv_b[None, :], wri, bri, lam[None, :], w_out.astype(BF16))


_I_E1, _I_E2, _I_G1, _I_G2, _I_R1, _I_R2 = 0, 1, 2, 3, 4, 5
_LOGIT_E0 = MOE_GROUPS


def _pack_bf16_pairs(lo, hi):
    lo_b = pltpu.bitcast(lo.astype(BF16).astype(F32), jnp.uint32)
    hi_b = pltpu.bitcast(hi.astype(BF16).astype(F32), jnp.uint32)
    return (lo_b >> 16) | (hi_b & jnp.uint32(0xFFFF0000))


def _unpack_bf16_pairs(w):
    lo = pltpu.bitcast(w << 16, F32).astype(BF16)
    hi = pltpu.bitcast(w & jnp.uint32(0xFFFF0000), F32).astype(BF16)
    return lo, hi


def _router_kernel(x_ref, nw_ref, wr_ref, br_ref, hnp_ref, info_ref, info_t_ref, cnt_ref):
    tm = x_ref.shape[0]

    @pl.when(pl.program_id(0) == 0)
    def _():
        cnt_ref[...] = jnp.zeros_like(cnt_ref)

    hn = _rms(x_ref[...], nw_ref[...])
    half = D_MODEL // 2
    hnp_ref[...] = _pack_bf16_pairs(hn[:, :half], hn[:, half:])

    logits = _dot_x3(hn, wr_ref[...]) + br_ref[...]
    lane = lax.broadcasted_iota(jnp.int32, (tm, LANES), 1).astype(F32)

    def first_max(vals):
        m = jnp.max(vals, axis=-1, keepdims=True)
        idx = jnp.min(jnp.where(vals == m, lane, float(LANES)), axis=-1, keepdims=True)
        return m, idx

    gl = jnp.where(lane < MOE_GROUPS, logits, NEG_BIG)
    gmax, gidx = first_max(gl)
    g_w = 1.0 / jnp.sum(jnp.exp(gl - gmax), axis=-1, keepdims=True)
    e0 = _LOGIT_E0 + MOE_PER_GROUP * gidx
    el = jnp.where((lane >= e0) & (lane < e0 + MOE_PER_GROUP), logits, NEG_BIG)
    m1, i1 = first_max(el)
    m2, i2 = first_max(jnp.where(lane == i1, NEG_BIG, el))
    p2 = jnp.exp(m2 - m1)
    w1 = 1.0 / (1.0 + p2)
    e1 = i1 - _LOGIT_E0
    e2 = i2 - _LOGIT_E0

    oh1 = jnp.where(lane == e1, 1.0, 0.0)
    oh2 = jnp.where(lane == e2, 1.0, 0.0)
    tri = _tri(tm)
    pre1 = jnp.dot(tri, oh1.astype(BF16), preferred_element_type=F32)
    pre2 = jnp.dot(tri, oh2.astype(BF16), preferred_element_type=F32)
    carry = cnt_ref[...]
    c1 = pre1[tm - 1:tm, :]
    r1 = jnp.sum(oh1 * (pre1 + carry), axis=-1, keepdims=True) - 1.0
    r2 = jnp.sum(oh2 * (pre2 + (carry + c1)), axis=-1, keepdims=True) - 1.0
    cnt_ref[...] = carry + c1 + pre2[tm - 1:tm, :]

    info = jnp.zeros((tm, LANES), F32)
    for ln, val in ((_I_E1, e1), (_I_E2, e2), (_I_G1, g_w * w1), (_I_G2, g_w * (w1 * p2)),
                    (_I_R1, r1), (_I_R2, r2)):
        info = jnp.where(lane == float(ln), val, info)
    info_ref[...] = info
    info_t_ref[...] = info.T[0:GROUP, :]


def _router(x2, norm_w, w_group, b_group, w_expert, b_expert):
    T, D = x2.shape
    tm = min(TOK_TILE, T)
    pad = LANES - MOE_GROUPS - MOE_EXPERTS
    wr = jnp.pad(jnp.concatenate([w_group, w_expert], axis=1), ((0, 0), (0, pad)))
    br = jnp.pad(jnp.concatenate([b_group, b_expert]), (0, pad))[None, :]
    return pl.pallas_call(
        _router_kernel,
        grid=(T // tm,),
        in_specs=[
            pl.BlockSpec((tm, D), lambda i: (i, 0)),
            _const_spec((1, D)),
            _const_spec((D, LANES)),
            _const_spec((1, LANES)),
        ],
        out_specs=[
            pl.BlockSpec((tm, D // 2), lambda i: (i, 0)),
            pl.BlockSpec((tm, LANES), lambda i: (i, 0)),
            pl.BlockSpec((GROUP, tm), lambda i: (0, i)),
            _const_spec((1, LANES)),
        ],
        out_shape=[
            jax.ShapeDtypeStruct((T, D // 2), jnp.uint32),
            jax.ShapeDtypeStruct((T, LANES), F32),
            jax.ShapeDtypeStruct((GROUP, T), F32),
            jax.ShapeDtypeStruct((1, LANES), F32),
        ],
        compiler_params=pltpu.CompilerParams(
            dimension_semantics=("arbitrary",), vmem_limit_bytes=VMEM_LIMIT),
        name="moe_router",
    )(x2, norm_w[None, :], wr, br)


def _plan_kernel(cnt_ref, info_ref, pos_ref, te_ref, nv_ref):
    nt = te_ref.shape[0]
    e1 = info_ref[_I_E1:_I_E1 + 1, :]
    e2 = info_ref[_I_E2:_I_E2 + 1, :]
    s1 = jnp.zeros_like(e1)
    s2 = jnp.zeros_like(e2)
    start = jnp.int32(0)
    for e in range(MOE_EXPERTS):
        tiles = lax.div(cnt_ref[e] + (ROW_TILE - 1), jnp.int32(ROW_TILE))
        seg = (start * ROW_TILE).astype(F32)
        s1 = jnp.where(e1 == float(e), seg, s1)
        s2 = jnp.where(e2 == float(e), seg, s2)

        def fill(i, carry, e=e):
            te_ref[i] = jnp.int32(e)
            return carry

        lax.fori_loop(start, start + tiles, fill, 0)
        start = start + tiles
    nv_ref[0] = start
    last = te_ref[jnp.maximum(start - 1, 0)]

    def fill_tail(i, carry):
        te_ref[i] = last
        return carry

    lax.fori_loop(start, nt, fill_tail, 0)
    pos_ref[0:1, :] = (s1 + info_ref[_I_R1:_I_R1 + 1, :]).astype(jnp.int32)
    pos_ref[1:2, :] = (s2 + info_ref[_I_R2:_I_R2 + 1, :]).astype(jnp.int32)


def _plan(cnt, info_t, nt):
    T = info_t.shape[1]
    return pl.pallas_call(
        _plan_kernel,
        in_specs=[
            pl.BlockSpec(memory_space=pltpu.SMEM),
            pl.BlockSpec(memory_space=pltpu.VMEM),
        ],
        out_specs=[
            pl.BlockSpec(memory_space=pltpu.VMEM),
            pl.BlockSpec(memory_space=pltpu.SMEM),
            pl.BlockSpec(memory_space=pltpu.SMEM),
        ],
        out_shape=[
            jax.ShapeDtypeStruct((2, T), jnp.int32),
            jax.ShapeDtypeStruct((nt,), jnp.int32),
            jax.ShapeDtypeStruct((1,), jnp.int32),
        ],
        name="moe_plan",
    )(cnt, info_t)


def _dispatch_kernel(p1_ref, p2_ref, hnp_ref, xs_in_ref, xs_ref, sem):
    del xs_in_ref
    tm = hnp_ref.shape[0]
    base = pl.program_id(0) * tm

    def row_copy(r, p):
        return pltpu.make_async_copy(hnp_ref.at[pl.ds(r, 1)], xs_ref.at[pl.ds(p, 1)], sem)

    def issue(r, carry):
        row_copy(r, p1_ref[base + r]).start()
        row_copy(r, p2_ref[base + r]).start()
        return carry

    lax.fori_loop(0, tm, issue, 0)

    def drain(r, carry):
        row_copy(0, 0).wait()
        row_copy(0, 0).wait()
        return carry

    lax.fori_loop(0, tm, drain, 0)


def _dispatch(hnp, pos1, pos2, n_rows):
    T, W = hnp.shape
    tm = min(TOK_TILE, T)
    return pl.pallas_call(
        _dispatch_kernel,
        grid_spec=pltpu.PrefetchScalarGridSpec(
            num_scalar_prefetch=2,
            grid=(T // tm,),
            in_specs=[
                pl.BlockSpec((tm, W), lambda i, p1, p2: (i, 0)),
                pl.BlockSpec(memory_space=pl.ANY),
            ],
            out_specs=pl.BlockSpec(memory_space=pl.ANY),
            scratch_shapes=[pltpu.SemaphoreType.DMA(())],
        ),
        out_shape=jax.ShapeDtypeStruct((n_rows, W), jnp.uint32),
        input_output_aliases={3: 0},
        compiler_params=pltpu.CompilerParams(dimension_semantics=("arbitrary",)),
        name="moe_dispatch",
    )(pos1, pos2, hnp, jnp.zeros((n_rows, W), jnp.uint32))


def _experts_kernel(te_ref, nv_ref, xs_ref, w1_ref, w3_ref, w2_ref, ys_ref, w1b, w3b, w2b):
    i = pl.program_id(0)

    @pl.when(i < nv_ref[0])
    def _():
        @pl.when((i == 0) | (te_ref[i] != te_ref[jnp.maximum(i - 1, 0)]))
        def _():
            w1b[...] = w1_ref[0].astype(BF16)
            w3b[...] = w3_ref[0].astype(BF16)
            w2b[...] = w2_ref[0].astype(BF16)

        lo, hi = _unpack_bf16_pairs(xs_ref[...])
        half = D_MODEL // 2
        d = functools.partial(jnp.dot, preferred_element_type=F32)
        h1 = d(lo, w1b[0:half, :]) + d(hi, w1b[half:, :])
        h3 = d(lo, w3b[0:half, :]) + d(hi, w3b[half:, :])
        hid = (h1 * _sigmoid(h1)) * h3
        ys_ref[...] = d(hid.astype(BF16), w2b[...])

    @pl.when(i >= nv_ref[0])
    def _():
        ys_ref[...] = jnp.zeros_like(ys_ref)


def _experts(xs, tile_expert, n_valid, w1, w3, w2):
    n_rows, W = xs.shape
    nt = n_rows // ROW_TILE
    D, F = w1.shape[1], w1.shape[2]

    def row_map(i, te, nv):
        return (i, 0)

    def w_map(i, te, nv):
        return (te[i], 0, 0)

    return pl.pallas_call(
        _experts_kernel,
        grid_spec=pltpu.PrefetchScalarGridSpec(
            num_scalar_prefetch=2,
            grid=(nt,),
            in_specs=[
                pl.BlockSpec((ROW_TILE, W), row_map),
                pl.BlockSpec((1, D, F), w_map),
                pl.BlockSpec((1, D, F), w_map),
                pl.BlockSpec((1, F, D), w_map),
            ],
            out_specs=pl.BlockSpec((ROW_TILE, D), row_map),
            scratch_shapes=[
                pltpu.VMEM((D, F), BF16),
                pltpu.VMEM((D, F), BF16),
                pltpu.VMEM((F, D), BF16),
            ],
        ),
        out_shape=jax.ShapeDtypeStruct((n_rows, D), F32),
        compiler_params=pltpu.CompilerParams(
            dimension_semantics=("arbitrary",), vmem_limit_bytes=VMEM_LIMIT),
        name="moe_experts",
    )(tile_expert, n_valid, xs, w1, w3, w2)


def _combine_kernel(p1_ref, p2_ref, x_ref, info_ref, fw_ref, ys_ref, out_ref, buf, sems, *, final):
    tm = x_ref.shape[0]
    i = pl.program_id(0)
    n = pl.num_programs(0)

    def row_copy(slot, k, r, p):
        return pltpu.make_async_copy(ys_ref.at[pl.ds(p, 1)], buf.at[slot, k, pl.ds(r, 1)],
                                     sems.at[slot])

    def fetch(step, slot):
        base = step * tm

        def issue(r, carry):
            row_copy(slot, 0, r, p1_ref[base + r]).start()
            row_copy(slot, 1, r, p2_ref[base + r]).start()
            return carry

        lax.fori_loop(0, tm, issue, 0)

    @pl.when(i == 0)
    def _():
        fetch(0, 0)

    @pl.when(i + 1 < n)
    def _():
        fetch(i + 1, (i + 1) % 2)

    slot = i % 2

    def drain(r, carry):
        row_copy(slot, 0, 0, 0).wait()
        row_copy(slot, 1, 0, 0).wait()
        return carry

    lax.fori_loop(0, tm, drain, 0)

    info = info_ref[...]
    g1 = info[:, _I_G1:_I_G1 + 1]
    g2 = info[:, _I_G2:_I_G2 + 1]
    y = x_ref[...] + (g1 * buf[slot, 0] + g2 * buf[slot, 1])
    if final:
        y = _rms(y, fw_ref[...])
    out_ref[...] = y


def _combine(x2, info, ys, pos1, pos2, final_w, final):
    T, D = x2.shape
    tm = min(TOK_TILE, T)
    return pl.pallas_call(
        functools.partial(_combine_kernel, final=final),
        grid_spec=pltpu.PrefetchScalarGridSpec(
            num_scalar_prefetch=2,
            grid=(T // tm,),
            in_specs=[
                pl.BlockSpec((tm, D), lambda i, p1, p2: (i, 0)),
                pl.BlockSpec((tm, LANES), lambda i, p1, p2: (i, 0)),
                pl.BlockSpec((1, D), lambda i, p1, p2: (0, 0)),
                pl.BlockSpec(memory_space=pl.ANY),
            ],
            out_specs=pl.BlockSpec((tm, D), lambda i, p1, p2: (i, 0)),
            scratch_shapes=[
                pltpu.VMEM((2, 2, tm, D), F32),
                pltpu.SemaphoreType.DMA((2,)),
            ],
        ),
        out_shape=jax.ShapeDtypeStruct((T, D), F32),
        compiler_params=pltpu.CompilerParams(
            dimension_semantics=("arbitrary",), vmem_limit_bytes=VMEM_LIMIT),
        name="moe_combine",
    )(pos1, pos2, x2, info, final_w[None, :], ys)


def _hier_moe(x2, norm_w, w_group, b_group, w_expert, b_expert, w1, w3, w2, final_w, final):
    T, D = x2.shape
    hnp, info, info_t, cnt = _router(x2, norm_w, w_group, b_group, w_expert, b_expert)
    nt = (2 * T) // ROW_TILE + MOE_EXPERTS
    pos, tile_expert, n_valid = _plan(cnt[0].astype(jnp.int32), info_t, nt)
    pos1, pos2 = pos[0], pos[1]
    xs = _dispatch(hnp, pos1, pos2, nt * ROW_TILE)
    ys = _experts(xs, tile_expert, n_valid, w1, w3, w2)
    return _combine(x2, info, ys, pos1, pos2, final_w, final)


def kernel(x, norm_mix, norm_ffn, norm_final, even_w_in, hgrn_lb_logits, hgrn_norm_w, gla_w_up,
           gla_b_up, gla_norm_w, even_w_out, odd_w_in, ret_norm_w, conv_w, conv_b, rglru_w_r,
           rglru_b_r, rglru_w_i, rglru_b_i, rglru_lambda, odd_w_out, moe_w_group, moe_b_group,
           moe_w_expert, moe_b_expert, moe_w1, moe_w3, moe_w2):
    B, L, D = x.shape
    depth = norm_mix.shape[0]
    lb_table = jnp.cumsum(jax.nn.softmax(hgrn_lb_logits.astype(F32), axis=0), axis=0)
    for l in range(depth):
        j = l // 2
        if l % 2 == 0:
            x = _even_mixer(x, norm_mix[l], even_w_in[j], lb_table[l], gla_w_up[j], gla_b_up[j],
                            hgrn_norm_w[j], gla_norm_w[j], even_w_out[j])
        else:
            x = _odd_mixer(x, norm_mix[l], odd_w_in[j], ret_norm_w[j], conv_w[j], conv_b[j],
                           rglru_w_r[j], rglru_b_r[j], rglru_w_i[j], rglru_b_i[j], rglru_lambda[j],
                           odd_w_out[j])
        x = _hier_moe(x.reshape(B * L, D), norm_ffn[l], moe_w_group[l], moe_b_group[l],
                      moe_w_expert[l], moe_b_expert[l], moe_w1[l], moe_w3[l], moe_w2[l],
                      norm_final, l == depth - 1).reshape(B, L, D)
    return x
```

```python
import functools

import numpy as np
import jax
import jax.numpy as jnp
from jax import lax
from jax.experimental import pallas as pl
from jax.experimental.pallas import tpu as pltpu
from jax.experimental.pallas import tpu_sc as plsc

F32 = jnp.float32
BF16 = jnp.bfloat16

EPS = 1e-6
CHUNK = 64
GROUP = 8
LOG2E = 1.4426950408889634
LANES = 128
D_MODEL = 1024
MIX_WIDTH = 1024
N_HEADS = 8
GLA_RANK = 16
GLA_TAU = 16.0
RET_HEADS = 4
RET_DK = 64
ROPE_BASE = 10000.0
LRU_WIDTH = 512
LRU_BLOCKS = 8
LRU_CONV = 4
LRU_C = 8.0
MOE_GROUPS = 4
MOE_PER_GROUP = 8
MOE_EXPERTS = MOE_GROUPS * MOE_PER_GROUP
MOE_DFF = 512
NEG_BIG = -1e30

MIX_TILE = 512
TOK_TILE = 512
ROW_TILE = 256
SC_CORES = 2
SC_WORKERS = 32
SC_ROWS = 128
VMEM_LIMIT = 56 * 1024 * 1024


def _dot(a, b):
    return jnp.dot(a.astype(BF16), b.astype(BF16), preferred_element_type=F32)


def _dot_nt(a, b):
    return lax.dot_general(a.astype(BF16), b.astype(BF16), (((1,), (1,)), ((), ())),
                           preferred_element_type=F32)


def _dot_tn(a, b):
    return lax.dot_general(a.astype(BF16), b.astype(BF16), (((0,), (0,)), ((), ())),
                           preferred_element_type=F32)


def _split2(x):
    hi = x.astype(BF16)
    lo = (x - hi.astype(F32)).astype(BF16)
    return hi, lo


def _split3(x):
    hi = x.astype(BF16)
    r = x - hi.astype(F32)
    mid = r.astype(BF16)
    lo = (r - mid.astype(F32)).astype(BF16)
    return hi, mid, lo


def _dot_x3(a, b):
    ah, al = _split2(a)
    bh, bl = _split2(b)
    d = functools.partial(jnp.dot, preferred_element_type=F32)
    return d(ah, bh) + (d(ah, bl) + d(al, bh))


def _sigmoid(x):
    return 1.0 / (1.0 + jnp.exp(-x))


def _softplus(x):
    return jnp.maximum(x, 0.0) + jnp.log(1.0 + jnp.exp(-jnp.abs(x)))


def _rms(x, w):
    return x * lax.rsqrt(jnp.mean(x * x, axis=-1, keepdims=True) + EPS) * w


def _tri(n):
    r = lax.broadcasted_iota(jnp.int32, (n, n), 0)
    c = lax.broadcasted_iota(jnp.int32, (n, n), 1)
    return jnp.where(r >= c, 1.0, 0.0).astype(BF16)


def _cumsum_rows(tri, g):
    hi, mid, lo = _split3(g)
    d = functools.partial(jnp.dot, preferred_element_type=F32)
    return d(tri, hi) + (d(tri, mid) + d(tri, lo))


_E_AQ, _E_AF, _E_AI, _E_AG = 0, 512, 1024, 1536
_E_BQ, _E_BK, _E_BV, _E_BR, _E_LOW = 2048, 2304, 2560, 3072, 3584
EVEN_COLS = 3712
_QK_COLS = 768


_LEVELS = (((32, 64), (0, 32), 31),
           ((16, 32), (0, 16), 15), ((48, 64), (32, 48), 47),
           ((8, 16), (0, 8), 7), ((24, 32), (16, 24), 23), ((40, 48), (32, 40), 39),
           ((56, 64), (48, 56), 55))


def _place_rows(x, r0):
    parts = []
    if r0 > 0:
        parts.append(jnp.zeros((r0, x.shape[1]), F32))
    parts.append(x)
    r1 = r0 + x.shape[0]
    if r1 < CHUNK:
        parts.append(jnp.zeros((CHUNK - r1, x.shape[1]), F32))
    return jnp.concatenate(parts, axis=0)


def _gla_unit_chunk(c, unit, heads, q_ref, k_ref, v_ref, b2, o_ref, st_ref, a_ref):
    ul = slice(unit * LANES, (unit + 1) * LANES)
    rows = pl.ds(pl.multiple_of(c * CHUNK, CHUNK), CHUNK)
    q = q_ref[rows, ul]
    k = k_ref[rows, ul]
    b = b2[:, ul]
    lane = lax.broadcasted_iota(jnp.int32, (1, LANES), 1)
    paired = len(heads) == 2

    trow = lax.broadcasted_iota(jnp.int32, (GROUP, LANES), 0)
    for j in range(CHUNK // GROUP):
        qj = q[j * GROUP:(j + 1) * GROUP]
        bj = b[j * GROUP:(j + 1) * GROUP]
        for s in range(0, GROUP, 2):
            pair = [qj * jnp.exp2(jnp.where(trow >= s + i, bj - bj[s + i:s + i + 1, :], NEG_BIG))
                    for i in (0, 1)]
            r0 = (j * GROUP + s) * GROUP
            a_ref[unit, r0:r0 + 2 * GROUP, :] = jnp.concatenate(pair, axis=0).astype(BF16)
    if paired:
        keys = jnp.concatenate([jnp.where(lane < 64, k, 0.0), jnp.where(lane >= 64, k, 0.0)], axis=0)
        scol = lane & 63
    else:
        keys = k
        scol = lane[:, :CHUNK]
    sc = _dot_nt(a_ref[unit], keys)
    groups = []
    for j in range(CHUNK // GROUP):
        acc = jnp.zeros((GROUP, sc.shape[1]), F32)
        for s in range(GROUP):
            r0 = (j * GROUP + s) * GROUP
            acc = jnp.where(scol == j * GROUP + s, sc[r0:r0 + GROUP, :], acc)
        groups.append(acc)
    p = jnp.concatenate(groups, axis=0)

    qparts, kparts = [], []
    for rq, rk, ref in _LEVELS:
        r = b[ref:ref + 1, :]
        qparts.append(_place_rows(q[rq[0]:rq[1]] * jnp.exp2(b[rq[0]:rq[1]] - r), rq[0]))
        kparts.append(_place_rows(k[rk[0]:rk[1]] * jnp.exp2(r - b[rk[0]:rk[1]]), rk[0]))
    qcat = jnp.concatenate(qparts, axis=1)
    kcat = jnp.concatenate(kparts, axis=1)

    eb = jnp.exp2(b)
    b_last = b[CHUNK - 1:CHUNK, :]
    kd = k * jnp.exp2(b_last - b)
    dec = jnp.exp2(b_last)
    if paired:
        vstack = jnp.concatenate(
            [v_ref[rows, h * LANES:(h + 1) * LANES] for h, _ in heads], axis=0)
        lane7 = lax.broadcasted_iota(jnp.int32, (1, len(_LEVELS) * LANES), 1)
    for head, lo in heads:
        vl = slice(head * LANES, (head + 1) * LANES)
        v = v_ref[rows, vl]
        if paired:
            own = (lane & 64) == lo
            qm = jnp.where(own, q, 0.0)
            qcm = jnp.where((lane7 & 64) == lo, qcat, 0.0)
        else:
            qm, qcm = q, qcat
        st = st_ref[head]
        o = _dot_nt(qm * eb, st)
        st_ref[head] = st * dec + _dot_tn(v, kd)
        s_off = _dot_nt(qcm, kcat)
        if paired:
            o = o + _dot(s_off, v) + _dot(jnp.where(own, p, 0.0), vstack)
        else:
            o = o + _dot(s_off + p, v)
        o_ref[rows, vl] = o


def _even_kernel(x_ref, nw_ref, win_ref, lb_ref, wup_ref, bup_ref, hw_ref, wout_ref, out_ref,
                 q_ref, k_ref, g_ref, v_ref, gate_ref, o_ref, st_ref, a_ref):
    tl = x_ref.shape[1]

    @pl.when(pl.program_id(1) == 0)
    def _():
        st_ref[...] = jnp.zeros_like(st_ref)

    x = x_ref[0]
    hn = _rms(x, nw_ref[...]).astype(BF16)

    def proj(c0, n):
        return jnp.dot(hn, win_ref[:, c0:c0 + n], preferred_element_type=F32)

    q_ref[:, 0:512] = proj(_E_AQ, 512)
    lb = lb_ref[...]
    f = lb + (1.0 - lb) * _sigmoid(proj(_E_AF, 512))
    k_ref[:, 0:512] = 1.0 - f
    g_ref[:, 0:512] = jnp.log(f)
    v_ref[:, 0:512] = proj(_E_AI, 512)
    ag = proj(_E_AG, 512)
    gate_ref[:, 0:512] = ag * _sigmoid(ag)
    q_ref[:, 512:768] = proj(_E_BQ, 256) * (64.0 ** -0.5)
    k_ref[:, 512:768] = proj(_E_BK, 256)
    v_ref[:, 512:1024] = proj(_E_BV, 512)
    br = proj(_E_BR, 512)
    gate_ref[:, 512:1024] = br * _sigmoid(br)
    z = _dot_x3(proj(_E_LOW, LANES), wup_ref[...]) + bup_ref[...]
    g_ref[:, 512:768] = -_softplus(-z) * (1.0 / GLA_TAU)

    tri = _tri(CHUNK)

    def chunk_body(c, carry):
        rows = pl.ds(pl.multiple_of(c * CHUNK, CHUNK), CHUNK)
        b2 = _cumsum_rows(tri, g_ref[rows, :]) * LOG2E
        for unit in range(4):
            _gla_unit_chunk(c, unit, ((unit, None),), q_ref, k_ref, v_ref, b2, o_ref, st_ref, a_ref)
        for unit in range(4, 6):
            h0 = 4 + 2 * (unit - 4)
            _gla_unit_chunk(c, unit, ((h0, 0), (h0 + 1, 64)), q_ref, k_ref, v_ref, b2, o_ref,
                            st_ref, a_ref)
        return carry

    lax.fori_loop(0, tl // CHUNK, chunk_body, 0)

    hw = hw_ref[...]
    ys = []
    for head in range(N_HEADS):
        sl = slice(head * LANES, (head + 1) * LANES)
        oh = o_ref[:, sl]
        ys.append(oh * lax.rsqrt(jnp.mean(oh * oh, axis=-1, keepdims=True) + EPS)
                  * hw[:, sl] * gate_ref[:, sl])
    y = jnp.concatenate(ys, axis=1).astype(BF16)
    out_ref[0] = x + jnp.dot(y, wout_ref[...], preferred_element_type=F32)


def _const_spec(shape):
    nd = len(shape)
    return pl.BlockSpec(shape, lambda *_: (0,) * nd)


def _even_mixer(x, norm_w, w_in, lb, w_up, b_up, hgrn_norm_w, gla_norm_w, w_out):
    B, L, D = x.shape
    tl = min(MIX_TILE, L)
    s = np.cumsum([0, 512, 512, 512, 512, 256, 256, 512, GLA_RANK, 512])
    cols = [w_in[:, s[i]:s[i + 1]] for i in range(9)]
    low = jnp.pad(cols[7], ((0, 0), (0, LANES - GLA_RANK)))
    w_in_r = jnp.concatenate(cols[:7] + [cols[8], low], axis=1).astype(BF16)
    w_up_p = jnp.pad(w_up, ((0, LANES - GLA_RANK), (0, 0)))
    hw = jnp.concatenate([hgrn_norm_w, gla_norm_w])[None, :]
    return pl.pallas_call(
        _even_kernel,
        grid=(B, L // tl),
        in_specs=[
            pl.BlockSpec((1, tl, D), lambda b, l: (b, l, 0)),
            _const_spec((1, D)),
            _const_spec((D, EVEN_COLS)),
            _const_spec((1, 512)),
            _const_spec((LANES, 256)),
            _const_spec((1, 256)),
            _const_spec((1, MIX_WIDTH)),
            _const_spec((MIX_WIDTH, D)),
        ],
        out_specs=pl.BlockSpec((1, tl, D), lambda b, l: (b, l, 0)),
        out_shape=jax.ShapeDtypeStruct((B, L, D), F32),
        scratch_shapes=[
            pltpu.VMEM((tl, _QK_COLS), F32),
            pltpu.VMEM((tl, _QK_COLS), F32),
            pltpu.VMEM((tl, _QK_COLS), F32),
            pltpu.VMEM((tl, MIX_WIDTH), F32),
            pltpu.VMEM((tl, MIX_WIDTH), F32),
            pltpu.VMEM((tl, MIX_WIDTH), F32),
            pltpu.VMEM((N_HEADS, LANES, LANES), F32),
            pltpu.VMEM((_QK_COLS // LANES, CHUNK * GROUP, LANES), BF16),
        ],
        compiler_params=pltpu.CompilerParams(
            dimension_semantics=("arbitrary", "arbitrary"), vmem_limit_bytes=VMEM_LIMIT),
        name="even_mixer",
    )(x, norm_w[None, :], w_in_r, lb[None, :], w_up_p, b_up[None, :], hw, w_out.astype(BF16))


_O_Q, _O_QR, _O_K, _O_KR, _O_V, _O_G, _O_X, _O_GATE = 0, 256, 512, 768, 1024, 1536, 2048, 2560
ODD_COLS = 3072


def _odd_kernel(x_ref, nw_ref, win_ref, cos_ref, sin_ref, dmat_ref, qdec_ref, kdec_ref, sdec_ref,
                rw_ref, cw_ref, cb_ref, wri_ref, bri_ref, lam_ref, wout_ref, out_ref,
                q_ref, k_ref, v_ref, o_ref, xe_ref, a_ref, u_ref, st_ref, h_ref):
    tl = x_ref.shape[1]

    @pl.when(pl.program_id(1) == 0)
    def _():
        st_ref[...] = jnp.zeros_like(st_ref)
        h_ref[...] = jnp.zeros_like(h_ref)
        xe_ref[0:8, :] = jnp.zeros((8, LRU_WIDTH), F32)

    x = x_ref[0]
    hn = _rms(x, nw_ref[...]).astype(BF16)

    def proj(c0, n):
        return jnp.dot(hn, win_ref[:, c0:c0 + n], preferred_element_type=F32)

    cos = cos_ref[...]
    sin = sin_ref[...]
    q_ref[...] = proj(_O_Q, 256) * cos + proj(_O_QR, 256) * sin
    k_ref[...] = (proj(_O_K, 256) * cos + proj(_O_KR, 256) * sin) * (RET_DK ** -0.5)
    v_ref[...] = proj(_O_V, 512)

    xe_ref[8:8 + tl, :] = proj(_O_X, LRU_WIDTH)
    cw = cw_ref[...]
    xc = cb_ref[...]
    for j in range(LRU_CONV):
        xc = xc + xe_ref[5 + j:5 + j + tl, :] * cw[j:j + 1, :]
    xe_ref[0:8, :] = xe_ref[tl:tl + 8, :]
    ri = jnp.dot(xc.astype(BF16), wri_ref[...], preferred_element_type=F32) + bri_ref[...]
    r = _sigmoid(ri[:, :LRU_WIDTH])
    i = _sigmoid(ri[:, LRU_WIDTH:])
    a = jnp.exp(-LRU_C * r * _softplus(-lam_ref[...]))
    a_ref[...] = a
    u_ref[...] = jnp.sqrt(1.0 - a * a) * (i * xc)

    lane = lax.broadcasted_iota(jnp.int32, (1, LANES), 1)
    crow = lax.broadcasted_iota(jnp.int32, (CHUNK, LRU_WIDTH), 0)

    def chunk_body(c, carry):
        rows = pl.ds(pl.multiple_of(c * CHUNK, CHUNK), CHUNK)
        for head in range(RET_HEADS):
            unit, half = head // 2, head % 2
            ul = slice(unit * LANES, (unit + 1) * LANES)
            vl = slice(head * LANES, (head + 1) * LANES)
            own = (lane < 64) if half == 0 else (lane >= 64)
            q = jnp.where(own, q_ref[rows, ul], 0.0)
            k = k_ref[rows, ul]
            v = v_ref[rows, vl]
            st = st_ref[head]
            o = _dot_nt(q, st) * qdec_ref[head]
            o = o + _dot(_dot_nt(q, k) * dmat_ref[head], v)
            st_ref[head] = st * sdec_ref[head] + _dot_tn(v, k * kdec_ref[head])
            o_ref[rows, vl] = o
        ca = a_ref[rows, :]
        ch = u_ref[rows, :]
        d = 1
        while d < CHUNK:
            keep = crow >= d
            sa = jnp.where(keep, pltpu.roll(ca, d, axis=0), 1.0)
            sh = jnp.where(keep, pltpu.roll(ch, d, axis=0), 0.0)
            ch = ca * sh + ch
            ca = ca * sa
            d *= 2
        ch = ch + ca * h_ref[...]
        h_ref[...] = ch[CHUNK - 1:CHUNK, :]
        o_ref[rows, 512:1024] = ch
        return carry

    lax.fori_loop(0, tl // CHUNK, chunk_body, 0)

    rw = rw_ref[...]
    ys = []
    for head in range(RET_HEADS):
        sl = slice(head * LANES, (head + 1) * LANES)
        oh = o_ref[:, sl]
        oh = oh - jnp.mean(oh, axis=-1, keepdims=True)
        ys.append(oh * lax.rsqrt(jnp.mean(oh * oh, axis=-1, keepdims=True) + EPS) * rw[:, sl])
    cg = proj(_O_G, 512)
    y_c = jnp.concatenate(ys, axis=1) * (cg * _sigmoid(cg))
    dg = proj(_O_GATE, LRU_WIDTH)
    gelu = 0.5 * dg * (1.0 + jnp.tanh(np.sqrt(2.0 / np.pi) * (dg + 0.044715 * (dg * dg * dg))))
    y_d = o_ref[:, 512:1024] * gelu
    y = jnp.concatenate([y_c, y_d], axis=1).astype(BF16)
    out_ref[0] = x + jnp.dot(y, wout_ref[...], preferred_element_type=F32)


def _rot_cols(w):
    d = w.shape[0]
    w4 = w.reshape(d, RET_HEADS, 2, RET_DK // 2)
    return jnp.stack([-w4[:, :, 1], w4[:, :, 0]], axis=2).reshape(d, RET_HEADS * RET_DK)


def _odd_mixer(x, norm_w, w_in, ret_norm_w, conv_w, conv_b, w_r, b_r, w_i, b_i, lam, w_out):
    B, L, D = x.shape
    tl = min(MIX_TILE, L)
    s = np.cumsum([0, 256, 256, 512, 512, 512, 512])
    wq, wk, wv, wg, wx, wgate = [w_in[:, s[i]:s[i + 1]] for i in range(6)]
    w_in_r = jnp.concatenate([wq, _rot_cols(wq), wk, _rot_cols(wk), wv, wg, wx, wgate],
                             axis=1).astype(BF16)
    half = RET_DK // 2
    inv = ROPE_BASE ** (-jnp.arange(half, dtype=F32) / half)
    ang = jnp.arange(L, dtype=F32)[:, None] * inv[None, :]
    cos = jnp.tile(jnp.cos(ang), (1, 2 * RET_HEADS))
    sin = jnp.tile(jnp.sin(ang), (1, 2 * RET_HEADS))
    lg = jnp.log1p(-jnp.exp2(-5.0 - jnp.arange(RET_HEADS, dtype=F32)))[:, None, None]
    t = jnp.arange(CHUNK, dtype=F32)
    rel = t[:, None] - t[None, :]
    dmat = jnp.where(rel >= 0, jnp.exp(lg * jnp.maximum(rel, 0.0)), 0.0)
    qdec = jnp.broadcast_to(jnp.exp(lg * (t[None, :, None] + 1.0)), (RET_HEADS, CHUNK, LANES))
    kdec = jnp.broadcast_to(jnp.exp(lg * (CHUNK - 1.0 - t[None, :, None])), (RET_HEADS, CHUNK, LANES))
    sdec = jnp.broadcast_to(jnp.exp(lg * float(CHUNK)), (RET_HEADS, 1, LANES))
    eye = jnp.eye(LRU_BLOCKS, dtype=F32)
    bd = LRU_WIDTH // LRU_BLOCKS

    def blockdiag(w):
        return (eye[:, None, :, None] * w[:, :, None, :]).reshape(LRU_WIDTH, LRU_WIDTH)

    wri = jnp.concatenate([blockdiag(w_r), blockdiag(w_i)], axis=1).astype(BF16)
    bri = jnp.concatenate([b_r, b_i])[None, :]
    return pl.pallas_call(
        _odd_kernel,
        grid=(B, L // tl),
        in_specs=[
            pl.BlockSpec((1, tl, D), lambda b, l: (b, l, 0)),
            _const_spec((1, D)),
            _const_spec((D, ODD_COLS)),
            pl.BlockSpec((tl, 256), lambda b, l: (l, 0)),
            pl.BlockSpec((tl, 256), lambda b, l: (l, 0)),
            _const_spec((RET_HEADS, CHUNK, CHUNK)),
            _const_spec((RET_HEADS, CHUNK, LANES)),
            _const_spec((RET_HEADS, CHUNK, LANES)),
            _const_spec((RET_HEADS, 1, LANES)),
            _const_spec((1, 512)),
            _const_spec((LRU_CONV, LRU_WIDTH)),
            _const_spec((1, LRU_WIDTH)),
            _const_spec((LRU_WIDTH, 2 * LRU_WIDTH)),
            _const_spec((1, 2 * LRU_WIDTH)),
            _const_spec((1, LRU_WIDTH)),
            _const_spec((MIX_WIDTH, D)),
        ],
        out_specs=pl.BlockSpec((1, tl, D), lambda b, l: (b, l, 0)),
        out_shape=jax.ShapeDtypeStruct((B, L, D), F32),
        scratch_shapes=[
            pltpu.VMEM((tl, 256), F32),
            pltpu.VMEM((tl, 256), F32),
            pltpu.VMEM((tl, 512), F32),
            pltpu.VMEM((tl, MIX_WIDTH), F32),
            pltpu.VMEM((tl + 8, LRU_WIDTH), F32),
            pltpu.VMEM((tl, LRU_WIDTH), F32),
            pltpu.VMEM((tl, LRU_WIDTH), F32),
            pltpu.VMEM((RET_HEADS, LANES, LANES), F32),
            pltpu.VMEM((1, LRU_WIDTH), F32),
        ],
        compiler_params=pltpu.CompilerParams(
            dimension_semantics=("arbitrary", "arbitrary"), vmem_limit_bytes=VMEM_LIMIT),
        name="odd_mixer",
    )(x, norm_w[None, :], w_in_r, cos, sin, dmat, qdec, kdec, sdec, ret_norm_w[None, :],
      conv_w, conv_b[None, :], wri, bri, lam[None, :], w_out.astype(BF16))


_I_E1, _I_E2, _I_G1, _I_G2, _I_R1, _I_R2 = 0, 1, 2, 3, 4, 5
_LOGIT_E0 = MOE_GROUPS


def _pack_bf16_pairs(lo, hi):
    lo_b = pltpu.bitcast(lo.astype(BF16).astype(F32), jnp.uint32)
    hi_b = pltpu.bitcast(hi.astype(BF16).astype(F32), jnp.uint32)
    return (lo_b >> 16) | (hi_b & jnp.uint32(0xFFFF0000))


def _unpack_bf16_pairs(w):
    lo = pltpu.bitcast(w << 16, F32).astype(BF16)
    hi = pltpu.bitcast(w & jnp.uint32(0xFFFF0000), F32).astype(BF16)
    return lo, hi


def _router_kernel(x_ref, nw_ref, wr_ref, br_ref, hnp_ref, info_ref, info_t_ref, cnt_ref):
    tm = x_ref.shape[0]

    @pl.when(pl.program_id(0) == 0)
    def _():
        cnt_ref[...] = jnp.zeros_like(cnt_ref)

    hn = _rms(x_ref[...], nw_ref[...])
    half = D_MODEL // 2
    hnp_ref[...] = _pack_bf16_pairs(hn[:, :half], hn[:, half:])

    logits = _dot_x3(hn, wr_ref[...]) + br_ref[...]
    lane = lax.broadcasted_iota(jnp.int32, (tm, LANES), 1).astype(F32)

    def first_max(vals):
        m = jnp.max(vals, axis=-1, keepdims=True)
        idx = jnp.min(jnp.where(vals == m, lane, float(LANES)), axis=-1, keepdims=True)
        return m, idx

    gl = jnp.where(lane < MOE_GROUPS, logits, NEG_BIG)
    gmax, gidx = first_max(gl)
    g_w = 1.0 / jnp.sum(jnp.exp(gl - gmax), axis=-1, keepdims=True)
    e0 = _LOGIT_E0 + MOE_PER_GROUP * gidx
    el = jnp.where((lane >= e0) & (lane < e0 + MOE_PER_GROUP), logits, NEG_BIG)
    m1, i1 = first_max(el)
    m2, i2 = first_max(jnp.where(lane == i1, NEG_BIG, el))
    p2 = jnp.exp(m2 - m1)
    w1 = 1.0 / (1.0 + p2)
    e1 = i1 - _LOGIT_E0
    e2 = i2 - _LOGIT_E0

    oh1 = jnp.where(lane == e1, 1.0, 0.0)
    oh2 = jnp.where(lane == e2, 1.0, 0.0)
    tri = _tri(tm)
    pre1 = jnp.dot(tri, oh1.astype(BF16), preferred_element_type=F32)
    pre2 = jnp.dot(tri, oh2.astype(BF16), preferred_element_type=F32)
    carry = cnt_ref[...]
    c1 = pre1[tm - 1:tm, :]
    r1 = jnp.sum(oh1 * (pre1 + carry), axis=-1, keepdims=True) - 1.0
    r2 = jnp.sum(oh2 * (pre2 + (carry + c1)), axis=-1, keepdims=True) - 1.0
    cnt_ref[...] = carry + c1 + pre2[tm - 1:tm, :]

    info = jnp.zeros((tm, LANES), F32)
    for ln, val in ((_I_E1, e1), (_I_E2, e2), (_I_G1, g_w * w1), (_I_G2, g_w * (w1 * p2)),
                    (_I_R1, r1), (_I_R2, r2)):
        info = jnp.where(lane == float(ln), val, info)
    info_ref[...] = info
    info_t_ref[...] = info.T[0:GROUP, :]


def _router(x2, norm_w, w_group, b_group, w_expert, b_expert):
    T, D = x2.shape
    tm = min(TOK_TILE, T)
    pad = LANES - MOE_GROUPS - MOE_EXPERTS
    wr = jnp.pad(jnp.concatenate([w_group, w_expert], axis=1), ((0, 0), (0, pad)))
    br = jnp.pad(jnp.concatenate([b_group, b_expert]), (0, pad))[None, :]
    return pl.pallas_call(
        _router_kernel,
        grid=(T // tm,),
        in_specs=[
            pl.BlockSpec((tm, D), lambda i: (i, 0)),
            _const_spec((1, D)),
            _const_spec((D, LANES)),
            _const_spec((1, LANES)),
        ],
        out_specs=[
            pl.BlockSpec((tm, D // 2), lambda i: (i, 0)),
            pl.BlockSpec((tm, LANES), lambda i: (i, 0)),
            pl.BlockSpec((GROUP, tm), lambda i: (0, i)),
            _const_spec((1, LANES)),
        ],
        out_shape=[
            jax.ShapeDtypeStruct((T, D // 2), jnp.uint32),
            jax.ShapeDtypeStruct((T, LANES), F32),
            jax.ShapeDtypeStruct((GROUP, T), F32),
            jax.ShapeDtypeStruct((1, LANES), F32),
        ],
        compiler_params=pltpu.CompilerParams(
            dimension_semantics=("arbitrary",), vmem_limit_bytes=VMEM_LIMIT),
        name="moe_router",
    )(x2, norm_w[None, :], wr, br)


def _plan_kernel(cnt_ref, info_ref, pos_ref, te_ref, nv_ref):
    nt = te_ref.shape[0]
    e1 = info_ref[_I_E1:_I_E1 + 1, :]
    e2 = info_ref[_I_E2:_I_E2 + 1, :]
    s1 = jnp.zeros_like(e1)
    s2 = jnp.zeros_like(e2)
    start = jnp.int32(0)
    for e in range(MOE_EXPERTS):
        tiles = lax.div(cnt_ref[e] + (ROW_TILE - 1), jnp.int32(ROW_TILE))
        seg = (start * ROW_TILE).astype(F32)
        s1 = jnp.where(e1 == float(e), seg, s1)
        s2 = jnp.where(e2 == float(e), seg, s2)

        def fill(i, carry, e=e):
            te_ref[i] = jnp.int32(e)
            return carry

        lax.fori_loop(start, start + tiles, fill, 0)
        start = start + tiles
    nv_ref[0] = start
    last = te_ref[jnp.maximum(start - 1, 0)]

    def fill_tail(i, carry):
        te_ref[i] = last
        return carry

    lax.fori_loop(start, nt, fill_tail, 0)
    pos_ref[0:1, :] = (s1 + info_ref[_I_R1:_I_R1 + 1, :]).astype(jnp.int32)
    pos_ref[1:2, :] = (s2 + info_ref[_I_R2:_I_R2 + 1, :]).astype(jnp.int32)


def _plan(cnt, info_t, nt):
    T = info_t.shape[1]
    return pl.pallas_call(
        _plan_kernel,
        in_specs=[
            pl.BlockSpec(memory_space=pltpu.SMEM),
            pl.BlockSpec(memory_space=pltpu.VMEM),
        ],
        out_specs=[
            pl.BlockSpec(memory_space=pltpu.VMEM),
            pl.BlockSpec(memory_space=pltpu.SMEM),
            pl.BlockSpec(memory_space=pltpu.SMEM),
        ],
        out_shape=[
            jax.ShapeDtypeStruct((2, T), jnp.int32),
            jax.ShapeDtypeStruct((nt,), jnp.int32),
            jax.ShapeDtypeStruct((1,), jnp.int32),
        ],
        name="moe_plan",
    )(cnt, info_t)


def _sc_worker():
    return lax.axis_index("s") * SC_CORES + lax.axis_index("c")


def _dispatch(hnp, pos1, pos2, n_rows):
    T, W = hnp.shape
    per_w = T // SC_WORKERS
    n_chunks = per_w // SC_ROWS
    mesh = plsc.VectorSubcoreMesh(core_axis_name="c", subcore_axis_name="s")

    @functools.partial(
        pl.kernel, mesh=mesh,
        out_type=jax.ShapeDtypeStruct((n_rows, W), jnp.uint32),
        scratch_types=[
            pltpu.VMEM((SC_ROWS,), jnp.int32),
            pltpu.VMEM((SC_ROWS,), jnp.int32),
            pltpu.VMEM((SC_ROWS, W), jnp.uint32),
        ],
        name="moe_dispatch_sc",
    )
    def scatter(hnp_hbm, p1_hbm, p2_hbm, xs_hbm, i1_v, i2_v, rows_v):
        base0 = _sc_worker() * per_w

        @pl.loop(0, n_chunks)
        def _(j):
            base = pl.multiple_of(base0 + j * SC_ROWS, SC_ROWS)
            pltpu.sync_copy(p1_hbm.at[pl.ds(base, SC_ROWS)], i1_v)
            pltpu.sync_copy(p2_hbm.at[pl.ds(base, SC_ROWS)], i2_v)
            pltpu.sync_copy(hnp_hbm.at[pl.ds(base, SC_ROWS)], rows_v)
            pltpu.sync_copy(rows_v, xs_hbm.at[i1_v])
            pltpu.sync_copy(rows_v, xs_hbm.at[i2_v])

    return scatter(hnp, pos1, pos2)


def _gather_rows(table, idx):
    T = idx.shape[0]
    W = table.shape[1]
    per_w = T // SC_WORKERS
    rows = SC_ROWS // 2
    n_chunks = per_w // rows
    mesh = plsc.VectorSubcoreMesh(core_axis_name="c", subcore_axis_name="s")

    @functools.partial(
        pl.kernel, mesh=mesh,
        out_type=jax.ShapeDtypeStruct((T, W), table.dtype),
        scratch_types=[
            pltpu.VMEM((rows,), jnp.int32),
            pltpu.VMEM((rows, W), table.dtype),
        ],
        name="moe_gather_sc",
    )
    def gather(table_hbm, idx_hbm, out_hbm, i_v, rows_v):
        base0 = _sc_worker() * per_w

        @pl.loop(0, n_chunks)
        def _(j):
            base = pl.multiple_of(base0 + j * rows, rows)
            pltpu.sync_copy(idx_hbm.at[pl.ds(base, rows)], i_v)
            pltpu.sync_copy(table_hbm.at[i_v], rows_v)
            pltpu.sync_copy(rows_v, out_hbm.at[pl.ds(base, rows)])

    return gather(table, idx)


def _experts_kernel(te_ref, nv_ref, xs_ref, w1_ref, w3_ref, w2_ref, ys_ref, w1b, w3b, w2b):
    i = pl.program_id(0)

    @pl.when(i < nv_ref[0])
    def _():
        @pl.when((i == 0) | (te_ref[i] != te_ref[jnp.maximum(i - 1, 0)]))
        def _():
            w1b[...] = w1_ref[0].astype(BF16)
            w3b[...] = w3_ref[0].astype(BF16)
            w2b[...] = w2_ref[0].astype(BF16)

        lo, hi = _unpack_bf16_pairs(xs_ref[...])
        half = D_MODEL // 2
        d = functools.partial(jnp.dot, preferred_element_type=F32)
        h1 = d(lo, w1b[0:half, :]) + d(hi, w1b[half:, :])
        h3 = d(lo, w3b[0:half, :]) + d(hi, w3b[half:, :])
        hid = (h1 * _sigmoid(h1)) * h3
        ys_ref[...] = d(hid.astype(BF16), w2b[...])

    @pl.when(i >= nv_ref[0])
    def _():
        ys_ref[...] = jnp.zeros_like(ys_ref)


def _experts(xs, tile_expert, n_valid, w1, w3, w2, layer):
    n_rows, W = xs.shape
    nt = n_rows // ROW_TILE
    D, F = w1.shape[1], w1.shape[2]

    def row_map(i, te, nv):
        return (i, 0)

    def w_map(i, te, nv):
        return (te[i] + layer * MOE_EXPERTS, 0, 0)

    return pl.pallas_call(
        _experts_kernel,
        grid_spec=pltpu.PrefetchScalarGridSpec(
            num_scalar_prefetch=2,
            grid=(nt,),
            in_specs=[
                pl.BlockSpec((ROW_TILE, W), row_map),
                pl.BlockSpec((1, D, F), w_map),
                pl.BlockSpec((1, D, F), w_map),
                pl.BlockSpec((1, F, D), w_map),
            ],
            out_specs=pl.BlockSpec((ROW_TILE, D), row_map),
            scratch_shapes=[
                pltpu.VMEM((D, F), BF16),
                pltpu.VMEM((D, F), BF16),
                pltpu.VMEM((F, D), BF16),
            ],
        ),
        out_shape=jax.ShapeDtypeStruct((n_rows, D), F32),
        compiler_params=pltpu.CompilerParams(
            dimension_semantics=("arbitrary",), vmem_limit_bytes=VMEM_LIMIT),
        name="moe_experts",
    )(tile_expert, n_valid, xs, w1, w3, w2)


def _combine_kernel(x_ref, info_ref, fw_ref, y1_ref, y2_ref, out_ref, *, final):
    info = info_ref[...]
    g1 = info[:, _I_G1:_I_G1 + 1]
    g2 = info[:, _I_G2:_I_G2 + 1]
    y = x_ref[...] + (g1 * y1_ref[...] + g2 * y2_ref[...])
    if final:
        y = _rms(y, fw_ref[...])
    out_ref[...] = y


def _combine(x2, info, y1, y2, final_w, final):
    T, D = x2.shape
    tm = min(TOK_TILE, T)
    row_spec = pl.BlockSpec((tm, D), lambda i: (i, 0))
    return pl.pallas_call(
        functools.partial(_combine_kernel, final=final),
        grid=(T // tm,),
        in_specs=[row_spec, pl.BlockSpec((tm, LANES), lambda i: (i, 0)), _const_spec((1, D)),
                  row_spec, row_spec],
        out_specs=row_spec,
        out_shape=jax.ShapeDtypeStruct((T, D), F32),
        compiler_params=pltpu.CompilerParams(
            dimension_semantics=("arbitrary",), vmem_limit_bytes=VMEM_LIMIT),
        name="moe_combine",
    )(x2, info, final_w[None, :], y1, y2)


def _hier_moe(x2, norm_w, w_group, b_group, w_expert, b_expert, w1, w3, w2, layer, final_w, final):
    T, D = x2.shape
    hnp, info, info_t, cnt = _router(x2, norm_w, w_group, b_group, w_expert, b_expert)
    nt = (2 * T) // ROW_TILE + MOE_EXPERTS
    pos, tile_expert, n_valid = _plan(cnt[0].astype(jnp.int32), info_t, nt)
    pos1, pos2 = pos[0], pos[1]
    xs = _dispatch(hnp, pos1, pos2, nt * ROW_TILE)
    ys = _experts(xs, tile_expert, n_valid, w1, w3, w2, layer)
    return _combine(x2, info, _gather_rows(ys, pos1), _gather_rows(ys, pos2), final_w, final)


def kernel(x, norm_mix, norm_ffn, norm_final, even_w_in, hgrn_lb_logits, hgrn_norm_w, gla_w_up,
           gla_b_up, gla_norm_w, even_w_out, odd_w_in, ret_norm_w, conv_w, conv_b, rglru_w_r,
           rglru_b_r, rglru_w_i, rglru_b_i, rglru_lambda, odd_w_out, moe_w_group, moe_b_group,
           moe_w_expert, moe_b_expert, moe_w1, moe_w3, moe_w2):
    B, L, D = x.shape
    depth = norm_mix.shape[0]
    lb_table = jnp.cumsum(jax.nn.softmax(hgrn_lb_logits.astype(F32), axis=0), axis=0)
    w1 = moe_w1.reshape((-1,) + moe_w1.shape[2:])
    w3 = moe_w3.reshape((-1,) + moe_w3.shape[2:])
    w2 = moe_w2.reshape((-1,) + moe_w2.shape[2:])
    for l in range(depth):
        j = l // 2
        if l % 2 == 0:
            x = _even_mixer(x, norm_mix[l], even_w_in[j], lb_table[l], gla_w_up[j], gla_b_up[j],
                            hgrn_norm_w[j], gla_norm_w[j], even_w_out[j])
        else:
            x = _odd_mixer(x, norm_mix[l], odd_w_in[j], ret_norm_w[j], conv_w[j], conv_b[j],
                           rglru_w_r[j], rglru_b_r[j], rglru_w_i[j], rglru_b_i[j], rglru_lambda[j],
                           odd_w_out[j])
        x = _hier_moe(x.reshape(B * L, D), norm_ffn[l], moe_w_group[l], moe_b_group[l],
                      moe_w_expert[l], moe_b_expert[l], w1, w3, w2, l,
                      norm_final, l == depth - 1).reshape(B, L, D)
    return x
```

```python
import functools

import numpy as np
import jax
import jax.numpy as jnp
from jax import lax
from jax.experimental import pallas as pl
from jax.experimental.pallas import tpu as pltpu
from jax.experimental.pallas import tpu_sc as plsc

F32 = jnp.float32
BF16 = jnp.bfloat16

EPS = 1e-6
CHUNK = 64
GROUP = 8
LOG2E = 1.4426950408889634
LANES = 128
D_MODEL = 1024
MIX_WIDTH = 1024
N_HEADS = 8
GLA_RANK = 16
GLA_TAU = 16.0
RET_HEADS = 4
RET_DK = 64
ROPE_BASE = 10000.0
LRU_WIDTH = 512
LRU_BLOCKS = 8
LRU_CONV = 4
LRU_C = 8.0
MOE_GROUPS = 4
MOE_PER_GROUP = 8
MOE_EXPERTS = MOE_GROUPS * MOE_PER_GROUP
MOE_DFF = 512
NEG_BIG = -1e30

MIX_TILE = 512
TOK_TILE = 512
ROW_TILE = 512
SC_CORES = 2
SC_WORKERS = 32
SC_ROWS = 128
VMEM_LIMIT = 56 * 1024 * 1024


def _dot(a, b):
    return jnp.dot(a.astype(BF16), b.astype(BF16), preferred_element_type=F32)


def _dot_nt(a, b):
    return lax.dot_general(a.astype(BF16), b.astype(BF16), (((1,), (1,)), ((), ())),
                           preferred_element_type=F32)


def _dot_tn(a, b):
    return lax.dot_general(a.astype(BF16), b.astype(BF16), (((0,), (0,)), ((), ())),
                           preferred_element_type=F32)


def _split2(x):
    hi = x.astype(BF16)
    lo = (x - hi.astype(F32)).astype(BF16)
    return hi, lo


def _split3(x):
    hi = x.astype(BF16)
    r = x - hi.astype(F32)
    mid = r.astype(BF16)
    lo = (r - mid.astype(F32)).astype(BF16)
    return hi, mid, lo


def _dot_x3(a, b):
    ah, al = _split2(a)
    bh, bl = _split2(b)
    d = functools.partial(jnp.dot, preferred_element_type=F32)
    return d(ah, bh) + (d(ah, bl) + d(al, bh))


def _sigmoid(x):
    return 1.0 / (1.0 + jnp.exp(-x))


def _softplus(x):
    return jnp.maximum(x, 0.0) + jnp.log(1.0 + jnp.exp(-jnp.abs(x)))


def _rms(x, w):
    return x * lax.rsqrt(jnp.mean(x * x, axis=-1, keepdims=True) + EPS) * w


def _tri(n):
    r = lax.broadcasted_iota(jnp.int32, (n, n), 0)
    c = lax.broadcasted_iota(jnp.int32, (n, n), 1)
    return jnp.where(r >= c, 1.0, 0.0).astype(BF16)


def _cumsum_rows(tri3, g):
    return jnp.dot(tri3, jnp.concatenate(_split3(g), axis=0), preferred_element_type=F32)


_E_AQ, _E_AF, _E_AI, _E_AG = 0, 512, 1024, 1536
_E_BQ, _E_BK, _E_BV, _E_BR, _E_LOW = 2048, 2304, 2560, 3072, 3584
EVEN_COLS = 3712
_QK_COLS = 768


_LEVEL_BLOCKS = (32, 16, 8)


def _level_tables():
    t = np.arange(CHUNK)
    out = []
    for blk in _LEVEL_BLOCKS:
        odd = (t // blk) % 2 == 1
        out.append((np.where(odd, (t // blk) * blk - 1, -1), np.where(~odd, (t // blk) * blk + blk - 1, -1)))
    return out


def _gla_chunk(c, q_ref, k_ref, v_ref, b2, o_ref, st_ref, a_ref):
    n_units = _QK_COLS // LANES
    rows = pl.ds(pl.multiple_of(c * CHUNK, CHUNK), CHUNK)
    lane = lax.broadcasted_iota(jnp.int32, (1, LANES), 1)
    trow = lax.broadcasted_iota(jnp.int32, (GROUP, LANES), 0)
    unit_heads = [((u, None),) for u in range(4)] + [((4, 0), (5, 64)), ((6, 0), (7, 64))]
    qs = [q_ref[rows, u * LANES:(u + 1) * LANES] for u in range(n_units)]
    ks = [k_ref[rows, u * LANES:(u + 1) * LANES] for u in range(n_units)]
    bs = [b2[:, u * LANES:(u + 1) * LANES] for u in range(n_units)]
    vs = [v_ref[rows, h * LANES:(h + 1) * LANES] for h in range(N_HEADS)]

    for u in range(n_units):
        for j in range(CHUNK // GROUP):
            qj = qs[u][j * GROUP:(j + 1) * GROUP]
            bj = bs[u][j * GROUP:(j + 1) * GROUP]
            for s in range(0, GROUP, 2):
                pair = [qj * jnp.exp2(jnp.where(trow >= s + i, bj - bj[s + i:s + i + 1, :], NEG_BIG))
                        for i in (0, 1)]
                r0 = (j * GROUP + s) * GROUP
                a_ref[r0:r0 + 2 * GROUP, u * LANES:(u + 1) * LANES] = (
                    jnp.concatenate(pair, axis=0).astype(BF16))

    zk = jnp.zeros((CHUNK, LANES), F32)
    scs = []
    for pr in range(n_units // 2):
        u0, u1 = 2 * pr, 2 * pr + 1
        if u0 < 4:
            keys = jnp.concatenate([jnp.concatenate([ks[u0], zk], axis=1),
                                    jnp.concatenate([zk, ks[u1]], axis=1)], axis=0)
        else:
            lo0, hi0 = jnp.where(lane < 64, ks[u0], 0.0), jnp.where(lane >= 64, ks[u0], 0.0)
            lo1, hi1 = jnp.where(lane < 64, ks[u1], 0.0), jnp.where(lane >= 64, ks[u1], 0.0)
            keys = jnp.concatenate([jnp.concatenate([lo0, zk], axis=1), jnp.concatenate([hi0, zk], axis=1),
                                    jnp.concatenate([zk, lo1], axis=1), jnp.concatenate([zk, hi1], axis=1)],
                                   axis=0)
        scs.append(_dot_nt(a_ref[:, u0 * LANES:(u1 + 1) * LANES], keys))

    tables = _level_tables()
    tq = lax.broadcasted_iota(jnp.int32, (CHUNK, CHUNK), 0)
    ts = lax.broadcasted_iota(jnp.int32, (CHUNK, CHUNK), 1)
    covers = [((tq // blk) % 2 == 1) & ((tq // blk) == (ts // blk) + 1) & ((ts // blk) % 2 == 0)
              for blk in _LEVEL_BLOCKS]
    s_offs, o_inter = {}, {}
    for u in range(n_units):
        q, k, b = qs[u], ks[u], bs[u]
        heads = unit_heads[u]
        if len(heads) == 2:
            qm = jnp.concatenate([jnp.where((lane & 64) == lo, q, 0.0) for _, lo in heads], axis=0)
            bm = jnp.concatenate([b, b], axis=0)
        else:
            qm, bm = q, b
        s_off = None
        for (qref, kref), cover in zip(tables, covers):
            qrows, krows = [], []
            for g in range(CHUNK // GROUP):
                sl = slice(g * GROUP, (g + 1) * GROUP)
                rq, rk = int(qref[g * GROUP]), int(kref[g * GROUP])
                qrows.append(jnp.exp2(b[sl] - b[rq:rq + 1, :]) if rq >= 0 else jnp.zeros((GROUP, LANES), F32))
                krows.append(k[sl] * jnp.exp2(b[rk:rk + 1, :] - b[sl]) if rk >= 0
                             else jnp.zeros((GROUP, LANES), F32))
            qfac = jnp.concatenate(qrows, axis=0)
            if len(heads) == 2:
                qfac = jnp.concatenate([qfac, qfac], axis=0)
                cov = jnp.concatenate([cover, cover], axis=0)
            else:
                cov = cover
            part = _dot_nt(qm * qfac, jnp.concatenate(krows, axis=0))
            s_off = jnp.where(cov, part, 0.0 if s_off is None else s_off)
        eb = jnp.exp2(bm)
        b_last = b[CHUNK - 1:CHUNK, :]
        kd = (k * jnp.exp2(b_last - b)).astype(BF16)
        dec = jnp.exp2(b_last)
        qe = qm * eb
        for i, (head, lo) in enumerate(heads):
            sl = slice(i * CHUNK, (i + 1) * CHUNK)
            st = st_ref[head]
            o_inter[head] = _dot_nt(qe[sl], st)
            st_ref[head] = st * dec + _dot_tn(vs[head], kd)
            s_offs[head] = s_off[sl]

    for u in range(n_units):
        sc = scs[u // 2]
        paired = len(unit_heads[u]) == 2
        width = LANES if paired else CHUNK
        c0 = (u % 2) * width
        scol = (lane & 63) if paired else lane[:, :CHUNK]
        groups = []
        for j in range(CHUNK // GROUP):
            acc = jnp.zeros((GROUP, width), F32)
            for s in range(GROUP):
                r0 = (j * GROUP + s) * GROUP
                acc = jnp.where(scol == j * GROUP + s, sc[r0:r0 + GROUP, c0:c0 + width], acc)
            groups.append(acc)
        p = jnp.concatenate(groups, axis=0)
        if paired:
            vstack = jnp.concatenate([vs[h] for h, _ in unit_heads[u]], axis=0)
        for head, lo in unit_heads[u]:
            if lo is None:
                o = o_inter[head] + _dot(s_offs[head] + p, vs[head])
            else:
                o = (o_inter[head] + _dot(s_offs[head], vs[head])
                     + _dot(jnp.where((lane & 64) == lo, p, 0.0), vstack))
            o_ref[rows, head * LANES:(head + 1) * LANES] = o


def _even_kernel(x_ref, nw_ref, win_ref, lb_ref, wup_ref, bup_ref, hw_ref, wout_ref, out_ref,
                 q_ref, k_ref, g_ref, v_ref, gate_ref, o_ref, st_ref, a_ref):
    tl = x_ref.shape[1]

    @pl.when(pl.program_id(1) == 0)
    def _():
        st_ref[...] = jnp.zeros_like(st_ref)

    x = x_ref[0]
    hn = _rms(x, nw_ref[...]).astype(BF16)

    def proj(c0, n):
        return jnp.dot(hn, win_ref[:, c0:c0 + n], preferred_element_type=F32)

    q_ref[:, 0:512] = proj(_E_AQ, 512)
    lb = lb_ref[...]
    f = lb + (1.0 - lb) * _sigmoid(proj(_E_AF, 512))
    k_ref[:, 0:512] = 1.0 - f
    g_ref[:, 0:512] = jnp.log(f)
    v_ref[:, 0:512] = proj(_E_AI, 512)
    ag = proj(_E_AG, 512)
    gate_ref[:, 0:512] = ag * _sigmoid(ag)
    q_ref[:, 512:768] = proj(_E_BQ, 256) * (64.0 ** -0.5)
    k_ref[:, 512:768] = proj(_E_BK, 256)
    v_ref[:, 512:1024] = proj(_E_BV, 512)
    br = proj(_E_BR, 512)
    gate_ref[:, 512:1024] = br * _sigmoid(br)
    z = _dot_x3(proj(_E_LOW, LANES), wup_ref[...]) + bup_ref[...]
    g_ref[:, 512:768] = -_softplus(-z) * (1.0 / GLA_TAU)

    tri = _tri(CHUNK)
    tri3 = jnp.concatenate([tri, tri, tri], axis=1)

    def chunk_body(c, carry):
        rows = pl.ds(pl.multiple_of(c * CHUNK, CHUNK), CHUNK)
        b2 = _cumsum_rows(tri3, g_ref[rows, :]) * LOG2E
        _gla_chunk(c, q_ref, k_ref, v_ref, b2, o_ref, st_ref, a_ref)
        return carry

    lax.fori_loop(0, tl // CHUNK, chunk_body, 0)

    hw = hw_ref[...]
    ys = []
    for head in range(N_HEADS):
        sl = slice(head * LANES, (head + 1) * LANES)
        oh = o_ref[:, sl]
        ys.append(oh * lax.rsqrt(jnp.mean(oh * oh, axis=-1, keepdims=True) + EPS)
                  * hw[:, sl] * gate_ref[:, sl])
    y = jnp.concatenate(ys, axis=1).astype(BF16)
    out_ref[0] = x + jnp.dot(y, wout_ref[...], preferred_element_type=F32)


def _const_spec(shape):
    nd = len(shape)
    return pl.BlockSpec(shape, lambda *_: (0,) * nd)


def _even_mixer(x, norm_w, w_in, lb, w_up, b_up, hgrn_norm_w, gla_norm_w, w_out):
    B, L, D = x.shape
    tl = min(MIX_TILE, L)
    s = np.cumsum([0, 512, 512, 512, 512, 256, 256, 512, GLA_RANK, 512])
    cols = [w_in[:, s[i]:s[i + 1]] for i in range(9)]
    low = jnp.pad(cols[7], ((0, 0), (0, LANES - GLA_RANK)))
    w_in_r = jnp.concatenate(cols[:7] + [cols[8], low], axis=1).astype(BF16)
    w_up_p = jnp.pad(w_up, ((0, LANES - GLA_RANK), (0, 0)))
    hw = jnp.concatenate([hgrn_norm_w, gla_norm_w])[None, :]
    return pl.pallas_call(
        _even_kernel,
        grid=(B, L // tl),
        in_specs=[
            pl.BlockSpec((1, tl, D), lambda b, l: (b, l, 0)),
            _const_spec((1, D)),
            _const_spec((D, EVEN_COLS)),
            _const_spec((1, 512)),
            _const_spec((LANES, 256)),
            _const_spec((1, 256)),
            _const_spec((1, MIX_WIDTH)),
            _const_spec((MIX_WIDTH, D)),
        ],
        out_specs=pl.BlockSpec((1, tl, D), lambda b, l: (b, l, 0)),
        out_shape=jax.ShapeDtypeStruct((B, L, D), F32),
        scratch_shapes=[
            pltpu.VMEM((tl, _QK_COLS), F32),
            pltpu.VMEM((tl, _QK_COLS), F32),
            pltpu.VMEM((tl, _QK_COLS), F32),
            pltpu.VMEM((tl, MIX_WIDTH), F32),
            pltpu.VMEM((tl, MIX_WIDTH), F32),
            pltpu.VMEM((tl, MIX_WIDTH), F32),
            pltpu.VMEM((N_HEADS, LANES, LANES), F32),
            pltpu.VMEM((CHUNK * GROUP, _QK_COLS), BF16),
        ],
        compiler_params=pltpu.CompilerParams(
            dimension_semantics=("arbitrary", "arbitrary"), vmem_limit_bytes=VMEM_LIMIT),
        name="even_mixer",
    )(x, norm_w[None, :], w_in_r, lb[None, :], w_up_p, b_up[None, :], hw, w_out.astype(BF16))


_O_Q, _O_QR, _O_K, _O_KR, _O_V, _O_G, _O_X, _O_GATE = 0, 256, 512, 768, 1024, 1536, 2048, 2560
ODD_COLS = 3072


def _odd_kernel(x_ref, nw_ref, win_ref, cos_ref, sin_ref, dmat_ref, qdec_ref, kdec_ref, sdec_ref,
                rw_ref, cw_ref, cb_ref, wri_ref, bri_ref, lam_ref, wout_ref, out_ref,
                q_ref, k_ref, v_ref, o_ref, xe_ref, a_ref, u_ref, st_ref, h_ref):
    tl = x_ref.shape[1]

    @pl.when(pl.program_id(1) == 0)
    def _():
        st_ref[...] = jnp.zeros_like(st_ref)
        h_ref[...] = jnp.zeros_like(h_ref)
        xe_ref[0:8, :] = jnp.zeros((8, LRU_WIDTH), F32)

    x = x_ref[0]
    hn = _rms(x, nw_ref[...]).astype(BF16)

    def proj(c0, n):
        return jnp.dot(hn, win_ref[:, c0:c0 + n], preferred_element_type=F32)

    cos = cos_ref[...]
    sin = sin_ref[...]
    q_ref[...] = proj(_O_Q, 256) * cos + proj(_O_QR, 256) * sin
    k_ref[...] = (proj(_O_K, 256) * cos + proj(_O_KR, 256) * sin) * (RET_DK ** -0.5)
    v_ref[...] = proj(_O_V, 512)

    xe_ref[8:8 + tl, :] = proj(_O_X, LRU_WIDTH)
    cw = cw_ref[...]
    xc = cb_ref[...]
    for j in range(LRU_CONV):
        xc = xc + xe_ref[5 + j:5 + j + tl, :] * cw[j:j + 1, :]
    xe_ref[0:8, :] = xe_ref[tl:tl + 8, :]
    ri = jnp.dot(xc.astype(BF16), wri_ref[...], preferred_element_type=F32) + bri_ref[...]
    r = _sigmoid(ri[:, :LRU_WIDTH])
    i = _sigmoid(ri[:, LRU_WIDTH:])
    a = jnp.exp(-LRU_C * r * _softplus(-lam_ref[...]))
    a_ref[...] = a
    u_ref[...] = jnp.sqrt(1.0 - a * a) * (i * xc)

    lane = lax.broadcasted_iota(jnp.int32, (1, LANES), 1)
    crow = lax.broadcasted_iota(jnp.int32, (CHUNK, LRU_WIDTH), 0)

    def chunk_body(c, carry):
        rows = pl.ds(pl.multiple_of(c * CHUNK, CHUNK), CHUNK)
        for head in range(RET_HEADS):
            unit, half = head // 2, head % 2
            ul = slice(unit * LANES, (unit + 1) * LANES)
            vl = slice(head * LANES, (head + 1) * LANES)
            own = (lane < 64) if half == 0 else (lane >= 64)
            q = jnp.where(own, q_ref[rows, ul], 0.0)
            k = k_ref[rows, ul]
            v = v_ref[rows, vl]
            st = st_ref[head]
            o = _dot_nt(q, st) * qdec_ref[head]
            o = o + _dot(_dot_nt(q, k) * dmat_ref[head], v)
            st_ref[head] = st * sdec_ref[head] + _dot_tn(v, k * kdec_ref[head])
            o_ref[rows, vl] = o
        ca = a_ref[rows, :]
        ch = u_ref[rows, :]
        d = 1
        while d < CHUNK:
            keep = crow >= d
            sa = jnp.where(keep, pltpu.roll(ca, d, axis=0), 1.0)
            sh = jnp.where(keep, pltpu.roll(ch, d, axis=0), 0.0)
            ch = ca * sh + ch
            ca = ca * sa
            d *= 2
        ch = ch + ca * h_ref[...]
        h_ref[...] = ch[CHUNK - 1:CHUNK, :]
        o_ref[rows, 512:1024] = ch
        return carry

    lax.fori_loop(0, tl // CHUNK, chunk_body, 0)

    rw = rw_ref[...]
    ys = []
    for head in range(RET_HEADS):
        sl = slice(head * LANES, (head + 1) * LANES)
        oh = o_ref[:, sl]
        oh = oh - jnp.mean(oh, axis=-1, keepdims=True)
        ys.append(oh * lax.rsqrt(jnp.mean(oh * oh, axis=-1, keepdims=True) + EPS) * rw[:, sl])
    cg = proj(_O_G, 512)
    y_c = jnp.concatenate(ys, axis=1) * (cg * _sigmoid(cg))
    dg = proj(_O_GATE, LRU_WIDTH)
    gelu = 0.5 * dg * (1.0 + jnp.tanh(np.sqrt(2.0 / np.pi) * (dg + 0.044715 * (dg * dg * dg))))
    y_d = o_ref[:, 512:1024] * gelu
    y = jnp.concatenate([y_c, y_d], axis=1).astype(BF16)
    out_ref[0] = x + jnp.dot(y, wout_ref[...], preferred_element_type=F32)


def _rot_cols(w):
    d = w.shape[0]
    w4 = w.reshape(d, RET_HEADS, 2, RET_DK // 2)
    return jnp.stack([-w4[:, :, 1], w4[:, :, 0]], axis=2).reshape(d, RET_HEADS * RET_DK)


def _odd_mixer(x, norm_w, w_in, ret_norm_w, conv_w, conv_b, w_r, b_r, w_i, b_i, lam, w_out):
    B, L, D = x.shape
    tl = min(MIX_TILE, L)
    s = np.cumsum([0, 256, 256, 512, 512, 512, 512])
    wq, wk, wv, wg, wx, wgate = [w_in[:, s[i]:s[i + 1]] for i in range(6)]
    w_in_r = jnp.concatenate([wq, _rot_cols(wq), wk, _rot_cols(wk), wv, wg, wx, wgate],
                             axis=1).astype(BF16)
    half = RET_DK // 2
    inv = ROPE_BASE ** (-jnp.arange(half, dtype=F32) / half)
    ang = jnp.arange(L, dtype=F32)[:, None] * inv[None, :]
    cos = jnp.tile(jnp.cos(ang), (1, 2 * RET_HEADS))
    sin = jnp.tile(jnp.sin(ang), (1, 2 * RET_HEADS))
    lg = jnp.log1p(-jnp.exp2(-5.0 - jnp.arange(RET_HEADS, dtype=F32)))[:, None, None]
    t = jnp.arange(CHUNK, dtype=F32)
    rel = t[:, None] - t[None, :]
    dmat = jnp.where(rel >= 0, jnp.exp(lg * jnp.maximum(rel, 0.0)), 0.0)
    qdec = jnp.broadcast_to(jnp.exp(lg * (t[None, :, None] + 1.0)), (RET_HEADS, CHUNK, LANES))
    kdec = jnp.broadcast_to(jnp.exp(lg * (CHUNK - 1.0 - t[None, :, None])), (RET_HEADS, CHUNK, LANES))
    sdec = jnp.broadcast_to(jnp.exp(lg * float(CHUNK)), (RET_HEADS, 1, LANES))
    eye = jnp.eye(LRU_BLOCKS, dtype=F32)
    bd = LRU_WIDTH // LRU_BLOCKS

    def blockdiag(w):
        return (eye[:, None, :, None] * w[:, :, None, :]).reshape(LRU_WIDTH, LRU_WIDTH)

    wri = jnp.concatenate([blockdiag(w_r), blockdiag(w_i)], axis=1).astype(BF16)
    bri = jnp.concatenate([b_r, b_i])[None, :]
    return pl.pallas_call(
        _odd_kernel,
        grid=(B, L // tl),
        in_specs=[
            pl.BlockSpec((1, tl, D), lambda b, l: (b, l, 0)),
            _const_spec((1, D)),
            _const_spec((D, ODD_COLS)),
            pl.BlockSpec((tl, 256), lambda b, l: (l, 0)),
            pl.BlockSpec((tl, 256), lambda b, l: (l, 0)),
            _const_spec((RET_HEADS, CHUNK, CHUNK)),
            _const_spec((RET_HEADS, CHUNK, LANES)),
            _const_spec((RET_HEADS, CHUNK, LANES)),
            _const_spec((RET_HEADS, 1, LANES)),
            _const_spec((1, 512)),
            _const_spec((LRU_CONV, LRU_WIDTH)),
            _const_spec((1, LRU_WIDTH)),
            _const_spec((LRU_WIDTH, 2 * LRU_WIDTH)),
            _const_spec((1, 2 * LRU_WIDTH)),
            _const_spec((1, LRU_WIDTH)),
            _const_spec((MIX_WIDTH, D)),
        ],
        out_specs=pl.BlockSpec((1, tl, D), lambda b, l: (b, l, 0)),
        out_shape=jax.ShapeDtypeStruct((B, L, D), F32),
        scratch_shapes=[
            pltpu.VMEM((tl, 256), F32),
            pltpu.VMEM((tl, 256), F32),
            pltpu.VMEM((tl, 512), F32),
            pltpu.VMEM((tl, MIX_WIDTH), F32),
            pltpu.VMEM((tl + 8, LRU_WIDTH), F32),
            pltpu.VMEM((tl, LRU_WIDTH), F32),
            pltpu.VMEM((tl, LRU_WIDTH), F32),
            pltpu.VMEM((RET_HEADS, LANES, LANES), F32),
            pltpu.VMEM((1, LRU_WIDTH), F32),
        ],
        compiler_params=pltpu.CompilerParams(
            dimension_semantics=("arbitrary", "arbitrary"), vmem_limit_bytes=VMEM_LIMIT),
        name="odd_mixer",
    )(x, norm_w[None, :], w_in_r, cos, sin, dmat, qdec, kdec, sdec, ret_norm_w[None, :],
      conv_w, conv_b[None, :], wri, bri, lam[None, :], w_out.astype(BF16))


_I_E1, _I_E2, _I_G1, _I_G2, _I_R1, _I_R2 = 0, 1, 2, 3, 4, 5
_LOGIT_E0 = MOE_GROUPS


def _pack_bf16_pairs(lo, hi):
    lo_b = pltpu.bitcast(lo.astype(BF16).astype(F32), jnp.uint32)
    hi_b = pltpu.bitcast(hi.astype(BF16).astype(F32), jnp.uint32)
    return (lo_b >> 16) | (hi_b & jnp.uint32(0xFFFF0000))


def _unpack_bf16_pairs(w):
    lo = pltpu.bitcast(w << 16, F32).astype(BF16)
    hi = pltpu.bitcast(w & jnp.uint32(0xFFFF0000), F32).astype(BF16)
    return lo, hi


def _router_kernel(x_ref, nw_ref, wr_ref, br_ref, hnp_ref, info_ref, info_t_ref, cnt_ref):
    tm = x_ref.shape[0]

    @pl.when(pl.program_id(0) == 0)
    def _():
        cnt_ref[...] = jnp.zeros_like(cnt_ref)

    hn = _rms(x_ref[...], nw_ref[...])
    half = D_MODEL // 2
    hnp_ref[...] = _pack_bf16_pairs(hn[:, :half], hn[:, half:])

    logits = _dot_x3(hn, wr_ref[...]) + br_ref[...]
    lane = lax.broadcasted_iota(jnp.int32, (tm, LANES), 1).astype(F32)

    def first_max(vals):
        m = jnp.max(vals, axis=-1, keepdims=True)
        idx = jnp.min(jnp.where(vals == m, lane, float(LANES)), axis=-1, keepdims=True)
        return m, idx

    gl = jnp.where(lane < MOE_GROUPS, logits, NEG_BIG)
    gmax, gidx = first_max(gl)
    g_w = 1.0 / jnp.sum(jnp.exp(gl - gmax), axis=-1, keepdims=True)
    e0 = _LOGIT_E0 + MOE_PER_GROUP * gidx
    el = jnp.where((lane >= e0) & (lane < e0 + MOE_PER_GROUP), logits, NEG_BIG)
    m1, i1 = first_max(el)
    m2, i2 = first_max(jnp.where(lane == i1, NEG_BIG, el))
    p2 = jnp.exp(m2 - m1)
    w1 = 1.0 / (1.0 + p2)
    e1 = i1 - _LOGIT_E0
    e2 = i2 - _LOGIT_E0

    oh1 = jnp.where(lane == e1, 1.0, 0.0)
    oh2 = jnp.where(lane == e2, 1.0, 0.0)
    tri = _tri(tm)
    pre1 = jnp.dot(tri, oh1.astype(BF16), preferred_element_type=F32)
    pre2 = jnp.dot(tri, oh2.astype(BF16), preferred_element_type=F32)
    carry = cnt_ref[...]
    c1 = pre1[tm - 1:tm, :]
    r1 = jnp.sum(oh1 * (pre1 + carry), axis=-1, keepdims=True) - 1.0
    r2 = jnp.sum(oh2 * (pre2 + (carry + c1)), axis=-1, keepdims=True) - 1.0
    cnt_ref[...] = carry + c1 + pre2[tm - 1:tm, :]

    info = jnp.zeros((tm, LANES), F32)
    for ln, val in ((_I_E1, e1), (_I_E2, e2), (_I_G1, g_w * w1), (_I_G2, g_w * (w1 * p2)),
                    (_I_R1, r1), (_I_R2, r2)):
        info = jnp.where(lane == float(ln), val, info)
    info_ref[...] = info
    info_t_ref[...] = info.T[0:GROUP, :]


def _router(x2, norm_w, w_group, b_group, w_expert, b_expert):
    T, D = x2.shape
    tm = min(TOK_TILE, T)
    pad = LANES - MOE_GROUPS - MOE_EXPERTS
    wr = jnp.pad(jnp.concatenate([w_group, w_expert], axis=1), ((0, 0), (0, pad)))
    br = jnp.pad(jnp.concatenate([b_group, b_expert]), (0, pad))[None, :]
    return pl.pallas_call(
        _router_kernel,
        grid=(T // tm,),
        in_specs=[
            pl.BlockSpec((tm, D), lambda i: (i, 0)),
            _const_spec((1, D)),
            _const_spec((D, LANES)),
            _const_spec((1, LANES)),
        ],
        out_specs=[
            pl.BlockSpec((tm, D // 2), lambda i: (i, 0)),
            pl.BlockSpec((tm, LANES), lambda i: (i, 0)),
            pl.BlockSpec((GROUP, tm), lambda i: (0, i)),
            _const_spec((1, LANES)),
        ],
        out_shape=[
            jax.ShapeDtypeStruct((T, D // 2), jnp.uint32),
            jax.ShapeDtypeStruct((T, LANES), F32),
            jax.ShapeDtypeStruct((GROUP, T), F32),
            jax.ShapeDtypeStruct((1, LANES), F32),
        ],
        compiler_params=pltpu.CompilerParams(
            dimension_semantics=("arbitrary",), vmem_limit_bytes=VMEM_LIMIT),
        name="moe_router",
    )(x2, norm_w[None, :], wr, br)


def _plan_kernel(cnt_ref, info_ref, pos_ref, te_ref, nv_ref):
    nt = te_ref.shape[0]
    e1 = info_ref[_I_E1:_I_E1 + 1, :]
    e2 = info_ref[_I_E2:_I_E2 + 1, :]
    s1 = jnp.zeros_like(e1)
    s2 = jnp.zeros_like(e2)
    start = jnp.int32(0)
    for e in range(MOE_EXPERTS):
        tiles = lax.div(cnt_ref[e] + (ROW_TILE - 1), jnp.int32(ROW_TILE))
        seg = (start * ROW_TILE).astype(F32)
        s1 = jnp.where(e1 == float(e), seg, s1)
        s2 = jnp.where(e2 == float(e), seg, s2)

        def fill(i, carry, e=e):
            te_ref[i] = jnp.int32(e)
            return carry

        lax.fori_loop(start, start + tiles, fill, 0)
        start = start + tiles
    nv_ref[0] = start
    last = te_ref[jnp.maximum(start - 1, 0)]

    def fill_tail(i, carry):
        te_ref[i] = last
        return carry

    lax.fori_loop(start, nt, fill_tail, 0)
    pos_ref[0:1, :] = (s1 + info_ref[_I_R1:_I_R1 + 1, :]).astype(jnp.int32)
    pos_ref[1:2, :] = (s2 + info_ref[_I_R2:_I_R2 + 1, :]).astype(jnp.int32)


def _plan(cnt, info_t, nt):
    T = info_t.shape[1]
    return pl.pallas_call(
        _plan_kernel,
        in_specs=[
            pl.BlockSpec(memory_space=pltpu.SMEM),
            pl.BlockSpec(memory_space=pltpu.VMEM),
        ],
        out_specs=[
            pl.BlockSpec(memory_space=pltpu.VMEM),
            pl.BlockSpec(memory_space=pltpu.SMEM),
            pl.BlockSpec(memory_space=pltpu.SMEM),
        ],
        out_shape=[
            jax.ShapeDtypeStruct((2, T), jnp.int32),
            jax.ShapeDtypeStruct((nt,), jnp.int32),
            jax.ShapeDtypeStruct((1,), jnp.int32),
        ],
        name="moe_plan",
    )(cnt, info_t)


def _sc_worker():
    return lax.axis_index("s") * SC_CORES + lax.axis_index("c")


def _dispatch(hnp, pos1, pos2, n_rows):
    T, W = hnp.shape
    per_w = T // SC_WORKERS
    n_chunks = per_w // SC_ROWS
    mesh = plsc.VectorSubcoreMesh(core_axis_name="c", subcore_axis_name="s")

    @functools.partial(
        pl.kernel, mesh=mesh,
        out_type=jax.ShapeDtypeStruct((n_rows, W), jnp.uint32),
        scratch_types=[
            pltpu.VMEM((SC_ROWS,), jnp.int32),
            pltpu.VMEM((SC_ROWS,), jnp.int32),
            pltpu.VMEM((SC_ROWS, W), jnp.uint32),
        ],
        name="moe_dispatch_sc",
    )
    def scatter(hnp_hbm, p1_hbm, p2_hbm, xs_hbm, i1_v, i2_v, rows_v):
        base0 = _sc_worker() * per_w

        @pl.loop(0, n_chunks)
        def _(j):
            base = pl.multiple_of(base0 + j * SC_ROWS, SC_ROWS)
            pltpu.sync_copy(p1_hbm.at[pl.ds(base, SC_ROWS)], i1_v)
            pltpu.sync_copy(p2_hbm.at[pl.ds(base, SC_ROWS)], i2_v)
            pltpu.sync_copy(hnp_hbm.at[pl.ds(base, SC_ROWS)], rows_v)
            pltpu.sync_copy(rows_v, xs_hbm.at[i1_v])
            pltpu.sync_copy(rows_v, xs_hbm.at[i2_v])

    return scatter(hnp, pos1, pos2)


def _gather_rows(table, idx):
    T = idx.shape[0]
    W = table.shape[1]
    per_w = T // SC_WORKERS
    rows = SC_ROWS // 2
    n_chunks = per_w // rows
    mesh = plsc.VectorSubcoreMesh(core_axis_name="c", subcore_axis_name="s")

    @functools.partial(
        pl.kernel, mesh=mesh,
        out_type=jax.ShapeDtypeStruct((T, W), table.dtype),
        scratch_types=[
            pltpu.VMEM((rows,), jnp.int32),
            pltpu.VMEM((rows, W), table.dtype),
        ],
        name="moe_gather_sc",
    )
    def gather(table_hbm, idx_hbm, out_hbm, i_v, rows_v):
        base0 = _sc_worker() * per_w

        @pl.loop(0, n_chunks)
        def _(j):
            base = pl.multiple_of(base0 + j * rows, rows)
            pltpu.sync_copy(idx_hbm.at[pl.ds(base, rows)], i_v)
            pltpu.sync_copy(table_hbm.at[i_v], rows_v)
            pltpu.sync_copy(rows_v, out_hbm.at[pl.ds(base, rows)])

    return gather(table, idx)


def _experts_kernel(te_ref, nv_ref, xs_ref, w1_ref, w3_ref, w2_ref, ys_ref, w1b, w3b, w2b):
    i = pl.program_id(0)

    @pl.when(i < nv_ref[0])
    def _():
        @pl.when((i == 0) | (te_ref[i] != te_ref[jnp.maximum(i - 1, 0)]))
        def _():
            w1b[...] = w1_ref[0].astype(BF16)
            w3b[...] = w3_ref[0].astype(BF16)
            w2b[...] = w2_ref[0].astype(BF16)

        lo, hi = _unpack_bf16_pairs(xs_ref[...])
        half = D_MODEL // 2
        d = functools.partial(jnp.dot, preferred_element_type=F32)
        h1 = d(lo, w1b[0:half, :]) + d(hi, w1b[half:, :])
        h3 = d(lo, w3b[0:half, :]) + d(hi, w3b[half:, :])
        hid = (h1 * _sigmoid(h1)) * h3
        ys_ref[...] = d(hid.astype(BF16), w2b[...])

    @pl.when(i >= nv_ref[0])
    def _():
        ys_ref[...] = jnp.zeros_like(ys_ref)


def _experts(xs, tile_expert, n_valid, w1, w3, w2, layer):
    n_rows, W = xs.shape
    nt = n_rows // ROW_TILE
    D, F = w1.shape[1], w1.shape[2]

    def row_map(i, te, nv):
        return (i, 0)

    def w_map(i, te, nv):
        return (te[i] + layer * MOE_EXPERTS, 0, 0)

    return pl.pallas_call(
        _experts_kernel,
        grid_spec=pltpu.PrefetchScalarGridSpec(
            num_scalar_prefetch=2,
            grid=(nt,),
            in_specs=[
                pl.BlockSpec((ROW_TILE, W), row_map),
                pl.BlockSpec((1, D, F), w_map),
                pl.BlockSpec((1, D, F), w_map),
                pl.BlockSpec((1, F, D), w_map),
            ],
            out_specs=pl.BlockSpec((ROW_TILE, D), row_map),
            scratch_shapes=[
                pltpu.VMEM((D, F), BF16),
                pltpu.VMEM((D, F), BF16),
                pltpu.VMEM((F, D), BF16),
            ],
        ),
        out_shape=jax.ShapeDtypeStruct((n_rows, D), F32),
        compiler_params=pltpu.CompilerParams(
            dimension_semantics=("arbitrary",), vmem_limit_bytes=VMEM_LIMIT),
        name="moe_experts",
    )(tile_expert, n_valid, xs, w1, w3, w2)


def _combine_kernel(x_ref, info_ref, fw_ref, y1_ref, y2_ref, out_ref, *, final):
    info = info_ref[...]
    g1 = info[:, _I_G1:_I_G1 + 1]
    g2 = info[:, _I_G2:_I_G2 + 1]
    y = x_ref[...] + (g1 * y1_ref[...] + g2 * y2_ref[...])
    if final:
        y = _rms(y, fw_ref[...])
    out_ref[...] = y


def _combine(x2, info, y1, y2, final_w, final):
    T, D = x2.shape
    tm = min(TOK_TILE, T)
    row_spec = pl.BlockSpec((tm, D), lambda i: (i, 0))
    return pl.pallas_call(
        functools.partial(_combine_kernel, final=final),
        grid=(T // tm,),
        in_specs=[row_spec, pl.BlockSpec((tm, LANES), lambda i: (i, 0)), _const_spec((1, D)),
                  row_spec, row_spec],
        out_specs=row_spec,
        out_shape=jax.ShapeDtypeStruct((T, D), F32),
        compiler_params=pltpu.CompilerParams(
            dimension_semantics=("arbitrary",), vmem_limit_bytes=VMEM_LIMIT),
        name="moe_combine",
    )(x2, info, final_w[None, :], y1, y2)


def _hier_moe(x2, norm_w, w_group, b_group, w_expert, b_expert, w1, w3, w2, layer, final_w, final):
    T, D = x2.shape
    hnp, info, info_t, cnt = _router(x2, norm_w, w_group, b_group, w_expert, b_expert)
    nt = (2 * T) // ROW_TILE + MOE_EXPERTS
    pos, tile_expert, n_valid = _plan(cnt[0].astype(jnp.int32), info_t, nt)
    pos1, pos2 = pos[0], pos[1]
    xs = _dispatch(hnp, pos1, pos2, nt * ROW_TILE)
    ys = _experts(xs, tile_expert, n_valid, w1, w3, w2, layer)
    return _combine(x2, info, _gather_rows(ys, pos1), _gather_rows(ys, pos2), final_w, final)


def kernel(x, norm_mix, norm_ffn, norm_final, even_w_in, hgrn_lb_logits, hgrn_norm_w, gla_w_up,
           gla_b_up, gla_norm_w, even_w_out, odd_w_in, ret_norm_w, conv_w, conv_b, rglru_w_r,
           rglru_b_r, rglru_w_i, rglru_b_i, rglru_lambda, odd_w_out, moe_w_group, moe_b_group,
           moe_w_expert, moe_b_expert, moe_w1, moe_w3, moe_w2):
    B, L, D = x.shape
    depth = norm_mix.shape[0]
    lb_table = jnp.cumsum(jax.nn.softmax(hgrn_lb_logits.astype(F32), axis=0), axis=0)
    w1 = moe_w1.reshape((-1,) + moe_w1.shape[2:])
    w3 = moe_w3.reshape((-1,) + moe_w3.shape[2:])
    w2 = moe_w2.reshape((-1,) + moe_w2.shape[2:])
    for l in range(depth):
        j = l // 2
        if l % 2 == 0:
            x = _even_mixer(x, norm_mix[l], even_w_in[j], lb_table[l], gla_w_up[j], gla_b_up[j],
                            hgrn_norm_w[j], gla_norm_w[j], even_w_out[j])
        else:
            x = _odd_mixer(x, norm_mix[l], odd_w_in[j], ret_norm_w[j], conv_w[j], conv_b[j],
                           rglru_w_r[j], rglru_b_r[j], rglru_w_i[j], rglru_b_i[j], rglru_lambda[j],
                           odd_w_out[j])
        x = _hier_moe(x.reshape(B * L, D), norm_ffn[l], moe_w_group[l], moe_b_group[l],
                      moe_w_expert[l], moe_b_expert[l], w1, w3, w2, l,
                      norm_final, l == depth - 1).reshape(B, L, D)
    return x
```

```python
import functools

import numpy as np
import jax
import jax.numpy as jnp
from jax import lax
from jax.experimental import pallas as pl
from jax.experimental.pallas import tpu as pltpu
from jax.experimental.pallas import tpu_sc as plsc

F32 = jnp.float32
BF16 = jnp.bfloat16

EPS = 1e-6
CHUNK = 64
GROUP = 8
LOG2E = 1.4426950408889634
LANES = 128
D_MODEL = 1024
MIX_WIDTH = 1024
N_HEADS = 8
GLA_RANK = 16
GLA_TAU = 16.0
RET_HEADS = 4
RET_DK = 64
ROPE_BASE = 10000.0
LRU_WIDTH = 512
LRU_BLOCKS = 8
LRU_CONV = 4
LRU_C = 8.0
MOE_GROUPS = 4
MOE_PER_GROUP = 8
MOE_EXPERTS = MOE_GROUPS * MOE_PER_GROUP
MOE_DFF = 512
NEG_BIG = -1e30

MIX_TILE = 512
TOK_TILE = 512
ROW_TILE = 512
SC_CORES = 2
SC_WORKERS = 32
SC_ROWS = 128
VMEM_LIMIT = 56 * 1024 * 1024


def _dot(a, b):
    return jnp.dot(a.astype(BF16), b.astype(BF16), preferred_element_type=F32)


def _dot_nt(a, b):
    return lax.dot_general(a.astype(BF16), b.astype(BF16), (((1,), (1,)), ((), ())),
                           preferred_element_type=F32)


def _dot_tn(a, b):
    return lax.dot_general(a.astype(BF16), b.astype(BF16), (((0,), (0,)), ((), ())),
                           preferred_element_type=F32)


def _split2(x):
    hi = x.astype(BF16)
    lo = (x - hi.astype(F32)).astype(BF16)
    return hi, lo


def _split3(x):
    hi = x.astype(BF16)
    r = x - hi.astype(F32)
    mid = r.astype(BF16)
    lo = (r - mid.astype(F32)).astype(BF16)
    return hi, mid, lo


def _dot_x3(a, b):
    ah, al = _split2(a)
    bh, bl = _split2(b)
    d = functools.partial(jnp.dot, preferred_element_type=F32)
    return d(ah, bh) + (d(ah, bl) + d(al, bh))


def _sigmoid(x):
    return 1.0 / (1.0 + jnp.exp(-x))


def _softplus(x):
    return jnp.maximum(x, 0.0) + jnp.log(1.0 + jnp.exp(-jnp.abs(x)))


def _rms(x, w):
    return x * lax.rsqrt(jnp.mean(x * x, axis=-1, keepdims=True) + EPS) * w


def _tri(n):
    r = lax.broadcasted_iota(jnp.int32, (n, n), 0)
    c = lax.broadcasted_iota(jnp.int32, (n, n), 1)
    return jnp.where(r >= c, 1.0, 0.0).astype(BF16)


def _cumsum_rows(tri3, g):
    return jnp.dot(tri3, jnp.concatenate(_split3(g), axis=0), preferred_element_type=F32)


_E_AQ, _E_AF, _E_AI, _E_AG = 0, 512, 1024, 1536
_E_BQ, _E_BK, _E_BV, _E_BR, _E_LOW = 2048, 2304, 2560, 3072, 3584
EVEN_COLS = 3712
_QK_COLS = 768


_LEVEL_BLOCKS = (32, 16, 8)


def _level_tables():
    t = np.arange(CHUNK)
    out = []
    for blk in _LEVEL_BLOCKS:
        odd = (t // blk) % 2 == 1
        out.append((np.where(odd, (t // blk) * blk - 1, -1), np.where(~odd, (t // blk) * blk + blk - 1, -1)))
    return out


def _gla_chunk(c, q_ref, k_ref, v_ref, b2, o_ref, st_ref, a_ref):
    n_units = _QK_COLS // LANES
    rows = pl.ds(pl.multiple_of(c * CHUNK, CHUNK), CHUNK)
    lane = lax.broadcasted_iota(jnp.int32, (1, LANES), 1)
    trow = lax.broadcasted_iota(jnp.int32, (GROUP, LANES), 0)
    unit_heads = [((u, None),) for u in range(4)] + [((4, 0), (5, 64)), ((6, 0), (7, 64))]
    qs = [q_ref[rows, u * LANES:(u + 1) * LANES] for u in range(n_units)]
    ks = [k_ref[rows, u * LANES:(u + 1) * LANES] for u in range(n_units)]
    bs = [b2[:, u * LANES:(u + 1) * LANES] for u in range(n_units)]
    vs = [v_ref[rows, h * LANES:(h + 1) * LANES] for h in range(N_HEADS)]

    for u in range(n_units):
        for j in range(CHUNK // GROUP):
            qj = qs[u][j * GROUP:(j + 1) * GROUP]
            bj = bs[u][j * GROUP:(j + 1) * GROUP]
            for s in range(0, GROUP, 2):
                pair = [qj * jnp.exp2(jnp.where(trow >= s + i, bj - bj[s + i:s + i + 1, :], NEG_BIG))
                        for i in (0, 1)]
                r0 = (j * GROUP + s) * GROUP
                a_ref[r0:r0 + 2 * GROUP, u * LANES:(u + 1) * LANES] = (
                    jnp.concatenate(pair, axis=0).astype(BF16))

    zk = jnp.zeros((CHUNK, LANES), F32)
    scs = []
    for pr in range(n_units // 2):
        u0, u1 = 2 * pr, 2 * pr + 1
        if u0 < 4:
            keys = jnp.concatenate([jnp.concatenate([ks[u0], zk], axis=1),
                                    jnp.concatenate([zk, ks[u1]], axis=1)], axis=0)
        else:
            lo0, hi0 = jnp.where(lane < 64, ks[u0], 0.0), jnp.where(lane >= 64, ks[u0], 0.0)
            lo1, hi1 = jnp.where(lane < 64, ks[u1], 0.0), jnp.where(lane >= 64, ks[u1], 0.0)
            keys = jnp.concatenate([jnp.concatenate([lo0, zk], axis=1), jnp.concatenate([hi0, zk], axis=1),
                                    jnp.concatenate([zk, lo1], axis=1), jnp.concatenate([zk, hi1], axis=1)],
                                   axis=0)
        scs.append(_dot_nt(a_ref[:, u0 * LANES:(u1 + 1) * LANES], keys))

    tables = _level_tables()
    tq = lax.broadcasted_iota(jnp.int32, (CHUNK, CHUNK), 0)
    ts = lax.broadcasted_iota(jnp.int32, (CHUNK, CHUNK), 1)
    covers = [((tq // blk) % 2 == 1) & ((tq // blk) == (ts // blk) + 1) & ((ts // blk) % 2 == 0)
              for blk in _LEVEL_BLOCKS]
    s_offs, o_inter = {}, {}
    for u in range(n_units):
        q, k, b = qs[u], ks[u], bs[u]
        heads = unit_heads[u]
        if len(heads) == 2:
            qm = jnp.concatenate([jnp.where((lane & 64) == lo, q, 0.0) for _, lo in heads], axis=0)
            bm = jnp.concatenate([b, b], axis=0)
        else:
            qm, bm = q, b
        s_off = None
        for (qref, kref), cover in zip(tables, covers):
            qrows, krows = [], []
            for g in range(CHUNK // GROUP):
                sl = slice(g * GROUP, (g + 1) * GROUP)
                rq, rk = int(qref[g * GROUP]), int(kref[g * GROUP])
                qrows.append(jnp.exp2(b[sl] - b[rq:rq + 1, :]) if rq >= 0 else jnp.zeros((GROUP, LANES), F32))
                krows.append(k[sl] * jnp.exp2(b[rk:rk + 1, :] - b[sl]) if rk >= 0
                             else jnp.zeros((GROUP, LANES), F32))
            qfac = jnp.concatenate(qrows, axis=0)
            if len(heads) == 2:
                qfac = jnp.concatenate([qfac, qfac], axis=0)
                cov = jnp.concatenate([cover, cover], axis=0)
            else:
                cov = cover
            part = _dot_nt(qm * qfac, jnp.concatenate(krows, axis=0))
            s_off = jnp.where(cov, part, 0.0 if s_off is None else s_off)
        eb = jnp.exp2(bm)
        b_last = b[CHUNK - 1:CHUNK, :]
        kd = (k * jnp.exp2(b_last - b)).astype(BF16)
        dec = jnp.exp2(b_last)
        qe = qm * eb
        for i, (head, lo) in enumerate(heads):
            sl = slice(i * CHUNK, (i + 1) * CHUNK)
            st = st_ref[head]
            o_inter[head] = _dot_nt(qe[sl], st)
            st_ref[head] = st * dec + _dot_tn(vs[head], kd)
            s_offs[head] = s_off[sl]

    for u in range(n_units):
        sc = scs[u // 2]
        paired = len(unit_heads[u]) == 2
        width = LANES if paired else CHUNK
        c0 = (u % 2) * width
        scol = (lane & 63) if paired else lane[:, :CHUNK]
        groups = []
        for j in range(CHUNK // GROUP):
            acc = jnp.zeros((GROUP, width), F32)
            for s in range(GROUP):
                r0 = (j * GROUP + s) * GROUP
                acc = jnp.where(scol == j * GROUP + s, sc[r0:r0 + GROUP, c0:c0 + width], acc)
            groups.append(acc)
        p = jnp.concatenate(groups, axis=0)
        if paired:
            vstack = jnp.concatenate([vs[h] for h, _ in unit_heads[u]], axis=0)
        for head, lo in unit_heads[u]:
            if lo is None:
                o = o_inter[head] + _dot(s_offs[head] + p, vs[head])
            else:
                o = (o_inter[head] + _dot(s_offs[head], vs[head])
                     + _dot(jnp.where((lane & 64) == lo, p, 0.0), vstack))
            o_ref[rows, head * LANES:(head + 1) * LANES] = o


def _even_kernel(x_ref, nw_ref, win_ref, lb_ref, wup_ref, bup_ref, hw_ref, wout_ref, out_ref,
                 q_ref, k_ref, g_ref, v_ref, gate_ref, o_ref, st_ref, a_ref):
    tl = x_ref.shape[1]

    @pl.when(pl.program_id(1) == 0)
    def _():
        st_ref[...] = jnp.zeros_like(st_ref)

    x = x_ref[0]
    hn = _rms(x, nw_ref[...]).astype(BF16)

    def proj(c0, n):
        return jnp.dot(hn, win_ref[:, c0:c0 + n], preferred_element_type=F32)

    q_ref[:, 0:512] = proj(_E_AQ, 512)
    lb = lb_ref[...]
    f = lb + (1.0 - lb) * _sigmoid(proj(_E_AF, 512))
    k_ref[:, 0:512] = 1.0 - f
    g_ref[:, 0:512] = jnp.log(f)
    v_ref[:, 0:512] = proj(_E_AI, 512)
    ag = proj(_E_AG, 512)
    gate_ref[:, 0:512] = ag * _sigmoid(ag)
    q_ref[:, 512:768] = proj(_E_BQ, 256) * (64.0 ** -0.5)
    k_ref[:, 512:768] = proj(_E_BK, 256)
    v_ref[:, 512:1024] = proj(_E_BV, 512)
    br = proj(_E_BR, 512)
    gate_ref[:, 512:1024] = br * _sigmoid(br)
    z = _dot_x3(proj(_E_LOW, LANES), wup_ref[...]) + bup_ref[...]
    g_ref[:, 512:768] = -_softplus(-z) * (1.0 / GLA_TAU)

    tri = _tri(CHUNK)
    tri3 = jnp.concatenate([tri, tri, tri], axis=1)

    def chunk_body(c, carry):
        rows = pl.ds(pl.multiple_of(c * CHUNK, CHUNK), CHUNK)
        b2 = _cumsum_rows(tri3, g_ref[rows, :]) * LOG2E
        _gla_chunk(c, q_ref, k_ref, v_ref, b2, o_ref, st_ref, a_ref)
        return carry

    lax.fori_loop(0, tl // CHUNK, chunk_body, 0)

    hw = hw_ref[...]
    ys = []
    for head in range(N_HEADS):
        sl = slice(head * LANES, (head + 1) * LANES)
        oh = o_ref[:, sl]
        ys.append(oh * lax.rsqrt(jnp.mean(oh * oh, axis=-1, keepdims=True) + EPS)
                  * hw[:, sl] * gate_ref[:, sl])
    y = jnp.concatenate(ys, axis=1).astype(BF16)
    out_ref[0] = x + jnp.dot(y, wout_ref[...], preferred_element_type=F32)


def _const_spec(shape):
    nd = len(shape)
    return pl.BlockSpec(shape, lambda *_: (0,) * nd)


def _even_mixer(x, norm_w, w_in, lb, w_up, b_up, hgrn_norm_w, gla_norm_w, w_out):
    B, L, D = x.shape
    tl = min(MIX_TILE, L)
    s = np.cumsum([0, 512, 512, 512, 512, 256, 256, 512, GLA_RANK, 512])
    cols = [w_in[:, s[i]:s[i + 1]] for i in range(9)]
    low = jnp.pad(cols[7], ((0, 0), (0, LANES - GLA_RANK)))
    w_in_r = jnp.concatenate(cols[:7] + [cols[8], low], axis=1).astype(BF16)
    w_up_p = jnp.pad(w_up, ((0, LANES - GLA_RANK), (0, 0)))
    hw = jnp.concatenate([hgrn_norm_w, gla_norm_w])[None, :]
    return pl.pallas_call(
        _even_kernel,
        grid=(B, L // tl),
        in_specs=[
            pl.BlockSpec((1, tl, D), lambda b, l: (b, l, 0)),
            _const_spec((1, D)),
            _const_spec((D, EVEN_COLS)),
            _const_spec((1, 512)),
            _const_spec((LANES, 256)),
            _const_spec((1, 256)),
            _const_spec((1, MIX_WIDTH)),
            _const_spec((MIX_WIDTH, D)),
        ],
        out_specs=pl.BlockSpec((1, tl, D), lambda b, l: (b, l, 0)),
        out_shape=jax.ShapeDtypeStruct((B, L, D), F32),
        scratch_shapes=[
            pltpu.VMEM((tl, _QK_COLS), F32),
            pltpu.VMEM((tl, _QK_COLS), F32),
            pltpu.VMEM((tl, _QK_COLS), F32),
            pltpu.VMEM((tl, MIX_WIDTH), F32),
            pltpu.VMEM((tl, MIX_WIDTH), F32),
            pltpu.VMEM((tl, MIX_WIDTH), F32),
            pltpu.VMEM((N_HEADS, LANES, LANES), F32),
            pltpu.VMEM((CHUNK * GROUP, _QK_COLS), BF16),
        ],
        compiler_params=pltpu.CompilerParams(
            dimension_semantics=("arbitrary", "arbitrary"), vmem_limit_bytes=VMEM_LIMIT),
        name="even_mixer",
    )(x, norm_w[None, :], w_in_r, lb[None, :], w_up_p, b_up[None, :], hw, w_out.astype(BF16))


_O_Q, _O_QR, _O_K, _O_KR, _O_V, _O_G, _O_X, _O_GATE = 0, 256, 512, 768, 1024, 1536, 2048, 2560
ODD_COLS = 3072


def _odd_kernel(x_ref, nw_ref, win_ref, cos_ref, sin_ref, dmat_ref, qdec_ref, kdec_ref, sdec_ref,
                rw_ref, cw_ref, cb_ref, wri_ref, bri_ref, lam_ref, wout_ref, out_ref,
                q_ref, k_ref, v_ref, o_ref, xe_ref, a_ref, u_ref, st_ref, h_ref):
    tl = x_ref.shape[1]

    @pl.when(pl.program_id(1) == 0)
    def _():
        st_ref[...] = jnp.zeros_like(st_ref)
        h_ref[...] = jnp.zeros_like(h_ref)
        xe_ref[0:8, :] = jnp.zeros((8, LRU_WIDTH), F32)

    x = x_ref[0]
    hn = _rms(x, nw_ref[...]).astype(BF16)

    def proj(c0, n):
        return jnp.dot(hn, win_ref[:, c0:c0 + n], preferred_element_type=F32)

    cos = cos_ref[...]
    sin = sin_ref[...]
    q_ref[...] = proj(_O_Q, 256) * cos + proj(_O_QR, 256) * sin
    k_ref[...] = (proj(_O_K, 256) * cos + proj(_O_KR, 256) * sin) * (RET_DK ** -0.5)
    v_ref[...] = proj(_O_V, 512)

    xe_ref[8:8 + tl, :] = proj(_O_X, LRU_WIDTH)
    cw = cw_ref[...]
    xc = cb_ref[...]
    for j in range(LRU_CONV):
        xc = xc + xe_ref[5 + j:5 + j + tl, :] * cw[j:j + 1, :]
    xe_ref[0:8, :] = xe_ref[tl:tl + 8, :]
    ri = jnp.dot(xc.astype(BF16), wri_ref[...], preferred_element_type=F32) + bri_ref[...]
    r = _sigmoid(ri[:, :LRU_WIDTH])
    i = _sigmoid(ri[:, LRU_WIDTH:])
    a = jnp.exp(-LRU_C * r * _softplus(-lam_ref[...]))
    a_ref[...] = a
    u_ref[...] = jnp.sqrt(1.0 - a * a) * (i * xc)

    lane = lax.broadcasted_iota(jnp.int32, (1, LANES), 1)
    crow = lax.broadcasted_iota(jnp.int32, (CHUNK, LRU_WIDTH), 0)

    def chunk_body(c, carry):
        rows = pl.ds(pl.multiple_of(c * CHUNK, CHUNK), CHUNK)
        for head in range(RET_HEADS):
            unit, half = head // 2, head % 2
            ul = slice(unit * LANES, (unit + 1) * LANES)
            vl = slice(head * LANES, (head + 1) * LANES)
            own = (lane < 64) if half == 0 else (lane >= 64)
            q = jnp.where(own, q_ref[rows, ul], 0.0)
            k = k_ref[rows, ul]
            v = v_ref[rows, vl]
            st = st_ref[head]
            o = _dot_nt(q, st) * qdec_ref[head]
            o = o + _dot(_dot_nt(q, k) * dmat_ref[head], v)
            st_ref[head] = st * sdec_ref[head] + _dot_tn(v, k * kdec_ref[head])
            o_ref[rows, vl] = o
        ca = a_ref[rows, :]
        ch = u_ref[rows, :]
        d = 1
        while d < CHUNK:
            keep = crow >= d
            sa = jnp.where(keep, pltpu.roll(ca, d, axis=0), 1.0)
            sh = jnp.where(keep, pltpu.roll(ch, d, axis=0), 0.0)
            ch = ca * sh + ch
            ca = ca * sa
            d *= 2
        ch = ch + ca * h_ref[...]
        h_ref[...] = ch[CHUNK - 1:CHUNK, :]
        o_ref[rows, 512:1024] = ch
        return carry

    lax.fori_loop(0, tl // CHUNK, chunk_body, 0, unroll=2)

    rw = rw_ref[...]
    ys = []
    for head in range(RET_HEADS):
        sl = slice(head * LANES, (head + 1) * LANES)
        oh = o_ref[:, sl]
        oh = oh - jnp.mean(oh, axis=-1, keepdims=True)
        ys.append(oh * lax.rsqrt(jnp.mean(oh * oh, axis=-1, keepdims=True) + EPS) * rw[:, sl])
    cg = proj(_O_G, 512)
    y_c = jnp.concatenate(ys, axis=1) * (cg * _sigmoid(cg))
    dg = proj(_O_GATE, LRU_WIDTH)
    gelu = 0.5 * dg * (1.0 + jnp.tanh(np.sqrt(2.0 / np.pi) * (dg + 0.044715 * (dg * dg * dg))))
    y_d = o_ref[:, 512:1024] * gelu
    y = jnp.concatenate([y_c, y_d], axis=1).astype(BF16)
    out_ref[0] = x + jnp.dot(y, wout_ref[...], preferred_element_type=F32)


def _rot_cols(w):
    d = w.shape[0]
    w4 = w.reshape(d, RET_HEADS, 2, RET_DK // 2)
    return jnp.stack([-w4[:, :, 1], w4[:, :, 0]], axis=2).reshape(d, RET_HEADS * RET_DK)


def _odd_mixer(x, norm_w, w_in, ret_norm_w, conv_w, conv_b, w_r, b_r, w_i, b_i, lam, w_out):
    B, L, D = x.shape
    tl = min(MIX_TILE, L)
    s = np.cumsum([0, 256, 256, 512, 512, 512, 512])
    wq, wk, wv, wg, wx, wgate = [w_in[:, s[i]:s[i + 1]] for i in range(6)]
    w_in_r = jnp.concatenate([wq, _rot_cols(wq), wk, _rot_cols(wk), wv, wg, wx, wgate],
                             axis=1).astype(BF16)
    half = RET_DK // 2
    inv = ROPE_BASE ** (-jnp.arange(half, dtype=F32) / half)
    ang = jnp.arange(L, dtype=F32)[:, None] * inv[None, :]
    cos = jnp.tile(jnp.cos(ang), (1, 2 * RET_HEADS))
    sin = jnp.tile(jnp.sin(ang), (1, 2 * RET_HEADS))
    lg = jnp.log1p(-jnp.exp2(-5.0 - jnp.arange(RET_HEADS, dtype=F32)))[:, None, None]
    t = jnp.arange(CHUNK, dtype=F32)
    rel = t[:, None] - t[None, :]
    dmat = jnp.where(rel >= 0, jnp.exp(lg * jnp.maximum(rel, 0.0)), 0.0)
    qdec = jnp.broadcast_to(jnp.exp(lg * (t[None, :, None] + 1.0)), (RET_HEADS, CHUNK, LANES))
    kdec = jnp.broadcast_to(jnp.exp(lg * (CHUNK - 1.0 - t[None, :, None])), (RET_HEADS, CHUNK, LANES))
    sdec = jnp.broadcast_to(jnp.exp(lg * float(CHUNK)), (RET_HEADS, 1, LANES))
    eye = jnp.eye(LRU_BLOCKS, dtype=F32)
    bd = LRU_WIDTH // LRU_BLOCKS

    def blockdiag(w):
        return (eye[:, None, :, None] * w[:, :, None, :]).reshape(LRU_WIDTH, LRU_WIDTH)

    wri = jnp.concatenate([blockdiag(w_r), blockdiag(w_i)], axis=1).astype(BF16)
    bri = jnp.concatenate([b_r, b_i])[None, :]
    return pl.pallas_call(
        _odd_kernel,
        grid=(B, L // tl),
        in_specs=[
            pl.BlockSpec((1, tl, D), lambda b, l: (b, l, 0)),
            _const_spec((1, D)),
            _const_spec((D, ODD_COLS)),
            pl.BlockSpec((tl, 256), lambda b, l: (l, 0)),
            pl.BlockSpec((tl, 256), lambda b, l: (l, 0)),
            _const_spec((RET_HEADS, CHUNK, CHUNK)),
            _const_spec((RET_HEADS, CHUNK, LANES)),
            _const_spec((RET_HEADS, CHUNK, LANES)),
            _const_spec((RET_HEADS, 1, LANES)),
            _const_spec((1, 512)),
            _const_spec((LRU_CONV, LRU_WIDTH)),
            _const_spec((1, LRU_WIDTH)),
            _const_spec((LRU_WIDTH, 2 * LRU_WIDTH)),
            _const_spec((1, 2 * LRU_WIDTH)),
            _const_spec((1, LRU_WIDTH)),
            _const_spec((MIX_WIDTH, D)),
        ],
        out_specs=pl.BlockSpec((1, tl, D), lambda b, l: (b, l, 0)),
        out_shape=jax.ShapeDtypeStruct((B, L, D), F32),
        scratch_shapes=[
            pltpu.VMEM((tl, 256), F32),
            pltpu.VMEM((tl, 256), F32),
            pltpu.VMEM((tl, 512), F32),
            pltpu.VMEM((tl, MIX_WIDTH), F32),
            pltpu.VMEM((tl + 8, LRU_WIDTH), F32),
            pltpu.VMEM((tl, LRU_WIDTH), F32),
            pltpu.VMEM((tl, LRU_WIDTH), F32),
            pltpu.VMEM((RET_HEADS, LANES, LANES), F32),
            pltpu.VMEM((1, LRU_WIDTH), F32),
        ],
        compiler_params=pltpu.CompilerParams(
            dimension_semantics=("arbitrary", "arbitrary"), vmem_limit_bytes=VMEM_LIMIT),
        name="odd_mixer",
    )(x, norm_w[None, :], w_in_r, cos, sin, dmat, qdec, kdec, sdec, ret_norm_w[None, :],
      conv_w, conv_b[None, :], wri, bri, lam[None, :], w_out.astype(BF16))


_I_E1, _I_E2, _I_G1, _I_G2, _I_R1, _I_R2 = 0, 1, 2, 3, 4, 5
_LOGIT_E0 = MOE_GROUPS


def _pack_bf16_pairs(lo, hi):
    lo_b = pltpu.bitcast(lo.astype(BF16).astype(F32), jnp.uint32)
    hi_b = pltpu.bitcast(hi.astype(BF16).astype(F32), jnp.uint32)
    return (lo_b >> 16) | (hi_b & jnp.uint32(0xFFFF0000))


def _unpack_pairs_f32(w):
    return pltpu.bitcast(w << 16, F32), pltpu.bitcast(w & jnp.uint32(0xFFFF0000), F32)


def _unpack_bf16_pairs(w):
    lo, hi = _unpack_pairs_f32(w)
    return lo.astype(BF16), hi.astype(BF16)


def _router_kernel(x_ref, nw_ref, wr_ref, br_ref, hnp_ref, info_ref, info_t_ref, cnt_ref):
    tm = x_ref.shape[0]

    @pl.when(pl.program_id(0) == 0)
    def _():
        cnt_ref[...] = jnp.zeros_like(cnt_ref)

    hn = _rms(x_ref[...], nw_ref[...])
    half = D_MODEL // 2
    hnp_ref[...] = _pack_bf16_pairs(hn[:, :half], hn[:, half:])

    logits = _dot_x3(hn, wr_ref[...]) + br_ref[...]
    lane = lax.broadcasted_iota(jnp.int32, (tm, LANES), 1).astype(F32)

    def first_max(vals):
        m = jnp.max(vals, axis=-1, keepdims=True)
        idx = jnp.min(jnp.where(vals == m, lane, float(LANES)), axis=-1, keepdims=True)
        return m, idx

    gl = jnp.where(lane < MOE_GROUPS, logits, NEG_BIG)
    gmax, gidx = first_max(gl)
    g_w = 1.0 / jnp.sum(jnp.exp(gl - gmax), axis=-1, keepdims=True)
    e0 = _LOGIT_E0 + MOE_PER_GROUP * gidx
    el = jnp.where((lane >= e0) & (lane < e0 + MOE_PER_GROUP), logits, NEG_BIG)
    m1, i1 = first_max(el)
    m2, i2 = first_max(jnp.where(lane == i1, NEG_BIG, el))
    p2 = jnp.exp(m2 - m1)
    w1 = 1.0 / (1.0 + p2)
    e1 = i1 - _LOGIT_E0
    e2 = i2 - _LOGIT_E0

    oh1 = jnp.where(lane == e1, 1.0, 0.0)
    oh2 = jnp.where(lane == e2, 1.0, 0.0)
    tri = _tri(tm)
    pre1 = jnp.dot(tri, oh1.astype(BF16), preferred_element_type=F32)
    pre2 = jnp.dot(tri, oh2.astype(BF16), preferred_element_type=F32)
    carry = cnt_ref[...]
    c1 = pre1[tm - 1:tm, :]
    r1 = jnp.sum(oh1 * (pre1 + carry), axis=-1, keepdims=True) - 1.0
    r2 = jnp.sum(oh2 * (pre2 + (carry + c1)), axis=-1, keepdims=True) - 1.0
    cnt_ref[...] = carry + c1 + pre2[tm - 1:tm, :]

    info = jnp.zeros((tm, LANES), F32)
    for ln, val in ((_I_E1, e1), (_I_E2, e2), (_I_G1, g_w * w1), (_I_G2, g_w * (w1 * p2)),
                    (_I_R1, r1), (_I_R2, r2)):
        info = jnp.where(lane == float(ln), val, info)
    info_ref[...] = info
    info_t_ref[...] = info.T[0:GROUP, :]


def _router(x2, norm_w, w_group, b_group, w_expert, b_expert):
    T, D = x2.shape
    tm = min(TOK_TILE, T)
    pad = LANES - MOE_GROUPS - MOE_EXPERTS
    wr = jnp.pad(jnp.concatenate([w_group, w_expert], axis=1), ((0, 0), (0, pad)))
    br = jnp.pad(jnp.concatenate([b_group, b_expert]), (0, pad))[None, :]
    return pl.pallas_call(
        _router_kernel,
        grid=(T // tm,),
        in_specs=[
            pl.BlockSpec((tm, D), lambda i: (i, 0)),
            _const_spec((1, D)),
            _const_spec((D, LANES)),
            _const_spec((1, LANES)),
        ],
        out_specs=[
            pl.BlockSpec((tm, D // 2), lambda i: (i, 0)),
            pl.BlockSpec((tm, LANES), lambda i: (i, 0)),
            pl.BlockSpec((GROUP, tm), lambda i: (0, i)),
            _const_spec((1, LANES)),
        ],
        out_shape=[
            jax.ShapeDtypeStruct((T, D // 2), jnp.uint32),
            jax.ShapeDtypeStruct((T, LANES), F32),
            jax.ShapeDtypeStruct((GROUP, T), F32),
            jax.ShapeDtypeStruct((1, LANES), F32),
        ],
        compiler_params=pltpu.CompilerParams(
            dimension_semantics=("arbitrary",), vmem_limit_bytes=VMEM_LIMIT),
        name="moe_router",
    )(x2, norm_w[None, :], wr, br)


def _plan_kernel(cnt_ref, info_ref, pos_ref, te_ref, nv_ref):
    nt = te_ref.shape[0]
    e1 = info_ref[_I_E1:_I_E1 + 1, :]
    e2 = info_ref[_I_E2:_I_E2 + 1, :]
    s1 = jnp.zeros_like(e1)
    s2 = jnp.zeros_like(e2)
    start = jnp.int32(0)
    for e in range(MOE_EXPERTS):
        tiles = lax.div(cnt_ref[e] + (ROW_TILE - 1), jnp.int32(ROW_TILE))
        seg = (start * ROW_TILE).astype(F32)
        s1 = jnp.where(e1 == float(e), seg, s1)
        s2 = jnp.where(e2 == float(e), seg, s2)

        def fill(i, carry, e=e):
            te_ref[i] = jnp.int32(e)
            return carry

        lax.fori_loop(start, start + tiles, fill, 0)
        start = start + tiles
    nv_ref[0] = start
    last = te_ref[jnp.maximum(start - 1, 0)]

    def fill_tail(i, carry):
        te_ref[i] = last
        return carry

    lax.fori_loop(start, nt, fill_tail, 0)
    pos_ref[0:1, :] = (s1 + info_ref[_I_R1:_I_R1 + 1, :]).astype(jnp.int32)
    pos_ref[1:2, :] = (s2 + info_ref[_I_R2:_I_R2 + 1, :]).astype(jnp.int32)


def _plan(cnt, info_t, nt):
    T = info_t.shape[1]
    return pl.pallas_call(
        _plan_kernel,
        in_specs=[
            pl.BlockSpec(memory_space=pltpu.SMEM),
            pl.BlockSpec(memory_space=pltpu.VMEM),
        ],
        out_specs=[
            pl.BlockSpec(memory_space=pltpu.VMEM),
            pl.BlockSpec(memory_space=pltpu.SMEM),
            pl.BlockSpec(memory_space=pltpu.SMEM),
        ],
        out_shape=[
            jax.ShapeDtypeStruct((2, T), jnp.int32),
            jax.ShapeDtypeStruct((nt,), jnp.int32),
            jax.ShapeDtypeStruct((1,), jnp.int32),
        ],
        name="moe_plan",
    )(cnt, info_t)


def _sc_worker():
    return lax.axis_index("s") * SC_CORES + lax.axis_index("c")


def _dispatch(hnp, pos1, pos2, n_rows):
    T, W = hnp.shape
    per_w = T // SC_WORKERS
    n_chunks = per_w // SC_ROWS
    mesh = plsc.VectorSubcoreMesh(core_axis_name="c", subcore_axis_name="s")

    @functools.partial(
        pl.kernel, mesh=mesh,
        out_type=jax.ShapeDtypeStruct((n_rows, W), jnp.uint32),
        scratch_types=[
            pltpu.VMEM((SC_ROWS,), jnp.int32),
            pltpu.VMEM((SC_ROWS,), jnp.int32),
            pltpu.VMEM((SC_ROWS, W), jnp.uint32),
        ],
        name="moe_dispatch_sc",
    )
    def scatter(hnp_hbm, p1_hbm, p2_hbm, xs_hbm, i1_v, i2_v, rows_v):
        base0 = _sc_worker() * per_w

        @pl.loop(0, n_chunks)
        def _(j):
            base = pl.multiple_of(base0 + j * SC_ROWS, SC_ROWS)
            pltpu.sync_copy(p1_hbm.at[pl.ds(base, SC_ROWS)], i1_v)
            pltpu.sync_copy(p2_hbm.at[pl.ds(base, SC_ROWS)], i2_v)
            pltpu.sync_copy(hnp_hbm.at[pl.ds(base, SC_ROWS)], rows_v)
            pltpu.sync_copy(rows_v, xs_hbm.at[i1_v])
            pltpu.sync_copy(rows_v, xs_hbm.at[i2_v])

    return scatter(hnp, pos1, pos2)


def _gather_rows(table, idx):
    T = idx.shape[0]
    W = table.shape[1]
    per_w = T // SC_WORKERS
    rows = SC_ROWS
    n_chunks = per_w // rows
    mesh = plsc.VectorSubcoreMesh(core_axis_name="c", subcore_axis_name="s")

    @functools.partial(
        pl.kernel, mesh=mesh,
        out_type=jax.ShapeDtypeStruct((T, W), table.dtype),
        scratch_types=[
            pltpu.VMEM((rows,), jnp.int32),
            pltpu.VMEM((rows, W), table.dtype),
        ],
        name="moe_gather_sc",
    )
    def gather(table_hbm, idx_hbm, out_hbm, i_v, rows_v):
        base0 = _sc_worker() * per_w

        @pl.loop(0, n_chunks)
        def _(j):
            base = pl.multiple_of(base0 + j * rows, rows)
            pltpu.sync_copy(idx_hbm.at[pl.ds(base, rows)], i_v)
            pltpu.sync_copy(table_hbm.at[i_v], rows_v)
            pltpu.sync_copy(rows_v, out_hbm.at[pl.ds(base, rows)])

    return gather(table, idx)


def _experts_kernel(te_ref, nv_ref, xs_ref, w1_ref, w3_ref, w2_ref, ys_ref, w1b, w3b, w2b):
    i = pl.program_id(0)

    @pl.when(i < nv_ref[0])
    def _():
        @pl.when((i == 0) | (te_ref[i] != te_ref[jnp.maximum(i - 1, 0)]))
        def _():
            w1b[...] = w1_ref[0].astype(BF16)
            w3b[...] = w3_ref[0].astype(BF16)
            w2b[...] = w2_ref[0].astype(BF16)

        lo, hi = _unpack_bf16_pairs(xs_ref[...])
        half = D_MODEL // 2
        d = functools.partial(jnp.dot, preferred_element_type=F32)
        h1 = d(lo, w1b[0:half, :]) + d(hi, w1b[half:, :])
        h3 = d(lo, w3b[0:half, :]) + d(hi, w3b[half:, :])
        hid = (h1 * _sigmoid(h1)) * h3
        y = d(hid.astype(BF16), w2b[...])
        ys_ref[...] = _pack_bf16_pairs(y[:, :half], y[:, half:])

    @pl.when(i >= nv_ref[0])
    def _():
        ys_ref[...] = jnp.zeros_like(ys_ref)


def _experts(xs, tile_expert, n_valid, w1, w3, w2, layer):
    n_rows, W = xs.shape
    nt = n_rows // ROW_TILE
    D, F = w1.shape[1], w1.shape[2]

    def row_map(i, te, nv):
        return (i, 0)

    def w_map(i, te, nv):
        return (te[i] + layer * MOE_EXPERTS, 0, 0)

    return pl.pallas_call(
        _experts_kernel,
        grid_spec=pltpu.PrefetchScalarGridSpec(
            num_scalar_prefetch=2,
            grid=(nt,),
            in_specs=[
                pl.BlockSpec((ROW_TILE, W), row_map),
                pl.BlockSpec((1, D, F), w_map),
                pl.BlockSpec((1, D, F), w_map),
                pl.BlockSpec((1, F, D), w_map),
            ],
            out_specs=pl.BlockSpec((ROW_TILE, D // 2), row_map),
            scratch_shapes=[
                pltpu.VMEM((D, F), BF16),
                pltpu.VMEM((D, F), BF16),
                pltpu.VMEM((F, D), BF16),
            ],
        ),
        out_shape=jax.ShapeDtypeStruct((n_rows, D // 2), jnp.uint32),
        compiler_params=pltpu.CompilerParams(
            dimension_semantics=("arbitrary",), vmem_limit_bytes=VMEM_LIMIT),
        name="moe_experts",
    )(tile_expert, n_valid, xs, w1, w3, w2)


def _combine_kernel(x_ref, info_ref, fw_ref, y1_ref, y2_ref, out_ref, *, final):
    info = info_ref[...]
    g1 = info[:, _I_G1:_I_G1 + 1]
    g2 = info[:, _I_G2:_I_G2 + 1]
    y1 = jnp.concatenate(_unpack_pairs_f32(y1_ref[...]), axis=1)
    y2 = jnp.concatenate(_unpack_pairs_f32(y2_ref[...]), axis=1)
    y = x_ref[...] + (g1 * y1 + g2 * y2)
    if final:
        y = _rms(y, fw_ref[...])
    out_ref[...] = y


def _combine(x2, info, y1, y2, final_w, final):
    T, D = x2.shape
    tm = min(TOK_TILE, T)
    row_spec = pl.BlockSpec((tm, D), lambda i: (i, 0))
    return pl.pallas_call(
        functools.partial(_combine_kernel, final=final),
        grid=(T // tm,),
        in_specs=[row_spec, pl.BlockSpec((tm, LANES), lambda i: (i, 0)), _const_spec((1, D)),
                  pl.BlockSpec((tm, D // 2), lambda i: (i, 0)),
                  pl.BlockSpec((tm, D // 2), lambda i: (i, 0))],
        out_specs=row_spec,
        out_shape=jax.ShapeDtypeStruct((T, D), F32),
        compiler_params=pltpu.CompilerParams(
            dimension_semantics=("arbitrary",), vmem_limit_bytes=VMEM_LIMIT),
        name="moe_combine",
    )(x2, info, final_w[None, :], y1, y2)


def _hier_moe(x2, norm_w, w_group, b_group, w_expert, b_expert, w1, w3, w2, layer, final_w, final):
    T, D = x2.shape
    hnp, info, info_t, cnt = _router(x2, norm_w, w_group, b_group, w_expert, b_expert)
    nt = (2 * T) // ROW_TILE + MOE_EXPERTS
    pos, tile_expert, n_valid = _plan(cnt[0].astype(jnp.int32), info_t, nt)
    pos1, pos2 = pos[0], pos[1]
    xs = _dispatch(hnp, pos1, pos2, nt * ROW_TILE)
    ys = _experts(xs, tile_expert, n_valid, w1, w3, w2, layer)
    return _combine(x2, info, _gather_rows(ys, pos1), _gather_rows(ys, pos2), final_w, final)


def kernel(x, norm_mix, norm_ffn, norm_final, even_w_in, hgrn_lb_logits, hgrn_norm_w, gla_w_up,
           gla_b_up, gla_norm_w, even_w_out, odd_w_in, ret_norm_w, conv_w, conv_b, rglru_w_r,
           rglru_b_r, rglru_w_i, rglru_b_i, rglru_lambda, odd_w_out, moe_w_group, moe_b_group,
           moe_w_expert, moe_b_expert, moe_w1, moe_w3, moe_w2):
    B, L, D = x.shape
    depth = norm_mix.shape[0]
    lb_table = jnp.cumsum(jax.nn.softmax(hgrn_lb_logits.astype(F32), axis=0), axis=0)
    w1 = moe_w1.reshape((-1,) + moe_w1.shape[2:])
    w3 = moe_w3.reshape((-1,) + moe_w3.shape[2:])
    w2 = moe_w2.reshape((-1,) + moe_w2.shape[2:])
    for l in range(depth):
        j = l // 2
        if l % 2 == 0:
            x = _even_mixer(x, norm_mix[l], even_w_in[j], lb_table[l], gla_w_up[j], gla_b_up[j],
                            hgrn_norm_w[j], gla_norm_w[j], even_w_out[j])
        else:
            x = _odd_mixer(x, norm_mix[l], odd_w_in[j], ret_norm_w[j], conv_w[j], conv_b[j],
                           rglru_w_r[j], rglru_b_r[j], rglru_w_i[j], rglru_b_i[j], rglru_lambda[j],
                           odd_w_out[j])
        x = _hier_moe(x.reshape(B * L, D), norm_ffn[l], moe_w_group[l], moe_b_group[l],
                      moe_w_expert[l], moe_b_expert[l], w1, w3, w2, l,
                      norm_final, l == depth - 1).reshape(B, L, D)
    return x
```

```python
import functools

import numpy as np
import jax
import jax.numpy as jnp
from jax import lax
from jax.experimental import pallas as pl
from jax.experimental.pallas import tpu as pltpu
from jax.experimental.pallas import tpu_sc as plsc

F32 = jnp.float32
BF16 = jnp.bfloat16

EPS = 1e-6
CHUNK = 64
GROUP = 8
LOG2E = 1.4426950408889634
LANES = 128
D_MODEL = 1024
MIX_WIDTH = 1024
N_HEADS = 8
GLA_RANK = 16
GLA_TAU = 16.0
RET_HEADS = 4
RET_DK = 64
ROPE_BASE = 10000.0
LRU_WIDTH = 512
LRU_BLOCKS = 8
LRU_CONV = 4
LRU_C = 8.0
MOE_GROUPS = 4
MOE_PER_GROUP = 8
MOE_EXPERTS = MOE_GROUPS * MOE_PER_GROUP
MOE_DFF = 512
NEG_BIG = -1e30

MIX_TILE = 512
TOK_TILE = 512
ROW_TILE = 512
MOE_PARTS = 2
SC_CORES = 2
SC_WORKERS = 32
SC_ROWS = 128
VMEM_LIMIT = 56 * 1024 * 1024


def _dot(a, b):
    return jnp.dot(a.astype(BF16), b.astype(BF16), preferred_element_type=F32)


def _dot_nt(a, b):
    return lax.dot_general(a.astype(BF16), b.astype(BF16), (((1,), (1,)), ((), ())),
                           preferred_element_type=F32)


def _dot_tn(a, b):
    return lax.dot_general(a.astype(BF16), b.astype(BF16), (((0,), (0,)), ((), ())),
                           preferred_element_type=F32)


def _split2(x):
    hi = x.astype(BF16)
    lo = (x - hi.astype(F32)).astype(BF16)
    return hi, lo


def _split3(x):
    hi = x.astype(BF16)
    r = x - hi.astype(F32)
    mid = r.astype(BF16)
    lo = (r - mid.astype(F32)).astype(BF16)
    return hi, mid, lo


def _dot_x3(a, b):
    ah, al = _split2(a)
    bh, bl = _split2(b)
    d = functools.partial(jnp.dot, preferred_element_type=F32)
    return d(ah, bh) + (d(ah, bl) + d(al, bh))


def _sigmoid(x):
    return 1.0 / (1.0 + jnp.exp(-x))


def _softplus(x):
    return jnp.maximum(x, 0.0) + jnp.log(1.0 + jnp.exp(-jnp.abs(x)))


def _rms(x, w):
    return x * lax.rsqrt(jnp.mean(x * x, axis=-1, keepdims=True) + EPS) * w


def _tri(n):
    r = lax.broadcasted_iota(jnp.int32, (n, n), 0)
    c = lax.broadcasted_iota(jnp.int32, (n, n), 1)
    return jnp.where(r >= c, 1.0, 0.0).astype(BF16)


def _cumsum_rows(tri3, g):
    return jnp.dot(tri3, jnp.concatenate(_split3(g), axis=0), preferred_element_type=F32)


_E_AQ, _E_AF, _E_AI, _E_AG = 0, 512, 1024, 1536
_E_BQ, _E_BK, _E_BV, _E_BR, _E_LOW = 2048, 2304, 2560, 3072, 3584
EVEN_COLS = 3712
_QK_COLS = 768


_LEVEL_BLOCKS = (32, 16, 8)


def _level_tables():
    t = np.arange(CHUNK)
    out = []
    for blk in _LEVEL_BLOCKS:
        odd = (t // blk) % 2 == 1
        out.append((np.where(odd, (t // blk) * blk - 1, -1), np.where(~odd, (t // blk) * blk + blk - 1, -1)))
    return out


def _gla_chunk(c, q_ref, k_ref, v_ref, b2, o_ref, st_ref, a_ref):
    n_units = _QK_COLS // LANES
    rows = pl.ds(pl.multiple_of(c * CHUNK, CHUNK), CHUNK)
    lane = lax.broadcasted_iota(jnp.int32, (1, LANES), 1)
    trow = lax.broadcasted_iota(jnp.int32, (GROUP, LANES), 0)
    unit_heads = [((u, None),) for u in range(4)] + [((4, 0), (5, 64)), ((6, 0), (7, 64))]
    qs = [q_ref[rows, u * LANES:(u + 1) * LANES] for u in range(n_units)]
    ks = [k_ref[rows, u * LANES:(u + 1) * LANES] for u in range(n_units)]
    bs = [b2[:, u * LANES:(u + 1) * LANES] for u in range(n_units)]
    vs = [v_ref[rows, h * LANES:(h + 1) * LANES] for h in range(N_HEADS)]

    for u in range(n_units):
        for j in range(CHUNK // GROUP):
            qj = qs[u][j * GROUP:(j + 1) * GROUP]
            bj = bs[u][j * GROUP:(j + 1) * GROUP]
            for s in range(0, GROUP, 2):
                pair = [qj * jnp.exp2(jnp.where(trow >= s + i, bj - bj[s + i:s + i + 1, :], NEG_BIG))
                        for i in (0, 1)]
                r0 = (j * GROUP + s) * GROUP
                a_ref[r0:r0 + 2 * GROUP, u * LANES:(u + 1) * LANES] = (
                    jnp.concatenate(pair, axis=0).astype(BF16))

    zk = jnp.zeros((CHUNK, LANES), F32)
    scs = []
    for pr in range(n_units // 2):
        u0, u1 = 2 * pr, 2 * pr + 1
        if u0 < 4:
            keys = jnp.concatenate([jnp.concatenate([ks[u0], zk], axis=1),
                                    jnp.concatenate([zk, ks[u1]], axis=1)], axis=0)
        else:
            lo0, hi0 = jnp.where(lane < 64, ks[u0], 0.0), jnp.where(lane >= 64, ks[u0], 0.0)
            lo1, hi1 = jnp.where(lane < 64, ks[u1], 0.0), jnp.where(lane >= 64, ks[u1], 0.0)
            keys = jnp.concatenate([jnp.concatenate([lo0, zk], axis=1), jnp.concatenate([hi0, zk], axis=1),
                                    jnp.concatenate([zk, lo1], axis=1), jnp.concatenate([zk, hi1], axis=1)],
                                   axis=0)
        scs.append(_dot_nt(a_ref[:, u0 * LANES:(u1 + 1) * LANES], keys))

    tables = _level_tables()
    tq = lax.broadcasted_iota(jnp.int32, (CHUNK, CHUNK), 0)
    ts = lax.broadcasted_iota(jnp.int32, (CHUNK, CHUNK), 1)
    covers = [((tq // blk) % 2 == 1) & ((tq // blk) == (ts // blk) + 1) & ((ts // blk) % 2 == 0)
              for blk in _LEVEL_BLOCKS]
    s_offs, o_inter = {}, {}
    for u in range(n_units):
        q, k, b = qs[u], ks[u], bs[u]
        heads = unit_heads[u]
        if len(heads) == 2:
            qm = jnp.concatenate([jnp.where((lane & 64) == lo, q, 0.0) for _, lo in heads], axis=0)
            bm = jnp.concatenate([b, b], axis=0)
        else:
            qm, bm = q, b
        s_off = None
        for (qref, kref), cover in zip(tables, covers):
            qrows, krows = [], []
            for g in range(CHUNK // GROUP):
                sl = slice(g * GROUP, (g + 1) * GROUP)
                rq, rk = int(qref[g * GROUP]), int(kref[g * GROUP])
                qrows.append(jnp.exp2(b[sl] - b[rq:rq + 1, :]) if rq >= 0 else jnp.zeros((GROUP, LANES), F32))
                krows.append(k[sl] * jnp.exp2(b[rk:rk + 1, :] - b[sl]) if rk >= 0
                             else jnp.zeros((GROUP, LANES), F32))
            qfac = jnp.concatenate(qrows, axis=0)
            if len(heads) == 2:
                qfac = jnp.concatenate([qfac, qfac], axis=0)
                cov = jnp.concatenate([cover, cover], axis=0)
            else:
                cov = cover
            part = _dot_nt(qm * qfac, jnp.concatenate(krows, axis=0))
            s_off = jnp.where(cov, part, 0.0 if s_off is None else s_off)
        eb = jnp.exp2(bm)
        b_last = b[CHUNK - 1:CHUNK, :]
        kd = (k * jnp.exp2(b_last - b)).astype(BF16)
        dec = jnp.exp2(b_last)
        qe = qm * eb
        for i, (head, lo) in enumerate(heads):
            sl = slice(i * CHUNK, (i + 1) * CHUNK)
            st = st_ref[head]
            o_inter[head] = _dot_nt(qe[sl], st)
            st_ref[head] = st * dec + _dot_tn(vs[head], kd)
            s_offs[head] = s_off[sl]

    for u in range(n_units):
        sc = scs[u // 2]
        paired = len(unit_heads[u]) == 2
        width = LANES if paired else CHUNK
        c0 = (u % 2) * width
        scol = (lane & 63) if paired else lane[:, :CHUNK]
        groups = []
        for j in range(CHUNK // GROUP):
            acc = jnp.zeros((GROUP, width), F32)
            for s in range(GROUP):
                r0 = (j * GROUP + s) * GROUP
                acc = jnp.where(scol == j * GROUP + s, sc[r0:r0 + GROUP, c0:c0 + width], acc)
            groups.append(acc)
        p = jnp.concatenate(groups, axis=0)
        if paired:
            vstack = jnp.concatenate([vs[h] for h, _ in unit_heads[u]], axis=0)
        for head, lo in unit_heads[u]:
            if lo is None:
                o = o_inter[head] + _dot(s_offs[head] + p, vs[head])
            else:
                o = (o_inter[head] + _dot(s_offs[head], vs[head])
                     + _dot(jnp.where((lane & 64) == lo, p, 0.0), vstack))
            o_ref[rows, head * LANES:(head + 1) * LANES] = o


def _even_kernel(x_ref, nw_ref, win_ref, lb_ref, wup_ref, bup_ref, hw_ref, wout_ref, out_ref,
                 q_ref, k_ref, g_ref, v_ref, gate_ref, o_ref, st_ref, a_ref):
    tl = x_ref.shape[1]

    @pl.when(pl.program_id(1) == 0)
    def _():
        st_ref[...] = jnp.zeros_like(st_ref)

    x = x_ref[0]
    hn = _rms(x, nw_ref[...]).astype(BF16)

    def proj(c0, n):
        return jnp.dot(hn, win_ref[:, c0:c0 + n], preferred_element_type=F32)

    q_ref[:, 0:512] = proj(_E_AQ, 512)
    lb = lb_ref[...]
    f = lb + (1.0 - lb) * _sigmoid(proj(_E_AF, 512))
    k_ref[:, 0:512] = 1.0 - f
    g_ref[:, 0:512] = jnp.log(f)
    v_ref[:, 0:512] = proj(_E_AI, 512)
    ag = proj(_E_AG, 512)
    gate_ref[:, 0:512] = ag * _sigmoid(ag)
    q_ref[:, 512:768] = proj(_E_BQ, 256) * (64.0 ** -0.5)
    k_ref[:, 512:768] = proj(_E_BK, 256)
    v_ref[:, 512:1024] = proj(_E_BV, 512)
    br = proj(_E_BR, 512)
    gate_ref[:, 512:1024] = br * _sigmoid(br)
    z = _dot_x3(proj(_E_LOW, LANES), wup_ref[...]) + bup_ref[...]
    g_ref[:, 512:768] = -_softplus(-z) * (1.0 / GLA_TAU)

    tri = _tri(CHUNK)
    tri3 = jnp.concatenate([tri, tri, tri], axis=1)

    def chunk_body(c, carry):
        rows = pl.ds(pl.multiple_of(c * CHUNK, CHUNK), CHUNK)
        b2 = _cumsum_rows(tri3, g_ref[rows, :]) * LOG2E
        _gla_chunk(c, q_ref, k_ref, v_ref, b2, o_ref, st_ref, a_ref)
        return carry

    lax.fori_loop(0, tl // CHUNK, chunk_body, 0)

    hw = hw_ref[...]
    ys = []
    for head in range(N_HEADS):
        sl = slice(head * LANES, (head + 1) * LANES)
        oh = o_ref[:, sl]
        ys.append(oh * lax.rsqrt(jnp.mean(oh * oh, axis=-1, keepdims=True) + EPS)
                  * hw[:, sl] * gate_ref[:, sl])
    y = jnp.concatenate(ys, axis=1).astype(BF16)
    out_ref[0] = x + jnp.dot(y, wout_ref[...], preferred_element_type=F32)


def _const_spec(shape):
    nd = len(shape)
    return pl.BlockSpec(shape, lambda *_: (0,) * nd)


def _even_mixer(x, norm_w, w_in, lb, w_up, b_up, hgrn_norm_w, gla_norm_w, w_out):
    B, L, D = x.shape
    tl = min(MIX_TILE, L)
    s = np.cumsum([0, 512, 512, 512, 512, 256, 256, 512, GLA_RANK, 512])
    cols = [w_in[:, s[i]:s[i + 1]] for i in range(9)]
    low = jnp.pad(cols[7], ((0, 0), (0, LANES - GLA_RANK)))
    w_in_r = jnp.concatenate(cols[:7] + [cols[8], low], axis=1).astype(BF16)
    w_up_p = jnp.pad(w_up, ((0, LANES - GLA_RANK), (0, 0)))
    hw = jnp.concatenate([hgrn_norm_w, gla_norm_w])[None, :]
    return pl.pallas_call(
        _even_kernel,
        grid=(B, L // tl),
        in_specs=[
            pl.BlockSpec((1, tl, D), lambda b, l: (b, l, 0)),
            _const_spec((1, D)),
            _const_spec((D, EVEN_COLS)),
            _const_spec((1, 512)),
            _const_spec((LANES, 256)),
            _const_spec((1, 256)),
            _const_spec((1, MIX_WIDTH)),
            _const_spec((MIX_WIDTH, D)),
        ],
        out_specs=pl.BlockSpec((1, tl, D), lambda b, l: (b, l, 0)),
        out_shape=jax.ShapeDtypeStruct((B, L, D), F32),
        scratch_shapes=[
            pltpu.VMEM((tl, _QK_COLS), F32),
            pltpu.VMEM((tl, _QK_COLS), F32),
            pltpu.VMEM((tl, _QK_COLS), F32),
            pltpu.VMEM((tl, MIX_WIDTH), F32),
            pltpu.VMEM((tl, MIX_WIDTH), F32),
            pltpu.VMEM((tl, MIX_WIDTH), F32),
            pltpu.VMEM((N_HEADS, LANES, LANES), F32),
            pltpu.VMEM((CHUNK * GROUP, _QK_COLS), BF16),
        ],
        compiler_params=pltpu.CompilerParams(
            dimension_semantics=("arbitrary", "arbitrary"), vmem_limit_bytes=VMEM_LIMIT),
        name="even_mixer",
    )(x, norm_w[None, :], w_in_r, lb[None, :], w_up_p, b_up[None, :], hw, w_out.astype(BF16))


_O_Q, _O_QR, _O_K, _O_KR, _O_V, _O_G, _O_X, _O_GATE = 0, 256, 512, 768, 1024, 1536, 2048, 2560
ODD_COLS = 3072


def _odd_kernel(x_ref, nw_ref, win_ref, cos_ref, sin_ref, dmat_ref, qdec_ref, kdec_ref, sdec_ref,
                rw_ref, cw_ref, cb_ref, wri_ref, bri_ref, lam_ref, wout_ref, out_ref,
                q_ref, k_ref, v_ref, o_ref, xe_ref, a_ref, u_ref, st_ref, h_ref):
    tl = x_ref.shape[1]

    @pl.when(pl.program_id(1) == 0)
    def _():
        st_ref[...] = jnp.zeros_like(st_ref)
        h_ref[...] = jnp.zeros_like(h_ref)
        xe_ref[0:8, :] = jnp.zeros((8, LRU_WIDTH), F32)

    x = x_ref[0]
    hn = _rms(x, nw_ref[...]).astype(BF16)

    def proj(c0, n):
        return jnp.dot(hn, win_ref[:, c0:c0 + n], preferred_element_type=F32)

    cos = cos_ref[...]
    sin = sin_ref[...]
    q_ref[...] = proj(_O_Q, 256) * cos + proj(_O_QR, 256) * sin
    k_ref[...] = (proj(_O_K, 256) * cos + proj(_O_KR, 256) * sin) * (RET_DK ** -0.5)
    v_ref[...] = proj(_O_V, 512)

    xe_ref[8:8 + tl, :] = proj(_O_X, LRU_WIDTH)
    cw = cw_ref[...]
    xc = cb_ref[...]
    for j in range(LRU_CONV):
        xc = xc + xe_ref[5 + j:5 + j + tl, :] * cw[j:j + 1, :]
    xe_ref[0:8, :] = xe_ref[tl:tl + 8, :]
    ri = jnp.dot(xc.astype(BF16), wri_ref[...], preferred_element_type=F32) + bri_ref[...]
    r = _sigmoid(ri[:, :LRU_WIDTH])
    i = _sigmoid(ri[:, LRU_WIDTH:])
    a = jnp.exp(-LRU_C * r * _softplus(-lam_ref[...]))
    a_ref[...] = a
    u_ref[...] = jnp.sqrt(1.0 - a * a) * (i * xc)

    lane = lax.broadcasted_iota(jnp.int32, (1, LANES), 1)
    crow = lax.broadcasted_iota(jnp.int32, (CHUNK, LRU_WIDTH), 0)

    def chunk_body(c, carry):
        rows = pl.ds(pl.multiple_of(c * CHUNK, CHUNK), CHUNK)
        for head in range(RET_HEADS):
            unit, half = head // 2, head % 2
            ul = slice(unit * LANES, (unit + 1) * LANES)
            vl = slice(head * LANES, (head + 1) * LANES)
            own = (lane < 64) if half == 0 else (lane >= 64)
            q = jnp.where(own, q_ref[rows, ul], 0.0)
            k = k_ref[rows, ul]
            v = v_ref[rows, vl]
            st = st_ref[head]
            o = _dot_nt(q, st) * qdec_ref[head]
            o = o + _dot(_dot_nt(q, k) * dmat_ref[head], v)
            st_ref[head] = st * sdec_ref[head] + _dot_tn(v, k * kdec_ref[head])
            o_ref[rows, vl] = o
        ca = a_ref[rows, :]
        ch = u_ref[rows, :]
        d = 1
        while d < CHUNK:
            keep = crow >= d
            sa = jnp.where(keep, pltpu.roll(ca, d, axis=0), 1.0)
            sh = jnp.where(keep, pltpu.roll(ch, d, axis=0), 0.0)
            ch = ca * sh + ch
            ca = ca * sa
            d *= 2
        ch = ch + ca * h_ref[...]
        h_ref[...] = ch[CHUNK - 1:CHUNK, :]
        o_ref[rows, 512:1024] = ch
        return carry

    lax.fori_loop(0, tl // CHUNK, chunk_body, 0, unroll=2)

    rw = rw_ref[...]
    ys = []
    for head in range(RET_HEADS):
        sl = slice(head * LANES, (head + 1) * LANES)
        oh = o_ref[:, sl]
        oh = oh - jnp.mean(oh, axis=-1, keepdims=True)
        ys.append(oh * lax.rsqrt(jnp.mean(oh * oh, axis=-1, keepdims=True) + EPS) * rw[:, sl])
    cg = proj(_O_G, 512)
    y_c = jnp.concatenate(ys, axis=1) * (cg * _sigmoid(cg))
    dg = proj(_O_GATE, LRU_WIDTH)
    gelu = 0.5 * dg * (1.0 + jnp.tanh(np.sqrt(2.0 / np.pi) * (dg + 0.044715 * (dg * dg * dg))))
    y_d = o_ref[:, 512:1024] * gelu
    y = jnp.concatenate([y_c, y_d], axis=1).astype(BF16)
    out_ref[0] = x + jnp.dot(y, wout_ref[...], preferred_element_type=F32)


def _rot_cols(w):
    d = w.shape[0]
    w4 = w.reshape(d, RET_HEADS, 2, RET_DK // 2)
    return jnp.stack([-w4[:, :, 1], w4[:, :, 0]], axis=2).reshape(d, RET_HEADS * RET_DK)


def _odd_mixer(x, norm_w, w_in, ret_norm_w, conv_w, conv_b, w_r, b_r, w_i, b_i, lam, w_out):
    B, L, D = x.shape
    tl = min(MIX_TILE, L)
    s = np.cumsum([0, 256, 256, 512, 512, 512, 512])
    wq, wk, wv, wg, wx, wgate = [w_in[:, s[i]:s[i + 1]] for i in range(6)]
    w_in_r = jnp.concatenate([wq, _rot_cols(wq), wk, _rot_cols(wk), wv, wg, wx, wgate],
                             axis=1).astype(BF16)
    half = RET_DK // 2
    inv = ROPE_BASE ** (-jnp.arange(half, dtype=F32) / half)
    ang = jnp.arange(L, dtype=F32)[:, None] * inv[None, :]
    cos = jnp.tile(jnp.cos(ang), (1, 2 * RET_HEADS))
    sin = jnp.tile(jnp.sin(ang), (1, 2 * RET_HEADS))
    lg = jnp.log1p(-jnp.exp2(-5.0 - jnp.arange(RET_HEADS, dtype=F32)))[:, None, None]
    t = jnp.arange(CHUNK, dtype=F32)
    rel = t[:, None] - t[None, :]
    dmat = jnp.where(rel >= 0, jnp.exp(lg * jnp.maximum(rel, 0.0)), 0.0)
    qdec = jnp.broadcast_to(jnp.exp(lg * (t[None, :, None] + 1.0)), (RET_HEADS, CHUNK, LANES))
    kdec = jnp.broadcast_to(jnp.exp(lg * (CHUNK - 1.0 - t[None, :, None])), (RET_HEADS, CHUNK, LANES))
    sdec = jnp.broadcast_to(jnp.exp(lg * float(CHUNK)), (RET_HEADS, 1, LANES))
    eye = jnp.eye(LRU_BLOCKS, dtype=F32)
    bd = LRU_WIDTH // LRU_BLOCKS

    def blockdiag(w):
        return (eye[:, None, :, None] * w[:, :, None, :]).reshape(LRU_WIDTH, LRU_WIDTH)

    wri = jnp.concatenate([blockdiag(w_r), blockdiag(w_i)], axis=1).astype(BF16)
    bri = jnp.concatenate([b_r, b_i])[None, :]
    return pl.pallas_call(
        _odd_kernel,
        grid=(B, L // tl),
        in_specs=[
            pl.BlockSpec((1, tl, D), lambda b, l: (b, l, 0)),
            _const_spec((1, D)),
            _const_spec((D, ODD_COLS)),
            pl.BlockSpec((tl, 256), lambda b, l: (l, 0)),
            pl.BlockSpec((tl, 256), lambda b, l: (l, 0)),
            _const_spec((RET_HEADS, CHUNK, CHUNK)),
            _const_spec((RET_HEADS, CHUNK, LANES)),
            _const_spec((RET_HEADS, CHUNK, LANES)),
            _const_spec((RET_HEADS, 1, LANES)),
            _const_spec((1, 512)),
            _const_spec((LRU_CONV, LRU_WIDTH)),
            _const_spec((1, LRU_WIDTH)),
            _const_spec((LRU_WIDTH, 2 * LRU_WIDTH)),
            _const_spec((1, 2 * LRU_WIDTH)),
            _const_spec((1, LRU_WIDTH)),
            _const_spec((MIX_WIDTH, D)),
        ],
        out_specs=pl.BlockSpec((1, tl, D), lambda b, l: (b, l, 0)),
        out_shape=jax.ShapeDtypeStruct((B, L, D), F32),
        scratch_shapes=[
            pltpu.VMEM((tl, 256), F32),
            pltpu.VMEM((tl, 256), F32),
            pltpu.VMEM((tl, 512), F32),
            pltpu.VMEM((tl, MIX_WIDTH), F32),
            pltpu.VMEM((tl + 8, LRU_WIDTH), F32),
            pltpu.VMEM((tl, LRU_WIDTH), F32),
            pltpu.VMEM((tl, LRU_WIDTH), F32),
            pltpu.VMEM((RET_HEADS, LANES, LANES), F32),
            pltpu.VMEM((1, LRU_WIDTH), F32),
        ],
        compiler_params=pltpu.CompilerParams(
            dimension_semantics=("arbitrary", "arbitrary"), vmem_limit_bytes=VMEM_LIMIT),
        name="odd_mixer",
    )(x, norm_w[None, :], w_in_r, cos, sin, dmat, qdec, kdec, sdec, ret_norm_w[None, :],
      conv_w, conv_b[None, :], wri, bri, lam[None, :], w_out.astype(BF16))


_I_E1, _I_E2, _I_G1, _I_G2, _I_R1, _I_R2 = 0, 1, 2, 3, 4, 5
_LOGIT_E0 = MOE_GROUPS


def _pack_bf16_pairs(lo, hi):
    lo_b = pltpu.bitcast(lo.astype(BF16).astype(F32), jnp.uint32)
    hi_b = pltpu.bitcast(hi.astype(BF16).astype(F32), jnp.uint32)
    return (lo_b >> 16) | (hi_b & jnp.uint32(0xFFFF0000))


def _unpack_pairs_f32(w):
    return pltpu.bitcast(w << 16, F32), pltpu.bitcast(w & jnp.uint32(0xFFFF0000), F32)


def _unpack_bf16_pairs(w):
    lo, hi = _unpack_pairs_f32(w)
    return lo.astype(BF16), hi.astype(BF16)


def _router_kernel(x_ref, nw_ref, wr_ref, br_ref, hnp_ref, info_ref, info_t_ref, cnt_ref):
    tm = x_ref.shape[0]

    @pl.when(pl.program_id(0) == 0)
    def _():
        cnt_ref[...] = jnp.zeros_like(cnt_ref)

    hn = _rms(x_ref[...], nw_ref[...])
    half = D_MODEL // 2
    hnp_ref[...] = _pack_bf16_pairs(hn[:, :half], hn[:, half:])

    logits = _dot_x3(hn, wr_ref[...]) + br_ref[...]
    lane = lax.broadcasted_iota(jnp.int32, (tm, LANES), 1).astype(F32)

    def first_max(vals):
        m = jnp.max(vals, axis=-1, keepdims=True)
        idx = jnp.min(jnp.where(vals == m, lane, float(LANES)), axis=-1, keepdims=True)
        return m, idx

    gl = jnp.where(lane < MOE_GROUPS, logits, NEG_BIG)
    gmax, gidx = first_max(gl)
    g_w = 1.0 / jnp.sum(jnp.exp(gl - gmax), axis=-1, keepdims=True)
    e0 = _LOGIT_E0 + MOE_PER_GROUP * gidx
    el = jnp.where((lane >= e0) & (lane < e0 + MOE_PER_GROUP), logits, NEG_BIG)
    m1, i1 = first_max(el)
    m2, i2 = first_max(jnp.where(lane == i1, NEG_BIG, el))
    p2 = jnp.exp(m2 - m1)
    w1 = 1.0 / (1.0 + p2)
    e1 = i1 - _LOGIT_E0
    e2 = i2 - _LOGIT_E0

    oh1 = jnp.where(lane == e1, 1.0, 0.0)
    oh2 = jnp.where(lane == e2, 1.0, 0.0)
    tri = _tri(tm)
    pre1 = jnp.dot(tri, oh1.astype(BF16), preferred_element_type=F32)
    pre2 = jnp.dot(tri, oh2.astype(BF16), preferred_element_type=F32)
    carry = cnt_ref[...]
    c1 = pre1[tm - 1:tm, :]
    r1 = jnp.sum(oh1 * (pre1 + carry), axis=-1, keepdims=True) - 1.0
    r2 = jnp.sum(oh2 * (pre2 + (carry + c1)), axis=-1, keepdims=True) - 1.0
    cnt_ref[...] = carry + c1 + pre2[tm - 1:tm, :]

    info = jnp.zeros((tm, LANES), F32)
    for ln, val in ((_I_E1, e1), (_I_E2, e2), (_I_G1, g_w * w1), (_I_G2, g_w * (w1 * p2)),
                    (_I_R1, r1), (_I_R2, r2)):
        info = jnp.where(lane == float(ln), val, info)
    info_ref[...] = info
    info_t_ref[...] = info.T[0:GROUP, :]


def _router(x2, part, norm_w, w_group, b_group, w_expert, b_expert):
    D = x2.shape[1]
    T = x2.shape[0] // MOE_PARTS
    tm = min(TOK_TILE, T)
    nb = T // tm
    pad = LANES - MOE_GROUPS - MOE_EXPERTS
    wr = jnp.pad(jnp.concatenate([w_group, w_expert], axis=1), ((0, 0), (0, pad)))
    br = jnp.pad(jnp.concatenate([b_group, b_expert]), (0, pad))[None, :]
    return pl.pallas_call(
        _router_kernel,
        grid=(nb,),
        in_specs=[
            pl.BlockSpec((tm, D), lambda i: (part * nb + i, 0)),
            _const_spec((1, D)),
            _const_spec((D, LANES)),
            _const_spec((1, LANES)),
        ],
        out_specs=[
            pl.BlockSpec((tm, D // 2), lambda i: (i, 0)),
            pl.BlockSpec((tm, LANES), lambda i: (i, 0)),
            pl.BlockSpec((GROUP, tm), lambda i: (0, i)),
            _const_spec((1, LANES)),
        ],
        out_shape=[
            jax.ShapeDtypeStruct((T, D // 2), jnp.uint32),
            jax.ShapeDtypeStruct((T, LANES), F32),
            jax.ShapeDtypeStruct((GROUP, T), F32),
            jax.ShapeDtypeStruct((1, LANES), F32),
        ],
        compiler_params=pltpu.CompilerParams(
            dimension_semantics=("arbitrary",), vmem_limit_bytes=VMEM_LIMIT),
        name="moe_router",
    )(x2, norm_w[None, :], wr, br)


def _plan_kernel(cnt_ref, info_ref, pos_ref, te_ref, nv_ref):
    nt = te_ref.shape[0]
    e1 = info_ref[_I_E1:_I_E1 + 1, :]
    e2 = info_ref[_I_E2:_I_E2 + 1, :]
    s1 = jnp.zeros_like(e1)
    s2 = jnp.zeros_like(e2)
    start = jnp.int32(0)
    for e in range(MOE_EXPERTS):
        tiles = lax.div(cnt_ref[e] + (ROW_TILE - 1), jnp.int32(ROW_TILE))
        seg = (start * ROW_TILE).astype(F32)
        s1 = jnp.where(e1 == float(e), seg, s1)
        s2 = jnp.where(e2 == float(e), seg, s2)

        def fill(i, carry, e=e):
            te_ref[i] = jnp.int32(e)
            return carry

        lax.fori_loop(start, start + tiles, fill, 0)
        start = start + tiles
    nv_ref[0] = start
    last = te_ref[jnp.maximum(start - 1, 0)]

    def fill_tail(i, carry):
        te_ref[i] = last
        return carry

    lax.fori_loop(start, nt, fill_tail, 0)
    pos_ref[0:1, :] = (s1 + info_ref[_I_R1:_I_R1 + 1, :]).astype(jnp.int32)
    pos_ref[1:2, :] = (s2 + info_ref[_I_R2:_I_R2 + 1, :]).astype(jnp.int32)


def _plan(cnt, info_t, nt):
    T = info_t.shape[1]
    return pl.pallas_call(
        _plan_kernel,
        in_specs=[
            pl.BlockSpec(memory_space=pltpu.SMEM),
            pl.BlockSpec(memory_space=pltpu.VMEM),
        ],
        out_specs=[
            pl.BlockSpec(memory_space=pltpu.VMEM),
            pl.BlockSpec(memory_space=pltpu.SMEM),
            pl.BlockSpec(memory_space=pltpu.SMEM),
        ],
        out_shape=[
            jax.ShapeDtypeStruct((2, T), jnp.int32),
            jax.ShapeDtypeStruct((nt,), jnp.int32),
            jax.ShapeDtypeStruct((1,), jnp.int32),
        ],
        name="moe_plan",
    )(cnt, info_t)


def _sc_worker():
    return lax.axis_index("s") * SC_CORES + lax.axis_index("c")


def _dispatch(hnp, pos1, pos2, n_rows):
    T, W = hnp.shape
    per_w = T // SC_WORKERS
    n_chunks = per_w // SC_ROWS
    mesh = plsc.VectorSubcoreMesh(core_axis_name="c", subcore_axis_name="s")

    @functools.partial(
        pl.kernel, mesh=mesh,
        out_type=jax.ShapeDtypeStruct((n_rows, W), jnp.uint32),
        scratch_types=[
            pltpu.VMEM((SC_ROWS,), jnp.int32),
            pltpu.VMEM((SC_ROWS,), jnp.int32),
            pltpu.VMEM((SC_ROWS, W), jnp.uint32),
        ],
        name="moe_dispatch_sc",
    )
    def scatter(hnp_hbm, p1_hbm, p2_hbm, xs_hbm, i1_v, i2_v, rows_v):
        base0 = _sc_worker() * per_w

        @pl.loop(0, n_chunks)
        def _(j):
            base = pl.multiple_of(base0 + j * SC_ROWS, SC_ROWS)
            pltpu.sync_copy(p1_hbm.at[pl.ds(base, SC_ROWS)], i1_v)
            pltpu.sync_copy(p2_hbm.at[pl.ds(base, SC_ROWS)], i2_v)
            pltpu.sync_copy(hnp_hbm.at[pl.ds(base, SC_ROWS)], rows_v)
            pltpu.sync_copy(rows_v, xs_hbm.at[i1_v])
            pltpu.sync_copy(rows_v, xs_hbm.at[i2_v])

    return scatter(hnp, pos1, pos2)


def _gather_rows(table, idx):
    T = idx.shape[0]
    W = table.shape[1]
    per_w = T // SC_WORKERS
    rows = SC_ROWS
    n_chunks = per_w // rows
    mesh = plsc.VectorSubcoreMesh(core_axis_name="c", subcore_axis_name="s")

    @functools.partial(
        pl.kernel, mesh=mesh,
        out_type=jax.ShapeDtypeStruct((T, W), table.dtype),
        scratch_types=[
            pltpu.VMEM((rows,), jnp.int32),
            pltpu.VMEM((rows, W), table.dtype),
        ],
        name="moe_gather_sc",
    )
    def gather(table_hbm, idx_hbm, out_hbm, i_v, rows_v):
        base0 = _sc_worker() * per_w

        @pl.loop(0, n_chunks)
        def _(j):
            base = pl.multiple_of(base0 + j * rows, rows)
            pltpu.sync_copy(idx_hbm.at[pl.ds(base, rows)], i_v)
            pltpu.sync_copy(table_hbm.at[i_v], rows_v)
            pltpu.sync_copy(rows_v, out_hbm.at[pl.ds(base, rows)])

    return gather(table, idx)


def _experts_kernel(te_ref, nv_ref, xs_ref, w1_ref, w3_ref, w2_ref, ys_ref, w1b, w3b, w2b):
    i = pl.program_id(0)

    @pl.when(i < nv_ref[0])
    def _():
        @pl.when((i == 0) | (te_ref[i] != te_ref[jnp.maximum(i - 1, 0)]))
        def _():
            w1b[...] = w1_ref[0].astype(BF16)
            w3b[...] = w3_ref[0].astype(BF16)
            w2b[...] = w2_ref[0].astype(BF16)

        lo, hi = _unpack_bf16_pairs(xs_ref[...])
        half = D_MODEL // 2
        d = functools.partial(jnp.dot, preferred_element_type=F32)
        h1 = d(lo, w1b[0:half, :]) + d(hi, w1b[half:, :])
        h3 = d(lo, w3b[0:half, :]) + d(hi, w3b[half:, :])
        hid = (h1 * _sigmoid(h1)) * h3
        y = d(hid.astype(BF16), w2b[...])
        ys_ref[...] = _pack_bf16_pairs(y[:, :half], y[:, half:])

    @pl.when(i >= nv_ref[0])
    def _():
        ys_ref[...] = jnp.zeros_like(ys_ref)


def _experts(xs, tile_expert, n_valid, w1, w3, w2, layer):
    n_rows, W = xs.shape
    nt = n_rows // ROW_TILE
    D, F = w1.shape[1], w1.shape[2]

    def row_map(i, te, nv):
        return (i, 0)

    def w_map(i, te, nv):
        return (te[i] + layer * MOE_EXPERTS, 0, 0)

    return pl.pallas_call(
        _experts_kernel,
        grid_spec=pltpu.PrefetchScalarGridSpec(
            num_scalar_prefetch=2,
            grid=(nt,),
            in_specs=[
                pl.BlockSpec((ROW_TILE, W), row_map),
                pl.BlockSpec((1, D, F), w_map),
                pl.BlockSpec((1, D, F), w_map),
                pl.BlockSpec((1, F, D), w_map),
            ],
            out_specs=pl.BlockSpec((ROW_TILE, D // 2), row_map),
            scratch_shapes=[
                pltpu.VMEM((D, F), BF16),
                pltpu.VMEM((D, F), BF16),
                pltpu.VMEM((F, D), BF16),
            ],
        ),
        out_shape=jax.ShapeDtypeStruct((n_rows, D // 2), jnp.uint32),
        compiler_params=pltpu.CompilerParams(
            dimension_semantics=("arbitrary",), vmem_limit_bytes=VMEM_LIMIT),
        name="moe_experts",
    )(tile_expert, n_valid, xs, w1, w3, w2)


def _combine_kernel(x_ref, info_ref, fw_ref, y1_ref, y2_ref, *rest, final):
    out_ref = rest[-1]
    info = info_ref[...]
    g1 = info[:, _I_G1:_I_G1 + 1]
    g2 = info[:, _I_G2:_I_G2 + 1]
    y1 = jnp.concatenate(_unpack_pairs_f32(y1_ref[...]), axis=1)
    y2 = jnp.concatenate(_unpack_pairs_f32(y2_ref[...]), axis=1)
    y = x_ref[...] + (g1 * y1 + g2 * y2)
    if final:
        y = _rms(y, fw_ref[...])
    out_ref[...] = y


def _combine(x2, part, prev, info, y1, y2, final_w, final):
    T, D = x2.shape
    tp = T // MOE_PARTS
    tm = min(TOK_TILE, tp)
    nb = tp // tm
    x_spec = pl.BlockSpec((tm, D), lambda i: (part * nb + i, 0))
    half_spec = pl.BlockSpec((tm, D // 2), lambda i: (i, 0))
    in_specs = [x_spec, pl.BlockSpec((tm, LANES), lambda i: (i, 0)), _const_spec((1, D)),
                half_spec, half_spec]
    args = [x2, info, final_w[None, :], y1, y2]
    aliases = {}
    if prev is not None:
        in_specs.append(pl.BlockSpec(memory_space=pl.ANY))
        args.append(prev)
        aliases = {len(args) - 1: 0}
    return pl.pallas_call(
        functools.partial(_combine_kernel, final=final),
        grid=(nb,),
        in_specs=in_specs,
        out_specs=x_spec,
        out_shape=jax.ShapeDtypeStruct((T, D), F32),
        input_output_aliases=aliases,
        compiler_params=pltpu.CompilerParams(
            dimension_semantics=("arbitrary",), vmem_limit_bytes=VMEM_LIMIT),
        name="moe_combine",
    )(*args)


def _hier_moe(x2, norm_w, w_group, b_group, w_expert, b_expert, w1, w3, w2, layer, final_w, final):
    T, D = x2.shape
    tp = T // MOE_PARTS
    nt = (2 * tp) // ROW_TILE + MOE_EXPERTS
    out = None
    for part in range(MOE_PARTS):
        hnp, info, info_t, cnt = _router(x2, part, norm_w, w_group, b_group, w_expert, b_expert)
        pos, tile_expert, n_valid = _plan(cnt[0].astype(jnp.int32), info_t, nt)
        pos1, pos2 = pos[0], pos[1]
        xs = _dispatch(hnp, pos1, pos2, nt * ROW_TILE)
        ys = _experts(xs, tile_expert, n_valid, w1, w3, w2, layer)
        out = _combine(x2, part, out, info, _gather_rows(ys, pos1), _gather_rows(ys, pos2),
                       final_w, final)
    return out


def kernel(x, norm_mix, norm_ffn, norm_final, even_w_in, hgrn_lb_logits, hgrn_norm_w, gla_w_up,
           gla_b_up, gla_norm_w, even_w_out, odd_w_in, ret_norm_w, conv_w, conv_b, rglru_w_r,
           rglru_b_r, rglru_w_i, rglru_b_i, rglru_lambda, odd_w_out, moe_w_group, moe_b_group,
           moe_w_expert, moe_b_expert, moe_w1, moe_w3, moe_w2):
    B, L, D = x.shape
    depth = norm_mix.shape[0]
    lb_table = jnp.cumsum(jax.nn.softmax(hgrn_lb_logits.astype(F32), axis=0), axis=0)
    w1 = moe_w1.reshape((-1,) + moe_w1.shape[2:])
    w3 = moe_w3.reshape((-1,) + moe_w3.shape[2:])
    w2 = moe_w2.reshape((-1,) + moe_w2.shape[2:])
    for l in range(depth):
        j = l // 2
        if l % 2 == 0:
            x = _even_mixer(x, norm_mix[l], even_w_in[j], lb_table[l], gla_w_up[j], gla_b_up[j],
                            hgrn_norm_w[j], gla_norm_w[j], even_w_out[j])
        else:
            x = _odd_mixer(x, norm_mix[l], odd_w_in[j], ret_norm_w[j], conv_w[j], conv_b[j],
                           rglru_w_r[j], rglru_b_r[j], rglru_w_i[j], rglru_b_i[j], rglru_lambda[j],
                           odd_w_out[j])
        x = _hier_moe(x.reshape(B * L, D), norm_ffn[l], moe_w_group[l], moe_b_group[l],
                      moe_w_expert[l], moe_b_expert[l], w1, w3, w2, l,
                      norm_final, l == depth - 1).reshape(B, L, D)
    return x
```

```python
import functools

import numpy as np
import jax
import jax.numpy as jnp
from jax import lax
from jax.experimental import pallas as pl
from jax.experimental.pallas import tpu as pltpu
from jax.experimental.pallas import tpu_sc as plsc

F32 = jnp.float32
BF16 = jnp.bfloat16

EPS = 1e-6
CHUNK = 64
GROUP = 8
LOG2E = 1.4426950408889634
LANES = 128
D_MODEL = 1024
MIX_WIDTH = 1024
N_HEADS = 8
GLA_RANK = 16
GLA_TAU = 16.0
RET_HEADS = 4
RET_DK = 64
ROPE_BASE = 10000.0
LRU_WIDTH = 512
LRU_BLOCKS = 8
LRU_CONV = 4
LRU_C = 8.0
MOE_GROUPS = 4
MOE_PER_GROUP = 8
MOE_EXPERTS = MOE_GROUPS * MOE_PER_GROUP
MOE_DFF = 512
NEG_BIG = -1e30

MIX_TILE = 512
TOK_TILE = 512
ROW_TILE = 512
MOE_PARTS = 2
SC_CORES = 2
SC_WORKERS = 32
SC_ROWS = 128
VMEM_LIMIT = 56 * 1024 * 1024


def _dot(a, b):
    return jnp.dot(a.astype(BF16), b.astype(BF16), preferred_element_type=F32)


def _dot_nt(a, b):
    return lax.dot_general(a.astype(BF16), b.astype(BF16), (((1,), (1,)), ((), ())),
                           preferred_element_type=F32)


def _dot_tn(a, b):
    return lax.dot_general(a.astype(BF16), b.astype(BF16), (((0,), (0,)), ((), ())),
                           preferred_element_type=F32)


def _split2(x):
    hi = x.astype(BF16)
    lo = (x - hi.astype(F32)).astype(BF16)
    return hi, lo


def _split3(x):
    hi = x.astype(BF16)
    r = x - hi.astype(F32)
    mid = r.astype(BF16)
    lo = (r - mid.astype(F32)).astype(BF16)
    return hi, mid, lo


def _dot_x3(a, b):
    ah, al = _split2(a)
    bh, bl = _split2(b)
    d = functools.partial(jnp.dot, preferred_element_type=F32)
    return d(ah, bh) + (d(ah, bl) + d(al, bh))


def _sigmoid(x):
    return 1.0 / (1.0 + jnp.exp(-x))


def _softplus(x):
    return jnp.maximum(x, 0.0) + jnp.log(1.0 + jnp.exp(-jnp.abs(x)))


def _rms(x, w):
    return x * lax.rsqrt(jnp.mean(x * x, axis=-1, keepdims=True) + EPS) * w


def _tri(n):
    r = lax.broadcasted_iota(jnp.int32, (n, n), 0)
    c = lax.broadcasted_iota(jnp.int32, (n, n), 1)
    return jnp.where(r >= c, 1.0, 0.0).astype(BF16)


def _cumsum_rows(tri3, g):
    return jnp.dot(tri3, jnp.concatenate(_split3(g), axis=0), preferred_element_type=F32)


_E_AQ, _E_AF, _E_AI, _E_AG = 0, 512, 1024, 1536
_E_BQ, _E_BK, _E_BV, _E_BR, _E_LOW = 2048, 2304, 2560, 3072, 3584
EVEN_COLS = 3712
_QK_COLS = 768


_LEVEL_BLOCKS = (32, 16, 8)


def _level_tables():
    t = np.arange(CHUNK)
    out = []
    for blk in _LEVEL_BLOCKS:
        odd = (t // blk) % 2 == 1
        out.append((np.where(odd, (t // blk) * blk - 1, -1), np.where(~odd, (t // blk) * blk + blk - 1, -1)))
    return out


def _gla_chunk(c, q_ref, k_ref, v_ref, b2, o_ref, st_ref, a_ref):
    n_units = _QK_COLS // LANES
    rows = pl.ds(pl.multiple_of(c * CHUNK, CHUNK), CHUNK)
    lane = lax.broadcasted_iota(jnp.int32, (1, LANES), 1)
    trow = lax.broadcasted_iota(jnp.int32, (GROUP, LANES), 0)
    unit_heads = [((u, None),) for u in range(4)] + [((4, 0), (5, 64)), ((6, 0), (7, 64))]
    qs = [q_ref[rows, u * LANES:(u + 1) * LANES] for u in range(n_units)]
    ks = [k_ref[rows, u * LANES:(u + 1) * LANES] for u in range(n_units)]
    bs = [b2[:, u * LANES:(u + 1) * LANES] for u in range(n_units)]
    vs = [v_ref[rows, h * LANES:(h + 1) * LANES] for h in range(N_HEADS)]

    for u in range(n_units):
        for j in range(CHUNK // GROUP):
            qj = qs[u][j * GROUP:(j + 1) * GROUP]
            bj = bs[u][j * GROUP:(j + 1) * GROUP]
            for s in range(0, GROUP, 2):
                pair = [qj * jnp.exp2(jnp.where(trow >= s + i, bj - bj[s + i:s + i + 1, :], NEG_BIG))
                        for i in (0, 1)]
                r0 = (j * GROUP + s) * GROUP
                a_ref[r0:r0 + 2 * GROUP, u * LANES:(u + 1) * LANES] = (
                    jnp.concatenate(pair, axis=0).astype(BF16))

    zk = jnp.zeros((CHUNK, LANES), F32)
    scs = []
    for pr in range(n_units // 2):
        u0, u1 = 2 * pr, 2 * pr + 1
        if u0 < 4:
            keys = jnp.concatenate([jnp.concatenate([ks[u0], zk], axis=1),
                                    jnp.concatenate([zk, ks[u1]], axis=1)], axis=0)
        else:
            lo0, hi0 = jnp.where(lane < 64, ks[u0], 0.0), jnp.where(lane >= 64, ks[u0], 0.0)
            lo1, hi1 = jnp.where(lane < 64, ks[u1], 0.0), jnp.where(lane >= 64, ks[u1], 0.0)
            keys = jnp.concatenate([jnp.concatenate([lo0, zk], axis=1), jnp.concatenate([hi0, zk], axis=1),
                                    jnp.concatenate([zk, lo1], axis=1), jnp.concatenate([zk, hi1], axis=1)],
                                   axis=0)
        scs.append(_dot_nt(a_ref[:, u0 * LANES:(u1 + 1) * LANES], keys))

    tables = _level_tables()
    tq = lax.broadcasted_iota(jnp.int32, (CHUNK, CHUNK), 0)
    ts = lax.broadcasted_iota(jnp.int32, (CHUNK, CHUNK), 1)
    covers = [((tq // blk) % 2 == 1) & ((tq // blk) == (ts // blk) + 1) & ((ts // blk) % 2 == 0)
              for blk in _LEVEL_BLOCKS]
    s_offs, o_inter = {}, {}
    for u in range(n_units):
        q, k, b = qs[u], ks[u], bs[u]
        heads = unit_heads[u]
        if len(heads) == 2:
            qm = jnp.concatenate([jnp.where((lane & 64) == lo, q, 0.0) for _, lo in heads], axis=0)
            bm = jnp.concatenate([b, b], axis=0)
        else:
            qm, bm = q, b
        s_off = None
        for (qref, kref), cover in zip(tables, covers):
            qrows, krows = [], []
            for g in range(CHUNK // GROUP):
                sl = slice(g * GROUP, (g + 1) * GROUP)
                rq, rk = int(qref[g * GROUP]), int(kref[g * GROUP])
                qrows.append(jnp.exp2(b[sl] - b[rq:rq + 1, :]) if rq >= 0 else jnp.zeros((GROUP, LANES), F32))
                krows.append(k[sl] * jnp.exp2(b[rk:rk + 1, :] - b[sl]) if rk >= 0
                             else jnp.zeros((GROUP, LANES), F32))
            qfac = jnp.concatenate(qrows, axis=0)
            if len(heads) == 2:
                qfac = jnp.concatenate([qfac, qfac], axis=0)
                cov = jnp.concatenate([cover, cover], axis=0)
            else:
                cov = cover
            part = _dot_nt(qm * qfac, jnp.concatenate(krows, axis=0))
            s_off = jnp.where(cov, part, 0.0 if s_off is None else s_off)
        eb = jnp.exp2(bm)
        b_last = b[CHUNK - 1:CHUNK, :]
        kd = (k * jnp.exp2(b_last - b)).astype(BF16)
        dec = jnp.exp2(b_last)
        qe = qm * eb
        for i, (head, lo) in enumerate(heads):
            sl = slice(i * CHUNK, (i + 1) * CHUNK)
            st = st_ref[head]
            o_inter[head] = _dot_nt(qe[sl], st)
            st_ref[head] = st * dec + _dot_tn(vs[head], kd)
            s_offs[head] = s_off[sl]

    for u in range(n_units):
        sc = scs[u // 2]
        paired = len(unit_heads[u]) == 2
        width = LANES if paired else CHUNK
        c0 = (u % 2) * width
        scol = (lane & 63) if paired else lane[:, :CHUNK]
        groups = []
        for j in range(CHUNK // GROUP):
            acc = jnp.zeros((GROUP, width), F32)
            for s in range(GROUP):
                r0 = (j * GROUP + s) * GROUP
                acc = jnp.where(scol == j * GROUP + s, sc[r0:r0 + GROUP, c0:c0 + width], acc)
            groups.append(acc)
        p = jnp.concatenate(groups, axis=0)
        if paired:
            vstack = jnp.concatenate([vs[h] for h, _ in unit_heads[u]], axis=0)
        for head, lo in unit_heads[u]:
            if lo is None:
                o = o_inter[head] + _dot(s_offs[head] + p, vs[head])
            else:
                o = (o_inter[head] + _dot(s_offs[head], vs[head])
                     + _dot(jnp.where((lane & 64) == lo, p, 0.0), vstack))
            o_ref[rows, head * LANES:(head + 1) * LANES] = o


def _even_kernel(x_ref, nw_ref, win_ref, lb_ref, wup_ref, bup_ref, hw_ref, wout_ref, out_ref,
                 q_ref, k_ref, g_ref, v_ref, gate_ref, o_ref, st_ref, a_ref):
    tl = x_ref.shape[1]

    @pl.when(pl.program_id(1) == 0)
    def _():
        st_ref[...] = jnp.zeros_like(st_ref)

    x = x_ref[0]
    hn = _rms(x, nw_ref[...]).astype(BF16)

    def proj(c0, n):
        return jnp.dot(hn, win_ref[:, c0:c0 + n], preferred_element_type=F32)

    q_ref[:, 0:512] = proj(_E_AQ, 512)
    lb = lb_ref[...]
    f = lb + (1.0 - lb) * _sigmoid(proj(_E_AF, 512))
    k_ref[:, 0:512] = 1.0 - f
    g_ref[:, 0:512] = jnp.log(f)
    v_ref[:, 0:512] = proj(_E_AI, 512)
    ag = proj(_E_AG, 512)
    gate_ref[:, 0:512] = ag * _sigmoid(ag)
    q_ref[:, 512:768] = proj(_E_BQ, 256) * (64.0 ** -0.5)
    k_ref[:, 512:768] = proj(_E_BK, 256)
    v_ref[:, 512:1024] = proj(_E_BV, 512)
    br = proj(_E_BR, 512)
    gate_ref[:, 512:1024] = br * _sigmoid(br)
    z = _dot_x3(proj(_E_LOW, LANES), wup_ref[...]) + bup_ref[...]
    g_ref[:, 512:768] = -_softplus(-z) * (1.0 / GLA_TAU)

    tri = _tri(CHUNK)
    tri3 = jnp.concatenate([tri, tri, tri], axis=1)

    def chunk_body(c, carry):
        rows = pl.ds(pl.multiple_of(c * CHUNK, CHUNK), CHUNK)
        b2 = _cumsum_rows(tri3, g_ref[rows, :]) * LOG2E
        _gla_chunk(c, q_ref, k_ref, v_ref, b2, o_ref, st_ref, a_ref)
        return carry

    lax.fori_loop(0, tl // CHUNK, chunk_body, 0)

    hw = hw_ref[...]
    ys = []
    for head in range(N_HEADS):
        sl = slice(head * LANES, (head + 1) * LANES)
        oh = o_ref[:, sl]
        ys.append(oh * lax.rsqrt(jnp.mean(oh * oh, axis=-1, keepdims=True) + EPS)
                  * hw[:, sl] * gate_ref[:, sl])
    y = jnp.concatenate(ys, axis=1).astype(BF16)
    out_ref[0] = x + jnp.dot(y, wout_ref[...], preferred_element_type=F32)


def _const_spec(shape):
    nd = len(shape)
    return pl.BlockSpec(shape, lambda *_: (0,) * nd)


def _even_mixer(x, norm_w, w_in, lb, w_up, b_up, hgrn_norm_w, gla_norm_w, w_out):
    B, L, D = x.shape
    tl = min(MIX_TILE, L)
    s = np.cumsum([0, 512, 512, 512, 512, 256, 256, 512, GLA_RANK, 512])
    cols = [w_in[:, s[i]:s[i + 1]] for i in range(9)]
    low = jnp.pad(cols[7], ((0, 0), (0, LANES - GLA_RANK)))
    w_in_r = jnp.concatenate(cols[:7] + [cols[8], low], axis=1).astype(BF16)
    w_up_p = jnp.pad(w_up, ((0, LANES - GLA_RANK), (0, 0)))
    hw = jnp.concatenate([hgrn_norm_w, gla_norm_w])[None, :]
    return pl.pallas_call(
        _even_kernel,
        grid=(B, L // tl),
        in_specs=[
            pl.BlockSpec((1, tl, D), lambda b, l: (b, l, 0)),
            _const_spec((1, D)),
            _const_spec((D, EVEN_COLS)),
            _const_spec((1, 512)),
            _const_spec((LANES, 256)),
            _const_spec((1, 256)),
            _const_spec((1, MIX_WIDTH)),
            _const_spec((MIX_WIDTH, D)),
        ],
        out_specs=pl.BlockSpec((1, tl, D), lambda b, l: (b, l, 0)),
        out_shape=jax.ShapeDtypeStruct((B, L, D), F32),
        scratch_shapes=[
            pltpu.VMEM((tl, _QK_COLS), F32),
            pltpu.VMEM((tl, _QK_COLS), F32),
            pltpu.VMEM((tl, _QK_COLS), F32),
            pltpu.VMEM((tl, MIX_WIDTH), F32),
            pltpu.VMEM((tl, MIX_WIDTH), F32),
            pltpu.VMEM((tl, MIX_WIDTH), F32),
            pltpu.VMEM((N_HEADS, LANES, LANES), F32),
            pltpu.VMEM((CHUNK * GROUP, _QK_COLS), BF16),
        ],
        compiler_params=pltpu.CompilerParams(
            dimension_semantics=("arbitrary", "arbitrary"), vmem_limit_bytes=VMEM_LIMIT),
        name="even_mixer",
    )(x, norm_w[None, :], w_in_r, lb[None, :], w_up_p, b_up[None, :], hw, w_out.astype(BF16))


_O_Q, _O_QR, _O_K, _O_KR, _O_V, _O_G, _O_X, _O_GATE = 0, 256, 512, 768, 1024, 1536, 2048, 2560
ODD_COLS = 3072


def _odd_kernel(x_ref, nw_ref, win_ref, cos_ref, sin_ref, dmat_ref, qdec_ref, kdec_ref, sdec_ref,
                rw_ref, cw_ref, cb_ref, wri_ref, bri_ref, lam_ref, wout_ref, out_ref,
                q_ref, k_ref, v_ref, o_ref, xe_ref, a_ref, u_ref, st_ref, h_ref):
    tl = x_ref.shape[1]

    @pl.when(pl.program_id(1) == 0)
    def _():
        st_ref[...] = jnp.zeros_like(st_ref)
        h_ref[...] = jnp.zeros_like(h_ref)
        xe_ref[0:8, :] = jnp.zeros((8, LRU_WIDTH), F32)

    x = x_ref[0]
    hn = _rms(x, nw_ref[...]).astype(BF16)

    def proj(c0, n):
        return jnp.dot(hn, win_ref[:, c0:c0 + n], preferred_element_type=F32)

    cos = cos_ref[...]
    sin = sin_ref[...]
    q_ref[...] = proj(_O_Q, 256) * cos + proj(_O_QR, 256) * sin
    k_ref[...] = (proj(_O_K, 256) * cos + proj(_O_KR, 256) * sin) * (RET_DK ** -0.5)
    v_ref[...] = proj(_O_V, 512)

    xe_ref[8:8 + tl, :] = proj(_O_X, LRU_WIDTH)
    cw = cw_ref[...]
    xc = cb_ref[...]
    for j in range(LRU_CONV):
        xc = xc + xe_ref[5 + j:5 + j + tl, :] * cw[j:j + 1, :]
    xe_ref[0:8, :] = xe_ref[tl:tl + 8, :]
    ri = jnp.dot(xc.astype(BF16), wri_ref[...], preferred_element_type=F32) + bri_ref[...]
    r = _sigmoid(ri[:, :LRU_WIDTH])
    i = _sigmoid(ri[:, LRU_WIDTH:])
    a = jnp.exp(-LRU_C * r * _softplus(-lam_ref[...]))
    a_ref[...] = a
    u_ref[...] = jnp.sqrt(1.0 - a * a) * (i * xc)

    lane = lax.broadcasted_iota(jnp.int32, (1, LANES), 1)
    crow = lax.broadcasted_iota(jnp.int32, (CHUNK, LRU_WIDTH), 0)

    def chunk_body(c, carry):
        rows = pl.ds(pl.multiple_of(c * CHUNK, CHUNK), CHUNK)
        for head in range(RET_HEADS):
            unit, half = head // 2, head % 2
            ul = slice(unit * LANES, (unit + 1) * LANES)
            vl = slice(head * LANES, (head + 1) * LANES)
            own = (lane < 64) if half == 0 else (lane >= 64)
            q = jnp.where(own, q_ref[rows, ul], 0.0)
            k = k_ref[rows, ul]
            v = v_ref[rows, vl]
            st = st_ref[head]
            o = _dot_nt(q, st) * qdec_ref[head]
            o = o + _dot(_dot_nt(q, k) * dmat_ref[head], v)
            st_ref[head] = st * sdec_ref[head] + _dot_tn(v, k * kdec_ref[head])
            o_ref[rows, vl] = o
        ca = a_ref[rows, :]
        ch = u_ref[rows, :]
        d = 1
        while d < CHUNK:
            keep = crow >= d
            sa = jnp.where(keep, pltpu.roll(ca, d, axis=0), 1.0)
            sh = jnp.where(keep, pltpu.roll(ch, d, axis=0), 0.0)
            ch = ca * sh + ch
            ca = ca * sa
            d *= 2
        ch = ch + ca * h_ref[...]
        h_ref[...] = ch[CHUNK - 1:CHUNK, :]
        o_ref[rows, 512:1024] = ch
        return carry

    lax.fori_loop(0, tl // CHUNK, chunk_body, 0, unroll=2)

    rw = rw_ref[...]
    ys = []
    for head in range(RET_HEADS):
        sl = slice(head * LANES, (head + 1) * LANES)
        oh = o_ref[:, sl]
        oh = oh - jnp.mean(oh, axis=-1, keepdims=True)
        ys.append(oh * lax.rsqrt(jnp.mean(oh * oh, axis=-1, keepdims=True) + EPS) * rw[:, sl])
    cg = proj(_O_G, 512)
    y_c = jnp.concatenate(ys, axis=1) * (cg * _sigmoid(cg))
    dg = proj(_O_GATE, LRU_WIDTH)
    gelu = 0.5 * dg * (1.0 + jnp.tanh(np.sqrt(2.0 / np.pi) * (dg + 0.044715 * (dg * dg * dg))))
    y_d = o_ref[:, 512:1024] * gelu
    y = jnp.concatenate([y_c, y_d], axis=1).astype(BF16)
    out_ref[0] = x + jnp.dot(y, wout_ref[...], preferred_element_type=F32)


def _rot_cols(w):
    d = w.shape[0]
    w4 = w.reshape(d, RET_HEADS, 2, RET_DK // 2)
    return jnp.stack([-w4[:, :, 1], w4[:, :, 0]], axis=2).reshape(d, RET_HEADS * RET_DK)


def _odd_mixer(x, norm_w, w_in, ret_norm_w, conv_w, conv_b, w_r, b_r, w_i, b_i, lam, w_out):
    B, L, D = x.shape
    tl = min(MIX_TILE, L)
    s = np.cumsum([0, 256, 256, 512, 512, 512, 512])
    wq, wk, wv, wg, wx, wgate = [w_in[:, s[i]:s[i + 1]] for i in range(6)]
    w_in_r = jnp.concatenate([wq, _rot_cols(wq), wk, _rot_cols(wk), wv, wg, wx, wgate],
                             axis=1).astype(BF16)
    half = RET_DK // 2
    inv = ROPE_BASE ** (-jnp.arange(half, dtype=F32) / half)
    ang = jnp.arange(L, dtype=F32)[:, None] * inv[None, :]
    cos = jnp.tile(jnp.cos(ang), (1, 2 * RET_HEADS))
    sin = jnp.tile(jnp.sin(ang), (1, 2 * RET_HEADS))
    lg = jnp.log1p(-jnp.exp2(-5.0 - jnp.arange(RET_HEADS, dtype=F32)))[:, None, None]
    t = jnp.arange(CHUNK, dtype=F32)
    rel = t[:, None] - t[None, :]
    dmat = jnp.where(rel >= 0, jnp.exp(lg * jnp.maximum(rel, 0.0)), 0.0)
    qdec = jnp.broadcast_to(jnp.exp(lg * (t[None, :, None] + 1.0)), (RET_HEADS, CHUNK, LANES))
    kdec = jnp.broadcast_to(jnp.exp(lg * (CHUNK - 1.0 - t[None, :, None])), (RET_HEADS, CHUNK, LANES))
    sdec = jnp.broadcast_to(jnp.exp(lg * float(CHUNK)), (RET_HEADS, 1, LANES))
    eye = jnp.eye(LRU_BLOCKS, dtype=F32)
    bd = LRU_WIDTH // LRU_BLOCKS

    def blockdiag(w):
        return (eye[:, None, :, None] * w[:, :, None, :]).reshape(LRU_WIDTH, LRU_WIDTH)

    wri = jnp.concatenate([blockdiag(w_r), blockdiag(w_i)], axis=1).astype(BF16)
    bri = jnp.concatenate([b_r, b_i])[None, :]
    return pl.pallas_call(
        _odd_kernel,
        grid=(B, L // tl),
        in_specs=[
            pl.BlockSpec((1, tl, D), lambda b, l: (b, l, 0)),
            _const_spec((1, D)),
            _const_spec((D, ODD_COLS)),
            pl.BlockSpec((tl, 256), lambda b, l: (l, 0)),
            pl.BlockSpec((tl, 256), lambda b, l: (l, 0)),
            _const_spec((RET_HEADS, CHUNK, CHUNK)),
            _const_spec((RET_HEADS, CHUNK, LANES)),
            _const_spec((RET_HEADS, CHUNK, LANES)),
            _const_spec((RET_HEADS, 1, LANES)),
            _const_spec((1, 512)),
            _const_spec((LRU_CONV, LRU_WIDTH)),
            _const_spec((1, LRU_WIDTH)),
            _const_spec((LRU_WIDTH, 2 * LRU_WIDTH)),
            _const_spec((1, 2 * LRU_WIDTH)),
            _const_spec((1, LRU_WIDTH)),
            _const_spec((MIX_WIDTH, D)),
        ],
        out_specs=pl.BlockSpec((1, tl, D), lambda b, l: (b, l, 0)),
        out_shape=jax.ShapeDtypeStruct((B, L, D), F32),
        scratch_shapes=[
            pltpu.VMEM((tl, 256), F32),
            pltpu.VMEM((tl, 256), F32),
            pltpu.VMEM((tl, 512), F32),
            pltpu.VMEM((tl, MIX_WIDTH), F32),
            pltpu.VMEM((tl + 8, LRU_WIDTH), F32),
            pltpu.VMEM((tl, LRU_WIDTH), F32),
            pltpu.VMEM((tl, LRU_WIDTH), F32),
            pltpu.VMEM((RET_HEADS, LANES, LANES), F32),
            pltpu.VMEM((1, LRU_WIDTH), F32),
        ],
        compiler_params=pltpu.CompilerParams(
            dimension_semantics=("arbitrary", "arbitrary"), vmem_limit_bytes=VMEM_LIMIT),
        name="odd_mixer",
    )(x, norm_w[None, :], w_in_r, cos, sin, dmat, qdec, kdec, sdec, ret_norm_w[None, :],
      conv_w, conv_b[None, :], wri, bri, lam[None, :], w_out.astype(BF16))


_I_E1, _I_E2, _I_G1, _I_G2, _I_R1, _I_R2 = 0, 1, 2, 3, 4, 5
_LOGIT_E0 = MOE_GROUPS


def _pack_bf16_pairs(lo, hi):
    lo_b = pltpu.bitcast(lo.astype(BF16).astype(F32), jnp.uint32)
    hi_b = pltpu.bitcast(hi.astype(BF16).astype(F32), jnp.uint32)
    return (lo_b >> 16) | (hi_b & jnp.uint32(0xFFFF0000))


def _unpack_pairs_f32(w):
    return pltpu.bitcast(w << 16, F32), pltpu.bitcast(w & jnp.uint32(0xFFFF0000), F32)


def _unpack_bf16_pairs(w):
    lo, hi = _unpack_pairs_f32(w)
    return lo.astype(BF16), hi.astype(BF16)


def _router_kernel(x_ref, nw_ref, wr_ref, br_ref, hnp_ref, info_ref, info_t_ref, cnt_ref):
    tm = x_ref.shape[0]

    @pl.when(pl.program_id(0) == 0)
    def _():
        cnt_ref[...] = jnp.zeros_like(cnt_ref)

    hn = _rms(x_ref[...], nw_ref[...])
    half = D_MODEL // 2
    hnp_ref[...] = _pack_bf16_pairs(hn[:, :half], hn[:, half:])

    logits = _dot_x3(hn, wr_ref[...]) + br_ref[...]
    lane = lax.broadcasted_iota(jnp.int32, (tm, LANES), 1).astype(F32)

    def first_max(vals):
        m = jnp.max(vals, axis=-1, keepdims=True)
        idx = jnp.min(jnp.where(vals == m, lane, float(LANES)), axis=-1, keepdims=True)
        return m, idx

    gl = jnp.where(lane < MOE_GROUPS, logits, NEG_BIG)
    gmax, gidx = first_max(gl)
    g_w = 1.0 / jnp.sum(jnp.exp(gl - gmax), axis=-1, keepdims=True)
    e0 = _LOGIT_E0 + MOE_PER_GROUP * gidx
    el = jnp.where((lane >= e0) & (lane < e0 + MOE_PER_GROUP), logits, NEG_BIG)
    m1, i1 = first_max(el)
    m2, i2 = first_max(jnp.where(lane == i1, NEG_BIG, el))
    p2 = jnp.exp(m2 - m1)
    w1 = 1.0 / (1.0 + p2)
    e1 = i1 - _LOGIT_E0
    e2 = i2 - _LOGIT_E0

    oh1 = jnp.where(lane == e1, 1.0, 0.0)
    oh2 = jnp.where(lane == e2, 1.0, 0.0)
    tri = _tri(tm)
    pre1 = jnp.dot(tri, oh1.astype(BF16), preferred_element_type=F32)
    pre2 = jnp.dot(tri, oh2.astype(BF16), preferred_element_type=F32)
    carry = cnt_ref[...]
    c1 = pre1[tm - 1:tm, :]
    r1 = jnp.sum(oh1 * (pre1 + carry), axis=-1, keepdims=True) - 1.0
    r2 = jnp.sum(oh2 * (pre2 + (carry + c1)), axis=-1, keepdims=True) - 1.0
    cnt_ref[...] = carry + c1 + pre2[tm - 1:tm, :]

    info = jnp.zeros((tm, LANES), F32)
    for ln, val in ((_I_E1, e1), (_I_E2, e2), (_I_G1, g_w * w1), (_I_G2, g_w * (w1 * p2)),
                    (_I_R1, r1), (_I_R2, r2)):
        info = jnp.where(lane == float(ln), val, info)
    info_ref[...] = info
    info_t_ref[...] = info.T[0:GROUP, :]


def _router(x2, part, norm_w, w_group, b_group, w_expert, b_expert):
    D = x2.shape[1]
    T = x2.shape[0] // MOE_PARTS
    tm = min(TOK_TILE, T)
    nb = T // tm
    pad = LANES - MOE_GROUPS - MOE_EXPERTS
    wr = jnp.pad(jnp.concatenate([w_group, w_expert], axis=1), ((0, 0), (0, pad)))
    br = jnp.pad(jnp.concatenate([b_group, b_expert]), (0, pad))[None, :]
    return pl.pallas_call(
        _router_kernel,
        grid=(nb,),
        in_specs=[
            pl.BlockSpec((tm, D), lambda i: (part * nb + i, 0)),
            _const_spec((1, D)),
            _const_spec((D, LANES)),
            _const_spec((1, LANES)),
        ],
        out_specs=[
            pl.BlockSpec((tm, D // 2), lambda i: (i, 0)),
            pl.BlockSpec((tm, LANES), lambda i: (i, 0)),
            pl.BlockSpec((GROUP, tm), lambda i: (0, i)),
            _const_spec((1, LANES)),
        ],
        out_shape=[
            jax.ShapeDtypeStruct((T, D // 2), jnp.uint32),
            jax.ShapeDtypeStruct((T, LANES), F32),
            jax.ShapeDtypeStruct((GROUP, T), F32),
            jax.ShapeDtypeStruct((1, LANES), F32),
        ],
        compiler_params=pltpu.CompilerParams(
            dimension_semantics=("arbitrary",), vmem_limit_bytes=VMEM_LIMIT),
        name="moe_router",
    )(x2, norm_w[None, :], wr, br)


def _plan_kernel(cnt_ref, info_ref, pos_ref, te_ref, nx_ref, nv_ref):
    nt = te_ref.shape[0]
    e1 = info_ref[_I_E1:_I_E1 + 1, :]
    e2 = info_ref[_I_E2:_I_E2 + 1, :]
    s1 = jnp.zeros_like(e1)
    s2 = jnp.zeros_like(e2)
    tiles = [lax.div(cnt_ref[e] + (ROW_TILE - 1), jnp.int32(ROW_TILE)) for e in range(MOE_EXPERTS)]
    following = [None] * MOE_EXPERTS
    nxt = jnp.int32(-1)
    for e in reversed(range(MOE_EXPERTS)):
        following[e] = nxt
        nxt = jnp.where(tiles[e] > 0, jnp.int32(e), nxt)
    start = jnp.int32(0)
    for e in range(MOE_EXPERTS):
        seg = (start * ROW_TILE).astype(F32)
        s1 = jnp.where(e1 == float(e), seg, s1)
        s2 = jnp.where(e2 == float(e), seg, s2)

        def fill(i, carry, e=e):
            te_ref[i] = jnp.int32(e)
            nx_ref[i] = following[e]
            return carry

        lax.fori_loop(start, start + tiles[e], fill, 0)
        start = start + tiles[e]
    nv_ref[0] = start

    def fill_tail(i, carry):
        te_ref[i] = jnp.int32(MOE_EXPERTS - 1)
        nx_ref[i] = jnp.int32(-1)
        return carry

    lax.fori_loop(start, nt, fill_tail, 0)
    pos_ref[0:1, :] = (s1 + info_ref[_I_R1:_I_R1 + 1, :]).astype(jnp.int32)
    pos_ref[1:2, :] = (s2 + info_ref[_I_R2:_I_R2 + 1, :]).astype(jnp.int32)


def _plan(cnt, info_t, nt):
    T = info_t.shape[1]
    smem = pl.BlockSpec(memory_space=pltpu.SMEM)
    return pl.pallas_call(
        _plan_kernel,
        in_specs=[smem, pl.BlockSpec(memory_space=pltpu.VMEM)],
        out_specs=[pl.BlockSpec(memory_space=pltpu.VMEM), smem, smem, smem],
        out_shape=[
            jax.ShapeDtypeStruct((2, T), jnp.int32),
            jax.ShapeDtypeStruct((nt,), jnp.int32),
            jax.ShapeDtypeStruct((nt,), jnp.int32),
            jax.ShapeDtypeStruct((1,), jnp.int32),
        ],
        name="moe_plan",
    )(cnt, info_t)


def _sc_worker():
    return lax.axis_index("s") * SC_CORES + lax.axis_index("c")


def _dispatch(hnp, pos1, pos2, n_rows):
    T, W = hnp.shape
    per_w = T // SC_WORKERS
    n_chunks = per_w // SC_ROWS
    mesh = plsc.VectorSubcoreMesh(core_axis_name="c", subcore_axis_name="s")

    @functools.partial(
        pl.kernel, mesh=mesh,
        out_type=jax.ShapeDtypeStruct((n_rows, W), jnp.uint32),
        scratch_types=[
            pltpu.VMEM((SC_ROWS,), jnp.int32),
            pltpu.VMEM((SC_ROWS,), jnp.int32),
            pltpu.VMEM((SC_ROWS, W), jnp.uint32),
        ],
        name="moe_dispatch_sc",
    )
    def scatter(hnp_hbm, p1_hbm, p2_hbm, xs_hbm, i1_v, i2_v, rows_v):
        base0 = _sc_worker() * per_w

        @pl.loop(0, n_chunks)
        def _(j):
            base = pl.multiple_of(base0 + j * SC_ROWS, SC_ROWS)
            pltpu.sync_copy(p1_hbm.at[pl.ds(base, SC_ROWS)], i1_v)
            pltpu.sync_copy(p2_hbm.at[pl.ds(base, SC_ROWS)], i2_v)
            pltpu.sync_copy(hnp_hbm.at[pl.ds(base, SC_ROWS)], rows_v)
            pltpu.sync_copy(rows_v, xs_hbm.at[i1_v])
            pltpu.sync_copy(rows_v, xs_hbm.at[i2_v])

    return scatter(hnp, pos1, pos2)


def _gather_rows(table, idx):
    T = idx.shape[0]
    W = table.shape[1]
    per_w = T // SC_WORKERS
    rows = SC_ROWS
    n_chunks = per_w // rows
    mesh = plsc.VectorSubcoreMesh(core_axis_name="c", subcore_axis_name="s")

    @functools.partial(
        pl.kernel, mesh=mesh,
        out_type=jax.ShapeDtypeStruct((T, W), table.dtype),
        scratch_types=[
            pltpu.VMEM((rows,), jnp.int32),
            pltpu.VMEM((rows, W), table.dtype),
        ],
        name="moe_gather_sc",
    )
    def gather(table_hbm, idx_hbm, out_hbm, i_v, rows_v):
        base0 = _sc_worker() * per_w

        @pl.loop(0, n_chunks)
        def _(j):
            base = pl.multiple_of(base0 + j * rows, rows)
            pltpu.sync_copy(idx_hbm.at[pl.ds(base, rows)], i_v)
            pltpu.sync_copy(table_hbm.at[i_v], rows_v)
            pltpu.sync_copy(rows_v, out_hbm.at[pl.ds(base, rows)])

    return gather(table, idx)


def _experts_kernel(te_ref, nx_ref, nv_ref, xs_ref, w1_hbm, w3_hbm, w2_hbm, ys_ref,
                    w1f, w3f, w2f, w1b, w3b, w2b, sem, *, layer):
    i = pl.program_id(0)

    def fetch(e):
        row = e + layer * MOE_EXPERTS
        return [pltpu.make_async_copy(w1_hbm.at[row], w1f, sem),
                pltpu.make_async_copy(w3_hbm.at[row], w3f, sem),
                pltpu.make_async_copy(w2_hbm.at[row], w2f, sem)]

    @pl.when(i < nv_ref[0])
    def _():
        @pl.when(i == 0)
        def _():
            for cp in fetch(te_ref[0]):
                cp.start()

        @pl.when((i == 0) | (te_ref[i] != te_ref[jnp.maximum(i - 1, 0)]))
        def _():
            for cp in fetch(te_ref[i]):
                cp.wait()
            w1b[...] = w1f[...].astype(BF16)
            w3b[...] = w3f[...].astype(BF16)
            w2b[...] = w2f[...].astype(BF16)

            @pl.when(nx_ref[i] >= 0)
            def _():
                for cp in fetch(nx_ref[i]):
                    cp.start()

        lo, hi = _unpack_bf16_pairs(xs_ref[...])
        half = D_MODEL // 2
        d = functools.partial(jnp.dot, preferred_element_type=F32)
        h1 = d(lo, w1b[0:half, :]) + d(hi, w1b[half:, :])
        h3 = d(lo, w3b[0:half, :]) + d(hi, w3b[half:, :])
        hid = (h1 * _sigmoid(h1)) * h3
        y = d(hid.astype(BF16), w2b[...])
        ys_ref[...] = _pack_bf16_pairs(y[:, :half], y[:, half:])

    @pl.when(i >= nv_ref[0])
    def _():
        ys_ref[...] = jnp.zeros_like(ys_ref)


def _experts(xs, tile_expert, next_expert, n_valid, w1, w3, w2, layer):
    n_rows, W = xs.shape
    nt = n_rows // ROW_TILE
    D, F = w1.shape[1], w1.shape[2]

    def row_map(i, te, nx, nv):
        return (i, 0)

    hbm = pl.BlockSpec(memory_space=pl.ANY)
    return pl.pallas_call(
        functools.partial(_experts_kernel, layer=layer),
        grid_spec=pltpu.PrefetchScalarGridSpec(
            num_scalar_prefetch=3,
            grid=(nt,),
            in_specs=[pl.BlockSpec((ROW_TILE, W), row_map), hbm, hbm, hbm],
            out_specs=pl.BlockSpec((ROW_TILE, D // 2), row_map),
            scratch_shapes=[
                pltpu.VMEM((D, F), F32),
                pltpu.VMEM((D, F), F32),
                pltpu.VMEM((F, D), F32),
                pltpu.VMEM((D, F), BF16),
                pltpu.VMEM((D, F), BF16),
                pltpu.VMEM((F, D), BF16),
                pltpu.SemaphoreType.DMA(()),
            ],
        ),
        out_shape=jax.ShapeDtypeStruct((n_rows, D // 2), jnp.uint32),
        compiler_params=pltpu.CompilerParams(
            dimension_semantics=("arbitrary",), vmem_limit_bytes=VMEM_LIMIT),
        name="moe_experts",
    )(tile_expert, next_expert, n_valid, xs, w1, w3, w2)


def _combine_kernel(x_ref, info_ref, fw_ref, y1_ref, y2_ref, out_ref, *, final):
    info = info_ref[...]
    g1 = info[:, _I_G1:_I_G1 + 1]
    g2 = info[:, _I_G2:_I_G2 + 1]
    y1 = jnp.concatenate(_unpack_pairs_f32(y1_ref[...]), axis=1)
    y2 = jnp.concatenate(_unpack_pairs_f32(y2_ref[...]), axis=1)
    y = x_ref[...] + (g1 * y1 + g2 * y2)
    if final:
        y = _rms(y, fw_ref[...])
    out_ref[...] = y


def _combine(x2, part, info, y1, y2, final_w, final):
    T, D = x2.shape
    tp = T // MOE_PARTS
    tm = min(TOK_TILE, tp)
    nb = tp // tm
    x_spec = pl.BlockSpec((tm, D), lambda i: (part * nb + i, 0))
    half_spec = pl.BlockSpec((tm, D // 2), lambda i: (i, 0))
    return pl.pallas_call(
        functools.partial(_combine_kernel, final=final),
        grid=(nb,),
        in_specs=[x_spec, pl.BlockSpec((tm, LANES), lambda i: (i, 0)), _const_spec((1, D)),
                  half_spec, half_spec],
        out_specs=x_spec,
        out_shape=jax.ShapeDtypeStruct((T, D), F32),
        input_output_aliases={0: 0},
        compiler_params=pltpu.CompilerParams(
            dimension_semantics=("arbitrary",), vmem_limit_bytes=VMEM_LIMIT),
        name="moe_combine",
    )(x2, info, final_w[None, :], y1, y2)


def _hier_moe(x2, norm_w, w_group, b_group, w_expert, b_expert, w1, w3, w2, layer, final_w, final):
    T, D = x2.shape
    tp = T // MOE_PARTS
    nt = (2 * tp) // ROW_TILE + MOE_EXPERTS
    routed = []
    for part in range(MOE_PARTS):
        hnp, info, info_t, cnt = _router(x2, part, norm_w, w_group, b_group, w_expert, b_expert)
        pos, tile_expert, next_expert, n_valid = _plan(cnt[0].astype(jnp.int32), info_t, nt)
        pos1, pos2 = pos[0], pos[1]
        xs = _dispatch(hnp, pos1, pos2, nt * ROW_TILE)
        ys = _experts(xs, tile_expert, next_expert, n_valid, w1, w3, w2, layer)
        routed.append((info, _gather_rows(ys, pos1), _gather_rows(ys, pos2)))
    out = x2
    for part, (info, y1, y2) in enumerate(routed):
        out = _combine(out, part, info, y1, y2, final_w, final)
    return out


def kernel(x, norm_mix, norm_ffn, norm_final, even_w_in, hgrn_lb_logits, hgrn_norm_w, gla_w_up,
           gla_b_up, gla_norm_w, even_w_out, odd_w_in, ret_norm_w, conv_w, conv_b, rglru_w_r,
           rglru_b_r, rglru_w_i, rglru_b_i, rglru_lambda, odd_w_out, moe_w_group, moe_b_group,
           moe_w_expert, moe_b_expert, moe_w1, moe_w3, moe_w2):
    B, L, D = x.shape
    depth = norm_mix.shape[0]
    lb_table = jnp.cumsum(jax.nn.softmax(hgrn_lb_logits.astype(F32), axis=0), axis=0)
    w1 = moe_w1.reshape((-1,) + moe_w1.shape[2:])
    w3 = moe_w3.reshape((-1,) + moe_w3.shape[2:])
    w2 = moe_w2.reshape((-1,) + moe_w2.shape[2:])
    for l in range(depth):
        j = l // 2
        if l % 2 == 0:
            x = _even_mixer(x, norm_mix[l], even_w_in[j], lb_table[l], gla_w_up[j], gla_b_up[j],
                            hgrn_norm_w[j], gla_norm_w[j], even_w_out[j])
        else:
            x = _odd_mixer(x, norm_mix[l], odd_w_in[j], ret_norm_w[j], conv_w[j], conv_b[j],
                           rglru_w_r[j], rglru_b_r[j], rglru_w_i[j], rglru_b_i[j], rglru_lambda[j],
                           odd_w_out[j])
        x = _hier_moe(x.reshape(B * L, D), norm_ffn[l], moe_w_group[l], moe_b_group[l],
                      moe_w_expert[l], moe_b_expert[l], w1, w3, w2, l,
                      norm_final, l == depth - 1).reshape(B, L, D)
    return x
```

```python
import functools

import numpy as np
import jax
import jax.numpy as jnp
from jax import lax
from jax.experimental import pallas as pl
from jax.experimental.pallas import tpu as pltpu
from jax.experimental.pallas import tpu_sc as plsc

F32 = jnp.float32
BF16 = jnp.bfloat16

EPS = 1e-6
CHUNK = 64
GROUP = 8
LOG2E = 1.4426950408889634
LANES = 128
D_MODEL = 1024
MIX_WIDTH = 1024
N_HEADS = 8
GLA_RANK = 16
GLA_TAU = 16.0
RET_HEADS = 4
RET_DK = 64
ROPE_BASE = 10000.0
LRU_WIDTH = 512
LRU_BLOCKS = 8
LRU_CONV = 4
LRU_C = 8.0
MOE_GROUPS = 4
MOE_PER_GROUP = 8
MOE_EXPERTS = MOE_GROUPS * MOE_PER_GROUP
MOE_DFF = 512
NEG_BIG = -1e30

MIX_TILE = 512
TOK_TILE = 512
ROW_TILE = 512
MOE_PARTS = 1
SC_CORES = 2
SC_WORKERS = 32
SC_ROWS = 128
VMEM_LIMIT = 56 * 1024 * 1024


def _dot(a, b):
    return jnp.dot(a.astype(BF16), b.astype(BF16), preferred_element_type=F32)


def _dot_nt(a, b):
    return lax.dot_general(a.astype(BF16), b.astype(BF16), (((1,), (1,)), ((), ())),
                           preferred_element_type=F32)


def _dot_tn(a, b):
    return lax.dot_general(a.astype(BF16), b.astype(BF16), (((0,), (0,)), ((), ())),
                           preferred_element_type=F32)


def _split2(x):
    hi = x.astype(BF16)
    lo = (x - hi.astype(F32)).astype(BF16)
    return hi, lo


def _split3(x):
    hi = x.astype(BF16)
    r = x - hi.astype(F32)
    mid = r.astype(BF16)
    lo = (r - mid.astype(F32)).astype(BF16)
    return hi, mid, lo


def _dot_x3(a, b):
    ah, al = _split2(a)
    bh, bl = _split2(b)
    d = functools.partial(jnp.dot, preferred_element_type=F32)
    return d(ah, bh) + (d(ah, bl) + d(al, bh))


def _sigmoid(x):
    return 1.0 / (1.0 + jnp.exp(-x))


def _softplus(x):
    return jnp.maximum(x, 0.0) + jnp.log(1.0 + jnp.exp(-jnp.abs(x)))


def _rms(x, w):
    return x * lax.rsqrt(jnp.mean(x * x, axis=-1, keepdims=True) + EPS) * w


def _tri(n):
    r = lax.broadcasted_iota(jnp.int32, (n, n), 0)
    c = lax.broadcasted_iota(jnp.int32, (n, n), 1)
    return jnp.where(r >= c, 1.0, 0.0).astype(BF16)


def _cumsum_rows(tri3, g):
    return jnp.dot(tri3, jnp.concatenate(_split3(g), axis=0), preferred_element_type=F32)


_E_AQ, _E_AF, _E_AI, _E_AG = 0, 512, 1024, 1536
_E_BQ, _E_BK, _E_BV, _E_BR, _E_LOW = 2048, 2304, 2560, 3072, 3584
EVEN_COLS = 3712
_QK_COLS = 768


_LEVEL_BLOCKS = (32, 16, 8)


def _level_tables():
    t = np.arange(CHUNK)
    out = []
    for blk in _LEVEL_BLOCKS:
        odd = (t // blk) % 2 == 1
        out.append((np.where(odd, (t // blk) * blk - 1, -1), np.where(~odd, (t // blk) * blk + blk - 1, -1)))
    return out


def _gla_chunk(c, q_ref, k_ref, v_ref, b2, o_ref, st_ref, a_ref):
    n_units = _QK_COLS // LANES
    rows = pl.ds(pl.multiple_of(c * CHUNK, CHUNK), CHUNK)
    lane = lax.broadcasted_iota(jnp.int32, (1, LANES), 1)
    trow = lax.broadcasted_iota(jnp.int32, (GROUP, LANES), 0)
    unit_heads = [((u, None),) for u in range(4)] + [((4, 0), (5, 64)), ((6, 0), (7, 64))]
    qs = [q_ref[rows, u * LANES:(u + 1) * LANES] for u in range(n_units)]
    ks = [k_ref[rows, u * LANES:(u + 1) * LANES] for u in range(n_units)]
    bs = [b2[:, u * LANES:(u + 1) * LANES] for u in range(n_units)]
    vs = [v_ref[rows, h * LANES:(h + 1) * LANES] for h in range(N_HEADS)]

    for u in range(n_units):
        for j in range(CHUNK // GROUP):
            qj = qs[u][j * GROUP:(j + 1) * GROUP]
            bj = bs[u][j * GROUP:(j + 1) * GROUP]
            for s in range(0, GROUP, 2):
                pair = [qj * jnp.exp2(jnp.where(trow >= s + i, bj - bj[s + i:s + i + 1, :], NEG_BIG))
                        for i in (0, 1)]
                r0 = (j * GROUP + s) * GROUP
                a_ref[r0:r0 + 2 * GROUP, u * LANES:(u + 1) * LANES] = (
                    jnp.concatenate(pair, axis=0).astype(BF16))

    zk = jnp.zeros((CHUNK, LANES), F32)
    scs = []
    for pr in range(n_units // 2):
        u0, u1 = 2 * pr, 2 * pr + 1
        if u0 < 4:
            keys = jnp.concatenate([jnp.concatenate([ks[u0], zk], axis=1),
                                    jnp.concatenate([zk, ks[u1]], axis=1)], axis=0)
        else:
            lo0, hi0 = jnp.where(lane < 64, ks[u0], 0.0), jnp.where(lane >= 64, ks[u0], 0.0)
            lo1, hi1 = jnp.where(lane < 64, ks[u1], 0.0), jnp.where(lane >= 64, ks[u1], 0.0)
            keys = jnp.concatenate([jnp.concatenate([lo0, zk], axis=1), jnp.concatenate([hi0, zk], axis=1),
                                    jnp.concatenate([zk, lo1], axis=1), jnp.concatenate([zk, hi1], axis=1)],
                                   axis=0)
        scs.append(_dot_nt(a_ref[:, u0 * LANES:(u1 + 1) * LANES], keys))

    tables = _level_tables()
    tq = lax.broadcasted_iota(jnp.int32, (CHUNK, CHUNK), 0)
    ts = lax.broadcasted_iota(jnp.int32, (CHUNK, CHUNK), 1)
    covers = [((tq // blk) % 2 == 1) & ((tq // blk) == (ts // blk) + 1) & ((ts // blk) % 2 == 0)
              for blk in _LEVEL_BLOCKS]
    s_offs, o_inter = {}, {}
    for u in range(n_units):
        q, k, b = qs[u], ks[u], bs[u]
        heads = unit_heads[u]
        if len(heads) == 2:
            qm = jnp.concatenate([jnp.where((lane & 64) == lo, q, 0.0) for _, lo in heads], axis=0)
            bm = jnp.concatenate([b, b], axis=0)
        else:
            qm, bm = q, b
        s_off = None
        for (qref, kref), cover in zip(tables, covers):
            qrows, krows = [], []
            for g in range(CHUNK // GROUP):
                sl = slice(g * GROUP, (g + 1) * GROUP)
                rq, rk = int(qref[g * GROUP]), int(kref[g * GROUP])
                qrows.append(jnp.exp2(b[sl] - b[rq:rq + 1, :]) if rq >= 0 else jnp.zeros((GROUP, LANES), F32))
                krows.append(k[sl] * jnp.exp2(b[rk:rk + 1, :] - b[sl]) if rk >= 0
                             else jnp.zeros((GROUP, LANES), F32))
            qfac = jnp.concatenate(qrows, axis=0)
            if len(heads) == 2:
                qfac = jnp.concatenate([qfac, qfac], axis=0)
                cov = jnp.concatenate([cover, cover], axis=0)
            else:
                cov = cover
            part = _dot_nt(qm * qfac, jnp.concatenate(krows, axis=0))
            s_off = jnp.where(cov, part, 0.0 if s_off is None else s_off)
        eb = jnp.exp2(bm)
        b_last = b[CHUNK - 1:CHUNK, :]
        kd = (k * jnp.exp2(b_last - b)).astype(BF16)
        dec = jnp.exp2(b_last)
        qe = qm * eb
        for i, (head, lo) in enumerate(heads):
            sl = slice(i * CHUNK, (i + 1) * CHUNK)
            st = st_ref[head]
            o_inter[head] = _dot_nt(qe[sl], st)
            st_ref[head] = st * dec + _dot_tn(vs[head], kd)
            s_offs[head] = s_off[sl]

    for u in range(n_units):
        sc = scs[u // 2]
        paired = len(unit_heads[u]) == 2
        width = LANES if paired else CHUNK
        c0 = (u % 2) * width
        scol = (lane & 63) if paired else lane[:, :CHUNK]
        groups = []
        for j in range(CHUNK // GROUP):
            acc = jnp.zeros((GROUP, width), F32)
            for s in range(GROUP):
                r0 = (j * GROUP + s) * GROUP
                acc = jnp.where(scol == j * GROUP + s, sc[r0:r0 + GROUP, c0:c0 + width], acc)
            groups.append(acc)
        p = jnp.concatenate(groups, axis=0)
        if paired:
            vstack = jnp.concatenate([vs[h] for h, _ in unit_heads[u]], axis=0)
        for head, lo in unit_heads[u]:
            if lo is None:
                o = o_inter[head] + _dot(s_offs[head] + p, vs[head])
            else:
                o = (o_inter[head] + _dot(s_offs[head], vs[head])
                     + _dot(jnp.where((lane & 64) == lo, p, 0.0), vstack))
            o_ref[rows, head * LANES:(head + 1) * LANES] = o


def _even_kernel(x_ref, nw_ref, win_ref, lb_ref, wup_ref, bup_ref, hw_ref, wout_ref, out_ref,
                 q_ref, k_ref, g_ref, v_ref, gate_ref, o_ref, st_ref, a_ref):
    tl = x_ref.shape[1]

    @pl.when(pl.program_id(1) == 0)
    def _():
        st_ref[...] = jnp.zeros_like(st_ref)

    x = x_ref[0]
    hn = _rms(x, nw_ref[...]).astype(BF16)

    def proj(c0, n):
        return jnp.dot(hn, win_ref[:, c0:c0 + n], preferred_element_type=F32)

    q_ref[:, 0:512] = proj(_E_AQ, 512)
    lb = lb_ref[...]
    f = lb + (1.0 - lb) * _sigmoid(proj(_E_AF, 512))
    k_ref[:, 0:512] = 1.0 - f
    g_ref[:, 0:512] = jnp.log(f)
    v_ref[:, 0:512] = proj(_E_AI, 512)
    ag = proj(_E_AG, 512)
    gate_ref[:, 0:512] = ag * _sigmoid(ag)
    q_ref[:, 512:768] = proj(_E_BQ, 256) * (64.0 ** -0.5)
    k_ref[:, 512:768] = proj(_E_BK, 256)
    v_ref[:, 512:1024] = proj(_E_BV, 512)
    br = proj(_E_BR, 512)
    gate_ref[:, 512:1024] = br * _sigmoid(br)
    z = _dot_x3(proj(_E_LOW, LANES), wup_ref[...]) + bup_ref[...]
    g_ref[:, 512:768] = -_softplus(-z) * (1.0 / GLA_TAU)

    tri = _tri(CHUNK)
    tri3 = jnp.concatenate([tri, tri, tri], axis=1)

    def chunk_body(c, carry):
        rows = pl.ds(pl.multiple_of(c * CHUNK, CHUNK), CHUNK)
        b2 = _cumsum_rows(tri3, g_ref[rows, :]) * LOG2E
        _gla_chunk(c, q_ref, k_ref, v_ref, b2, o_ref, st_ref, a_ref)
        return carry

    lax.fori_loop(0, tl // CHUNK, chunk_body, 0)

    hw = hw_ref[...]
    ys = []
    for head in range(N_HEADS):
        sl = slice(head * LANES, (head + 1) * LANES)
        oh = o_ref[:, sl]
        ys.append(oh * lax.rsqrt(jnp.mean(oh * oh, axis=-1, keepdims=True) + EPS)
                  * hw[:, sl] * gate_ref[:, sl])
    y = jnp.concatenate(ys, axis=1).astype(BF16)
    out_ref[0] = x + jnp.dot(y, wout_ref[...], preferred_element_type=F32)


def _const_spec(shape):
    nd = len(shape)
    return pl.BlockSpec(shape, lambda *_: (0,) * nd)


def _even_mixer(x, norm_w, w_in, lb, w_up, b_up, hgrn_norm_w, gla_norm_w, w_out):
    B, L, D = x.shape
    tl = min(MIX_TILE, L)
    s = np.cumsum([0, 512, 512, 512, 512, 256, 256, 512, GLA_RANK, 512])
    cols = [w_in[:, s[i]:s[i + 1]] for i in range(9)]
    low = jnp.pad(cols[7], ((0, 0), (0, LANES - GLA_RANK)))
    w_in_r = jnp.concatenate(cols[:7] + [cols[8], low], axis=1).astype(BF16)
    w_up_p = jnp.pad(w_up, ((0, LANES - GLA_RANK), (0, 0)))
    hw = jnp.concatenate([hgrn_norm_w, gla_norm_w])[None, :]
    return pl.pallas_call(
        _even_kernel,
        grid=(B, L // tl),
        in_specs=[
            pl.BlockSpec((1, tl, D), lambda b, l: (b, l, 0)),
            _const_spec((1, D)),
            _const_spec((D, EVEN_COLS)),
            _const_spec((1, 512)),
            _const_spec((LANES, 256)),
            _const_spec((1, 256)),
            _const_spec((1, MIX_WIDTH)),
            _const_spec((MIX_WIDTH, D)),
        ],
        out_specs=pl.BlockSpec((1, tl, D), lambda b, l: (b, l, 0)),
        out_shape=jax.ShapeDtypeStruct((B, L, D), F32),
        scratch_shapes=[
            pltpu.VMEM((tl, _QK_COLS), F32),
            pltpu.VMEM((tl, _QK_COLS), F32),
            pltpu.VMEM((tl, _QK_COLS), F32),
            pltpu.VMEM((tl, MIX_WIDTH), F32),
            pltpu.VMEM((tl, MIX_WIDTH), F32),
            pltpu.VMEM((tl, MIX_WIDTH), F32),
            pltpu.VMEM((N_HEADS, LANES, LANES), F32),
            pltpu.VMEM((CHUNK * GROUP, _QK_COLS), BF16),
        ],
        compiler_params=pltpu.CompilerParams(
            dimension_semantics=("arbitrary", "arbitrary"), vmem_limit_bytes=VMEM_LIMIT),
        name="even_mixer",
    )(x, norm_w[None, :], w_in_r, lb[None, :], w_up_p, b_up[None, :], hw, w_out.astype(BF16))


_O_Q, _O_QR, _O_K, _O_KR, _O_V, _O_G, _O_X, _O_GATE = 0, 256, 512, 768, 1024, 1536, 2048, 2560
ODD_COLS = 3072


def _odd_kernel(x_ref, nw_ref, win_ref, cos_ref, sin_ref, dmat_ref, qdec_ref, kdec_ref, sdec_ref,
                rw_ref, cw_ref, cb_ref, wri_ref, bri_ref, lam_ref, wout_ref, out_ref,
                q_ref, k_ref, v_ref, o_ref, xe_ref, a_ref, u_ref, st_ref, h_ref):
    tl = x_ref.shape[1]

    @pl.when(pl.program_id(1) == 0)
    def _():
        st_ref[...] = jnp.zeros_like(st_ref)
        h_ref[...] = jnp.zeros_like(h_ref)
        xe_ref[0:8, :] = jnp.zeros((8, LRU_WIDTH), F32)

    x = x_ref[0]
    hn = _rms(x, nw_ref[...]).astype(BF16)

    def proj(c0, n):
        return jnp.dot(hn, win_ref[:, c0:c0 + n], preferred_element_type=F32)

    cos = cos_ref[...]
    sin = sin_ref[...]
    q_ref[...] = proj(_O_Q, 256) * cos + proj(_O_QR, 256) * sin
    k_ref[...] = (proj(_O_K, 256) * cos + proj(_O_KR, 256) * sin) * (RET_DK ** -0.5)
    v_ref[...] = proj(_O_V, 512)

    xe_ref[8:8 + tl, :] = proj(_O_X, LRU_WIDTH)
    cw = cw_ref[...]
    xc = cb_ref[...]
    for j in range(LRU_CONV):
        xc = xc + xe_ref[5 + j:5 + j + tl, :] * cw[j:j + 1, :]
    xe_ref[0:8, :] = xe_ref[tl:tl + 8, :]
    ri = jnp.dot(xc.astype(BF16), wri_ref[...], preferred_element_type=F32) + bri_ref[...]
    r = _sigmoid(ri[:, :LRU_WIDTH])
    i = _sigmoid(ri[:, LRU_WIDTH:])
    a = jnp.exp(-LRU_C * r * _softplus(-lam_ref[...]))
    a_ref[...] = a
    u_ref[...] = jnp.sqrt(1.0 - a * a) * (i * xc)

    lane = lax.broadcasted_iota(jnp.int32, (1, LANES), 1)
    crow = lax.broadcasted_iota(jnp.int32, (CHUNK, LRU_WIDTH), 0)

    def chunk_body(c, carry):
        rows = pl.ds(pl.multiple_of(c * CHUNK, CHUNK), CHUNK)
        for head in range(RET_HEADS):
            unit, half = head // 2, head % 2
            ul = slice(unit * LANES, (unit + 1) * LANES)
            vl = slice(head * LANES, (head + 1) * LANES)
            own = (lane < 64) if half == 0 else (lane >= 64)
            q = jnp.where(own, q_ref[rows, ul], 0.0)
            k = k_ref[rows, ul]
            v = v_ref[rows, vl]
            st = st_ref[head]
            o = _dot_nt(q, st) * qdec_ref[head]
            o = o + _dot(_dot_nt(q, k) * dmat_ref[head], v)
            st_ref[head] = st * sdec_ref[head] + _dot_tn(v, k * kdec_ref[head])
            o_ref[rows, vl] = o
        ca = a_ref[rows, :]
        ch = u_ref[rows, :]
        d = 1
        while d < CHUNK:
            keep = crow >= d
            sa = jnp.where(keep, pltpu.roll(ca, d, axis=0), 1.0)
            sh = jnp.where(keep, pltpu.roll(ch, d, axis=0), 0.0)
            ch = ca * sh + ch
            ca = ca * sa
            d *= 2
        ch = ch + ca * h_ref[...]
        h_ref[...] = ch[CHUNK - 1:CHUNK, :]
        o_ref[rows, 512:1024] = ch
        return carry

    lax.fori_loop(0, tl // CHUNK, chunk_body, 0, unroll=2)

    rw = rw_ref[...]
    ys = []
    for head in range(RET_HEADS):
        sl = slice(head * LANES, (head + 1) * LANES)
        oh = o_ref[:, sl]
        oh = oh - jnp.mean(oh, axis=-1, keepdims=True)
        ys.append(oh * lax.rsqrt(jnp.mean(oh * oh, axis=-1, keepdims=True) + EPS) * rw[:, sl])
    cg = proj(_O_G, 512)
    y_c = jnp.concatenate(ys, axis=1) * (cg * _sigmoid(cg))
    dg = proj(_O_GATE, LRU_WIDTH)
    gelu = 0.5 * dg * (1.0 + jnp.tanh(np.sqrt(2.0 / np.pi) * (dg + 0.044715 * (dg * dg * dg))))
    y_d = o_ref[:, 512:1024] * gelu
    y = jnp.concatenate([y_c, y_d], axis=1).astype(BF16)
    out_ref[0] = x + jnp.dot(y, wout_ref[...], preferred_element_type=F32)


def _rot_cols(w):
    d = w.shape[0]
    w4 = w.reshape(d, RET_HEADS, 2, RET_DK // 2)
    return jnp.stack([-w4[:, :, 1], w4[:, :, 0]], axis=2).reshape(d, RET_HEADS * RET_DK)


def _odd_mixer(x, norm_w, w_in, ret_norm_w, conv_w, conv_b, w_r, b_r, w_i, b_i, lam, w_out):
    B, L, D = x.shape
    tl = min(MIX_TILE, L)
    s = np.cumsum([0, 256, 256, 512, 512, 512, 512])
    wq, wk, wv, wg, wx, wgate = [w_in[:, s[i]:s[i + 1]] for i in range(6)]
    w_in_r = jnp.concatenate([wq, _rot_cols(wq), wk, _rot_cols(wk), wv, wg, wx, wgate],
                             axis=1).astype(BF16)
    half = RET_DK // 2
    inv = ROPE_BASE ** (-jnp.arange(half, dtype=F32) / half)
    ang = jnp.arange(L, dtype=F32)[:, None] * inv[None, :]
    cos = jnp.tile(jnp.cos(ang), (1, 2 * RET_HEADS))
    sin = jnp.tile(jnp.sin(ang), (1, 2 * RET_HEADS))
    lg = jnp.log1p(-jnp.exp2(-5.0 - jnp.arange(RET_HEADS, dtype=F32)))[:, None, None]
    t = jnp.arange(CHUNK, dtype=F32)
    rel = t[:, None] - t[None, :]
    dmat = jnp.where(rel >= 0, jnp.exp(lg * jnp.maximum(rel, 0.0)), 0.0)
    qdec = jnp.broadcast_to(jnp.exp(lg * (t[None, :, None] + 1.0)), (RET_HEADS, CHUNK, LANES))
    kdec = jnp.broadcast_to(jnp.exp(lg * (CHUNK - 1.0 - t[None, :, None])), (RET_HEADS, CHUNK, LANES))
    sdec = jnp.broadcast_to(jnp.exp(lg * float(CHUNK)), (RET_HEADS, 1, LANES))
    eye = jnp.eye(LRU_BLOCKS, dtype=F32)
    bd = LRU_WIDTH // LRU_BLOCKS

    def blockdiag(w):
        return (eye[:, None, :, None] * w[:, :, None, :]).reshape(LRU_WIDTH, LRU_WIDTH)

    wri = jnp.concatenate([blockdiag(w_r), blockdiag(w_i)], axis=1).astype(BF16)
    bri = jnp.concatenate([b_r, b_i])[None, :]
    return pl.pallas_call(
        _odd_kernel,
        grid=(B, L // tl),
        in_specs=[
            pl.BlockSpec((1, tl, D), lambda b, l: (b, l, 0)),
            _const_spec((1, D)),
            _const_spec((D, ODD_COLS)),
            pl.BlockSpec((tl, 256), lambda b, l: (l, 0)),
            pl.BlockSpec((tl, 256), lambda b, l: (l, 0)),
            _const_spec((RET_HEADS, CHUNK, CHUNK)),
            _const_spec((RET_HEADS, CHUNK, LANES)),
            _const_spec((RET_HEADS, CHUNK, LANES)),
            _const_spec((RET_HEADS, 1, LANES)),
            _const_spec((1, 512)),
            _const_spec((LRU_CONV, LRU_WIDTH)),
            _const_spec((1, LRU_WIDTH)),
            _const_spec((LRU_WIDTH, 2 * LRU_WIDTH)),
            _const_spec((1, 2 * LRU_WIDTH)),
            _const_spec((1, LRU_WIDTH)),
            _const_spec((MIX_WIDTH, D)),
        ],
        out_specs=pl.BlockSpec((1, tl, D), lambda b, l: (b, l, 0)),
        out_shape=jax.ShapeDtypeStruct((B, L, D), F32),
        scratch_shapes=[
            pltpu.VMEM((tl, 256), F32),
            pltpu.VMEM((tl, 256), F32),
            pltpu.VMEM((tl, 512), F32),
            pltpu.VMEM((tl, MIX_WIDTH), F32),
            pltpu.VMEM((tl + 8, LRU_WIDTH), F32),
            pltpu.VMEM((tl, LRU_WIDTH), F32),
            pltpu.VMEM((tl, LRU_WIDTH), F32),
            pltpu.VMEM((RET_HEADS, LANES, LANES), F32),
            pltpu.VMEM((1, LRU_WIDTH), F32),
        ],
        compiler_params=pltpu.CompilerParams(
            dimension_semantics=("arbitrary", "arbitrary"), vmem_limit_bytes=VMEM_LIMIT),
        name="odd_mixer",
    )(x, norm_w[None, :], w_in_r, cos, sin, dmat, qdec, kdec, sdec, ret_norm_w[None, :],
      conv_w, conv_b[None, :], wri, bri, lam[None, :], w_out.astype(BF16))


_I_E1, _I_E2, _I_G1, _I_G2, _I_R1, _I_R2 = 0, 1, 2, 3, 4, 5
_LOGIT_E0 = MOE_GROUPS


def _pack_bf16_pairs(lo, hi):
    lo_b = pltpu.bitcast(lo.astype(BF16).astype(F32), jnp.uint32)
    hi_b = pltpu.bitcast(hi.astype(BF16).astype(F32), jnp.uint32)
    return (lo_b >> 16) | (hi_b & jnp.uint32(0xFFFF0000))


def _unpack_pairs_f32(w):
    return pltpu.bitcast(w << 16, F32), pltpu.bitcast(w & jnp.uint32(0xFFFF0000), F32)


def _unpack_bf16_pairs(w):
    lo, hi = _unpack_pairs_f32(w)
    return lo.astype(BF16), hi.astype(BF16)


def _router_kernel(x_ref, nw_ref, wr_ref, br_ref, hnp_ref, info_ref, info_t_ref, cnt_ref):
    tm = x_ref.shape[0]

    @pl.when(pl.program_id(0) == 0)
    def _():
        cnt_ref[...] = jnp.zeros_like(cnt_ref)

    hn = _rms(x_ref[...], nw_ref[...])
    half = D_MODEL // 2
    hnp_ref[...] = _pack_bf16_pairs(hn[:, :half], hn[:, half:])

    logits = _dot_x3(hn, wr_ref[...]) + br_ref[...]
    lane = lax.broadcasted_iota(jnp.int32, (tm, LANES), 1).astype(F32)

    def first_max(vals):
        m = jnp.max(vals, axis=-1, keepdims=True)
        idx = jnp.min(jnp.where(vals == m, lane, float(LANES)), axis=-1, keepdims=True)
        return m, idx

    gl = jnp.where(lane < MOE_GROUPS, logits, NEG_BIG)
    gmax, gidx = first_max(gl)
    g_w = 1.0 / jnp.sum(jnp.exp(gl - gmax), axis=-1, keepdims=True)
    e0 = _LOGIT_E0 + MOE_PER_GROUP * gidx
    el = jnp.where((lane >= e0) & (lane < e0 + MOE_PER_GROUP), logits, NEG_BIG)
    m1, i1 = first_max(el)
    m2, i2 = first_max(jnp.where(lane == i1, NEG_BIG, el))
    p2 = jnp.exp(m2 - m1)
    w1 = 1.0 / (1.0 + p2)
    e1 = i1 - _LOGIT_E0
    e2 = i2 - _LOGIT_E0

    oh1 = jnp.where(lane == e1, 1.0, 0.0)
    oh2 = jnp.where(lane == e2, 1.0, 0.0)
    tri = _tri(tm)
    pre1 = jnp.dot(tri, oh1.astype(BF16), preferred_element_type=F32)
    pre2 = jnp.dot(tri, oh2.astype(BF16), preferred_element_type=F32)
    carry = cnt_ref[...]
    c1 = pre1[tm - 1:tm, :]
    r1 = jnp.sum(oh1 * (pre1 + carry), axis=-1, keepdims=True) - 1.0
    r2 = jnp.sum(oh2 * (pre2 + (carry + c1)), axis=-1, keepdims=True) - 1.0
    cnt_ref[...] = carry + c1 + pre2[tm - 1:tm, :]

    info = jnp.zeros((tm, LANES), F32)
    for ln, val in ((_I_E1, e1), (_I_E2, e2), (_I_G1, g_w * w1), (_I_G2, g_w * (w1 * p2)),
                    (_I_R1, r1), (_I_R2, r2)):
        info = jnp.where(lane == float(ln), val, info)
    info_ref[...] = info
    info_t_ref[...] = info.T[0:GROUP, :]


def _router(x2, part, norm_w, w_group, b_group, w_expert, b_expert):
    D = x2.shape[1]
    T = x2.shape[0] // MOE_PARTS
    tm = min(TOK_TILE, T)
    nb = T // tm
    pad = LANES - MOE_GROUPS - MOE_EXPERTS
    wr = jnp.pad(jnp.concatenate([w_group, w_expert], axis=1), ((0, 0), (0, pad)))
    br = jnp.pad(jnp.concatenate([b_group, b_expert]), (0, pad))[None, :]
    return pl.pallas_call(
        _router_kernel,
        grid=(nb,),
        in_specs=[
            pl.BlockSpec((tm, D), lambda i: (part * nb + i, 0)),
            _const_spec((1, D)),
            _const_spec((D, LANES)),
            _const_spec((1, LANES)),
        ],
        out_specs=[
            pl.BlockSpec((tm, D // 2), lambda i: (i, 0)),
            pl.BlockSpec((tm, LANES), lambda i: (i, 0)),
            pl.BlockSpec((GROUP, tm), lambda i: (0, i)),
            _const_spec((1, LANES)),
        ],
        out_shape=[
            jax.ShapeDtypeStruct((T, D // 2), jnp.uint32),
            jax.ShapeDtypeStruct((T, LANES), F32),
            jax.ShapeDtypeStruct((GROUP, T), F32),
            jax.ShapeDtypeStruct((1, LANES), F32),
        ],
        compiler_params=pltpu.CompilerParams(
            dimension_semantics=("arbitrary",), vmem_limit_bytes=VMEM_LIMIT),
        name="moe_router",
    )(x2, norm_w[None, :], wr, br)


def _plan_kernel(cnt_ref, info_ref, pos_ref, te_ref, nx_ref, nv_ref):
    nt = te_ref.shape[0]
    e1 = info_ref[_I_E1:_I_E1 + 1, :]
    e2 = info_ref[_I_E2:_I_E2 + 1, :]
    s1 = jnp.zeros_like(e1)
    s2 = jnp.zeros_like(e2)
    tiles = [lax.div(cnt_ref[e] + (ROW_TILE - 1), jnp.int32(ROW_TILE)) for e in range(MOE_EXPERTS)]
    following = [None] * MOE_EXPERTS
    nxt = jnp.int32(-1)
    for e in reversed(range(MOE_EXPERTS)):
        following[e] = nxt
        nxt = jnp.where(tiles[e] > 0, jnp.int32(e), nxt)
    start = jnp.int32(0)
    for e in range(MOE_EXPERTS):
        seg = (start * ROW_TILE).astype(F32)
        s1 = jnp.where(e1 == float(e), seg, s1)
        s2 = jnp.where(e2 == float(e), seg, s2)

        def fill(i, carry, e=e):
            te_ref[i] = jnp.int32(e)
            nx_ref[i] = following[e]
            return carry

        lax.fori_loop(start, start + tiles[e], fill, 0)
        start = start + tiles[e]
    nv_ref[0] = start

    def fill_tail(i, carry):
        te_ref[i] = jnp.int32(MOE_EXPERTS - 1)
        nx_ref[i] = jnp.int32(-1)
        return carry

    lax.fori_loop(start, nt, fill_tail, 0)
    pos_ref[0:1, :] = (s1 + info_ref[_I_R1:_I_R1 + 1, :]).astype(jnp.int32)
    pos_ref[1:2, :] = (s2 + info_ref[_I_R2:_I_R2 + 1, :]).astype(jnp.int32)


def _plan(cnt, info_t, nt):
    T = info_t.shape[1]
    smem = pl.BlockSpec(memory_space=pltpu.SMEM)
    return pl.pallas_call(
        _plan_kernel,
        in_specs=[smem, pl.BlockSpec(memory_space=pltpu.VMEM)],
        out_specs=[pl.BlockSpec(memory_space=pltpu.VMEM), smem, smem, smem],
        out_shape=[
            jax.ShapeDtypeStruct((2, T), jnp.int32),
            jax.ShapeDtypeStruct((nt,), jnp.int32),
            jax.ShapeDtypeStruct((nt,), jnp.int32),
            jax.ShapeDtypeStruct((1,), jnp.int32),
        ],
        name="moe_plan",
    )(cnt, info_t)


def _sc_worker():
    return lax.axis_index("s") * SC_CORES + lax.axis_index("c")


def _dispatch(hnp, pos1, pos2, n_rows):
    T, W = hnp.shape
    per_w = T // SC_WORKERS
    n_chunks = per_w // SC_ROWS
    mesh = plsc.VectorSubcoreMesh(core_axis_name="c", subcore_axis_name="s")

    @functools.partial(
        pl.kernel, mesh=mesh,
        out_type=jax.ShapeDtypeStruct((n_rows, W), jnp.uint32),
        scratch_types=[
            pltpu.VMEM((SC_ROWS,), jnp.int32),
            pltpu.VMEM((SC_ROWS,), jnp.int32),
            pltpu.VMEM((SC_ROWS, W), jnp.uint32),
        ],
        name="moe_dispatch_sc",
    )
    def scatter(hnp_hbm, p1_hbm, p2_hbm, xs_hbm, i1_v, i2_v, rows_v):
        base0 = _sc_worker() * per_w

        @pl.loop(0, n_chunks)
        def _(j):
            base = pl.multiple_of(base0 + j * SC_ROWS, SC_ROWS)
            pltpu.sync_copy(p1_hbm.at[pl.ds(base, SC_ROWS)], i1_v)
            pltpu.sync_copy(p2_hbm.at[pl.ds(base, SC_ROWS)], i2_v)
            pltpu.sync_copy(hnp_hbm.at[pl.ds(base, SC_ROWS)], rows_v)
            pltpu.sync_copy(rows_v, xs_hbm.at[i1_v])
            pltpu.sync_copy(rows_v, xs_hbm.at[i2_v])

    return scatter(hnp, pos1, pos2)


def _gather_rows(table, idx):
    T = idx.shape[0]
    W = table.shape[1]
    per_w = T // SC_WORKERS
    rows = SC_ROWS
    n_chunks = per_w // rows
    mesh = plsc.VectorSubcoreMesh(core_axis_name="c", subcore_axis_name="s")

    @functools.partial(
        pl.kernel, mesh=mesh,
        out_type=jax.ShapeDtypeStruct((T, W), table.dtype),
        scratch_types=[
            pltpu.VMEM((rows,), jnp.int32),
            pltpu.VMEM((rows, W), table.dtype),
        ],
        name="moe_gather_sc",
    )
    def gather(table_hbm, idx_hbm, out_hbm, i_v, rows_v):
        base0 = _sc_worker() * per_w

        @pl.loop(0, n_chunks)
        def _(j):
            base = pl.multiple_of(base0 + j * rows, rows)
            pltpu.sync_copy(idx_hbm.at[pl.ds(base, rows)], i_v)
            pltpu.sync_copy(table_hbm.at[i_v], rows_v)
            pltpu.sync_copy(rows_v, out_hbm.at[pl.ds(base, rows)])

    return gather(table, idx)


def _experts_kernel(te_ref, nx_ref, nv_ref, xs_ref, w1_hbm, w3_hbm, w2_hbm, ys_ref,
                    w1f, w3f, w2f, w1b, w3b, w2b, sem, *, layer):
    i = pl.program_id(0)

    def fetch(e):
        row = e + layer * MOE_EXPERTS
        return [pltpu.make_async_copy(w1_hbm.at[row], w1f, sem),
                pltpu.make_async_copy(w3_hbm.at[row], w3f, sem),
                pltpu.make_async_copy(w2_hbm.at[row], w2f, sem)]

    @pl.when(i < nv_ref[0])
    def _():
        @pl.when(i == 0)
        def _():
            for cp in fetch(te_ref[0]):
                cp.start()

        @pl.when((i == 0) | (te_ref[i] != te_ref[jnp.maximum(i - 1, 0)]))
        def _():
            for cp in fetch(te_ref[i]):
                cp.wait()
            w1b[...] = w1f[...].astype(BF16)
            w3b[...] = w3f[...].astype(BF16)
            w2b[...] = w2f[...].astype(BF16)

            @pl.when(nx_ref[i] >= 0)
            def _():
                for cp in fetch(nx_ref[i]):
                    cp.start()

        lo, hi = _unpack_bf16_pairs(xs_ref[...])
        half = D_MODEL // 2
        d = functools.partial(jnp.dot, preferred_element_type=F32)
        h1 = d(lo, w1b[0:half, :]) + d(hi, w1b[half:, :])
        h3 = d(lo, w3b[0:half, :]) + d(hi, w3b[half:, :])
        hid = (h1 * _sigmoid(h1)) * h3
        y = d(hid.astype(BF16), w2b[...])
        ys_ref[...] = _pack_bf16_pairs(y[:, :half], y[:, half:])

    @pl.when(i >= nv_ref[0])
    def _():
        ys_ref[...] = jnp.zeros_like(ys_ref)


def _experts(xs, tile_expert, next_expert, n_valid, w1, w3, w2, layer):
    n_rows, W = xs.shape
    nt = n_rows // ROW_TILE
    D, F = w1.shape[1], w1.shape[2]

    def row_map(i, te, nx, nv):
        return (i, 0)

    hbm = pl.BlockSpec(memory_space=pl.ANY)
    return pl.pallas_call(
        functools.partial(_experts_kernel, layer=layer),
        grid_spec=pltpu.PrefetchScalarGridSpec(
            num_scalar_prefetch=3,
            grid=(nt,),
            in_specs=[pl.BlockSpec((ROW_TILE, W), row_map), hbm, hbm, hbm],
            out_specs=pl.BlockSpec((ROW_TILE, D // 2), row_map),
            scratch_shapes=[
                pltpu.VMEM((D, F), F32),
                pltpu.VMEM((D, F), F32),
                pltpu.VMEM((F, D), F32),
                pltpu.VMEM((D, F), BF16),
                pltpu.VMEM((D, F), BF16),
                pltpu.VMEM((F, D), BF16),
                pltpu.SemaphoreType.DMA(()),
            ],
        ),
        out_shape=jax.ShapeDtypeStruct((n_rows, D // 2), jnp.uint32),
        compiler_params=pltpu.CompilerParams(
            dimension_semantics=("arbitrary",), vmem_limit_bytes=VMEM_LIMIT),
        name="moe_experts",
    )(tile_expert, next_expert, n_valid, xs, w1, w3, w2)


def _combine_kernel(x_ref, info_ref, fw_ref, y1_ref, y2_ref, out_ref, *, final):
    info = info_ref[...]
    g1 = info[:, _I_G1:_I_G1 + 1]
    g2 = info[:, _I_G2:_I_G2 + 1]
    y1 = jnp.concatenate(_unpack_pairs_f32(y1_ref[...]), axis=1)
    y2 = jnp.concatenate(_unpack_pairs_f32(y2_ref[...]), axis=1)
    y = x_ref[...] + (g1 * y1 + g2 * y2)
    if final:
        y = _rms(y, fw_ref[...])
    out_ref[...] = y


def _combine(x2, part, info, y1, y2, final_w, final):
    T, D = x2.shape
    tp = T // MOE_PARTS
    tm = min(TOK_TILE, tp)
    nb = tp // tm
    x_spec = pl.BlockSpec((tm, D), lambda i: (part * nb + i, 0))
    half_spec = pl.BlockSpec((tm, D // 2), lambda i: (i, 0))
    return pl.pallas_call(
        functools.partial(_combine_kernel, final=final),
        grid=(nb,),
        in_specs=[x_spec, pl.BlockSpec((tm, LANES), lambda i: (i, 0)), _const_spec((1, D)),
                  half_spec, half_spec],
        out_specs=x_spec,
        out_shape=jax.ShapeDtypeStruct((T, D), F32),
        input_output_aliases={0: 0},
        compiler_params=pltpu.CompilerParams(
            dimension_semantics=("arbitrary",), vmem_limit_bytes=VMEM_LIMIT),
        name="moe_combine",
    )(x2, info, final_w[None, :], y1, y2)


def _hier_moe(x2, norm_w, w_group, b_group, w_expert, b_expert, w1, w3, w2, layer, final_w, final):
    T, D = x2.shape
    tp = T // MOE_PARTS
    nt = (2 * tp) // ROW_TILE + MOE_EXPERTS
    routed = []
    for part in range(MOE_PARTS):
        hnp, info, info_t, cnt = _router(x2, part, norm_w, w_group, b_group, w_expert, b_expert)
        pos, tile_expert, next_expert, n_valid = _plan(cnt[0].astype(jnp.int32), info_t, nt)
        pos1, pos2 = pos[0], pos[1]
        xs = _dispatch(hnp, pos1, pos2, nt * ROW_TILE)
        ys = _experts(xs, tile_expert, next_expert, n_valid, w1, w3, w2, layer)
        routed.append((info, _gather_rows(ys, pos1), _gather_rows(ys, pos2)))
    out = x2
    for part, (info, y1, y2) in enumerate(routed):
        out = _combine(out, part, info, y1, y2, final_w, final)
    return out


def kernel(x, norm_mix, norm_ffn, norm_final, even_w_in, hgrn_lb_logits, hgrn_norm_w, gla_w_up,
           gla_b_up, gla_norm_w, even_w_out, odd_w_in, ret_norm_w, conv_w, conv_b, rglru_w_r,
           rglru_b_r, rglru_w_i, rglru_b_i, rglru_lambda, odd_w_out, moe_w_group, moe_b_group,
           moe_w_expert, moe_b_expert, moe_w1, moe_w3, moe_w2):
    B, L, D = x.shape
    depth = norm_mix.shape[0]
    lb_table = jnp.cumsum(jax.nn.softmax(hgrn_lb_logits.astype(F32), axis=0), axis=0)
    w1 = moe_w1.reshape((-1,) + moe_w1.shape[2:])
    w3 = moe_w3.reshape((-1,) + moe_w3.shape[2:])
    w2 = moe_w2.reshape((-1,) + moe_w2.shape[2:])
    for l in range(depth):
        j = l // 2
        if l % 2 == 0:
            x = _even_mixer(x, norm_mix[l], even_w_in[j], lb_table[l], gla_w_up[j], gla_b_up[j],
                            hgrn_norm_w[j], gla_norm_w[j], even_w_out[j])
        else:
            x = _odd_mixer(x, norm_mix[l], odd_w_in[j], ret_norm_w[j], conv_w[j], conv_b[j],
                           rglru_w_r[j], rglru_b_r[j], rglru_w_i[j], rglru_b_i[j], rglru_lambda[j],
                           odd_w_out[j])
        x = _hier_moe(x.reshape(B * L, D), norm_ffn[l], moe_w_group[l], moe_b_group[l],
                      moe_w_expert[l], moe_b_expert[l], w1, w3, w2, l,
                      norm_final, l == depth - 1).reshape(B, L, D)
    return x
```

```python
import functools

import numpy as np
import jax
import jax.numpy as jnp
from jax import lax
from jax.experimental import pallas as pl
from jax.experimental.pallas import tpu as pltpu
from jax.experimental.pallas import tpu_sc as plsc

F32 = jnp.float32
BF16 = jnp.bfloat16

EPS = 1e-6
CHUNK = 64
GROUP = 8
LOG2E = 1.4426950408889634
LANES = 128
D_MODEL = 1024
MIX_WIDTH = 1024
N_HEADS = 8
GLA_RANK = 16
GLA_TAU = 16.0
RET_HEADS = 4
RET_DK = 64
ROPE_BASE = 10000.0
LRU_WIDTH = 512
LRU_BLOCKS = 8
LRU_CONV = 4
LRU_C = 8.0
MOE_GROUPS = 4
MOE_PER_GROUP = 8
MOE_EXPERTS = MOE_GROUPS * MOE_PER_GROUP
MOE_DFF = 512
NEG_BIG = -1e30

MIX_TILE = 512
TOK_TILE = 512
ROW_TILE = 512
MOE_PARTS = 1
SC_CORES = 2
SC_WORKERS = 32
SC_ROWS = 128
VMEM_LIMIT = 56 * 1024 * 1024


def _dot(a, b):
    return jnp.dot(a.astype(BF16), b.astype(BF16), preferred_element_type=F32)


def _dot_nt(a, b):
    return lax.dot_general(a.astype(BF16), b.astype(BF16), (((1,), (1,)), ((), ())),
                           preferred_element_type=F32)


def _dot_tn(a, b):
    return lax.dot_general(a.astype(BF16), b.astype(BF16), (((0,), (0,)), ((), ())),
                           preferred_element_type=F32)


def _split2(x):
    hi = x.astype(BF16)
    lo = (x - hi.astype(F32)).astype(BF16)
    return hi, lo


def _split3(x):
    hi = x.astype(BF16)
    r = x - hi.astype(F32)
    mid = r.astype(BF16)
    lo = (r - mid.astype(F32)).astype(BF16)
    return hi, mid, lo


def _dot_x3(a, b):
    ah, al = _split2(a)
    bh, bl = _split2(b)
    d = functools.partial(jnp.dot, preferred_element_type=F32)
    return d(ah, bh) + (d(ah, bl) + d(al, bh))


def _sigmoid(x):
    return 1.0 / (1.0 + jnp.exp(-x))


def _softplus(x):
    return jnp.maximum(x, 0.0) + jnp.log(1.0 + jnp.exp(-jnp.abs(x)))


def _rms(x, w):
    return x * lax.rsqrt(jnp.mean(x * x, axis=-1, keepdims=True) + EPS) * w


def _tri(n):
    r = lax.broadcasted_iota(jnp.int32, (n, n), 0)
    c = lax.broadcasted_iota(jnp.int32, (n, n), 1)
    return jnp.where(r >= c, 1.0, 0.0).astype(BF16)


def _cumsum_rows(tri3, g):
    return jnp.dot(tri3, jnp.concatenate(_split3(g), axis=0), preferred_element_type=F32)


_E_AQ, _E_AF, _E_AI, _E_AG = 0, 512, 1024, 1536
_E_BQ, _E_BK, _E_BV, _E_BR, _E_LOW = 2048, 2304, 2560, 3072, 3584
EVEN_COLS = 3712
_QK_COLS = 768


_LEVEL_BLOCKS = (32, 16, 8)


def _level_tables():
    t = np.arange(CHUNK)
    out = []
    for blk in _LEVEL_BLOCKS:
        odd = (t // blk) % 2 == 1
        out.append((np.where(odd, (t // blk) * blk - 1, -1), np.where(~odd, (t // blk) * blk + blk - 1, -1)))
    return out


def _gla_chunk(c, q_ref, k_ref, v_ref, b2, o_ref, st_ref, a_ref):
    n_units = _QK_COLS // LANES
    rows = pl.ds(pl.multiple_of(c * CHUNK, CHUNK), CHUNK)
    lane = lax.broadcasted_iota(jnp.int32, (1, LANES), 1)
    trow = lax.broadcasted_iota(jnp.int32, (GROUP, LANES), 0)
    unit_heads = [((u, None),) for u in range(4)] + [((4, 0), (5, 64)), ((6, 0), (7, 64))]
    qs = [q_ref[rows, u * LANES:(u + 1) * LANES] for u in range(n_units)]
    ks = [k_ref[rows, u * LANES:(u + 1) * LANES] for u in range(n_units)]
    bs = [b2[:, u * LANES:(u + 1) * LANES] for u in range(n_units)]
    vs = [v_ref[rows, h * LANES:(h + 1) * LANES] for h in range(N_HEADS)]

    for u in range(n_units):
        for j in range(CHUNK // GROUP):
            qj = qs[u][j * GROUP:(j + 1) * GROUP]
            bj = bs[u][j * GROUP:(j + 1) * GROUP]
            for s in range(0, GROUP, 2):
                pair = [qj * jnp.exp2(jnp.where(trow >= s + i, bj - bj[s + i:s + i + 1, :], NEG_BIG))
                        for i in (0, 1)]
                r0 = (j * GROUP + s) * GROUP
                a_ref[r0:r0 + 2 * GROUP, u * LANES:(u + 1) * LANES] = (
                    jnp.concatenate(pair, axis=0).astype(BF16))

    zk = jnp.zeros((CHUNK, LANES), F32)
    scs = []
    for pr in range(n_units // 2):
        u0, u1 = 2 * pr, 2 * pr + 1
        if u0 < 4:
            keys = jnp.concatenate([jnp.concatenate([ks[u0], zk], axis=1),
                                    jnp.concatenate([zk, ks[u1]], axis=1)], axis=0)
        else:
            lo0, hi0 = jnp.where(lane < 64, ks[u0], 0.0), jnp.where(lane >= 64, ks[u0], 0.0)
            lo1, hi1 = jnp.where(lane < 64, ks[u1], 0.0), jnp.where(lane >= 64, ks[u1], 0.0)
            keys = jnp.concatenate([jnp.concatenate([lo0, zk], axis=1), jnp.concatenate([hi0, zk], axis=1),
                                    jnp.concatenate([zk, lo1], axis=1), jnp.concatenate([zk, hi1], axis=1)],
                                   axis=0)
        scs.append(_dot_nt(a_ref[:, u0 * LANES:(u1 + 1) * LANES], keys))

    tables = _level_tables()
    tq = lax.broadcasted_iota(jnp.int32, (CHUNK, CHUNK), 0)
    ts = lax.broadcasted_iota(jnp.int32, (CHUNK, CHUNK), 1)
    covers = [((tq // blk) % 2 == 1) & ((tq // blk) == (ts // blk) + 1) & ((ts // blk) % 2 == 0)
              for blk in _LEVEL_BLOCKS]
    s_offs, o_inter = {}, {}
    for u in range(n_units):
        q, k, b = qs[u], ks[u], bs[u]
        heads = unit_heads[u]
        if len(heads) == 2:
            qm = jnp.concatenate([jnp.where((lane & 64) == lo, q, 0.0) for _, lo in heads], axis=0)
            bm = jnp.concatenate([b, b], axis=0)
        else:
            qm, bm = q, b
        s_off = None
        for (qref, kref), cover in zip(tables, covers):
            qrows, krows = [], []
            for g in range(CHUNK // GROUP):
                sl = slice(g * GROUP, (g + 1) * GROUP)
                rq, rk = int(qref[g * GROUP]), int(kref[g * GROUP])
                qrows.append(jnp.exp2(b[sl] - b[rq:rq + 1, :]) if rq >= 0 else jnp.zeros((GROUP, LANES), F32))
                krows.append(k[sl] * jnp.exp2(b[rk:rk + 1, :] - b[sl]) if rk >= 0
                             else jnp.zeros((GROUP, LANES), F32))
            qfac = jnp.concatenate(qrows, axis=0)
            if len(heads) == 2:
                qfac = jnp.concatenate([qfac, qfac], axis=0)
                cov = jnp.concatenate([cover, cover], axis=0)
            else:
                cov = cover
            part = _dot_nt(qm * qfac, jnp.concatenate(krows, axis=0))
            s_off = jnp.where(cov, part, 0.0 if s_off is None else s_off)
        eb = jnp.exp2(bm)
        b_last = b[CHUNK - 1:CHUNK, :]
        kd = (k * jnp.exp2(b_last - b)).astype(BF16)
        dec = jnp.exp2(b_last)
        qe = qm * eb
        for i, (head, lo) in enumerate(heads):
            sl = slice(i * CHUNK, (i + 1) * CHUNK)
            st = st_ref[head]
            o_inter[head] = _dot_nt(qe[sl], st)
            st_ref[head] = st * dec + _dot_tn(vs[head], kd)
            s_offs[head] = s_off[sl]

    for u in range(n_units):
        sc = scs[u // 2]
        paired = len(unit_heads[u]) == 2
        width = LANES if paired else CHUNK
        c0 = (u % 2) * width
        scol = (lane & 63) if paired else lane[:, :CHUNK]
        groups = []
        for j in range(CHUNK // GROUP):
            acc = jnp.zeros((GROUP, width), F32)
            for s in range(GROUP):
                r0 = (j * GROUP + s) * GROUP
                acc = jnp.where(scol == j * GROUP + s, sc[r0:r0 + GROUP, c0:c0 + width], acc)
            groups.append(acc)
        p = jnp.concatenate(groups, axis=0)
        if paired:
            vstack = jnp.concatenate([vs[h] for h, _ in unit_heads[u]], axis=0)
        for head, lo in unit_heads[u]:
            if lo is None:
                o = o_inter[head] + _dot(s_offs[head] + p, vs[head])
            else:
                o = (o_inter[head] + _dot(s_offs[head], vs[head])
                     + _dot(jnp.where((lane & 64) == lo, p, 0.0), vstack))
            o_ref[rows, head * LANES:(head + 1) * LANES] = o


def _even_kernel(x_ref, nw_ref, win_ref, lb_ref, wup_ref, bup_ref, hw_ref, wout_ref, out_ref,
                 q_ref, k_ref, g_ref, v_ref, gate_ref, o_ref, st_ref, a_ref):
    tl = x_ref.shape[1]

    @pl.when(pl.program_id(1) == 0)
    def _():
        st_ref[...] = jnp.zeros_like(st_ref)

    x = x_ref[0]
    hn = _rms(x, nw_ref[...]).astype(BF16)

    def proj(c0, n):
        return jnp.dot(hn, win_ref[:, c0:c0 + n], preferred_element_type=F32)

    q_ref[:, 0:512] = proj(_E_AQ, 512)
    lb = lb_ref[...]
    f = lb + (1.0 - lb) * _sigmoid(proj(_E_AF, 512))
    k_ref[:, 0:512] = 1.0 - f
    g_ref[:, 0:512] = jnp.log(f)
    v_ref[:, 0:512] = proj(_E_AI, 512)
    ag = proj(_E_AG, 512)
    gate_ref[:, 0:512] = ag * _sigmoid(ag)
    q_ref[:, 512:768] = proj(_E_BQ, 256) * (64.0 ** -0.5)
    k_ref[:, 512:768] = proj(_E_BK, 256)
    v_ref[:, 512:1024] = proj(_E_BV, 512)
    br = proj(_E_BR, 512)
    gate_ref[:, 512:1024] = br * _sigmoid(br)
    z = _dot_x3(proj(_E_LOW, LANES), wup_ref[...]) + bup_ref[...]
    g_ref[:, 512:768] = -_softplus(-z) * (1.0 / GLA_TAU)

    tri = _tri(CHUNK)
    tri3 = jnp.concatenate([tri, tri, tri], axis=1)

    def chunk_body(c, carry):
        rows = pl.ds(pl.multiple_of(c * CHUNK, CHUNK), CHUNK)
        b2 = _cumsum_rows(tri3, g_ref[rows, :]) * LOG2E
        _gla_chunk(c, q_ref, k_ref, v_ref, b2, o_ref, st_ref, a_ref)
        return carry

    lax.fori_loop(0, tl // CHUNK, chunk_body, 0)

    hw = hw_ref[...]
    ys = []
    for head in range(N_HEADS):
        sl = slice(head * LANES, (head + 1) * LANES)
        oh = o_ref[:, sl]
        ys.append(oh * lax.rsqrt(jnp.mean(oh * oh, axis=-1, keepdims=True) + EPS)
                  * hw[:, sl] * gate_ref[:, sl])
    y = jnp.concatenate(ys, axis=1).astype(BF16)
    out_ref[0] = x + jnp.dot(y, wout_ref[...], preferred_element_type=F32)


def _const_spec(shape):
    nd = len(shape)
    return pl.BlockSpec(shape, lambda *_: (0,) * nd)


def _even_mixer(x, norm_w, w_in, lb, w_up, b_up, hgrn_norm_w, gla_norm_w, w_out):
    B, L, D = x.shape
    tl = min(MIX_TILE, L)
    s = np.cumsum([0, 512, 512, 512, 512, 256, 256, 512, GLA_RANK, 512])
    cols = [w_in[:, s[i]:s[i + 1]] for i in range(9)]
    low = jnp.pad(cols[7], ((0, 0), (0, LANES - GLA_RANK)))
    w_in_r = jnp.concatenate(cols[:7] + [cols[8], low], axis=1).astype(BF16)
    w_up_p = jnp.pad(w_up, ((0, LANES - GLA_RANK), (0, 0)))
    hw = jnp.concatenate([hgrn_norm_w, gla_norm_w])[None, :]
    return pl.pallas_call(
        _even_kernel,
        grid=(B, L // tl),
        in_specs=[
            pl.BlockSpec((1, tl, D), lambda b, l: (b, l, 0)),
            _const_spec((1, D)),
            _const_spec((D, EVEN_COLS)),
            _const_spec((1, 512)),
            _const_spec((LANES, 256)),
            _const_spec((1, 256)),
            _const_spec((1, MIX_WIDTH)),
            _const_spec((MIX_WIDTH, D)),
        ],
        out_specs=pl.BlockSpec((1, tl, D), lambda b, l: (b, l, 0)),
        out_shape=jax.ShapeDtypeStruct((B, L, D), F32),
        scratch_shapes=[
            pltpu.VMEM((tl, _QK_COLS), F32),
            pltpu.VMEM((tl, _QK_COLS), F32),
            pltpu.VMEM((tl, _QK_COLS), F32),
            pltpu.VMEM((tl, MIX_WIDTH), F32),
            pltpu.VMEM((tl, MIX_WIDTH), F32),
            pltpu.VMEM((tl, MIX_WIDTH), F32),
            pltpu.VMEM((N_HEADS, LANES, LANES), F32),
            pltpu.VMEM((CHUNK * GROUP, _QK_COLS), BF16),
        ],
        compiler_params=pltpu.CompilerParams(
            dimension_semantics=("arbitrary", "arbitrary"), vmem_limit_bytes=VMEM_LIMIT),
        name="even_mixer",
    )(x, norm_w[None, :], w_in_r, lb[None, :], w_up_p, b_up[None, :], hw, w_out.astype(BF16))


_O_Q, _O_QR, _O_K, _O_KR, _O_V, _O_G, _O_X, _O_GATE = 0, 256, 512, 768, 1024, 1536, 2048, 2560
ODD_COLS = 3072


def _odd_kernel(x_ref, nw_ref, win_ref, cos_ref, sin_ref, dmat_ref, qdec_ref, kdec_ref, sdec_ref,
                rw_ref, cw_ref, cb_ref, wri_ref, bri_ref, lam_ref, wout_ref, out_ref,
                q_ref, k_ref, v_ref, o_ref, xe_ref, a_ref, u_ref, st_ref, h_ref):
    tl = x_ref.shape[1]

    @pl.when(pl.program_id(1) == 0)
    def _():
        st_ref[...] = jnp.zeros_like(st_ref)
        h_ref[...] = jnp.zeros_like(h_ref)
        xe_ref[0:8, :] = jnp.zeros((8, LRU_WIDTH), F32)

    x = x_ref[0]
    hn = _rms(x, nw_ref[...]).astype(BF16)

    def proj(c0, n):
        return jnp.dot(hn, win_ref[:, c0:c0 + n], preferred_element_type=F32)

    cos = cos_ref[...]
    sin = sin_ref[...]
    q_ref[...] = proj(_O_Q, 256) * cos + proj(_O_QR, 256) * sin
    k_ref[...] = (proj(_O_K, 256) * cos + proj(_O_KR, 256) * sin) * (RET_DK ** -0.5)
    v_ref[...] = proj(_O_V, 512)

    xe_ref[8:8 + tl, :] = proj(_O_X, LRU_WIDTH)
    cw = cw_ref[...]
    xc = cb_ref[...]
    for j in range(LRU_CONV):
        xc = xc + xe_ref[5 + j:5 + j + tl, :] * cw[j:j + 1, :]
    xe_ref[0:8, :] = xe_ref[tl:tl + 8, :]
    ri = jnp.dot(xc.astype(BF16), wri_ref[...], preferred_element_type=F32) + bri_ref[...]
    r = _sigmoid(ri[:, :LRU_WIDTH])
    i = _sigmoid(ri[:, LRU_WIDTH:])
    a = jnp.exp(-LRU_C * r * _softplus(-lam_ref[...]))
    a_ref[...] = a
    u_ref[...] = jnp.sqrt(1.0 - a * a) * (i * xc)

    lane = lax.broadcasted_iota(jnp.int32, (1, LANES), 1)
    crow = lax.broadcasted_iota(jnp.int32, (CHUNK, LRU_WIDTH), 0)

    def chunk_body(c, carry):
        rows = pl.ds(pl.multiple_of(c * CHUNK, CHUNK), CHUNK)
        for head in range(RET_HEADS):
            unit, half = head // 2, head % 2
            ul = slice(unit * LANES, (unit + 1) * LANES)
            vl = slice(head * LANES, (head + 1) * LANES)
            own = (lane < 64) if half == 0 else (lane >= 64)
            q = jnp.where(own, q_ref[rows, ul], 0.0)
            k = k_ref[rows, ul]
            v = v_ref[rows, vl]
            st = st_ref[head]
            o = _dot_nt(q, st) * qdec_ref[head]
            o = o + _dot(_dot_nt(q, k) * dmat_ref[head], v)
            st_ref[head] = st * sdec_ref[head] + _dot_tn(v, k * kdec_ref[head])
            o_ref[rows, vl] = o
        ca = a_ref[rows, :]
        ch = u_ref[rows, :]
        d = 1
        while d < CHUNK:
            keep = crow >= d
            sa = jnp.where(keep, pltpu.roll(ca, d, axis=0), 1.0)
            sh = jnp.where(keep, pltpu.roll(ch, d, axis=0), 0.0)
            ch = ca * sh + ch
            ca = ca * sa
            d *= 2
        ch = ch + ca * h_ref[...]
        h_ref[...] = ch[CHUNK - 1:CHUNK, :]
        o_ref[rows, 512:1024] = ch
        return carry

    lax.fori_loop(0, tl // CHUNK, chunk_body, 0, unroll=2)

    rw = rw_ref[...]
    ys = []
    for head in range(RET_HEADS):
        sl = slice(head * LANES, (head + 1) * LANES)
        oh = o_ref[:, sl]
        oh = oh - jnp.mean(oh, axis=-1, keepdims=True)
        ys.append(oh * lax.rsqrt(jnp.mean(oh * oh, axis=-1, keepdims=True) + EPS) * rw[:, sl])
    cg = proj(_O_G, 512)
    y_c = jnp.concatenate(ys, axis=1) * (cg * _sigmoid(cg))
    dg = proj(_O_GATE, LRU_WIDTH)
    gelu = 0.5 * dg * (1.0 + jnp.tanh(np.sqrt(2.0 / np.pi) * (dg + 0.044715 * (dg * dg * dg))))
    y_d = o_ref[:, 512:1024] * gelu
    y = jnp.concatenate([y_c, y_d], axis=1).astype(BF16)
    out_ref[0] = x + jnp.dot(y, wout_ref[...], preferred_element_type=F32)


def _rot_cols(w):
    d = w.shape[0]
    w4 = w.reshape(d, RET_HEADS, 2, RET_DK // 2)
    return jnp.stack([-w4[:, :, 1], w4[:, :, 0]], axis=2).reshape(d, RET_HEADS * RET_DK)


def _odd_mixer(x, norm_w, w_in, ret_norm_w, conv_w, conv_b, w_r, b_r, w_i, b_i, lam, w_out):
    B, L, D = x.shape
    tl = min(MIX_TILE, L)
    s = np.cumsum([0, 256, 256, 512, 512, 512, 512])
    wq, wk, wv, wg, wx, wgate = [w_in[:, s[i]:s[i + 1]] for i in range(6)]
    w_in_r = jnp.concatenate([wq, _rot_cols(wq), wk, _rot_cols(wk), wv, wg, wx, wgate],
                             axis=1).astype(BF16)
    half = RET_DK // 2
    inv = ROPE_BASE ** (-jnp.arange(half, dtype=F32) / half)
    ang = jnp.arange(L, dtype=F32)[:, None] * inv[None, :]
    cos = jnp.tile(jnp.cos(ang), (1, 2 * RET_HEADS))
    sin = jnp.tile(jnp.sin(ang), (1, 2 * RET_HEADS))
    lg = jnp.log1p(-jnp.exp2(-5.0 - jnp.arange(RET_HEADS, dtype=F32)))[:, None, None]
    t = jnp.arange(CHUNK, dtype=F32)
    rel = t[:, None] - t[None, :]
    dmat = jnp.where(rel >= 0, jnp.exp(lg * jnp.maximum(rel, 0.0)), 0.0)
    qdec = jnp.broadcast_to(jnp.exp(lg * (t[None, :, None] + 1.0)), (RET_HEADS, CHUNK, LANES))
    kdec = jnp.broadcast_to(jnp.exp(lg * (CHUNK - 1.0 - t[None, :, None])), (RET_HEADS, CHUNK, LANES))
    sdec = jnp.broadcast_to(jnp.exp(lg * float(CHUNK)), (RET_HEADS, 1, LANES))
    eye = jnp.eye(LRU_BLOCKS, dtype=F32)
    bd = LRU_WIDTH // LRU_BLOCKS

    def blockdiag(w):
        return (eye[:, None, :, None] * w[:, :, None, :]).reshape(LRU_WIDTH, LRU_WIDTH)

    wri = jnp.concatenate([blockdiag(w_r), blockdiag(w_i)], axis=1).astype(BF16)
    bri = jnp.concatenate([b_r, b_i])[None, :]
    return pl.pallas_call(
        _odd_kernel,
        grid=(B, L // tl),
        in_specs=[
            pl.BlockSpec((1, tl, D), lambda b, l: (b, l, 0)),
            _const_spec((1, D)),
            _const_spec((D, ODD_COLS)),
            pl.BlockSpec((tl, 256), lambda b, l: (l, 0)),
            pl.BlockSpec((tl, 256), lambda b, l: (l, 0)),
            _const_spec((RET_HEADS, CHUNK, CHUNK)),
            _const_spec((RET_HEADS, CHUNK, LANES)),
            _const_spec((RET_HEADS, CHUNK, LANES)),
            _const_spec((RET_HEADS, 1, LANES)),
            _const_spec((1, 512)),
            _const_spec((LRU_CONV, LRU_WIDTH)),
            _const_spec((1, LRU_WIDTH)),
            _const_spec((LRU_WIDTH, 2 * LRU_WIDTH)),
            _const_spec((1, 2 * LRU_WIDTH)),
            _const_spec((1, LRU_WIDTH)),
            _const_spec((MIX_WIDTH, D)),
        ],
        out_specs=pl.BlockSpec((1, tl, D), lambda b, l: (b, l, 0)),
        out_shape=jax.ShapeDtypeStruct((B, L, D), F32),
        scratch_shapes=[
            pltpu.VMEM((tl, 256), F32),
            pltpu.VMEM((tl, 256), F32),
            pltpu.VMEM((tl, 512), F32),
            pltpu.VMEM((tl, MIX_WIDTH), F32),
            pltpu.VMEM((tl + 8, LRU_WIDTH), F32),
            pltpu.VMEM((tl, LRU_WIDTH), F32),
            pltpu.VMEM((tl, LRU_WIDTH), F32),
            pltpu.VMEM((RET_HEADS, LANES, LANES), F32),
            pltpu.VMEM((1, LRU_WIDTH), F32),
        ],
        compiler_params=pltpu.CompilerParams(
            dimension_semantics=("arbitrary", "arbitrary"), vmem_limit_bytes=VMEM_LIMIT),
        name="odd_mixer",
    )(x, norm_w[None, :], w_in_r, cos, sin, dmat, qdec, kdec, sdec, ret_norm_w[None, :],
      conv_w, conv_b[None, :], wri, bri, lam[None, :], w_out.astype(BF16))


_I_E1, _I_E2, _I_G1, _I_G2, _I_R1, _I_R2 = 0, 1, 2, 3, 4, 5
_LOGIT_E0 = MOE_GROUPS


def _pack_bf16_pairs(lo, hi):
    lo_b = pltpu.bitcast(lo.astype(BF16).astype(F32), jnp.uint32)
    hi_b = pltpu.bitcast(hi.astype(BF16).astype(F32), jnp.uint32)
    return (lo_b >> 16) | (hi_b & jnp.uint32(0xFFFF0000))


def _unpack_pairs_f32(w):
    return pltpu.bitcast(w << 16, F32), pltpu.bitcast(w & jnp.uint32(0xFFFF0000), F32)


def _unpack_bf16_pairs(w):
    lo, hi = _unpack_pairs_f32(w)
    return lo.astype(BF16), hi.astype(BF16)


def _router_kernel(x_ref, nw_ref, wr_ref, br_ref, hnp_ref, info_ref, info_t_ref, cnt_ref):
    tm = x_ref.shape[0]

    @pl.when(pl.program_id(0) == 0)
    def _():
        cnt_ref[...] = jnp.zeros_like(cnt_ref)

    hn = _rms(x_ref[...], nw_ref[...])
    half = D_MODEL // 2
    hnp_ref[...] = _pack_bf16_pairs(hn[:, :half], hn[:, half:])

    logits = _dot_x3(hn, wr_ref[...]) + br_ref[...]
    lane = lax.broadcasted_iota(jnp.int32, (tm, LANES), 1).astype(F32)

    def first_max(vals):
        m = jnp.max(vals, axis=-1, keepdims=True)
        idx = jnp.min(jnp.where(vals == m, lane, float(LANES)), axis=-1, keepdims=True)
        return m, idx

    gl = jnp.where(lane < MOE_GROUPS, logits, NEG_BIG)
    gmax, gidx = first_max(gl)
    g_w = 1.0 / jnp.sum(jnp.exp(gl - gmax), axis=-1, keepdims=True)
    e0 = _LOGIT_E0 + MOE_PER_GROUP * gidx
    el = jnp.where((lane >= e0) & (lane < e0 + MOE_PER_GROUP), logits, NEG_BIG)
    m1, i1 = first_max(el)
    m2, i2 = first_max(jnp.where(lane == i1, NEG_BIG, el))
    p2 = jnp.exp(m2 - m1)
    w1 = 1.0 / (1.0 + p2)
    e1 = i1 - _LOGIT_E0
    e2 = i2 - _LOGIT_E0

    oh1 = jnp.where(lane == e1, 1.0, 0.0)
    oh2 = jnp.where(lane == e2, 1.0, 0.0)
    tri = _tri(tm)
    pre1 = jnp.dot(tri, oh1.astype(BF16), preferred_element_type=F32)
    pre2 = jnp.dot(tri, oh2.astype(BF16), preferred_element_type=F32)
    carry = cnt_ref[...]
    c1 = pre1[tm - 1:tm, :]
    r1 = jnp.sum(oh1 * (pre1 + carry), axis=-1, keepdims=True) - 1.0
    r2 = jnp.sum(oh2 * (pre2 + (carry + c1)), axis=-1, keepdims=True) - 1.0
    cnt_ref[...] = carry + c1 + pre2[tm - 1:tm, :]

    info = jnp.zeros((tm, LANES), F32)
    for ln, val in ((_I_E1, e1), (_I_E2, e2), (_I_G1, g_w * w1), (_I_G2, g_w * (w1 * p2)),
                    (_I_R1, r1), (_I_R2, r2)):
        info = jnp.where(lane == float(ln), val, info)
    info_ref[...] = info
    info_t_ref[...] = info.T[0:GROUP, :]


def _router(x2, part, norm_w, w_group, b_group, w_expert, b_expert):
    D = x2.shape[1]
    T = x2.shape[0] // MOE_PARTS
    tm = min(TOK_TILE, T)
    nb = T // tm
    pad = LANES - MOE_GROUPS - MOE_EXPERTS
    wr = jnp.pad(jnp.concatenate([w_group, w_expert], axis=1), ((0, 0), (0, pad)))
    br = jnp.pad(jnp.concatenate([b_group, b_expert]), (0, pad))[None, :]
    return pl.pallas_call(
        _router_kernel,
        grid=(nb,),
        in_specs=[
            pl.BlockSpec((tm, D), lambda i: (part * nb + i, 0)),
            _const_spec((1, D)),
            _const_spec((D, LANES)),
            _const_spec((1, LANES)),
        ],
        out_specs=[
            pl.BlockSpec((tm, D // 2), lambda i: (i, 0)),
            pl.BlockSpec((tm, LANES), lambda i: (i, 0)),
            pl.BlockSpec((GROUP, tm), lambda i: (0, i)),
            _const_spec((1, LANES)),
        ],
        out_shape=[
            jax.ShapeDtypeStruct((T, D // 2), jnp.uint32),
            jax.ShapeDtypeStruct((T, LANES), F32),
            jax.ShapeDtypeStruct((GROUP, T), F32),
            jax.ShapeDtypeStruct((1, LANES), F32),
        ],
        compiler_params=pltpu.CompilerParams(
            dimension_semantics=("arbitrary",), vmem_limit_bytes=VMEM_LIMIT),
        name="moe_router",
    )(x2, norm_w[None, :], wr, br)


def _plan_kernel(cnt_ref, info_ref, pos_ref, te_ref, nx_ref, nv_ref):
    nt = te_ref.shape[0]
    e1 = info_ref[_I_E1:_I_E1 + 1, :]
    e2 = info_ref[_I_E2:_I_E2 + 1, :]
    s1 = jnp.zeros_like(e1)
    s2 = jnp.zeros_like(e2)
    tiles = [lax.div(cnt_ref[e] + (ROW_TILE - 1), jnp.int32(ROW_TILE)) for e in range(MOE_EXPERTS)]
    following = [None] * MOE_EXPERTS
    nxt = jnp.int32(-1)
    for e in reversed(range(MOE_EXPERTS)):
        following[e] = nxt
        nxt = jnp.where(tiles[e] > 0, jnp.int32(e), nxt)
    start = jnp.int32(0)
    for e in range(MOE_EXPERTS):
        seg = (start * ROW_TILE).astype(F32)
        s1 = jnp.where(e1 == float(e), seg, s1)
        s2 = jnp.where(e2 == float(e), seg, s2)

        def fill(i, carry, e=e):
            te_ref[i] = jnp.int32(e)
            nx_ref[i] = following[e]
            return carry

        lax.fori_loop(start, start + tiles[e], fill, 0)
        start = start + tiles[e]
    nv_ref[0] = start

    def fill_tail(i, carry):
        te_ref[i] = jnp.int32(MOE_EXPERTS - 1)
        nx_ref[i] = jnp.int32(-1)
        return carry

    lax.fori_loop(start, nt, fill_tail, 0)
    pos_ref[0:1, :] = (s1 + info_ref[_I_R1:_I_R1 + 1, :]).astype(jnp.int32)
    pos_ref[1:2, :] = (s2 + info_ref[_I_R2:_I_R2 + 1, :]).astype(jnp.int32)


def _plan(cnt, info_t, nt):
    T = info_t.shape[1]
    smem = pl.BlockSpec(memory_space=pltpu.SMEM)
    return pl.pallas_call(
        _plan_kernel,
        in_specs=[smem, pl.BlockSpec(memory_space=pltpu.VMEM)],
        out_specs=[pl.BlockSpec(memory_space=pltpu.VMEM), smem, smem, smem],
        out_shape=[
            jax.ShapeDtypeStruct((2, T), jnp.int32),
            jax.ShapeDtypeStruct((nt,), jnp.int32),
            jax.ShapeDtypeStruct((nt,), jnp.int32),
            jax.ShapeDtypeStruct((1,), jnp.int32),
        ],
        name="moe_plan",
    )(cnt, info_t)


def _sc_worker():
    return lax.axis_index("s") * SC_CORES + lax.axis_index("c")


def _dispatch(hnp, pos1, pos2, n_rows):
    T, W = hnp.shape
    per_w = T // SC_WORKERS
    n_chunks = per_w // SC_ROWS
    mesh = plsc.VectorSubcoreMesh(core_axis_name="c", subcore_axis_name="s")

    @functools.partial(
        pl.kernel, mesh=mesh,
        out_type=jax.ShapeDtypeStruct((n_rows, W), jnp.uint32),
        scratch_types=[
            pltpu.VMEM((SC_ROWS,), jnp.int32),
            pltpu.VMEM((SC_ROWS,), jnp.int32),
            pltpu.VMEM((SC_ROWS, W), jnp.uint32),
        ],
        name="moe_dispatch_sc",
    )
    def scatter(hnp_hbm, p1_hbm, p2_hbm, xs_hbm, i1_v, i2_v, rows_v):
        base0 = _sc_worker() * per_w

        @pl.loop(0, n_chunks)
        def _(j):
            base = pl.multiple_of(base0 + j * SC_ROWS, SC_ROWS)
            pltpu.sync_copy(p1_hbm.at[pl.ds(base, SC_ROWS)], i1_v)
            pltpu.sync_copy(p2_hbm.at[pl.ds(base, SC_ROWS)], i2_v)
            pltpu.sync_copy(hnp_hbm.at[pl.ds(base, SC_ROWS)], rows_v)
            pltpu.sync_copy(rows_v, xs_hbm.at[i1_v])
            pltpu.sync_copy(rows_v, xs_hbm.at[i2_v])

    return scatter(hnp, pos1, pos2)


def _gather_rows(table, idx):
    T = idx.shape[0]
    W = table.shape[1]
    per_w = T // SC_WORKERS
    rows = SC_ROWS
    n_chunks = per_w // rows
    mesh = plsc.VectorSubcoreMesh(core_axis_name="c", subcore_axis_name="s")

    @functools.partial(
        pl.kernel, mesh=mesh,
        out_type=jax.ShapeDtypeStruct((T, W), table.dtype),
        scratch_types=[
            pltpu.VMEM((rows,), jnp.int32),
            pltpu.VMEM((rows, W), table.dtype),
        ],
        name="moe_gather_sc",
    )
    def gather(table_hbm, idx_hbm, out_hbm, i_v, rows_v):
        base0 = _sc_worker() * per_w

        @pl.loop(0, n_chunks)
        def _(j):
            base = pl.multiple_of(base0 + j * rows, rows)
            pltpu.sync_copy(idx_hbm.at[pl.ds(base, rows)], i_v)
            pltpu.sync_copy(table_hbm.at[i_v], rows_v)
            pltpu.sync_copy(rows_v, out_hbm.at[pl.ds(base, rows)])

    return gather(table, idx)


def _experts_kernel(te_ref, nx_ref, nv_ref, xs_ref, w1_hbm, w3_hbm, w2_hbm, ys_ref,
                    w1f, w3f, w2f, w1b, w3b, w2b, sem, *, layer):
    i = pl.program_id(0)

    def fetch(e):
        row = e + layer * MOE_EXPERTS
        return [pltpu.make_async_copy(w1_hbm.at[row], w1f, sem),
                pltpu.make_async_copy(w3_hbm.at[row], w3f, sem),
                pltpu.make_async_copy(w2_hbm.at[row], w2f, sem)]

    @pl.when(i < nv_ref[0])
    def _():
        @pl.when(i == 0)
        def _():
            for cp in fetch(te_ref[0]):
                cp.start()

        @pl.when((i == 0) | (te_ref[i] != te_ref[jnp.maximum(i - 1, 0)]))
        def _():
            for cp in fetch(te_ref[i]):
                cp.wait()
            w1b[...] = w1f[...].astype(BF16)
            w3b[...] = w3f[...].astype(BF16)
            w2b[...] = w2f[...].astype(BF16)

            @pl.when(nx_ref[i] >= 0)
            def _():
                for cp in fetch(nx_ref[i]):
                    cp.start(priority=1)

        lo, hi = _unpack_bf16_pairs(xs_ref[...])
        half = D_MODEL // 2
        d = functools.partial(jnp.dot, preferred_element_type=F32)
        h1 = d(lo, w1b[0:half, :]) + d(hi, w1b[half:, :])
        h3 = d(lo, w3b[0:half, :]) + d(hi, w3b[half:, :])
        hid = (h1 * _sigmoid(h1)) * h3
        y = d(hid.astype(BF16), w2b[...])
        ys_ref[...] = _pack_bf16_pairs(y[:, :half], y[:, half:])

    @pl.when(i >= nv_ref[0])
    def _():
        ys_ref[...] = jnp.zeros_like(ys_ref)


def _experts(xs, tile_expert, next_expert, n_valid, w1, w3, w2, layer):
    n_rows, W = xs.shape
    nt = n_rows // ROW_TILE
    D, F = w1.shape[1], w1.shape[2]

    def row_map(i, te, nx, nv):
        return (i, 0)

    hbm = pl.BlockSpec(memory_space=pl.ANY)
    return pl.pallas_call(
        functools.partial(_experts_kernel, layer=layer),
        grid_spec=pltpu.PrefetchScalarGridSpec(
            num_scalar_prefetch=3,
            grid=(nt,),
            in_specs=[pl.BlockSpec((ROW_TILE, W), row_map), hbm, hbm, hbm],
            out_specs=pl.BlockSpec((ROW_TILE, D // 2), row_map),
            scratch_shapes=[
                pltpu.VMEM((D, F), F32),
                pltpu.VMEM((D, F), F32),
                pltpu.VMEM((F, D), F32),
                pltpu.VMEM((D, F), BF16),
                pltpu.VMEM((D, F), BF16),
                pltpu.VMEM((F, D), BF16),
                pltpu.SemaphoreType.DMA(()),
            ],
        ),
        out_shape=jax.ShapeDtypeStruct((n_rows, D // 2), jnp.uint32),
        compiler_params=pltpu.CompilerParams(
            dimension_semantics=("arbitrary",), vmem_limit_bytes=VMEM_LIMIT),
        name="moe_experts",
    )(tile_expert, next_expert, n_valid, xs, w1, w3, w2)


def _combine_kernel(x_ref, info_ref, fw_ref, y1_ref, y2_ref, out_ref, *, final):
    info = info_ref[...]
    g1 = info[:, _I_G1:_I_G1 + 1]
    g2 = info[:, _I_G2:_I_G2 + 1]
    y1 = jnp.concatenate(_unpack_pairs_f32(y1_ref[...]), axis=1)
    y2 = jnp.concatenate(_unpack_pairs_f32(y2_ref[...]), axis=1)
    y = x_ref[...] + (g1 * y1 + g2 * y2)
    if final:
        y = _rms(y, fw_ref[...])
    out_ref[...] = y


def _combine(x2, part, info, y1, y2, final_w, final):
    T, D = x2.shape
    tp = T // MOE_PARTS
    tm = min(TOK_TILE, tp)
    nb = tp // tm
    x_spec = pl.BlockSpec((tm, D), lambda i: (part * nb + i, 0))
    half_spec = pl.BlockSpec((tm, D // 2), lambda i: (i, 0))
    return pl.pallas_call(
        functools.partial(_combine_kernel, final=final),
        grid=(nb,),
        in_specs=[x_spec, pl.BlockSpec((tm, LANES), lambda i: (i, 0)), _const_spec((1, D)),
                  half_spec, half_spec],
        out_specs=x_spec,
        out_shape=jax.ShapeDtypeStruct((T, D), F32),
        input_output_aliases={0: 0},
        compiler_params=pltpu.CompilerParams(
            dimension_semantics=("arbitrary",), vmem_limit_bytes=VMEM_LIMIT),
        name="moe_combine",
    )(x2, info, final_w[None, :], y1, y2)


def _hier_moe(x2, norm_w, w_group, b_group, w_expert, b_expert, w1, w3, w2, layer, final_w, final):
    T, D = x2.shape
    tp = T // MOE_PARTS
    nt = (2 * tp) // ROW_TILE + MOE_EXPERTS
    routed = []
    for part in range(MOE_PARTS):
        hnp, info, info_t, cnt = _router(x2, part, norm_w, w_group, b_group, w_expert, b_expert)
        pos, tile_expert, next_expert, n_valid = _plan(cnt[0].astype(jnp.int32), info_t, nt)
        pos1, pos2 = pos[0], pos[1]
        xs = _dispatch(hnp, pos1, pos2, nt * ROW_TILE)
        ys = _experts(xs, tile_expert, next_expert, n_valid, w1, w3, w2, layer)
        routed.append((info, _gather_rows(ys, pos1), _gather_rows(ys, pos2)))
    out = x2
    for part, (info, y1, y2) in enumerate(routed):
        out = _combine(out, part, info, y1, y2, final_w, final)
    return out


def kernel(x, norm_mix, norm_ffn, norm_final, even_w_in, hgrn_lb_logits, hgrn_norm_w, gla_w_up,
           gla_b_up, gla_norm_w, even_w_out, odd_w_in, ret_norm_w, conv_w, conv_b, rglru_w_r,
           rglru_b_r, rglru_w_i, rglru_b_i, rglru_lambda, odd_w_out, moe_w_group, moe_b_group,
           moe_w_expert, moe_b_expert, moe_w1, moe_w3, moe_w2):
    B, L, D = x.shape
    depth = norm_mix.shape[0]
    lb_table = jnp.cumsum(jax.nn.softmax(hgrn_lb_logits.astype(F32), axis=0), axis=0)
    w1 = moe_w1.reshape((-1,) + moe_w1.shape[2:])
    w3 = moe_w3.reshape((-1,) + moe_w3.shape[2:])
    w2 = moe_w2.reshape((-1,) + moe_w2.shape[2:])
    for l in range(depth):
        j = l // 2
        if l % 2 == 0:
            x = _even_mixer(x, norm_mix[l], even_w_in[j], lb_table[l], gla_w_up[j], gla_b_up[j],
                            hgrn_norm_w[j], gla_norm_w[j], even_w_out[j])
        else:
            x = _odd_mixer(x, norm_mix[l], odd_w_in[j], ret_norm_w[j], conv_w[j], conv_b[j],
                           rglru_w_r[j], rglru_b_r[j], rglru_w_i[j], rglru_b_i[j], rglru_lambda[j],
                           odd_w_out[j])
        x = _hier_moe(x.reshape(B * L, D), norm_ffn[l], moe_w_group[l], moe_b_group[l],
                      moe_w_expert[l], moe_b_expert[l], w1, w3, w2, l,
                      norm_final, l == depth - 1).reshape(B, L, D)
    return x
```

```python
import functools

import numpy as np
import jax
import jax.numpy as jnp
from jax import lax
from jax.experimental import pallas as pl
from jax.experimental.pallas import tpu as pltpu
from jax.experimental.pallas import tpu_sc as plsc

F32 = jnp.float32
BF16 = jnp.bfloat16

EPS = 1e-6
CHUNK = 64
GROUP = 8
LOG2E = 1.4426950408889634
LANES = 128
D_MODEL = 1024
MIX_WIDTH = 1024
N_HEADS = 8
GLA_RANK = 16
GLA_TAU = 16.0
RET_HEADS = 4
RET_DK = 64
ROPE_BASE = 10000.0
LRU_WIDTH = 512
LRU_BLOCKS = 8
LRU_CONV = 4
LRU_C = 8.0
MOE_GROUPS = 4
MOE_PER_GROUP = 8
MOE_EXPERTS = MOE_GROUPS * MOE_PER_GROUP
MOE_DFF = 512
NEG_BIG = -1e30

MIX_TILE = 512
TOK_TILE = 512
ROW_TILE = 512
WEIGHT_DMA_PARTS = 8
MOE_PARTS = 1
SC_CORES = 2
SC_WORKERS = 32
SC_ROWS = 128
VMEM_LIMIT = 56 * 1024 * 1024


def _dot(a, b):
    return jnp.dot(a.astype(BF16), b.astype(BF16), preferred_element_type=F32)


def _dot_nt(a, b):
    return lax.dot_general(a.astype(BF16), b.astype(BF16), (((1,), (1,)), ((), ())),
                           preferred_element_type=F32)


def _dot_tn(a, b):
    return lax.dot_general(a.astype(BF16), b.astype(BF16), (((0,), (0,)), ((), ())),
                           preferred_element_type=F32)


def _split2(x):
    hi = x.astype(BF16)
    lo = (x - hi.astype(F32)).astype(BF16)
    return hi, lo


def _split3(x):
    hi = x.astype(BF16)
    r = x - hi.astype(F32)
    mid = r.astype(BF16)
    lo = (r - mid.astype(F32)).astype(BF16)
    return hi, mid, lo


def _dot_x3(a, b):
    ah, al = _split2(a)
    bh, bl = _split2(b)
    d = functools.partial(jnp.dot, preferred_element_type=F32)
    return d(ah, bh) + (d(ah, bl) + d(al, bh))


def _sigmoid(x):
    return 1.0 / (1.0 + jnp.exp(-x))


def _softplus(x):
    return jnp.maximum(x, 0.0) + jnp.log(1.0 + jnp.exp(-jnp.abs(x)))


def _rms(x, w):
    return x * lax.rsqrt(jnp.mean(x * x, axis=-1, keepdims=True) + EPS) * w


def _tri(n):
    r = lax.broadcasted_iota(jnp.int32, (n, n), 0)
    c = lax.broadcasted_iota(jnp.int32, (n, n), 1)
    return jnp.where(r >= c, 1.0, 0.0).astype(BF16)


def _cumsum_rows(tri3, g):
    return jnp.dot(tri3, jnp.concatenate(_split3(g), axis=0), preferred_element_type=F32)


_E_AQ, _E_AF, _E_AI, _E_AG = 0, 512, 1024, 1536
_E_BQ, _E_BK, _E_BV, _E_BR, _E_LOW = 2048, 2304, 2560, 3072, 3584
EVEN_COLS = 3712
_QK_COLS = 768


_LEVEL_BLOCKS = (32, 16, 8)


def _level_tables():
    t = np.arange(CHUNK)
    out = []
    for blk in _LEVEL_BLOCKS:
        odd = (t // blk) % 2 == 1
        out.append((np.where(odd, (t // blk) * blk - 1, -1), np.where(~odd, (t // blk) * blk + blk - 1, -1)))
    return out


def _gla_chunk(c, q_ref, k_ref, v_ref, b2, o_ref, st_ref, a_ref):
    n_units = _QK_COLS // LANES
    rows = pl.ds(pl.multiple_of(c * CHUNK, CHUNK), CHUNK)
    lane = lax.broadcasted_iota(jnp.int32, (1, LANES), 1)
    trow = lax.broadcasted_iota(jnp.int32, (GROUP, LANES), 0)
    unit_heads = [((u, None),) for u in range(4)] + [((4, 0), (5, 64)), ((6, 0), (7, 64))]
    qs = [q_ref[rows, u * LANES:(u + 1) * LANES] for u in range(n_units)]
    ks = [k_ref[rows, u * LANES:(u + 1) * LANES] for u in range(n_units)]
    bs = [b2[:, u * LANES:(u + 1) * LANES] for u in range(n_units)]
    vs = [v_ref[rows, h * LANES:(h + 1) * LANES] for h in range(N_HEADS)]

    for u in range(n_units):
        for j in range(CHUNK // GROUP):
            qj = qs[u][j * GROUP:(j + 1) * GROUP]
            bj = bs[u][j * GROUP:(j + 1) * GROUP]
            for s in range(0, GROUP, 2):
                pair = [qj * jnp.exp2(jnp.where(trow >= s + i, bj - bj[s + i:s + i + 1, :], NEG_BIG))
                        for i in (0, 1)]
                r0 = (j * GROUP + s) * GROUP
                a_ref[r0:r0 + 2 * GROUP, u * LANES:(u + 1) * LANES] = (
                    jnp.concatenate(pair, axis=0).astype(BF16))

    zk = jnp.zeros((CHUNK, LANES), F32)
    scs = []
    for pr in range(n_units // 2):
        u0, u1 = 2 * pr, 2 * pr + 1
        if u0 < 4:
            keys = jnp.concatenate([jnp.concatenate([ks[u0], zk], axis=1),
                                    jnp.concatenate([zk, ks[u1]], axis=1)], axis=0)
        else:
            lo0, hi0 = jnp.where(lane < 64, ks[u0], 0.0), jnp.where(lane >= 64, ks[u0], 0.0)
            lo1, hi1 = jnp.where(lane < 64, ks[u1], 0.0), jnp.where(lane >= 64, ks[u1], 0.0)
            keys = jnp.concatenate([jnp.concatenate([lo0, zk], axis=1), jnp.concatenate([hi0, zk], axis=1),
                                    jnp.concatenate([zk, lo1], axis=1), jnp.concatenate([zk, hi1], axis=1)],
                                   axis=0)
        scs.append(_dot_nt(a_ref[:, u0 * LANES:(u1 + 1) * LANES], keys))

    tables = _level_tables()
    tq = lax.broadcasted_iota(jnp.int32, (CHUNK, CHUNK), 0)
    ts = lax.broadcasted_iota(jnp.int32, (CHUNK, CHUNK), 1)
    covers = [((tq // blk) % 2 == 1) & ((tq // blk) == (ts // blk) + 1) & ((ts // blk) % 2 == 0)
              for blk in _LEVEL_BLOCKS]
    s_offs, o_inter = {}, {}
    for u in range(n_units):
        q, k, b = qs[u], ks[u], bs[u]
        heads = unit_heads[u]
        if len(heads) == 2:
            qm = jnp.concatenate([jnp.where((lane & 64) == lo, q, 0.0) for _, lo in heads], axis=0)
            bm = jnp.concatenate([b, b], axis=0)
        else:
            qm, bm = q, b
        s_off = None
        for (qref, kref), cover in zip(tables, covers):
            qrows, krows = [], []
            for g in range(CHUNK // GROUP):
                sl = slice(g * GROUP, (g + 1) * GROUP)
                rq, rk = int(qref[g * GROUP]), int(kref[g * GROUP])
                qrows.append(jnp.exp2(b[sl] - b[rq:rq + 1, :]) if rq >= 0 else jnp.zeros((GROUP, LANES), F32))
                krows.append(k[sl] * jnp.exp2(b[rk:rk + 1, :] - b[sl]) if rk >= 0
                             else jnp.zeros((GROUP, LANES), F32))
            qfac = jnp.concatenate(qrows, axis=0)
            if len(heads) == 2:
                qfac = jnp.concatenate([qfac, qfac], axis=0)
                cov = jnp.concatenate([cover, cover], axis=0)
            else:
                cov = cover
            part = _dot_nt(qm * qfac, jnp.concatenate(krows, axis=0))
            s_off = jnp.where(cov, part, 0.0 if s_off is None else s_off)
        eb = jnp.exp2(bm)
        b_last = b[CHUNK - 1:CHUNK, :]
        kd = (k * jnp.exp2(b_last - b)).astype(BF16)
        dec = jnp.exp2(b_last)
        qe = qm * eb
        for i, (head, lo) in enumerate(heads):
            sl = slice(i * CHUNK, (i + 1) * CHUNK)
            st = st_ref[head]
            o_inter[head] = _dot_nt(qe[sl], st)
            st_ref[head] = st * dec + _dot_tn(vs[head], kd)
            s_offs[head] = s_off[sl]

    for u in range(n_units):
        sc = scs[u // 2]
        paired = len(unit_heads[u]) == 2
        width = LANES if paired else CHUNK
        c0 = (u % 2) * width
        scol = (lane & 63) if paired else lane[:, :CHUNK]
        groups = []
        for j in range(CHUNK // GROUP):
            acc = jnp.zeros((GROUP, width), F32)
            for s in range(GROUP):
                r0 = (j * GROUP + s) * GROUP
                acc = jnp.where(scol == j * GROUP + s, sc[r0:r0 + GROUP, c0:c0 + width], acc)
            groups.append(acc)
        p = jnp.concatenate(groups, axis=0)
        if paired:
            vstack = jnp.concatenate([vs[h] for h, _ in unit_heads[u]], axis=0)
        for head, lo in unit_heads[u]:
            if lo is None:
                o = o_inter[head] + _dot(s_offs[head] + p, vs[head])
            else:
                o = (o_inter[head] + _dot(s_offs[head], vs[head])
                     + _dot(jnp.where((lane & 64) == lo, p, 0.0), vstack))
            o_ref[rows, head * LANES:(head + 1) * LANES] = o


def _even_kernel(x_ref, nw_ref, win_ref, lb_ref, wup_ref, bup_ref, hw_ref, wout_ref, out_ref,
                 q_ref, k_ref, g_ref, v_ref, gate_ref, o_ref, st_ref, a_ref):
    tl = x_ref.shape[1]

    @pl.when(pl.program_id(1) == 0)
    def _():
        st_ref[...] = jnp.zeros_like(st_ref)

    x = x_ref[0]
    hn = _rms(x, nw_ref[...]).astype(BF16)

    def proj(c0, n):
        return jnp.dot(hn, win_ref[:, c0:c0 + n], preferred_element_type=F32)

    q_ref[:, 0:512] = proj(_E_AQ, 512)
    lb = lb_ref[...]
    f = lb + (1.0 - lb) * _sigmoid(proj(_E_AF, 512))
    k_ref[:, 0:512] = 1.0 - f
    g_ref[:, 0:512] = jnp.log(f)
    v_ref[:, 0:512] = proj(_E_AI, 512)
    ag = proj(_E_AG, 512)
    gate_ref[:, 0:512] = ag * _sigmoid(ag)
    q_ref[:, 512:768] = proj(_E_BQ, 256) * (64.0 ** -0.5)
    k_ref[:, 512:768] = proj(_E_BK, 256)
    v_ref[:, 512:1024] = proj(_E_BV, 512)
    br = proj(_E_BR, 512)
    gate_ref[:, 512:1024] = br * _sigmoid(br)
    z = _dot_x3(proj(_E_LOW, LANES), wup_ref[...]) + bup_ref[...]
    g_ref[:, 512:768] = -_softplus(-z) * (1.0 / GLA_TAU)

    tri = _tri(CHUNK)
    tri3 = jnp.concatenate([tri, tri, tri], axis=1)

    def chunk_body(c, carry):
        rows = pl.ds(pl.multiple_of(c * CHUNK, CHUNK), CHUNK)
        b2 = _cumsum_rows(tri3, g_ref[rows, :]) * LOG2E
        _gla_chunk(c, q_ref, k_ref, v_ref, b2, o_ref, st_ref, a_ref)
        return carry

    lax.fori_loop(0, tl // CHUNK, chunk_body, 0)

    hw = hw_ref[...]
    ys = []
    for head in range(N_HEADS):
        sl = slice(head * LANES, (head + 1) * LANES)
        oh = o_ref[:, sl]
        ys.append(oh * lax.rsqrt(jnp.mean(oh * oh, axis=-1, keepdims=True) + EPS)
                  * hw[:, sl] * gate_ref[:, sl])
    y = jnp.concatenate(ys, axis=1).astype(BF16)
    out_ref[0] = x + jnp.dot(y, wout_ref[...], preferred_element_type=F32)


def _const_spec(shape):
    nd = len(shape)
    return pl.BlockSpec(shape, lambda *_: (0,) * nd)


def _even_mixer(x, norm_w, w_in, lb, w_up, b_up, hgrn_norm_w, gla_norm_w, w_out):
    B, L, D = x.shape
    tl = min(MIX_TILE, L)
    s = np.cumsum([0, 512, 512, 512, 512, 256, 256, 512, GLA_RANK, 512])
    cols = [w_in[:, s[i]:s[i + 1]] for i in range(9)]
    low = jnp.pad(cols[7], ((0, 0), (0, LANES - GLA_RANK)))
    w_in_r = jnp.concatenate(cols[:7] + [cols[8], low], axis=1).astype(BF16)
    w_up_p = jnp.pad(w_up, ((0, LANES - GLA_RANK), (0, 0)))
    hw = jnp.concatenate([hgrn_norm_w, gla_norm_w])[None, :]
    return pl.pallas_call(
        _even_kernel,
        grid=(B, L // tl),
        in_specs=[
            pl.BlockSpec((1, tl, D), lambda b, l: (b, l, 0)),
            _const_spec((1, D)),
            _const_spec((D, EVEN_COLS)),
            _const_spec((1, 512)),
            _const_spec((LANES, 256)),
            _const_spec((1, 256)),
            _const_spec((1, MIX_WIDTH)),
            _const_spec((MIX_WIDTH, D)),
        ],
        out_specs=pl.BlockSpec((1, tl, D), lambda b, l: (b, l, 0)),
        out_shape=jax.ShapeDtypeStruct((B, L, D), F32),
        scratch_shapes=[
            pltpu.VMEM((tl, _QK_COLS), F32),
            pltpu.VMEM((tl, _QK_COLS), F32),
            pltpu.VMEM((tl, _QK_COLS), F32),
            pltpu.VMEM((tl, MIX_WIDTH), F32),
            pltpu.VMEM((tl, MIX_WIDTH), F32),
            pltpu.VMEM((tl, MIX_WIDTH), F32),
            pltpu.VMEM((N_HEADS, LANES, LANES), F32),
            pltpu.VMEM((CHUNK * GROUP, _QK_COLS), BF16),
        ],
        compiler_params=pltpu.CompilerParams(
            dimension_semantics=("arbitrary", "arbitrary"), vmem_limit_bytes=VMEM_LIMIT),
        name="even_mixer",
    )(x, norm_w[None, :], w_in_r, lb[None, :], w_up_p, b_up[None, :], hw, w_out.astype(BF16))


_O_Q, _O_QR, _O_K, _O_KR, _O_V, _O_G, _O_X, _O_GATE = 0, 256, 512, 768, 1024, 1536, 2048, 2560
ODD_COLS = 3072


def _odd_kernel(x_ref, nw_ref, win_ref, cos_ref, sin_ref, dmat_ref, qdec_ref, kdec_ref, sdec_ref,
                rw_ref, cw_ref, cb_ref, wri_ref, bri_ref, lam_ref, wout_ref, out_ref,
                q_ref, k_ref, v_ref, o_ref, xe_ref, a_ref, u_ref, st_ref, h_ref):
    tl = x_ref.shape[1]

    @pl.when(pl.program_id(1) == 0)
    def _():
        st_ref[...] = jnp.zeros_like(st_ref)
        h_ref[...] = jnp.zeros_like(h_ref)
        xe_ref[0:8, :] = jnp.zeros((8, LRU_WIDTH), F32)

    x = x_ref[0]
    hn = _rms(x, nw_ref[...]).astype(BF16)

    def proj(c0, n):
        return jnp.dot(hn, win_ref[:, c0:c0 + n], preferred_element_type=F32)

    cos = cos_ref[...]
    sin = sin_ref[...]
    q_ref[...] = proj(_O_Q, 256) * cos + proj(_O_QR, 256) * sin
    k_ref[...] = (proj(_O_K, 256) * cos + proj(_O_KR, 256) * sin) * (RET_DK ** -0.5)
    v_ref[...] = proj(_O_V, 512)

    xe_ref[8:8 + tl, :] = proj(_O_X, LRU_WIDTH)
    cw = cw_ref[...]
    xc = cb_ref[...]
    for j in range(LRU_CONV):
        xc = xc + xe_ref[5 + j:5 + j + tl, :] * cw[j:j + 1, :]
    xe_ref[0:8, :] = xe_ref[tl:tl + 8, :]
    ri = jnp.dot(xc.astype(BF16), wri_ref[...], preferred_element_type=F32) + bri_ref[...]
    r = _sigmoid(ri[:, :LRU_WIDTH])
    i = _sigmoid(ri[:, LRU_WIDTH:])
    a = jnp.exp(-LRU_C * r * _softplus(-lam_ref[...]))
    a_ref[...] = a
    u_ref[...] = jnp.sqrt(1.0 - a * a) * (i * xc)

    lane = lax.broadcasted_iota(jnp.int32, (1, LANES), 1)
    crow = lax.broadcasted_iota(jnp.int32, (CHUNK, LRU_WIDTH), 0)

    def chunk_body(c, carry):
        rows = pl.ds(pl.multiple_of(c * CHUNK, CHUNK), CHUNK)
        for head in range(RET_HEADS):
            unit, half = head // 2, head % 2
            ul = slice(unit * LANES, (unit + 1) * LANES)
            vl = slice(head * LANES, (head + 1) * LANES)
            own = (lane < 64) if half == 0 else (lane >= 64)
            q = jnp.where(own, q_ref[rows, ul], 0.0)
            k = k_ref[rows, ul]
            v = v_ref[rows, vl]
            st = st_ref[head]
            o = _dot_nt(q, st) * qdec_ref[head]
            o = o + _dot(_dot_nt(q, k) * dmat_ref[head], v)
            st_ref[head] = st * sdec_ref[head] + _dot_tn(v, k * kdec_ref[head])
            o_ref[rows, vl] = o
        ca = a_ref[rows, :]
        ch = u_ref[rows, :]
        d = 1
        while d < CHUNK:
            keep = crow >= d
            sa = jnp.where(keep, pltpu.roll(ca, d, axis=0), 1.0)
            sh = jnp.where(keep, pltpu.roll(ch, d, axis=0), 0.0)
            ch = ca * sh + ch
            ca = ca * sa
            d *= 2
        ch = ch + ca * h_ref[...]
        h_ref[...] = ch[CHUNK - 1:CHUNK, :]
        o_ref[rows, 512:1024] = ch
        return carry

    lax.fori_loop(0, tl // CHUNK, chunk_body, 0, unroll=2)

    rw = rw_ref[...]
    ys = []
    for head in range(RET_HEADS):
        sl = slice(head * LANES, (head + 1) * LANES)
        oh = o_ref[:, sl]
        oh = oh - jnp.mean(oh, axis=-1, keepdims=True)
        ys.append(oh * lax.rsqrt(jnp.mean(oh * oh, axis=-1, keepdims=True) + EPS) * rw[:, sl])
    cg = proj(_O_G, 512)
    y_c = jnp.concatenate(ys, axis=1) * (cg * _sigmoid(cg))
    dg = proj(_O_GATE, LRU_WIDTH)
    gelu = 0.5 * dg * (1.0 + jnp.tanh(np.sqrt(2.0 / np.pi) * (dg + 0.044715 * (dg * dg * dg))))
    y_d = o_ref[:, 512:1024] * gelu
    y = jnp.concatenate([y_c, y_d], axis=1).astype(BF16)
    out_ref[0] = x + jnp.dot(y, wout_ref[...], preferred_element_type=F32)


def _rot_cols(w):
    d = w.shape[0]
    w4 = w.reshape(d, RET_HEADS, 2, RET_DK // 2)
    return jnp.stack([-w4[:, :, 1], w4[:, :, 0]], axis=2).reshape(d, RET_HEADS * RET_DK)


def _odd_mixer(x, norm_w, w_in, ret_norm_w, conv_w, conv_b, w_r, b_r, w_i, b_i, lam, w_out):
    B, L, D = x.shape
    tl = min(MIX_TILE, L)
    s = np.cumsum([0, 256, 256, 512, 512, 512, 512])
    wq, wk, wv, wg, wx, wgate = [w_in[:, s[i]:s[i + 1]] for i in range(6)]
    w_in_r = jnp.concatenate([wq, _rot_cols(wq), wk, _rot_cols(wk), wv, wg, wx, wgate],
                             axis=1).astype(BF16)
    half = RET_DK // 2
    inv = ROPE_BASE ** (-jnp.arange(half, dtype=F32) / half)
    ang = jnp.arange(L, dtype=F32)[:, None] * inv[None, :]
    cos = jnp.tile(jnp.cos(ang), (1, 2 * RET_HEADS))
    sin = jnp.tile(jnp.sin(ang), (1, 2 * RET_HEADS))
    lg = jnp.log1p(-jnp.exp2(-5.0 - jnp.arange(RET_HEADS, dtype=F32)))[:, None, None]
    t = jnp.arange(CHUNK, dtype=F32)
    rel = t[:, None] - t[None, :]
    dmat = jnp.where(rel >= 0, jnp.exp(lg * jnp.maximum(rel, 0.0)), 0.0)
    qdec = jnp.broadcast_to(jnp.exp(lg * (t[None, :, None] + 1.0)), (RET_HEADS, CHUNK, LANES))
    kdec = jnp.broadcast_to(jnp.exp(lg * (CHUNK - 1.0 - t[None, :, None])), (RET_HEADS, CHUNK, LANES))
    sdec = jnp.broadcast_to(jnp.exp(lg * float(CHUNK)), (RET_HEADS, 1, LANES))
    eye = jnp.eye(LRU_BLOCKS, dtype=F32)
    bd = LRU_WIDTH // LRU_BLOCKS

    def blockdiag(w):
        return (eye[:, None, :, None] * w[:, :, None, :]).reshape(LRU_WIDTH, LRU_WIDTH)

    wri = jnp.concatenate([blockdiag(w_r), blockdiag(w_i)], axis=1).astype(BF16)
    bri = jnp.concatenate([b_r, b_i])[None, :]
    return pl.pallas_call(
        _odd_kernel,
        grid=(B, L // tl),
        in_specs=[
            pl.BlockSpec((1, tl, D), lambda b, l: (b, l, 0)),
            _const_spec((1, D)),
            _const_spec((D, ODD_COLS)),
            pl.BlockSpec((tl, 256), lambda b, l: (l, 0)),
            pl.BlockSpec((tl, 256), lambda b, l: (l, 0)),
            _const_spec((RET_HEADS, CHUNK, CHUNK)),
            _const_spec((RET_HEADS, CHUNK, LANES)),
            _const_spec((RET_HEADS, CHUNK, LANES)),
            _const_spec((RET_HEADS, 1, LANES)),
            _const_spec((1, 512)),
            _const_spec((LRU_CONV, LRU_WIDTH)),
            _const_spec((1, LRU_WIDTH)),
            _const_spec((LRU_WIDTH, 2 * LRU_WIDTH)),
            _const_spec((1, 2 * LRU_WIDTH)),
            _const_spec((1, LRU_WIDTH)),
            _const_spec((MIX_WIDTH, D)),
        ],
        out_specs=pl.BlockSpec((1, tl, D), lambda b, l: (b, l, 0)),
        out_shape=jax.ShapeDtypeStruct((B, L, D), F32),
        scratch_shapes=[
            pltpu.VMEM((tl, 256), F32),
            pltpu.VMEM((tl, 256), F32),
            pltpu.VMEM((tl, 512), F32),
            pltpu.VMEM((tl, MIX_WIDTH), F32),
            pltpu.VMEM((tl + 8, LRU_WIDTH), F32),
            pltpu.VMEM((tl, LRU_WIDTH), F32),
            pltpu.VMEM((tl, LRU_WIDTH), F32),
            pltpu.VMEM((RET_HEADS, LANES, LANES), F32),
            pltpu.VMEM((1, LRU_WIDTH), F32),
        ],
        compiler_params=pltpu.CompilerParams(
            dimension_semantics=("arbitrary", "arbitrary"), vmem_limit_bytes=VMEM_LIMIT),
        name="odd_mixer",
    )(x, norm_w[None, :], w_in_r, cos, sin, dmat, qdec, kdec, sdec, ret_norm_w[None, :],
      conv_w, conv_b[None, :], wri, bri, lam[None, :], w_out.astype(BF16))


_I_E1, _I_E2, _I_G1, _I_G2, _I_R1, _I_R2 = 0, 1, 2, 3, 4, 5
_LOGIT_E0 = MOE_GROUPS


def _pack_bf16_pairs(lo, hi):
    lo_b = pltpu.bitcast(lo.astype(BF16).astype(F32), jnp.uint32)
    hi_b = pltpu.bitcast(hi.astype(BF16).astype(F32), jnp.uint32)
    return (lo_b >> 16) | (hi_b & jnp.uint32(0xFFFF0000))


def _unpack_pairs_f32(w):
    return pltpu.bitcast(w << 16, F32), pltpu.bitcast(w & jnp.uint32(0xFFFF0000), F32)


def _unpack_bf16_pairs(w):
    lo, hi = _unpack_pairs_f32(w)
    return lo.astype(BF16), hi.astype(BF16)


def _router_kernel(x_ref, nw_ref, wr_ref, br_ref, hnp_ref, info_ref, info_t_ref, cnt_ref):
    tm = x_ref.shape[0]

    @pl.when(pl.program_id(0) == 0)
    def _():
        cnt_ref[...] = jnp.zeros_like(cnt_ref)

    hn = _rms(x_ref[...], nw_ref[...])
    half = D_MODEL // 2
    hnp_ref[...] = _pack_bf16_pairs(hn[:, :half], hn[:, half:])

    logits = _dot_x3(hn, wr_ref[...]) + br_ref[...]
    lane = lax.broadcasted_iota(jnp.int32, (tm, LANES), 1).astype(F32)

    def first_max(vals):
        m = jnp.max(vals, axis=-1, keepdims=True)
        idx = jnp.min(jnp.where(vals == m, lane, float(LANES)), axis=-1, keepdims=True)
        return m, idx

    gl = jnp.where(lane < MOE_GROUPS, logits, NEG_BIG)
    gmax, gidx = first_max(gl)
    g_w = 1.0 / jnp.sum(jnp.exp(gl - gmax), axis=-1, keepdims=True)
    e0 = _LOGIT_E0 + MOE_PER_GROUP * gidx
    el = jnp.where((lane >= e0) & (lane < e0 + MOE_PER_GROUP), logits, NEG_BIG)
    m1, i1 = first_max(el)
    m2, i2 = first_max(jnp.where(lane == i1, NEG_BIG, el))
    p2 = jnp.exp(m2 - m1)
    w1 = 1.0 / (1.0 + p2)
    e1 = i1 - _LOGIT_E0
    e2 = i2 - _LOGIT_E0

    oh1 = jnp.where(lane == e1, 1.0, 0.0)
    oh2 = jnp.where(lane == e2, 1.0, 0.0)
    tri = _tri(tm)
    pre1 = jnp.dot(tri, oh1.astype(BF16), preferred_element_type=F32)
    pre2 = jnp.dot(tri, oh2.astype(BF16), preferred_element_type=F32)
    carry = cnt_ref[...]
    c1 = pre1[tm - 1:tm, :]
    r1 = jnp.sum(oh1 * (pre1 + carry), axis=-1, keepdims=True) - 1.0
    r2 = jnp.sum(oh2 * (pre2 + (carry + c1)), axis=-1, keepdims=True) - 1.0
    cnt_ref[...] = carry + c1 + pre2[tm - 1:tm, :]

    info = jnp.zeros((tm, LANES), F32)
    for ln, val in ((_I_E1, e1), (_I_E2, e2), (_I_G1, g_w * w1), (_I_G2, g_w * (w1 * p2)),
                    (_I_R1, r1), (_I_R2, r2)):
        info = jnp.where(lane == float(ln), val, info)
    info_ref[...] = info
    info_t_ref[...] = info.T[0:GROUP, :]


def _router(x2, part, norm_w, w_group, b_group, w_expert, b_expert):
    D = x2.shape[1]
    T = x2.shape[0] // MOE_PARTS
    tm = min(TOK_TILE, T)
    nb = T // tm
    pad = LANES - MOE_GROUPS - MOE_EXPERTS
    wr = jnp.pad(jnp.concatenate([w_group, w_expert], axis=1), ((0, 0), (0, pad)))
    br = jnp.pad(jnp.concatenate([b_group, b_expert]), (0, pad))[None, :]
    return pl.pallas_call(
        _router_kernel,
        grid=(nb,),
        in_specs=[
            pl.BlockSpec((tm, D), lambda i: (part * nb + i, 0)),
            _const_spec((1, D)),
            _const_spec((D, LANES)),
            _const_spec((1, LANES)),
        ],
        out_specs=[
            pl.BlockSpec((tm, D // 2), lambda i: (i, 0)),
            pl.BlockSpec((tm, LANES), lambda i: (i, 0)),
            pl.BlockSpec((GROUP, tm), lambda i: (0, i)),
            _const_spec((1, LANES)),
        ],
        out_shape=[
            jax.ShapeDtypeStruct((T, D // 2), jnp.uint32),
            jax.ShapeDtypeStruct((T, LANES), F32),
            jax.ShapeDtypeStruct((GROUP, T), F32),
            jax.ShapeDtypeStruct((1, LANES), F32),
        ],
        compiler_params=pltpu.CompilerParams(
            dimension_semantics=("arbitrary",), vmem_limit_bytes=VMEM_LIMIT),
        name="moe_router",
    )(x2, norm_w[None, :], wr, br)


def _plan_kernel(cnt_ref, info_ref, pos_ref, te_ref, nx_ref, nv_ref):
    nt = te_ref.shape[0]
    e1 = info_ref[_I_E1:_I_E1 + 1, :]
    e2 = info_ref[_I_E2:_I_E2 + 1, :]
    s1 = jnp.zeros_like(e1)
    s2 = jnp.zeros_like(e2)
    tiles = [lax.div(cnt_ref[e] + (ROW_TILE - 1), jnp.int32(ROW_TILE)) for e in range(MOE_EXPERTS)]
    following = [None] * MOE_EXPERTS
    nxt = jnp.int32(-1)
    for e in reversed(range(MOE_EXPERTS)):
        following[e] = nxt
        nxt = jnp.where(tiles[e] > 0, jnp.int32(e), nxt)
    start = jnp.int32(0)
    for e in range(MOE_EXPERTS):
        seg = (start * ROW_TILE).astype(F32)
        s1 = jnp.where(e1 == float(e), seg, s1)
        s2 = jnp.where(e2 == float(e), seg, s2)

        def fill(i, carry, e=e):
            te_ref[i] = jnp.int32(e)
            nx_ref[i] = following[e]
            return carry

        lax.fori_loop(start, start + tiles[e], fill, 0)
        start = start + tiles[e]
    nv_ref[0] = start

    def fill_tail(i, carry):
        te_ref[i] = jnp.int32(MOE_EXPERTS - 1)
        nx_ref[i] = jnp.int32(-1)
        return carry

    lax.fori_loop(start, nt, fill_tail, 0)
    pos_ref[0:1, :] = (s1 + info_ref[_I_R1:_I_R1 + 1, :]).astype(jnp.int32)
    pos_ref[1:2, :] = (s2 + info_ref[_I_R2:_I_R2 + 1, :]).astype(jnp.int32)


def _plan(cnt, info_t, nt):
    T = info_t.shape[1]
    smem = pl.BlockSpec(memory_space=pltpu.SMEM)
    return pl.pallas_call(
        _plan_kernel,
        in_specs=[smem, pl.BlockSpec(memory_space=pltpu.VMEM)],
        out_specs=[pl.BlockSpec(memory_space=pltpu.VMEM), smem, smem, smem],
        out_shape=[
            jax.ShapeDtypeStruct((2, T), jnp.int32),
            jax.ShapeDtypeStruct((nt,), jnp.int32),
            jax.ShapeDtypeStruct((nt,), jnp.int32),
            jax.ShapeDtypeStruct((1,), jnp.int32),
        ],
        name="moe_plan",
    )(cnt, info_t)


def _sc_worker():
    return lax.axis_index("s") * SC_CORES + lax.axis_index("c")


def _dispatch(hnp, pos1, pos2, n_rows):
    T, W = hnp.shape
    per_w = T // SC_WORKERS
    n_chunks = per_w // SC_ROWS
    mesh = plsc.VectorSubcoreMesh(core_axis_name="c", subcore_axis_name="s")

    @functools.partial(
        pl.kernel, mesh=mesh,
        out_type=jax.ShapeDtypeStruct((n_rows, W), jnp.uint32),
        scratch_types=[
            pltpu.VMEM((SC_ROWS,), jnp.int32),
            pltpu.VMEM((SC_ROWS,), jnp.int32),
            pltpu.VMEM((SC_ROWS, W), jnp.uint32),
        ],
        name="moe_dispatch_sc",
    )
    def scatter(hnp_hbm, p1_hbm, p2_hbm, xs_hbm, i1_v, i2_v, rows_v):
        base0 = _sc_worker() * per_w

        @pl.loop(0, n_chunks)
        def _(j):
            base = pl.multiple_of(base0 + j * SC_ROWS, SC_ROWS)
            pltpu.sync_copy(p1_hbm.at[pl.ds(base, SC_ROWS)], i1_v)
            pltpu.sync_copy(p2_hbm.at[pl.ds(base, SC_ROWS)], i2_v)
            pltpu.sync_copy(hnp_hbm.at[pl.ds(base, SC_ROWS)], rows_v)
            pltpu.sync_copy(rows_v, xs_hbm.at[i1_v])
            pltpu.sync_copy(rows_v, xs_hbm.at[i2_v])

    return scatter(hnp, pos1, pos2)


def _gather_rows(table, idx):
    T = idx.shape[0]
    W = table.shape[1]
    per_w = T // SC_WORKERS
    rows = SC_ROWS
    n_chunks = per_w // rows
    mesh = plsc.VectorSubcoreMesh(core_axis_name="c", subcore_axis_name="s")

    @functools.partial(
        pl.kernel, mesh=mesh,
        out_type=jax.ShapeDtypeStruct((T, W), table.dtype),
        scratch_types=[
            pltpu.VMEM((rows,), jnp.int32),
            pltpu.VMEM((rows, W), table.dtype),
        ],
        name="moe_gather_sc",
    )
    def gather(table_hbm, idx_hbm, out_hbm, i_v, rows_v):
        base0 = _sc_worker() * per_w

        @pl.loop(0, n_chunks)
        def _(j):
            base = pl.multiple_of(base0 + j * rows, rows)
            pltpu.sync_copy(idx_hbm.at[pl.ds(base, rows)], i_v)
            pltpu.sync_copy(table_hbm.at[i_v], rows_v)
            pltpu.sync_copy(rows_v, out_hbm.at[pl.ds(base, rows)])

    return gather(table, idx)


def _experts_kernel(te_ref, nx_ref, nv_ref, xs_ref, w1_hbm, w3_hbm, w2_hbm, ys_ref,
                    w1f, w3f, w2f, w1b, w3b, w2b, sem, *, layer):
    i = pl.program_id(0)

    def fetch(e):
        row = e + layer * MOE_EXPERTS
        cps = []
        for src, dst in ((w1_hbm, w1f), (w3_hbm, w3f), (w2_hbm, w2f)):
            n = dst.shape[0] // WEIGHT_DMA_PARTS
            for c in range(WEIGHT_DMA_PARTS):
                cps.append(pltpu.make_async_copy(src.at[row, pl.ds(c * n, n)], dst.at[pl.ds(c * n, n)], sem))
        return cps

    @pl.when(i < nv_ref[0])
    def _():
        @pl.when(i == 0)
        def _():
            for cp in fetch(te_ref[0]):
                cp.start()

        @pl.when((i == 0) | (te_ref[i] != te_ref[jnp.maximum(i - 1, 0)]))
        def _():
            for cp in fetch(te_ref[i]):
                cp.wait()
            w1b[...] = w1f[...].astype(BF16)
            w3b[...] = w3f[...].astype(BF16)
            w2b[...] = w2f[...].astype(BF16)

            @pl.when(nx_ref[i] >= 0)
            def _():
                for cp in fetch(nx_ref[i]):
                    cp.start()

        lo, hi = _unpack_bf16_pairs(xs_ref[...])
        half = D_MODEL // 2
        d = functools.partial(jnp.dot, preferred_element_type=F32)
        h1 = d(lo, w1b[0:half, :]) + d(hi, w1b[half:, :])
        h3 = d(lo, w3b[0:half, :]) + d(hi, w3b[half:, :])
        hid = (h1 * _sigmoid(h1)) * h3
        y = d(hid.astype(BF16), w2b[...])
        ys_ref[...] = _pack_bf16_pairs(y[:, :half], y[:, half:])

    @pl.when(i >= nv_ref[0])
    def _():
        ys_ref[...] = jnp.zeros_like(ys_ref)


def _experts(xs, tile_expert, next_expert, n_valid, w1, w3, w2, layer):
    n_rows, W = xs.shape
    nt = n_rows // ROW_TILE
    D, F = w1.shape[1], w1.shape[2]

    def row_map(i, te, nx, nv):
        return (i, 0)

    hbm = pl.BlockSpec(memory_space=pl.ANY)
    return pl.pallas_call(
        functools.partial(_experts_kernel, layer=layer),
        grid_spec=pltpu.PrefetchScalarGridSpec(
            num_scalar_prefetch=3,
            grid=(nt,),
            in_specs=[pl.BlockSpec((ROW_TILE, W), row_map), hbm, hbm, hbm],
            out_specs=pl.BlockSpec((ROW_TILE, D // 2), row_map),
            scratch_shapes=[
                pltpu.VMEM((D, F), F32),
                pltpu.VMEM((D, F), F32),
                pltpu.VMEM((F, D), F32),
                pltpu.VMEM((D, F), BF16),
                pltpu.VMEM((D, F), BF16),
                pltpu.VMEM((F, D), BF16),
                pltpu.SemaphoreType.DMA(()),
            ],
        ),
        out_shape=jax.ShapeDtypeStruct((n_rows, D // 2), jnp.uint32),
        compiler_params=pltpu.CompilerParams(
            dimension_semantics=("arbitrary",), vmem_limit_bytes=VMEM_LIMIT),
        name="moe_experts",
    )(tile_expert, next_expert, n_valid, xs, w1, w3, w2)


def _combine_kernel(x_ref, info_ref, fw_ref, y1_ref, y2_ref, out_ref, *, final):
    info = info_ref[...]
    g1 = info[:, _I_G1:_I_G1 + 1]
    g2 = info[:, _I_G2:_I_G2 + 1]
    y1 = jnp.concatenate(_unpack_pairs_f32(y1_ref[...]), axis=1)
    y2 = jnp.concatenate(_unpack_pairs_f32(y2_ref[...]), axis=1)
    y = x_ref[...] + (g1 * y1 + g2 * y2)
    if final:
        y = _rms(y, fw_ref[...])
    out_ref[...] = y


def _combine(x2, part, info, y1, y2, final_w, final):
    T, D = x2.shape
    tp = T // MOE_PARTS
    tm = min(TOK_TILE, tp)
    nb = tp // tm
    x_spec = pl.BlockSpec((tm, D), lambda i: (part * nb + i, 0))
    half_spec = pl.BlockSpec((tm, D // 2), lambda i: (i, 0))
    return pl.pallas_call(
        functools.partial(_combine_kernel, final=final),
        grid=(nb,),
        in_specs=[x_spec, pl.BlockSpec((tm, LANES), lambda i: (i, 0)), _const_spec((1, D)),
                  half_spec, half_spec],
        out_specs=x_spec,
        out_shape=jax.ShapeDtypeStruct((T, D), F32),
        input_output_aliases={0: 0},
        compiler_params=pltpu.CompilerParams(
            dimension_semantics=("arbitrary",), vmem_limit_bytes=VMEM_LIMIT),
        name="moe_combine",
    )(x2, info, final_w[None, :], y1, y2)


def _hier_moe(x2, norm_w, w_group, b_group, w_expert, b_expert, w1, w3, w2, layer, final_w, final):
    T, D = x2.shape
    tp = T // MOE_PARTS
    nt = (2 * tp) // ROW_TILE + MOE_EXPERTS
    routed = []
    for part in range(MOE_PARTS):
        hnp, info, info_t, cnt = _router(x2, part, norm_w, w_group, b_group, w_expert, b_expert)
        pos, tile_expert, next_expert, n_valid = _plan(cnt[0].astype(jnp.int32), info_t, nt)
        pos1, pos2 = pos[0], pos[1]
        xs = _dispatch(hnp, pos1, pos2, nt * ROW_TILE)
        ys = _experts(xs, tile_expert, next_expert, n_valid, w1, w3, w2, layer)
        routed.append((info, _gather_rows(ys, pos1), _gather_rows(ys, pos2)))
    out = x2
    for part, (info, y1, y2) in enumerate(routed):
        out = _combine(out, part, info, y1, y2, final_w, final)
    return out


def kernel(x, norm_mix, norm_ffn, norm_final, even_w_in, hgrn_lb_logits, hgrn_norm_w, gla_w_up,
           gla_b_up, gla_norm_w, even_w_out, odd_w_in, ret_norm_w, conv_w, conv_b, rglru_w_r,
           rglru_b_r, rglru_w_i, rglru_b_i, rglru_lambda, odd_w_out, moe_w_group, moe_b_group,
           moe_w_expert, moe_b_expert, moe_w1, moe_w3, moe_w2):
    B, L, D = x.shape
    depth = norm_mix.shape[0]
    lb_table = jnp.cumsum(jax.nn.softmax(hgrn_lb_logits.astype(F32), axis=0), axis=0)
    w1 = moe_w1.reshape((-1,) + moe_w1.shape[2:])
    w3 = moe_w3.reshape((-1,) + moe_w3.shape[2:])
    w2 = moe_w2.reshape((-1,) + moe_w2.shape[2:])
    for l in range(depth):
        j = l // 2
        if l % 2 == 0:
            x = _even_mixer(x, norm_mix[l], even_w_in[j], lb_table[l], gla_w_up[j], gla_b_up[j],
                            hgrn_norm_w[j], gla_norm_w[j], even_w_out[j])
        else:
            x = _odd_mixer(x, norm_mix[l], odd_w_in[j], ret_norm_w[j], conv_w[j], conv_b[j],
                           rglru_w_r[j], rglru_b_r[j], rglru_w_i[j], rglru_b_i[j], rglru_lambda[j],
                           odd_w_out[j])
        x = _hier_moe(x.reshape(B * L, D), norm_ffn[l], moe_w_group[l], moe_b_group[l],
                      moe_w_expert[l], moe_b_expert[l], w1, w3, w2, l,
                      norm_final, l == depth - 1).reshape(B, L, D)
    return x
```

```python
import functools

import numpy as np
import jax
import jax.numpy as jnp
from jax import lax
from jax.experimental import pallas as pl
from jax.experimental.pallas import tpu as pltpu
from jax.experimental.pallas import tpu_sc as plsc

F32 = jnp.float32
BF16 = jnp.bfloat16

EPS = 1e-6
CHUNK = 64
GROUP = 8
LOG2E = 1.4426950408889634
LANES = 128
D_MODEL = 1024
MIX_WIDTH = 1024
N_HEADS = 8
GLA_RANK = 16
GLA_TAU = 16.0
RET_HEADS = 4
RET_DK = 64
ROPE_BASE = 10000.0
LRU_WIDTH = 512
LRU_BLOCKS = 8
LRU_CONV = 4
LRU_C = 8.0
MOE_GROUPS = 4
MOE_PER_GROUP = 8
MOE_EXPERTS = MOE_GROUPS * MOE_PER_GROUP
MOE_DFF = 512
NEG_BIG = -1e30

MIX_TILE = 512
TOK_TILE = 512
ROW_TILE = 512
COMBINE_PARTS = 4
SC_CORES = 2
SC_WORKERS = 32
SC_ROWS = 128
VMEM_LIMIT = 56 * 1024 * 1024


def _dot(a, b):
    return jnp.dot(a.astype(BF16), b.astype(BF16), preferred_element_type=F32)


def _dot_nt(a, b):
    return lax.dot_general(a.astype(BF16), b.astype(BF16), (((1,), (1,)), ((), ())),
                           preferred_element_type=F32)


def _dot_tn(a, b):
    return lax.dot_general(a.astype(BF16), b.astype(BF16), (((0,), (0,)), ((), ())),
                           preferred_element_type=F32)


def _split2(x):
    hi = x.astype(BF16)
    lo = (x - hi.astype(F32)).astype(BF16)
    return hi, lo


def _split3(x):
    hi = x.astype(BF16)
    r = x - hi.astype(F32)
    mid = r.astype(BF16)
    lo = (r - mid.astype(F32)).astype(BF16)
    return hi, mid, lo


def _dot_x3(a, b):
    ah, al = _split2(a)
    bh, bl = _split2(b)
    d = functools.partial(jnp.dot, preferred_element_type=F32)
    return d(ah, bh) + (d(ah, bl) + d(al, bh))


def _sigmoid(x):
    return 1.0 / (1.0 + jnp.exp(-x))


def _softplus(x):
    return jnp.maximum(x, 0.0) + jnp.log(1.0 + jnp.exp(-jnp.abs(x)))


def _rms(x, w):
    return x * lax.rsqrt(jnp.mean(x * x, axis=-1, keepdims=True) + EPS) * w


def _tri(n):
    r = lax.broadcasted_iota(jnp.int32, (n, n), 0)
    c = lax.broadcasted_iota(jnp.int32, (n, n), 1)
    return jnp.where(r >= c, 1.0, 0.0).astype(BF16)


def _cumsum_rows(tri3, g):
    return jnp.dot(tri3, jnp.concatenate(_split3(g), axis=0), preferred_element_type=F32)


_E_AQ, _E_AF, _E_AI, _E_AG = 0, 512, 1024, 1536
_E_BQ, _E_BK, _E_BV, _E_BR, _E_LOW = 2048, 2304, 2560, 3072, 3584
EVEN_COLS = 3712
_QK_COLS = 768


_LEVEL_BLOCKS = (32, 16, 8)


def _level_tables():
    t = np.arange(CHUNK)
    out = []
    for blk in _LEVEL_BLOCKS:
        odd = (t // blk) % 2 == 1
        out.append((np.where(odd, (t // blk) * blk - 1, -1), np.where(~odd, (t // blk) * blk + blk - 1, -1)))
    return out


def _gla_chunk(c, q_ref, k_ref, v_ref, b2, o_ref, st_ref, a_ref):
    n_units = _QK_COLS // LANES
    rows = pl.ds(pl.multiple_of(c * CHUNK, CHUNK), CHUNK)
    lane = lax.broadcasted_iota(jnp.int32, (1, LANES), 1)
    trow = lax.broadcasted_iota(jnp.int32, (GROUP, LANES), 0)
    unit_heads = [((u, None),) for u in range(4)] + [((4, 0), (5, 64)), ((6, 0), (7, 64))]
    qs = [q_ref[rows, u * LANES:(u + 1) * LANES] for u in range(n_units)]
    ks = [k_ref[rows, u * LANES:(u + 1) * LANES] for u in range(n_units)]
    bs = [b2[:, u * LANES:(u + 1) * LANES] for u in range(n_units)]
    vs = [v_ref[rows, h * LANES:(h + 1) * LANES] for h in range(N_HEADS)]

    for u in range(n_units):
        for j in range(CHUNK // GROUP):
            qj = qs[u][j * GROUP:(j + 1) * GROUP]
            bj = bs[u][j * GROUP:(j + 1) * GROUP]
            for s in range(0, GROUP, 2):
                pair = [qj * jnp.exp2(jnp.where(trow >= s + i, bj - bj[s + i:s + i + 1, :], NEG_BIG))
                        for i in (0, 1)]
                r0 = (j * GROUP + s) * GROUP
                a_ref[r0:r0 + 2 * GROUP, u * LANES:(u + 1) * LANES] = (
                    jnp.concatenate(pair, axis=0).astype(BF16))

    zk = jnp.zeros((CHUNK, LANES), F32)
    scs = []
    for pr in range(n_units // 2):
        u0, u1 = 2 * pr, 2 * pr + 1
        if u0 < 4:
            keys = jnp.concatenate([jnp.concatenate([ks[u0], zk], axis=1),
                                    jnp.concatenate([zk, ks[u1]], axis=1)], axis=0)
        else:
            lo0, hi0 = jnp.where(lane < 64, ks[u0], 0.0), jnp.where(lane >= 64, ks[u0], 0.0)
            lo1, hi1 = jnp.where(lane < 64, ks[u1], 0.0), jnp.where(lane >= 64, ks[u1], 0.0)
            keys = jnp.concatenate([jnp.concatenate([lo0, zk], axis=1), jnp.concatenate([hi0, zk], axis=1),
                                    jnp.concatenate([zk, lo1], axis=1), jnp.concatenate([zk, hi1], axis=1)],
                                   axis=0)
        scs.append(_dot_nt(a_ref[:, u0 * LANES:(u1 + 1) * LANES], keys))

    tables = _level_tables()
    tq = lax.broadcasted_iota(jnp.int32, (CHUNK, CHUNK), 0)
    ts = lax.broadcasted_iota(jnp.int32, (CHUNK, CHUNK), 1)
    covers = [((tq // blk) % 2 == 1) & ((tq // blk) == (ts // blk) + 1) & ((ts // blk) % 2 == 0)
              for blk in _LEVEL_BLOCKS]
    s_offs, o_inter = {}, {}
    for u in range(n_units):
        q, k, b = qs[u], ks[u], bs[u]
        heads = unit_heads[u]
        if len(heads) == 2:
            qm = jnp.concatenate([jnp.where((lane & 64) == lo, q, 0.0) for _, lo in heads], axis=0)
            bm = jnp.concatenate([b, b], axis=0)
        else:
            qm, bm = q, b
        s_off = None
        for (qref, kref), cover in zip(tables, covers):
            qrows, krows = [], []
            for g in range(CHUNK // GROUP):
                sl = slice(g * GROUP, (g + 1) * GROUP)
                rq, rk = int(qref[g * GROUP]), int(kref[g * GROUP])
                qrows.append(jnp.exp2(b[sl] - b[rq:rq + 1, :]) if rq >= 0 else jnp.zeros((GROUP, LANES), F32))
                krows.append(k[sl] * jnp.exp2(b[rk:rk + 1, :] - b[sl]) if rk >= 0
                             else jnp.zeros((GROUP, LANES), F32))
            qfac = jnp.concatenate(qrows, axis=0)
            if len(heads) == 2:
                qfac = jnp.concatenate([qfac, qfac], axis=0)
                cov = jnp.concatenate([cover, cover], axis=0)
            else:
                cov = cover
            part = _dot_nt(qm * qfac, jnp.concatenate(krows, axis=0))
            s_off = jnp.where(cov, part, 0.0 if s_off is None else s_off)
        eb = jnp.exp2(bm)
        b_last = b[CHUNK - 1:CHUNK, :]
        kd = (k * jnp.exp2(b_last - b)).astype(BF16)
        dec = jnp.exp2(b_last)
        qe = qm * eb
        for i, (head, lo) in enumerate(heads):
            sl = slice(i * CHUNK, (i + 1) * CHUNK)
            st = st_ref[head]
            o_inter[head] = _dot_nt(qe[sl], st)
            st_ref[head] = st * dec + _dot_tn(vs[head], kd)
            s_offs[head] = s_off[sl]

    for u in range(n_units):
        sc = scs[u // 2]
        paired = len(unit_heads[u]) == 2
        width = LANES if paired else CHUNK
        c0 = (u % 2) * width
        scol = (lane & 63) if paired else lane[:, :CHUNK]
        groups = []
        for j in range(CHUNK // GROUP):
            acc = jnp.zeros((GROUP, width), F32)
            for s in range(GROUP):
                r0 = (j * GROUP + s) * GROUP
                acc = jnp.where(scol == j * GROUP + s, sc[r0:r0 + GROUP, c0:c0 + width], acc)
            groups.append(acc)
        p = jnp.concatenate(groups, axis=0)
        if paired:
            vstack = jnp.concatenate([vs[h] for h, _ in unit_heads[u]], axis=0)
        for head, lo in unit_heads[u]:
            if lo is None:
                o = o_inter[head] + _dot(s_offs[head] + p, vs[head])
            else:
                o = (o_inter[head] + _dot(s_offs[head], vs[head])
                     + _dot(jnp.where((lane & 64) == lo, p, 0.0), vstack))
            o_ref[rows, head * LANES:(head + 1) * LANES] = o


def _even_kernel(x_ref, nw_ref, win_ref, lb_ref, wup_ref, bup_ref, hw_ref, wout_ref, out_ref,
                 q_ref, k_ref, g_ref, v_ref, gate_ref, o_ref, st_ref, a_ref):
    tl = x_ref.shape[1]

    @pl.when(pl.program_id(1) == 0)
    def _():
        st_ref[...] = jnp.zeros_like(st_ref)

    x = x_ref[0]
    hn = _rms(x, nw_ref[...]).astype(BF16)

    def proj(c0, n):
        return jnp.dot(hn, win_ref[:, c0:c0 + n], preferred_element_type=F32)

    q_ref[:, 0:512] = proj(_E_AQ, 512)
    lb = lb_ref[...]
    f = lb + (1.0 - lb) * _sigmoid(proj(_E_AF, 512))
    k_ref[:, 0:512] = 1.0 - f
    g_ref[:, 0:512] = jnp.log(f)
    v_ref[:, 0:512] = proj(_E_AI, 512)
    ag = proj(_E_AG, 512)
    gate_ref[:, 0:512] = ag * _sigmoid(ag)
    q_ref[:, 512:768] = proj(_E_BQ, 256) * (64.0 ** -0.5)
    k_ref[:, 512:768] = proj(_E_BK, 256)
    v_ref[:, 512:1024] = proj(_E_BV, 512)
    br = proj(_E_BR, 512)
    gate_ref[:, 512:1024] = br * _sigmoid(br)
    z = _dot_x3(proj(_E_LOW, LANES), wup_ref[...]) + bup_ref[...]
    g_ref[:, 512:768] = -_softplus(-z) * (1.0 / GLA_TAU)

    tri = _tri(CHUNK)
    tri3 = jnp.concatenate([tri, tri, tri], axis=1)

    def chunk_body(c, carry):
        rows = pl.ds(pl.multiple_of(c * CHUNK, CHUNK), CHUNK)
        b2 = _cumsum_rows(tri3, g_ref[rows, :]) * LOG2E
        _gla_chunk(c, q_ref, k_ref, v_ref, b2, o_ref, st_ref, a_ref)
        return carry

    lax.fori_loop(0, tl // CHUNK, chunk_body, 0)

    hw = hw_ref[...]
    ys = []
    for head in range(N_HEADS):
        sl = slice(head * LANES, (head + 1) * LANES)
        oh = o_ref[:, sl]
        ys.append(oh * lax.rsqrt(jnp.mean(oh * oh, axis=-1, keepdims=True) + EPS)
                  * hw[:, sl] * gate_ref[:, sl])
    y = jnp.concatenate(ys, axis=1).astype(BF16)
    out_ref[0] = x + jnp.dot(y, wout_ref[...], preferred_element_type=F32)


def _const_spec(shape):
    nd = len(shape)
    return pl.BlockSpec(shape, lambda *_: (0,) * nd)


def _even_mixer(x, norm_w, w_in, lb, w_up, b_up, hgrn_norm_w, gla_norm_w, w_out):
    B, L, D = x.shape
    tl = min(MIX_TILE, L)
    s = np.cumsum([0, 512, 512, 512, 512, 256, 256, 512, GLA_RANK, 512])
    cols = [w_in[:, s[i]:s[i + 1]] for i in range(9)]
    low = jnp.pad(cols[7], ((0, 0), (0, LANES - GLA_RANK)))
    w_in_r = jnp.concatenate(cols[:7] + [cols[8], low], axis=1).astype(BF16)
    w_up_p = jnp.pad(w_up, ((0, LANES - GLA_RANK), (0, 0)))
    hw = jnp.concatenate([hgrn_norm_w, gla_norm_w])[None, :]
    return pl.pallas_call(
        _even_kernel,
        grid=(B, L // tl),
        in_specs=[
            pl.BlockSpec((1, tl, D), lambda b, l: (b, l, 0)),
            _const_spec((1, D)),
            _const_spec((D, EVEN_COLS)),
            _const_spec((1, 512)),
            _const_spec((LANES, 256)),
            _const_spec((1, 256)),
            _const_spec((1, MIX_WIDTH)),
            _const_spec((MIX_WIDTH, D)),
        ],
        out_specs=pl.BlockSpec((1, tl, D), lambda b, l: (b, l, 0)),
        out_shape=jax.ShapeDtypeStruct((B, L, D), F32),
        scratch_shapes=[
            pltpu.VMEM((tl, _QK_COLS), F32),
            pltpu.VMEM((tl, _QK_COLS), F32),
            pltpu.VMEM((tl, _QK_COLS), F32),
            pltpu.VMEM((tl, MIX_WIDTH), F32),
            pltpu.VMEM((tl, MIX_WIDTH), F32),
            pltpu.VMEM((tl, MIX_WIDTH), F32),
            pltpu.VMEM((N_HEADS, LANES, LANES), F32),
            pltpu.VMEM((CHUNK * GROUP, _QK_COLS), BF16),
        ],
        compiler_params=pltpu.CompilerParams(
            dimension_semantics=("arbitrary", "arbitrary"), vmem_limit_bytes=VMEM_LIMIT),
        name="even_mixer",
    )(x, norm_w[None, :], w_in_r, lb[None, :], w_up_p, b_up[None, :], hw, w_out.astype(BF16))


_O_Q, _O_QR, _O_K, _O_KR, _O_V, _O_G, _O_X, _O_GATE = 0, 256, 512, 768, 1024, 1536, 2048, 2560
ODD_COLS = 3072


def _odd_kernel(x_ref, nw_ref, win_ref, cos_ref, sin_ref, dmat_ref, qdec_ref, kdec_ref, sdec_ref,
                rw_ref, cw_ref, cb_ref, wri_ref, bri_ref, lam_ref, wout_ref, out_ref,
                q_ref, k_ref, v_ref, o_ref, xe_ref, a_ref, u_ref, st_ref, h_ref):
    tl = x_ref.shape[1]

    @pl.when(pl.program_id(1) == 0)
    def _():
        st_ref[...] = jnp.zeros_like(st_ref)
        h_ref[...] = jnp.zeros_like(h_ref)
        xe_ref[0:8, :] = jnp.zeros((8, LRU_WIDTH), F32)

    x = x_ref[0]
    hn = _rms(x, nw_ref[...]).astype(BF16)

    def proj(c0, n):
        return jnp.dot(hn, win_ref[:, c0:c0 + n], preferred_element_type=F32)

    cos = cos_ref[...]
    sin = sin_ref[...]
    q_ref[...] = proj(_O_Q, 256) * cos + proj(_O_QR, 256) * sin
    k_ref[...] = (proj(_O_K, 256) * cos + proj(_O_KR, 256) * sin) * (RET_DK ** -0.5)
    v_ref[...] = proj(_O_V, 512)

    xe_ref[8:8 + tl, :] = proj(_O_X, LRU_WIDTH)
    cw = cw_ref[...]
    xc = cb_ref[...]
    for j in range(LRU_CONV):
        xc = xc + xe_ref[5 + j:5 + j + tl, :] * cw[j:j + 1, :]
    xe_ref[0:8, :] = xe_ref[tl:tl + 8, :]
    ri = jnp.dot(xc.astype(BF16), wri_ref[...], preferred_element_type=F32) + bri_ref[...]
    r = _sigmoid(ri[:, :LRU_WIDTH])
    i = _sigmoid(ri[:, LRU_WIDTH:])
    a = jnp.exp(-LRU_C * r * _softplus(-lam_ref[...]))
    a_ref[...] = a
    u_ref[...] = jnp.sqrt(1.0 - a * a) * (i * xc)

    lane = lax.broadcasted_iota(jnp.int32, (1, LANES), 1)
    crow = lax.broadcasted_iota(jnp.int32, (CHUNK, LRU_WIDTH), 0)

    def chunk_body(c, carry):
        rows = pl.ds(pl.multiple_of(c * CHUNK, CHUNK), CHUNK)
        for head in range(RET_HEADS):
            unit, half = head // 2, head % 2
            ul = slice(unit * LANES, (unit + 1) * LANES)
            vl = slice(head * LANES, (head + 1) * LANES)
            own = (lane < 64) if half == 0 else (lane >= 64)
            q = jnp.where(own, q_ref[rows, ul], 0.0)
            k = k_ref[rows, ul]
            v = v_ref[rows, vl]
            st = st_ref[head]
            o = _dot_nt(q, st) * qdec_ref[head]
            o = o + _dot(_dot_nt(q, k) * dmat_ref[head], v)
            st_ref[head] = st * sdec_ref[head] + _dot_tn(v, k * kdec_ref[head])
            o_ref[rows, vl] = o
        ca = a_ref[rows, :]
        ch = u_ref[rows, :]
        d = 1
        while d < CHUNK:
            keep = crow >= d
            sa = jnp.where(keep, pltpu.roll(ca, d, axis=0), 1.0)
            sh = jnp.where(keep, pltpu.roll(ch, d, axis=0), 0.0)
            ch = ca * sh + ch
            ca = ca * sa
            d *= 2
        ch = ch + ca * h_ref[...]
        h_ref[...] = ch[CHUNK - 1:CHUNK, :]
        o_ref[rows, 512:1024] = ch
        return carry

    lax.fori_loop(0, tl // CHUNK, chunk_body, 0, unroll=2)

    rw = rw_ref[...]
    ys = []
    for head in range(RET_HEADS):
        sl = slice(head * LANES, (head + 1) * LANES)
        oh = o_ref[:, sl]
        oh = oh - jnp.mean(oh, axis=-1, keepdims=True)
        ys.append(oh * lax.rsqrt(jnp.mean(oh * oh, axis=-1, keepdims=True) + EPS) * rw[:, sl])
    cg = proj(_O_G, 512)
    y_c = jnp.concatenate(ys, axis=1) * (cg * _sigmoid(cg))
    dg = proj(_O_GATE, LRU_WIDTH)
    gelu = 0.5 * dg * (1.0 + jnp.tanh(np.sqrt(2.0 / np.pi) * (dg + 0.044715 * (dg * dg * dg))))
    y_d = o_ref[:, 512:1024] * gelu
    y = jnp.concatenate([y_c, y_d], axis=1).astype(BF16)
    out_ref[0] = x + jnp.dot(y, wout_ref[...], preferred_element_type=F32)


def _rot_cols(w):
    d = w.shape[0]
    w4 = w.reshape(d, RET_HEADS, 2, RET_DK // 2)
    return jnp.stack([-w4[:, :, 1], w4[:, :, 0]], axis=2).reshape(d, RET_HEADS * RET_DK)


def _odd_mixer(x, norm_w, w_in, ret_norm_w, conv_w, conv_b, w_r, b_r, w_i, b_i, lam, w_out):
    B, L, D = x.shape
    tl = min(MIX_TILE, L)
    s = np.cumsum([0, 256, 256, 512, 512, 512, 512])
    wq, wk, wv, wg, wx, wgate = [w_in[:, s[i]:s[i + 1]] for i in range(6)]
    w_in_r = jnp.concatenate([wq, _rot_cols(wq), wk, _rot_cols(wk), wv, wg, wx, wgate],
                             axis=1).astype(BF16)
    half = RET_DK // 2
    inv = ROPE_BASE ** (-jnp.arange(half, dtype=F32) / half)
    ang = jnp.arange(L, dtype=F32)[:, None] * inv[None, :]
    cos = jnp.tile(jnp.cos(ang), (1, 2 * RET_HEADS))
    sin = jnp.tile(jnp.sin(ang), (1, 2 * RET_HEADS))
    lg = jnp.log1p(-jnp.exp2(-5.0 - jnp.arange(RET_HEADS, dtype=F32)))[:, None, None]
    t = jnp.arange(CHUNK, dtype=F32)
    rel = t[:, None] - t[None, :]
    dmat = jnp.where(rel >= 0, jnp.exp(lg * jnp.maximum(rel, 0.0)), 0.0)
    qdec = jnp.broadcast_to(jnp.exp(lg * (t[None, :, None] + 1.0)), (RET_HEADS, CHUNK, LANES))
    kdec = jnp.broadcast_to(jnp.exp(lg * (CHUNK - 1.0 - t[None, :, None])), (RET_HEADS, CHUNK, LANES))
    sdec = jnp.broadcast_to(jnp.exp(lg * float(CHUNK)), (RET_HEADS, 1, LANES))
    eye = jnp.eye(LRU_BLOCKS, dtype=F32)
    bd = LRU_WIDTH // LRU_BLOCKS

    def blockdiag(w):
        return (eye[:, None, :, None] * w[:, :, None, :]).reshape(LRU_WIDTH, LRU_WIDTH)

    wri = jnp.concatenate([blockdiag(w_r), blockdiag(w_i)], axis=1).astype(BF16)
    bri = jnp.concatenate([b_r, b_i])[None, :]
    return pl.pallas_call(
        _odd_kernel,
        grid=(B, L // tl),
        in_specs=[
            pl.BlockSpec((1, tl, D), lambda b, l: (b, l, 0)),
            _const_spec((1, D)),
            _const_spec((D, ODD_COLS)),
            pl.BlockSpec((tl, 256), lambda b, l: (l, 0)),
            pl.BlockSpec((tl, 256), lambda b, l: (l, 0)),
            _const_spec((RET_HEADS, CHUNK, CHUNK)),
            _const_spec((RET_HEADS, CHUNK, LANES)),
            _const_spec((RET_HEADS, CHUNK, LANES)),
            _const_spec((RET_HEADS, 1, LANES)),
            _const_spec((1, 512)),
            _const_spec((LRU_CONV, LRU_WIDTH)),
            _const_spec((1, LRU_WIDTH)),
            _const_spec((LRU_WIDTH, 2 * LRU_WIDTH)),
            _const_spec((1, 2 * LRU_WIDTH)),
            _const_spec((1, LRU_WIDTH)),
            _const_spec((MIX_WIDTH, D)),
        ],
        out_specs=pl.BlockSpec((1, tl, D), lambda b, l: (b, l, 0)),
        out_shape=jax.ShapeDtypeStruct((B, L, D), F32),
        scratch_shapes=[
            pltpu.VMEM((tl, 256), F32),
            pltpu.VMEM((tl, 256), F32),
            pltpu.VMEM((tl, 512), F32),
            pltpu.VMEM((tl, MIX_WIDTH), F32),
            pltpu.VMEM((tl + 8, LRU_WIDTH), F32),
            pltpu.VMEM((tl, LRU_WIDTH), F32),
            pltpu.VMEM((tl, LRU_WIDTH), F32),
            pltpu.VMEM((RET_HEADS, LANES, LANES), F32),
            pltpu.VMEM((1, LRU_WIDTH), F32),
        ],
        compiler_params=pltpu.CompilerParams(
            dimension_semantics=("arbitrary", "arbitrary"), vmem_limit_bytes=VMEM_LIMIT),
        name="odd_mixer",
    )(x, norm_w[None, :], w_in_r, cos, sin, dmat, qdec, kdec, sdec, ret_norm_w[None, :],
      conv_w, conv_b[None, :], wri, bri, lam[None, :], w_out.astype(BF16))


_I_E1, _I_E2, _I_G1, _I_G2, _I_R1, _I_R2 = 0, 1, 2, 3, 4, 5
_LOGIT_E0 = MOE_GROUPS


def _pack_bf16_pairs(lo, hi):
    lo_b = pltpu.bitcast(lo.astype(BF16).astype(F32), jnp.uint32)
    hi_b = pltpu.bitcast(hi.astype(BF16).astype(F32), jnp.uint32)
    return (lo_b >> 16) | (hi_b & jnp.uint32(0xFFFF0000))


def _unpack_pairs_f32(w):
    return pltpu.bitcast(w << 16, F32), pltpu.bitcast(w & jnp.uint32(0xFFFF0000), F32)


def _unpack_bf16_pairs(w):
    lo, hi = _unpack_pairs_f32(w)
    return lo.astype(BF16), hi.astype(BF16)


def _router_kernel(x_ref, nw_ref, wr_ref, br_ref, hnp_ref, info_ref, info_t_ref, cnt_ref):
    tm = x_ref.shape[0]

    @pl.when(pl.program_id(0) == 0)
    def _():
        cnt_ref[...] = jnp.zeros_like(cnt_ref)

    hn = _rms(x_ref[...], nw_ref[...])
    half = D_MODEL // 2
    hnp_ref[...] = _pack_bf16_pairs(hn[:, :half], hn[:, half:])

    logits = _dot_x3(hn, wr_ref[...]) + br_ref[...]
    lane = lax.broadcasted_iota(jnp.int32, (tm, LANES), 1).astype(F32)

    def first_max(vals):
        m = jnp.max(vals, axis=-1, keepdims=True)
        idx = jnp.min(jnp.where(vals == m, lane, float(LANES)), axis=-1, keepdims=True)
        return m, idx

    gl = jnp.where(lane < MOE_GROUPS, logits, NEG_BIG)
    gmax, gidx = first_max(gl)
    g_w = 1.0 / jnp.sum(jnp.exp(gl - gmax), axis=-1, keepdims=True)
    e0 = _LOGIT_E0 + MOE_PER_GROUP * gidx
    el = jnp.where((lane >= e0) & (lane < e0 + MOE_PER_GROUP), logits, NEG_BIG)
    m1, i1 = first_max(el)
    m2, i2 = first_max(jnp.where(lane == i1, NEG_BIG, el))
    p2 = jnp.exp(m2 - m1)
    w1 = 1.0 / (1.0 + p2)
    e1 = i1 - _LOGIT_E0
    e2 = i2 - _LOGIT_E0

    oh1 = jnp.where(lane == e1, 1.0, 0.0)
    oh2 = jnp.where(lane == e2, 1.0, 0.0)
    tri = _tri(tm)
    pre1 = jnp.dot(tri, oh1.astype(BF16), preferred_element_type=F32)
    pre2 = jnp.dot(tri, oh2.astype(BF16), preferred_element_type=F32)
    carry = cnt_ref[...]
    c1 = pre1[tm - 1:tm, :]
    r1 = jnp.sum(oh1 * (pre1 + carry), axis=-1, keepdims=True) - 1.0
    r2 = jnp.sum(oh2 * (pre2 + (carry + c1)), axis=-1, keepdims=True) - 1.0
    cnt_ref[...] = carry + c1 + pre2[tm - 1:tm, :]

    info = jnp.zeros((tm, LANES), F32)
    for ln, val in ((_I_E1, e1), (_I_E2, e2), (_I_G1, g_w * w1), (_I_G2, g_w * (w1 * p2)),
                    (_I_R1, r1), (_I_R2, r2)):
        info = jnp.where(lane == float(ln), val, info)
    info_ref[...] = info
    info_t_ref[...] = info.T[0:GROUP, :]


def _router(x2, norm_w, w_group, b_group, w_expert, b_expert):
    T, D = x2.shape
    tm = min(TOK_TILE, T)
    pad = LANES - MOE_GROUPS - MOE_EXPERTS
    wr = jnp.pad(jnp.concatenate([w_group, w_expert], axis=1), ((0, 0), (0, pad)))
    br = jnp.pad(jnp.concatenate([b_group, b_expert]), (0, pad))[None, :]
    return pl.pallas_call(
        _router_kernel,
        grid=(T // tm,),
        in_specs=[
            pl.BlockSpec((tm, D), lambda i: (i, 0)),
            _const_spec((1, D)),
            _const_spec((D, LANES)),
            _const_spec((1, LANES)),
        ],
        out_specs=[
            pl.BlockSpec((tm, D // 2), lambda i: (i, 0)),
            pl.BlockSpec((tm, LANES), lambda i: (i, 0)),
            pl.BlockSpec((GROUP, tm), lambda i: (0, i)),
            _const_spec((1, LANES)),
        ],
        out_shape=[
            jax.ShapeDtypeStruct((T, D // 2), jnp.uint32),
            jax.ShapeDtypeStruct((T, LANES), F32),
            jax.ShapeDtypeStruct((GROUP, T), F32),
            jax.ShapeDtypeStruct((1, LANES), F32),
        ],
        compiler_params=pltpu.CompilerParams(
            dimension_semantics=("arbitrary",), vmem_limit_bytes=VMEM_LIMIT),
        name="moe_router",
    )(x2, norm_w[None, :], wr, br)


def _plan_kernel(cnt_ref, info_ref, pos_ref, te_ref, nx_ref, nv_ref):
    nt = te_ref.shape[0]
    e1 = info_ref[_I_E1:_I_E1 + 1, :]
    e2 = info_ref[_I_E2:_I_E2 + 1, :]
    s1 = jnp.zeros_like(e1)
    s2 = jnp.zeros_like(e2)
    tiles = [lax.div(cnt_ref[e] + (ROW_TILE - 1), jnp.int32(ROW_TILE)) for e in range(MOE_EXPERTS)]
    following = [None] * MOE_EXPERTS
    nxt = jnp.int32(-1)
    for e in reversed(range(MOE_EXPERTS)):
        following[e] = nxt
        nxt = jnp.where(tiles[e] > 0, jnp.int32(e), nxt)
    start = jnp.int32(0)
    for e in range(MOE_EXPERTS):
        seg = (start * ROW_TILE).astype(F32)
        s1 = jnp.where(e1 == float(e), seg, s1)
        s2 = jnp.where(e2 == float(e), seg, s2)

        def fill(i, carry, e=e):
            te_ref[i] = jnp.int32(e)
            nx_ref[i] = following[e]
            return carry

        lax.fori_loop(start, start + tiles[e], fill, 0)
        start = start + tiles[e]
    nv_ref[0] = start

    def fill_tail(i, carry):
        te_ref[i] = jnp.int32(MOE_EXPERTS - 1)
        nx_ref[i] = jnp.int32(-1)
        return carry

    lax.fori_loop(start, nt, fill_tail, 0)
    pos_ref[0:1, :] = (s1 + info_ref[_I_R1:_I_R1 + 1, :]).astype(jnp.int32)
    pos_ref[1:2, :] = (s2 + info_ref[_I_R2:_I_R2 + 1, :]).astype(jnp.int32)


def _plan(cnt, info_t, nt):
    T = info_t.shape[1]
    smem = pl.BlockSpec(memory_space=pltpu.SMEM)
    return pl.pallas_call(
        _plan_kernel,
        in_specs=[smem, pl.BlockSpec(memory_space=pltpu.VMEM)],
        out_specs=[pl.BlockSpec(memory_space=pltpu.VMEM), smem, smem, smem],
        out_shape=[
            jax.ShapeDtypeStruct((2, T), jnp.int32),
            jax.ShapeDtypeStruct((nt,), jnp.int32),
            jax.ShapeDtypeStruct((nt,), jnp.int32),
            jax.ShapeDtypeStruct((1,), jnp.int32),
        ],
        name="moe_plan",
    )(cnt, info_t)


def _sc_worker():
    return lax.axis_index("s") * SC_CORES + lax.axis_index("c")


def _dispatch(hnp, pos1, pos2, n_rows):
    T, W = hnp.shape
    per_w = T // SC_WORKERS
    n_chunks = per_w // SC_ROWS
    mesh = plsc.VectorSubcoreMesh(core_axis_name="c", subcore_axis_name="s")

    @functools.partial(
        pl.kernel, mesh=mesh,
        out_type=jax.ShapeDtypeStruct((n_rows, W), jnp.uint32),
        scratch_types=[
            pltpu.VMEM((SC_ROWS,), jnp.int32),
            pltpu.VMEM((SC_ROWS,), jnp.int32),
            pltpu.VMEM((SC_ROWS, W), jnp.uint32),
        ],
        name="moe_dispatch_sc",
    )
    def scatter(hnp_hbm, p1_hbm, p2_hbm, xs_hbm, i1_v, i2_v, rows_v):
        base0 = _sc_worker() * per_w

        @pl.loop(0, n_chunks)
        def _(j):
            base = pl.multiple_of(base0 + j * SC_ROWS, SC_ROWS)
            pltpu.sync_copy(p1_hbm.at[pl.ds(base, SC_ROWS)], i1_v)
            pltpu.sync_copy(p2_hbm.at[pl.ds(base, SC_ROWS)], i2_v)
            pltpu.sync_copy(hnp_hbm.at[pl.ds(base, SC_ROWS)], rows_v)
            pltpu.sync_copy(rows_v, xs_hbm.at[i1_v])
            pltpu.sync_copy(rows_v, xs_hbm.at[i2_v])

    return scatter(hnp, pos1, pos2)


def _gather_rows(table, idx):
    T = idx.shape[0]
    W = table.shape[1]
    per_w = T // SC_WORKERS
    rows = SC_ROWS
    n_chunks = per_w // rows
    mesh = plsc.VectorSubcoreMesh(core_axis_name="c", subcore_axis_name="s")

    @functools.partial(
        pl.kernel, mesh=mesh,
        out_type=jax.ShapeDtypeStruct((T, W), table.dtype),
        scratch_types=[
            pltpu.VMEM((rows,), jnp.int32),
            pltpu.VMEM((rows, W), table.dtype),
        ],
        name="moe_gather_sc",
    )
    def gather(table_hbm, idx_hbm, out_hbm, i_v, rows_v):
        base0 = _sc_worker() * per_w

        @pl.loop(0, n_chunks)
        def _(j):
            base = pl.multiple_of(base0 + j * rows, rows)
            pltpu.sync_copy(idx_hbm.at[pl.ds(base, rows)], i_v)
            pltpu.sync_copy(table_hbm.at[i_v], rows_v)
            pltpu.sync_copy(rows_v, out_hbm.at[pl.ds(base, rows)])

    return gather(table, idx)


def _experts_kernel(te_ref, nx_ref, nv_ref, xs_ref, w1_hbm, w3_hbm, w2_hbm, ys_ref,
                    w1f, w3f, w2f, w1b, w3b, w2b, sem, *, layer):
    i = pl.program_id(0)

    def fetch(e):
        row = e + layer * MOE_EXPERTS
        return [pltpu.make_async_copy(w1_hbm.at[row], w1f, sem),
                pltpu.make_async_copy(w3_hbm.at[row], w3f, sem),
                pltpu.make_async_copy(w2_hbm.at[row], w2f, sem)]

    @pl.when(i < nv_ref[0])
    def _():
        @pl.when(i == 0)
        def _():
            for cp in fetch(te_ref[0]):
                cp.start()

        @pl.when((i == 0) | (te_ref[i] != te_ref[jnp.maximum(i - 1, 0)]))
        def _():
            for cp in fetch(te_ref[i]):
                cp.wait()
            w1b[...] = w1f[...].astype(BF16)
            w3b[...] = w3f[...].astype(BF16)
            w2b[...] = w2f[...].astype(BF16)

            @pl.when(nx_ref[i] >= 0)
            def _():
                for cp in fetch(nx_ref[i]):
                    cp.start()

        lo, hi = _unpack_bf16_pairs(xs_ref[...])
        half = D_MODEL // 2
        d = functools.partial(jnp.dot, preferred_element_type=F32)
        h1 = d(lo, w1b[0:half, :]) + d(hi, w1b[half:, :])
        h3 = d(lo, w3b[0:half, :]) + d(hi, w3b[half:, :])
        hid = (h1 * _sigmoid(h1)) * h3
        y = d(hid.astype(BF16), w2b[...])
        ys_ref[...] = _pack_bf16_pairs(y[:, :half], y[:, half:])

    @pl.when(i >= nv_ref[0])
    def _():
        ys_ref[...] = jnp.zeros_like(ys_ref)


def _experts(xs, tile_expert, next_expert, n_valid, w1, w3, w2, layer):
    n_rows, W = xs.shape
    nt = n_rows // ROW_TILE
    D, F = w1.shape[1], w1.shape[2]

    def row_map(i, te, nx, nv):
        return (i, 0)

    hbm = pl.BlockSpec(memory_space=pl.ANY)
    return pl.pallas_call(
        functools.partial(_experts_kernel, layer=layer),
        grid_spec=pltpu.PrefetchScalarGridSpec(
            num_scalar_prefetch=3,
            grid=(nt,),
            in_specs=[pl.BlockSpec((ROW_TILE, W), row_map), hbm, hbm, hbm],
            out_specs=pl.BlockSpec((ROW_TILE, D // 2), row_map),
            scratch_shapes=[
                pltpu.VMEM((D, F), F32),
                pltpu.VMEM((D, F), F32),
                pltpu.VMEM((F, D), F32),
                pltpu.VMEM((D, F), BF16),
                pltpu.VMEM((D, F), BF16),
                pltpu.VMEM((F, D), BF16),
                pltpu.SemaphoreType.DMA(()),
            ],
        ),
        out_shape=jax.ShapeDtypeStruct((n_rows, D // 2), jnp.uint32),
        compiler_params=pltpu.CompilerParams(
            dimension_semantics=("arbitrary",), vmem_limit_bytes=VMEM_LIMIT),
        name="moe_experts",
    )(tile_expert, next_expert, n_valid, xs, w1, w3, w2)


def _combine_kernel(x_ref, info_ref, fw_ref, y1_ref, y2_ref, out_ref, *, final):
    info = info_ref[...]
    g1 = info[:, _I_G1:_I_G1 + 1]
    g2 = info[:, _I_G2:_I_G2 + 1]
    y1 = jnp.concatenate(_unpack_pairs_f32(y1_ref[...]), axis=1)
    y2 = jnp.concatenate(_unpack_pairs_f32(y2_ref[...]), axis=1)
    y = x_ref[...] + (g1 * y1 + g2 * y2)
    if final:
        y = _rms(y, fw_ref[...])
    out_ref[...] = y


def _combine(x2, part, info, y1, y2, final_w, final):
    T, D = x2.shape
    tp = T // COMBINE_PARTS
    tm = min(TOK_TILE, tp)
    nb = tp // tm
    x_spec = pl.BlockSpec((tm, D), lambda i: (part * nb + i, 0))
    half_spec = pl.BlockSpec((tm, D // 2), lambda i: (i, 0))
    return pl.pallas_call(
        functools.partial(_combine_kernel, final=final),
        grid=(nb,),
        in_specs=[x_spec, pl.BlockSpec((tm, LANES), lambda i: (part * nb + i, 0)),
                  _const_spec((1, D)), half_spec, half_spec],
        out_specs=x_spec,
        out_shape=jax.ShapeDtypeStruct((T, D), F32),
        input_output_aliases={0: 0},
        compiler_params=pltpu.CompilerParams(
            dimension_semantics=("arbitrary",), vmem_limit_bytes=VMEM_LIMIT),
        name="moe_combine",
    )(x2, info, final_w[None, :], y1, y2)


def _hier_moe(x2, norm_w, w_group, b_group, w_expert, b_expert, w1, w3, w2, layer, final_w, final):
    T, D = x2.shape
    nt = (2 * T) // ROW_TILE + MOE_EXPERTS
    hnp, info, info_t, cnt = _router(x2, norm_w, w_group, b_group, w_expert, b_expert)
    pos, tile_expert, next_expert, n_valid = _plan(cnt[0].astype(jnp.int32), info_t, nt)
    xs = _dispatch(hnp, pos[0], pos[1], nt * ROW_TILE)
    ys = _experts(xs, tile_expert, next_expert, n_valid, w1, w3, w2, layer)
    tp = T // COMBINE_PARTS
    gathered = [(_gather_rows(ys, pos[0, p * tp:(p + 1) * tp]), _gather_rows(ys, pos[1, p * tp:(p + 1) * tp]))
                for p in range(COMBINE_PARTS)]
    out = x2
    for part, (y1, y2) in enumerate(gathered):
        out = _combine(out, part, info, y1, y2, final_w, final)
    return out


def kernel(x, norm_mix, norm_ffn, norm_final, even_w_in, hgrn_lb_logits, hgrn_norm_w, gla_w_up,
           gla_b_up, gla_norm_w, even_w_out, odd_w_in, ret_norm_w, conv_w, conv_b, rglru_w_r,
           rglru_b_r, rglru_w_i, rglru_b_i, rglru_lambda, odd_w_out, moe_w_group, moe_b_group,
           moe_w_expert, moe_b_expert, moe_w1, moe_w3, moe_w2):
    B, L, D = x.shape
    depth = norm_mix.shape[0]
    lb_table = jnp.cumsum(jax.nn.softmax(hgrn_lb_logits.astype(F32), axis=0), axis=0)
    w1 = moe_w1.reshape((-1,) + moe_w1.shape[2:])
    w3 = moe_w3.reshape((-1,) + moe_w3.shape[2:])
    w2 = moe_w2.reshape((-1,) + moe_w2.shape[2:])
    for l in range(depth):
        j = l // 2
        if l % 2 == 0:
            x = _even_mixer(x, norm_mix[l], even_w_in[j], lb_table[l], gla_w_up[j], gla_b_up[j],
                            hgrn_norm_w[j], gla_norm_w[j], even_w_out[j])
        else:
            x = _odd_mixer(x, norm_mix[l], odd_w_in[j], ret_norm_w[j], conv_w[j], conv_b[j],
                           rglru_w_r[j], rglru_b_r[j], rglru_w_i[j], rglru_b_i[j], rglru_lambda[j],
                           odd_w_out[j])
        x = _hier_moe(x.reshape(B * L, D), norm_ffn[l], moe_w_group[l], moe_b_group[l],
                      moe_w_expert[l], moe_b_expert[l], w1, w3, w2, l,
                      norm_final, l == depth - 1).reshape(B, L, D)
    return x
```

```python
import functools

import numpy as np
import jax
import jax.numpy as jnp
from jax import lax
from jax.experimental import pallas as pl
from jax.experimental.pallas import tpu as pltpu
from jax.experimental.pallas import tpu_sc as plsc

F32 = jnp.float32
BF16 = jnp.bfloat16

EPS = 1e-6
CHUNK = 64
GROUP = 8
LOG2E = 1.4426950408889634
LANES = 128
D_MODEL = 1024
MIX_WIDTH = 1024
N_HEADS = 8
GLA_RANK = 16
GLA_TAU = 16.0
RET_HEADS = 4
RET_DK = 64
ROPE_BASE = 10000.0
LRU_WIDTH = 512
LRU_BLOCKS = 8
LRU_CONV = 4
LRU_C = 8.0
MOE_GROUPS = 4
MOE_PER_GROUP = 8
MOE_EXPERTS = MOE_GROUPS * MOE_PER_GROUP
MOE_DFF = 512
NEG_BIG = -1e30

MIX_TILE = 512
EVEN_SPLIT = 2
TOK_TILE = 512
ROW_TILE = 512
COMBINE_PARTS = 4
SC_CORES = 2
SC_WORKERS = 32
SC_ROWS = 128
VMEM_LIMIT = 56 * 1024 * 1024


def _dot(a, b):
    return jnp.dot(a.astype(BF16), b.astype(BF16), preferred_element_type=F32)


def _dot_nt(a, b):
    return lax.dot_general(a.astype(BF16), b.astype(BF16), (((1,), (1,)), ((), ())),
                           preferred_element_type=F32)


def _dot_tn(a, b):
    return lax.dot_general(a.astype(BF16), b.astype(BF16), (((0,), (0,)), ((), ())),
                           preferred_element_type=F32)


def _split2(x):
    hi = x.astype(BF16)
    lo = (x - hi.astype(F32)).astype(BF16)
    return hi, lo


def _split3(x):
    hi = x.astype(BF16)
    r = x - hi.astype(F32)
    mid = r.astype(BF16)
    lo = (r - mid.astype(F32)).astype(BF16)
    return hi, mid, lo


def _dot_x3(a, b):
    ah, al = _split2(a)
    bh, bl = _split2(b)
    d = functools.partial(jnp.dot, preferred_element_type=F32)
    return d(ah, bh) + (d(ah, bl) + d(al, bh))


def _sigmoid(x):
    return 1.0 / (1.0 + jnp.exp(-x))


def _softplus(x):
    return jnp.maximum(x, 0.0) + jnp.log(1.0 + jnp.exp(-jnp.abs(x)))


def _rms(x, w):
    return x * lax.rsqrt(jnp.mean(x * x, axis=-1, keepdims=True) + EPS) * w


def _tri(n):
    r = lax.broadcasted_iota(jnp.int32, (n, n), 0)
    c = lax.broadcasted_iota(jnp.int32, (n, n), 1)
    return jnp.where(r >= c, 1.0, 0.0).astype(BF16)


def _cumsum_rows(tri3, g):
    return jnp.dot(tri3, jnp.concatenate(_split3(g), axis=0), preferred_element_type=F32)


_E_AQ, _E_AF, _E_AI, _E_AG = 0, 512, 1024, 1536
_E_BQ, _E_BK, _E_BV, _E_BR, _E_LOW = 2048, 2304, 2560, 3072, 3584
EVEN_COLS = 3712
_QK_COLS = 768


_LEVEL_BLOCKS = (32, 16, 8)


def _level_tables():
    t = np.arange(CHUNK)
    out = []
    for blk in _LEVEL_BLOCKS:
        odd = (t // blk) % 2 == 1
        out.append((np.where(odd, (t // blk) * blk - 1, -1), np.where(~odd, (t // blk) * blk + blk - 1, -1)))
    return out


def _gla_chunk(rows, q_ref, k_ref, v_ref, b2, o_ref, st_ref, a_ref):
    n_units = _QK_COLS // LANES
    lane = lax.broadcasted_iota(jnp.int32, (1, LANES), 1)
    trow = lax.broadcasted_iota(jnp.int32, (GROUP, LANES), 0)
    unit_heads = [((u, None),) for u in range(4)] + [((4, 0), (5, 64)), ((6, 0), (7, 64))]
    qs = [q_ref[rows, u * LANES:(u + 1) * LANES] for u in range(n_units)]
    ks = [k_ref[rows, u * LANES:(u + 1) * LANES] for u in range(n_units)]
    bs = [b2[:, u * LANES:(u + 1) * LANES] for u in range(n_units)]
    vs = [v_ref[rows, h * LANES:(h + 1) * LANES] for h in range(N_HEADS)]

    for u in range(n_units):
        for j in range(CHUNK // GROUP):
            qj = qs[u][j * GROUP:(j + 1) * GROUP]
            bj = bs[u][j * GROUP:(j + 1) * GROUP]
            for s in range(0, GROUP, 2):
                pair = [qj * jnp.exp2(jnp.where(trow >= s + i, bj - bj[s + i:s + i + 1, :], NEG_BIG))
                        for i in (0, 1)]
                r0 = (j * GROUP + s) * GROUP
                a_ref[r0:r0 + 2 * GROUP, u * LANES:(u + 1) * LANES] = (
                    jnp.concatenate(pair, axis=0).astype(BF16))

    zk = jnp.zeros((CHUNK, LANES), F32)
    scs = []
    for pr in range(n_units // 2):
        u0, u1 = 2 * pr, 2 * pr + 1
        if u0 < 4:
            keys = jnp.concatenate([jnp.concatenate([ks[u0], zk], axis=1),
                                    jnp.concatenate([zk, ks[u1]], axis=1)], axis=0)
        else:
            lo0, hi0 = jnp.where(lane < 64, ks[u0], 0.0), jnp.where(lane >= 64, ks[u0], 0.0)
            lo1, hi1 = jnp.where(lane < 64, ks[u1], 0.0), jnp.where(lane >= 64, ks[u1], 0.0)
            keys = jnp.concatenate([jnp.concatenate([lo0, zk], axis=1), jnp.concatenate([hi0, zk], axis=1),
                                    jnp.concatenate([zk, lo1], axis=1), jnp.concatenate([zk, hi1], axis=1)],
                                   axis=0)
        scs.append(_dot_nt(a_ref[:, u0 * LANES:(u1 + 1) * LANES], keys))

    tables = _level_tables()
    tq = lax.broadcasted_iota(jnp.int32, (CHUNK, CHUNK), 0)
    ts = lax.broadcasted_iota(jnp.int32, (CHUNK, CHUNK), 1)
    covers = [((tq // blk) % 2 == 1) & ((tq // blk) == (ts // blk) + 1) & ((ts // blk) % 2 == 0)
              for blk in _LEVEL_BLOCKS]
    s_offs, o_inter = {}, {}
    for u in range(n_units):
        q, k, b = qs[u], ks[u], bs[u]
        heads = unit_heads[u]
        if len(heads) == 2:
            qm = jnp.concatenate([jnp.where((lane & 64) == lo, q, 0.0) for _, lo in heads], axis=0)
            bm = jnp.concatenate([b, b], axis=0)
        else:
            qm, bm = q, b
        s_off = None
        for (qref, kref), cover in zip(tables, covers):
            qrows, krows = [], []
            for g in range(CHUNK // GROUP):
                sl = slice(g * GROUP, (g + 1) * GROUP)
                rq, rk = int(qref[g * GROUP]), int(kref[g * GROUP])
                qrows.append(jnp.exp2(b[sl] - b[rq:rq + 1, :]) if rq >= 0 else jnp.zeros((GROUP, LANES), F32))
                krows.append(k[sl] * jnp.exp2(b[rk:rk + 1, :] - b[sl]) if rk >= 0
                             else jnp.zeros((GROUP, LANES), F32))
            qfac = jnp.concatenate(qrows, axis=0)
            if len(heads) == 2:
                qfac = jnp.concatenate([qfac, qfac], axis=0)
                cov = jnp.concatenate([cover, cover], axis=0)
            else:
                cov = cover
            part = _dot_nt(qm * qfac, jnp.concatenate(krows, axis=0))
            s_off = jnp.where(cov, part, 0.0 if s_off is None else s_off)
        eb = jnp.exp2(bm)
        b_last = b[CHUNK - 1:CHUNK, :]
        kd = (k * jnp.exp2(b_last - b)).astype(BF16)
        dec = jnp.exp2(b_last)
        qe = qm * eb
        for i, (head, lo) in enumerate(heads):
            sl = slice(i * CHUNK, (i + 1) * CHUNK)
            st = st_ref[head]
            o_inter[head] = _dot_nt(qe[sl], st)
            st_ref[head] = st * dec + _dot_tn(vs[head], kd)
            s_offs[head] = s_off[sl]

    for u in range(n_units):
        sc = scs[u // 2]
        paired = len(unit_heads[u]) == 2
        width = LANES if paired else CHUNK
        c0 = (u % 2) * width
        scol = (lane & 63) if paired else lane[:, :CHUNK]
        groups = []
        for j in range(CHUNK // GROUP):
            acc = jnp.zeros((GROUP, width), F32)
            for s in range(GROUP):
                r0 = (j * GROUP + s) * GROUP
                acc = jnp.where(scol == j * GROUP + s, sc[r0:r0 + GROUP, c0:c0 + width], acc)
            groups.append(acc)
        p = jnp.concatenate(groups, axis=0)
        if paired:
            vstack = jnp.concatenate([vs[h] for h, _ in unit_heads[u]], axis=0)
        for head, lo in unit_heads[u]:
            if lo is None:
                o = o_inter[head] + _dot(s_offs[head] + p, vs[head])
            else:
                o = (o_inter[head] + _dot(s_offs[head], vs[head])
                     + _dot(jnp.where((lane & 64) == lo, p, 0.0), vstack))
            o_ref[rows, head * LANES:(head + 1) * LANES] = o


def _even_kernel(x_ref, nw_ref, win_ref, lb_ref, wup_ref, bup_ref, hw_ref, wout_ref, out_ref,
                 q_ref, k_ref, g_ref, v_ref, gate_ref, o_ref, st_ref, a_ref):
    tl = x_ref.shape[1]
    parts = [(r0, tl // EVEN_SPLIT) for r0 in range(0, tl, tl // EVEN_SPLIT)]

    @pl.when(pl.program_id(1) == 0)
    def _():
        st_ref[...] = jnp.zeros_like(st_ref)

    def project(r0, n):
        rs = slice(r0, r0 + n)
        hn = _rms(x_ref[0, rs, :], nw_ref[...]).astype(BF16)

        def proj(c0, w):
            return jnp.dot(hn, win_ref[:, c0:c0 + w], preferred_element_type=F32)

        q_ref[rs, 0:512] = proj(_E_AQ, 512)
        lb = lb_ref[...]
        f = lb + (1.0 - lb) * _sigmoid(proj(_E_AF, 512))
        k_ref[rs, 0:512] = 1.0 - f
        g_ref[rs, 0:512] = jnp.log(f)
        v_ref[rs, 0:512] = proj(_E_AI, 512)
        ag = proj(_E_AG, 512)
        gate_ref[rs, 0:512] = ag * _sigmoid(ag)
        q_ref[rs, 512:768] = proj(_E_BQ, 256) * (64.0 ** -0.5)
        k_ref[rs, 512:768] = proj(_E_BK, 256)
        v_ref[rs, 512:1024] = proj(_E_BV, 512)
        br = proj(_E_BR, 512)
        gate_ref[rs, 512:1024] = br * _sigmoid(br)
        z = _dot_x3(proj(_E_LOW, LANES), wup_ref[...]) + bup_ref[...]
        g_ref[rs, 512:768] = -_softplus(-z) * (1.0 / GLA_TAU)

    tri = _tri(CHUNK)
    tri3 = jnp.concatenate([tri, tri, tri], axis=1)

    def finish(r0, n):
        rs = slice(r0, r0 + n)
        hw = hw_ref[...]
        ys = []
        for head in range(N_HEADS):
            sl = slice(head * LANES, (head + 1) * LANES)
            oh = o_ref[rs, sl]
            ys.append(oh * lax.rsqrt(jnp.mean(oh * oh, axis=-1, keepdims=True) + EPS)
                      * hw[:, sl] * gate_ref[rs, sl])
        y = jnp.concatenate(ys, axis=1).astype(BF16)
        out_ref[0, rs, :] = x_ref[0, rs, :] + jnp.dot(y, wout_ref[...], preferred_element_type=F32)

    for r0, n in parts:
        project(r0, n)
    for r0, n in parts:
        for c in range(r0 // CHUNK, (r0 + n) // CHUNK):
            rows = slice(c * CHUNK, (c + 1) * CHUNK)
            b2 = _cumsum_rows(tri3, g_ref[rows, :]) * LOG2E
            _gla_chunk(rows, q_ref, k_ref, v_ref, b2, o_ref, st_ref, a_ref.at[c % 2])
    for r0, n in parts:
        finish(r0, n)


def _const_spec(shape):
    nd = len(shape)
    return pl.BlockSpec(shape, lambda *_: (0,) * nd)


def _even_mixer(x, norm_w, w_in, lb, w_up, b_up, hgrn_norm_w, gla_norm_w, w_out):
    B, L, D = x.shape
    tl = min(MIX_TILE, L)
    s = np.cumsum([0, 512, 512, 512, 512, 256, 256, 512, GLA_RANK, 512])
    cols = [w_in[:, s[i]:s[i + 1]] for i in range(9)]
    low = jnp.pad(cols[7], ((0, 0), (0, LANES - GLA_RANK)))
    w_in_r = jnp.concatenate(cols[:7] + [cols[8], low], axis=1).astype(BF16)
    w_up_p = jnp.pad(w_up, ((0, LANES - GLA_RANK), (0, 0)))
    hw = jnp.concatenate([hgrn_norm_w, gla_norm_w])[None, :]
    return pl.pallas_call(
        _even_kernel,
        grid=(B, L // tl),
        in_specs=[
            pl.BlockSpec((1, tl, D), lambda b, l: (b, l, 0)),
            _const_spec((1, D)),
            _const_spec((D, EVEN_COLS)),
            _const_spec((1, 512)),
            _const_spec((LANES, 256)),
            _const_spec((1, 256)),
            _const_spec((1, MIX_WIDTH)),
            _const_spec((MIX_WIDTH, D)),
        ],
        out_specs=pl.BlockSpec((1, tl, D), lambda b, l: (b, l, 0)),
        out_shape=jax.ShapeDtypeStruct((B, L, D), F32),
        scratch_shapes=[
            pltpu.VMEM((tl, _QK_COLS), F32),
            pltpu.VMEM((tl, _QK_COLS), F32),
            pltpu.VMEM((tl, _QK_COLS), F32),
            pltpu.VMEM((tl, MIX_WIDTH), F32),
            pltpu.VMEM((tl, MIX_WIDTH), F32),
            pltpu.VMEM((tl, MIX_WIDTH), F32),
            pltpu.VMEM((N_HEADS, LANES, LANES), F32),
            pltpu.VMEM((2, CHUNK * GROUP, _QK_COLS), BF16),
        ],
        compiler_params=pltpu.CompilerParams(
            dimension_semantics=("arbitrary", "arbitrary"), vmem_limit_bytes=VMEM_LIMIT),
        name="even_mixer",
    )(x, norm_w[None, :], w_in_r, lb[None, :], w_up_p, b_up[None, :], hw, w_out.astype(BF16))


_O_Q, _O_QR, _O_K, _O_KR, _O_V, _O_G, _O_X, _O_GATE = 0, 256, 512, 768, 1024, 1536, 2048, 2560
ODD_COLS = 3072


def _odd_kernel(x_ref, nw_ref, win_ref, cos_ref, sin_ref, dmat_ref, qdec_ref, kdec_ref, sdec_ref,
                rw_ref, cw_ref, cb_ref, wri_ref, bri_ref, lam_ref, wout_ref, out_ref,
                q_ref, k_ref, v_ref, o_ref, xe_ref, a_ref, u_ref, st_ref, h_ref):
    tl = x_ref.shape[1]

    @pl.when(pl.program_id(1) == 0)
    def _():
        st_ref[...] = jnp.zeros_like(st_ref)
        h_ref[...] = jnp.zeros_like(h_ref)
        xe_ref[0:8, :] = jnp.zeros((8, LRU_WIDTH), F32)

    x = x_ref[0]
    hn = _rms(x, nw_ref[...]).astype(BF16)

    def proj(c0, n):
        return jnp.dot(hn, win_ref[:, c0:c0 + n], preferred_element_type=F32)

    cos = cos_ref[...]
    sin = sin_ref[...]
    q_ref[...] = proj(_O_Q, 256) * cos + proj(_O_QR, 256) * sin
    k_ref[...] = (proj(_O_K, 256) * cos + proj(_O_KR, 256) * sin) * (RET_DK ** -0.5)
    v_ref[...] = proj(_O_V, 512)

    xe_ref[8:8 + tl, :] = proj(_O_X, LRU_WIDTH)
    cw = cw_ref[...]
    xc = cb_ref[...]
    for j in range(LRU_CONV):
        xc = xc + xe_ref[5 + j:5 + j + tl, :] * cw[j:j + 1, :]
    xe_ref[0:8, :] = xe_ref[tl:tl + 8, :]
    ri = jnp.dot(xc.astype(BF16), wri_ref[...], preferred_element_type=F32) + bri_ref[...]
    r = _sigmoid(ri[:, :LRU_WIDTH])
    i = _sigmoid(ri[:, LRU_WIDTH:])
    a = jnp.exp(-LRU_C * r * _softplus(-lam_ref[...]))
    a_ref[...] = a
    u_ref[...] = jnp.sqrt(1.0 - a * a) * (i * xc)

    lane = lax.broadcasted_iota(jnp.int32, (1, LANES), 1)
    crow = lax.broadcasted_iota(jnp.int32, (CHUNK, LRU_WIDTH), 0)

    def chunk_body(c, carry):
        rows = pl.ds(pl.multiple_of(c * CHUNK, CHUNK), CHUNK)
        for head in range(RET_HEADS):
            unit, half = head // 2, head % 2
            ul = slice(unit * LANES, (unit + 1) * LANES)
            vl = slice(head * LANES, (head + 1) * LANES)
            own = (lane < 64) if half == 0 else (lane >= 64)
            q = jnp.where(own, q_ref[rows, ul], 0.0)
            k = k_ref[rows, ul]
            v = v_ref[rows, vl]
            st = st_ref[head]
            o = _dot_nt(q, st) * qdec_ref[head]
            o = o + _dot(_dot_nt(q, k) * dmat_ref[head], v)
            st_ref[head] = st * sdec_ref[head] + _dot_tn(v, k * kdec_ref[head])
            o_ref[rows, vl] = o
        ca = a_ref[rows, :]
        ch = u_ref[rows, :]
        d = 1
        while d < CHUNK:
            keep = crow >= d
            sa = jnp.where(keep, pltpu.roll(ca, d, axis=0), 1.0)
            sh = jnp.where(keep, pltpu.roll(ch, d, axis=0), 0.0)
            ch = ca * sh + ch
            ca = ca * sa
            d *= 2
        ch = ch + ca * h_ref[...]
        h_ref[...] = ch[CHUNK - 1:CHUNK, :]
        o_ref[rows, 512:1024] = ch
        return carry

    lax.fori_loop(0, tl // CHUNK, chunk_body, 0, unroll=2)

    rw = rw_ref[...]
    ys = []
    for head in range(RET_HEADS):
        sl = slice(head * LANES, (head + 1) * LANES)
        oh = o_ref[:, sl]
        oh = oh - jnp.mean(oh, axis=-1, keepdims=True)
        ys.append(oh * lax.rsqrt(jnp.mean(oh * oh, axis=-1, keepdims=True) + EPS) * rw[:, sl])
    cg = proj(_O_G, 512)
    y_c = jnp.concatenate(ys, axis=1) * (cg * _sigmoid(cg))
    dg = proj(_O_GATE, LRU_WIDTH)
    gelu = 0.5 * dg * (1.0 + jnp.tanh(np.sqrt(2.0 / np.pi) * (dg + 0.044715 * (dg * dg * dg))))
    y_d = o_ref[:, 512:1024] * gelu
    y = jnp.concatenate([y_c, y_d], axis=1).astype(BF16)
    out_ref[0] = x + jnp.dot(y, wout_ref[...], preferred_element_type=F32)


def _rot_cols(w):
    d = w.shape[0]
    w4 = w.reshape(d, RET_HEADS, 2, RET_DK // 2)
    return jnp.stack([-w4[:, :, 1], w4[:, :, 0]], axis=2).reshape(d, RET_HEADS * RET_DK)


def _odd_mixer(x, norm_w, w_in, ret_norm_w, conv_w, conv_b, w_r, b_r, w_i, b_i, lam, w_out):
    B, L, D = x.shape
    tl = min(MIX_TILE, L)
    s = np.cumsum([0, 256, 256, 512, 512, 512, 512])
    wq, wk, wv, wg, wx, wgate = [w_in[:, s[i]:s[i + 1]] for i in range(6)]
    w_in_r = jnp.concatenate([wq, _rot_cols(wq), wk, _rot_cols(wk), wv, wg, wx, wgate],
                             axis=1).astype(BF16)
    half = RET_DK // 2
    inv = ROPE_BASE ** (-jnp.arange(half, dtype=F32) / half)
    ang = jnp.arange(L, dtype=F32)[:, None] * inv[None, :]
    cos = jnp.tile(jnp.cos(ang), (1, 2 * RET_HEADS))
    sin = jnp.tile(jnp.sin(ang), (1, 2 * RET_HEADS))
    lg = jnp.log1p(-jnp.exp2(-5.0 - jnp.arange(RET_HEADS, dtype=F32)))[:, None, None]
    t = jnp.arange(CHUNK, dtype=F32)
    rel = t[:, None] - t[None, :]
    dmat = jnp.where(rel >= 0, jnp.exp(lg * jnp.maximum(rel, 0.0)), 0.0)
    qdec = jnp.broadcast_to(jnp.exp(lg * (t[None, :, None] + 1.0)), (RET_HEADS, CHUNK, LANES))
    kdec = jnp.broadcast_to(jnp.exp(lg * (CHUNK - 1.0 - t[None, :, None])), (RET_HEADS, CHUNK, LANES))
    sdec = jnp.broadcast_to(jnp.exp(lg * float(CHUNK)), (RET_HEADS, 1, LANES))
    eye = jnp.eye(LRU_BLOCKS, dtype=F32)
    bd = LRU_WIDTH // LRU_BLOCKS

    def blockdiag(w):
        return (eye[:, None, :, None] * w[:, :, None, :]).reshape(LRU_WIDTH, LRU_WIDTH)

    wri = jnp.concatenate([blockdiag(w_r), blockdiag(w_i)], axis=1).astype(BF16)
    bri = jnp.concatenate([b_r, b_i])[None, :]
    return pl.pallas_call(
        _odd_kernel,
        grid=(B, L // tl),
        in_specs=[
            pl.BlockSpec((1, tl, D), lambda b, l: (b, l, 0)),
            _const_spec((1, D)),
            _const_spec((D, ODD_COLS)),
            pl.BlockSpec((tl, 256), lambda b, l: (l, 0)),
            pl.BlockSpec((tl, 256), lambda b, l: (l, 0)),
            _const_spec((RET_HEADS, CHUNK, CHUNK)),
            _const_spec((RET_HEADS, CHUNK, LANES)),
            _const_spec((RET_HEADS, CHUNK, LANES)),
            _const_spec((RET_HEADS, 1, LANES)),
            _const_spec((1, 512)),
            _const_spec((LRU_CONV, LRU_WIDTH)),
            _const_spec((1, LRU_WIDTH)),
            _const_spec((LRU_WIDTH, 2 * LRU_WIDTH)),
            _const_spec((1, 2 * LRU_WIDTH)),
            _const_spec((1, LRU_WIDTH)),
            _const_spec((MIX_WIDTH, D)),
        ],
        out_specs=pl.BlockSpec((1, tl, D), lambda b, l: (b, l, 0)),
        out_shape=jax.ShapeDtypeStruct((B, L, D), F32),
        scratch_shapes=[
            pltpu.VMEM((tl, 256), F32),
            pltpu.VMEM((tl, 256), F32),
            pltpu.VMEM((tl, 512), F32),
            pltpu.VMEM((tl, MIX_WIDTH), F32),
            pltpu.VMEM((tl + 8, LRU_WIDTH), F32),
            pltpu.VMEM((tl, LRU_WIDTH), F32),
            pltpu.VMEM((tl, LRU_WIDTH), F32),
            pltpu.VMEM((RET_HEADS, LANES, LANES), F32),
            pltpu.VMEM((1, LRU_WIDTH), F32),
        ],
        compiler_params=pltpu.CompilerParams(
            dimension_semantics=("arbitrary", "arbitrary"), vmem_limit_bytes=VMEM_LIMIT),
        name="odd_mixer",
    )(x, norm_w[None, :], w_in_r, cos, sin, dmat, qdec, kdec, sdec, ret_norm_w[None, :],
      conv_w, conv_b[None, :], wri, bri, lam[None, :], w_out.astype(BF16))


_I_E1, _I_E2, _I_G1, _I_G2, _I_R1, _I_R2 = 0, 1, 2, 3, 4, 5
_LOGIT_E0 = MOE_GROUPS


def _pack_bf16_pairs(lo, hi):
    lo_b = pltpu.bitcast(lo.astype(BF16).astype(F32), jnp.uint32)
    hi_b = pltpu.bitcast(hi.astype(BF16).astype(F32), jnp.uint32)
    return (lo_b >> 16) | (hi_b & jnp.uint32(0xFFFF0000))


def _unpack_pairs_f32(w):
    return pltpu.bitcast(w << 16, F32), pltpu.bitcast(w & jnp.uint32(0xFFFF0000), F32)


def _unpack_bf16_pairs(w):
    lo, hi = _unpack_pairs_f32(w)
    return lo.astype(BF16), hi.astype(BF16)


def _router_kernel(x_ref, nw_ref, wr_ref, br_ref, hnp_ref, info_ref, info_t_ref, cnt_ref):
    tm = x_ref.shape[0]

    @pl.when(pl.program_id(0) == 0)
    def _():
        cnt_ref[...] = jnp.zeros_like(cnt_ref)

    hn = _rms(x_ref[...], nw_ref[...])
    half = D_MODEL // 2
    hnp_ref[...] = _pack_bf16_pairs(hn[:, :half], hn[:, half:])

    logits = _dot_x3(hn, wr_ref[...]) + br_ref[...]
    lane = lax.broadcasted_iota(jnp.int32, (tm, LANES), 1).astype(F32)

    def first_max(vals):
        m = jnp.max(vals, axis=-1, keepdims=True)
        idx = jnp.min(jnp.where(vals == m, lane, float(LANES)), axis=-1, keepdims=True)
        return m, idx

    gl = jnp.where(lane < MOE_GROUPS, logits, NEG_BIG)
    gmax, gidx = first_max(gl)
    g_w = 1.0 / jnp.sum(jnp.exp(gl - gmax), axis=-1, keepdims=True)
    e0 = _LOGIT_E0 + MOE_PER_GROUP * gidx
    el = jnp.where((lane >= e0) & (lane < e0 + MOE_PER_GROUP), logits, NEG_BIG)
    m1, i1 = first_max(el)
    m2, i2 = first_max(jnp.where(lane == i1, NEG_BIG, el))
    p2 = jnp.exp(m2 - m1)
    w1 = 1.0 / (1.0 + p2)
    e1 = i1 - _LOGIT_E0
    e2 = i2 - _LOGIT_E0

    oh1 = jnp.where(lane == e1, 1.0, 0.0)
    oh2 = jnp.where(lane == e2, 1.0, 0.0)
    tri = _tri(tm)
    pre1 = jnp.dot(tri, oh1.astype(BF16), preferred_element_type=F32)
    pre2 = jnp.dot(tri, oh2.astype(BF16), preferred_element_type=F32)
    carry = cnt_ref[...]
    c1 = pre1[tm - 1:tm, :]
    r1 = jnp.sum(oh1 * (pre1 + carry), axis=-1, keepdims=True) - 1.0
    r2 = jnp.sum(oh2 * (pre2 + (carry + c1)), axis=-1, keepdims=True) - 1.0
    cnt_ref[...] = carry + c1 + pre2[tm - 1:tm, :]

    info = jnp.zeros((tm, LANES), F32)
    for ln, val in ((_I_E1, e1), (_I_E2, e2), (_I_G1, g_w * w1), (_I_G2, g_w * (w1 * p2)),
                    (_I_R1, r1), (_I_R2, r2)):
        info = jnp.where(lane == float(ln), val, info)
    info_ref[...] = info
    info_t_ref[...] = info.T[0:GROUP, :]


def _router(x2, norm_w, w_group, b_group, w_expert, b_expert):
    T, D = x2.shape
    tm = min(TOK_TILE, T)
    pad = LANES - MOE_GROUPS - MOE_EXPERTS
    wr = jnp.pad(jnp.concatenate([w_group, w_expert], axis=1), ((0, 0), (0, pad)))
    br = jnp.pad(jnp.concatenate([b_group, b_expert]), (0, pad))[None, :]
    return pl.pallas_call(
        _router_kernel,
        grid=(T // tm,),
        in_specs=[
            pl.BlockSpec((tm, D), lambda i: (i, 0)),
            _const_spec((1, D)),
            _const_spec((D, LANES)),
            _const_spec((1, LANES)),
        ],
        out_specs=[
            pl.BlockSpec((tm, D // 2), lambda i: (i, 0)),
            pl.BlockSpec((tm, LANES), lambda i: (i, 0)),
            pl.BlockSpec((GROUP, tm), lambda i: (0, i)),
            _const_spec((1, LANES)),
        ],
        out_shape=[
            jax.ShapeDtypeStruct((T, D // 2), jnp.uint32),
            jax.ShapeDtypeStruct((T, LANES), F32),
            jax.ShapeDtypeStruct((GROUP, T), F32),
            jax.ShapeDtypeStruct((1, LANES), F32),
        ],
        compiler_params=pltpu.CompilerParams(
            dimension_semantics=("arbitrary",), vmem_limit_bytes=VMEM_LIMIT),
        name="moe_router",
    )(x2, norm_w[None, :], wr, br)


def _plan_kernel(cnt_ref, info_ref, pos_ref, te_ref, nx_ref, nv_ref):
    nt = te_ref.shape[0]
    e1 = info_ref[_I_E1:_I_E1 + 1, :]
    e2 = info_ref[_I_E2:_I_E2 + 1, :]
    s1 = jnp.zeros_like(e1)
    s2 = jnp.zeros_like(e2)
    tiles = [lax.div(cnt_ref[e] + (ROW_TILE - 1), jnp.int32(ROW_TILE)) for e in range(MOE_EXPERTS)]
    following = [None] * MOE_EXPERTS
    nxt = jnp.int32(-1)
    for e in reversed(range(MOE_EXPERTS)):
        following[e] = nxt
        nxt = jnp.where(tiles[e] > 0, jnp.int32(e), nxt)
    start = jnp.int32(0)
    for e in range(MOE_EXPERTS):
        seg = (start * ROW_TILE).astype(F32)
        s1 = jnp.where(e1 == float(e), seg, s1)
        s2 = jnp.where(e2 == float(e), seg, s2)

        def fill(i, carry, e=e):
            te_ref[i] = jnp.int32(e)
            nx_ref[i] = following[e]
            return carry

        lax.fori_loop(start, start + tiles[e], fill, 0)
        start = start + tiles[e]
    nv_ref[0] = start

    def fill_tail(i, carry):
        te_ref[i] = jnp.int32(MOE_EXPERTS - 1)
        nx_ref[i] = jnp.int32(-1)
        return carry

    lax.fori_loop(start, nt, fill_tail, 0)
    pos_ref[0:1, :] = (s1 + info_ref[_I_R1:_I_R1 + 1, :]).astype(jnp.int32)
    pos_ref[1:2, :] = (s2 + info_ref[_I_R2:_I_R2 + 1, :]).astype(jnp.int32)


def _plan(cnt, info_t, nt):
    T = info_t.shape[1]
    smem = pl.BlockSpec(memory_space=pltpu.SMEM)
    return pl.pallas_call(
        _plan_kernel,
        in_specs=[smem, pl.BlockSpec(memory_space=pltpu.VMEM)],
        out_specs=[pl.BlockSpec(memory_space=pltpu.VMEM), smem, smem, smem],
        out_shape=[
            jax.ShapeDtypeStruct((2, T), jnp.int32),
            jax.ShapeDtypeStruct((nt,), jnp.int32),
            jax.ShapeDtypeStruct((nt,), jnp.int32),
            jax.ShapeDtypeStruct((1,), jnp.int32),
        ],
        name="moe_plan",
    )(cnt, info_t)


def _sc_worker():
    return lax.axis_index("s") * SC_CORES + lax.axis_index("c")


def _dispatch(hnp, pos1, pos2, n_rows):
    T, W = hnp.shape
    per_w = T // SC_WORKERS
    n_chunks = per_w // SC_ROWS
    mesh = plsc.VectorSubcoreMesh(core_axis_name="c", subcore_axis_name="s")

    @functools.partial(
        pl.kernel, mesh=mesh,
        out_type=jax.ShapeDtypeStruct((n_rows, W), jnp.uint32),
        scratch_types=[
            pltpu.VMEM((SC_ROWS,), jnp.int32),
            pltpu.VMEM((SC_ROWS,), jnp.int32),
            pltpu.VMEM((SC_ROWS, W), jnp.uint32),
        ],
        name="moe_dispatch_sc",
    )
    def scatter(hnp_hbm, p1_hbm, p2_hbm, xs_hbm, i1_v, i2_v, rows_v):
        base0 = _sc_worker() * per_w

        @pl.loop(0, n_chunks)
        def _(j):
            base = pl.multiple_of(base0 + j * SC_ROWS, SC_ROWS)
            pltpu.sync_copy(p1_hbm.at[pl.ds(base, SC_ROWS)], i1_v)
            pltpu.sync_copy(p2_hbm.at[pl.ds(base, SC_ROWS)], i2_v)
            pltpu.sync_copy(hnp_hbm.at[pl.ds(base, SC_ROWS)], rows_v)
            pltpu.sync_copy(rows_v, xs_hbm.at[i1_v])
            pltpu.sync_copy(rows_v, xs_hbm.at[i2_v])

    return scatter(hnp, pos1, pos2)


def _gather_rows(table, idx):
    T = idx.shape[0]
    W = table.shape[1]
    per_w = T // SC_WORKERS
    rows = SC_ROWS
    n_chunks = per_w // rows
    mesh = plsc.VectorSubcoreMesh(core_axis_name="c", subcore_axis_name="s")

    @functools.partial(
        pl.kernel, mesh=mesh,
        out_type=jax.ShapeDtypeStruct((T, W), table.dtype),
        scratch_types=[
            pltpu.VMEM((rows,), jnp.int32),
            pltpu.VMEM((rows, W), table.dtype),
        ],
        name="moe_gather_sc",
    )
    def gather(table_hbm, idx_hbm, out_hbm, i_v, rows_v):
        base0 = _sc_worker() * per_w

        @pl.loop(0, n_chunks)
        def _(j):
            base = pl.multiple_of(base0 + j * rows, rows)
            pltpu.sync_copy(idx_hbm.at[pl.ds(base, rows)], i_v)
            pltpu.sync_copy(table_hbm.at[i_v], rows_v)
            pltpu.sync_copy(rows_v, out_hbm.at[pl.ds(base, rows)])

    return gather(table, idx)


def _experts_kernel(te_ref, nx_ref, nv_ref, xs_ref, w1_hbm, w3_hbm, w2_hbm, ys_ref,
                    w1f, w3f, w2f, w1b, w3b, w2b, sem, *, layer):
    i = pl.program_id(0)

    def fetch(e):
        row = e + layer * MOE_EXPERTS
        return [pltpu.make_async_copy(w1_hbm.at[row], w1f, sem),
                pltpu.make_async_copy(w3_hbm.at[row], w3f, sem),
                pltpu.make_async_copy(w2_hbm.at[row], w2f, sem)]

    @pl.when(i < nv_ref[0])
    def _():
        @pl.when(i == 0)
        def _():
            for cp in fetch(te_ref[0]):
                cp.start()

        @pl.when((i == 0) | (te_ref[i] != te_ref[jnp.maximum(i - 1, 0)]))
        def _():
            for cp in fetch(te_ref[i]):
                cp.wait()
            w1b[...] = w1f[...].astype(BF16)
            w3b[...] = w3f[...].astype(BF16)
            w2b[...] = w2f[...].astype(BF16)

            @pl.when(nx_ref[i] >= 0)
            def _():
                for cp in fetch(nx_ref[i]):
                    cp.start()

        lo, hi = _unpack_bf16_pairs(xs_ref[...])
        half = D_MODEL // 2
        d = functools.partial(jnp.dot, preferred_element_type=F32)
        h1 = d(lo, w1b[0:half, :]) + d(hi, w1b[half:, :])
        h3 = d(lo, w3b[0:half, :]) + d(hi, w3b[half:, :])
        hid = (h1 * _sigmoid(h1)) * h3
        y = d(hid.astype(BF16), w2b[...])
        ys_ref[...] = _pack_bf16_pairs(y[:, :half], y[:, half:])

    @pl.when(i >= nv_ref[0])
    def _():
        ys_ref[...] = jnp.zeros_like(ys_ref)


def _experts(xs, tile_expert, next_expert, n_valid, w1, w3, w2, layer):
    n_rows, W = xs.shape
    nt = n_rows // ROW_TILE
    D, F = w1.shape[1], w1.shape[2]

    def row_map(i, te, nx, nv):
        return (i, 0)

    hbm = pl.BlockSpec(memory_space=pl.ANY)
    return pl.pallas_call(
        functools.partial(_experts_kernel, layer=layer),
        grid_spec=pltpu.PrefetchScalarGridSpec(
            num_scalar_prefetch=3,
            grid=(nt,),
            in_specs=[pl.BlockSpec((ROW_TILE, W), row_map), hbm, hbm, hbm],
            out_specs=pl.BlockSpec((ROW_TILE, D // 2), row_map),
            scratch_shapes=[
                pltpu.VMEM((D, F), F32),
                pltpu.VMEM((D, F), F32),
                pltpu.VMEM((F, D), F32),
                pltpu.VMEM((D, F), BF16),
                pltpu.VMEM((D, F), BF16),
                pltpu.VMEM((F, D), BF16),
                pltpu.SemaphoreType.DMA(()),
            ],
        ),
        out_shape=jax.ShapeDtypeStruct((n_rows, D // 2), jnp.uint32),
        compiler_params=pltpu.CompilerParams(
            dimension_semantics=("arbitrary",), vmem_limit_bytes=VMEM_LIMIT),
        name="moe_experts",
    )(tile_expert, next_expert, n_valid, xs, w1, w3, w2)


def _combine_kernel(x_ref, info_ref, fw_ref, y1_ref, y2_ref, out_ref, *, final):
    info = info_ref[...]
    g1 = info[:, _I_G1:_I_G1 + 1]
    g2 = info[:, _I_G2:_I_G2 + 1]
    y1 = jnp.concatenate(_unpack_pairs_f32(y1_ref[...]), axis=1)
    y2 = jnp.concatenate(_unpack_pairs_f32(y2_ref[...]), axis=1)
    y = x_ref[...] + (g1 * y1 + g2 * y2)
    if final:
        y = _rms(y, fw_ref[...])
    out_ref[...] = y


def _combine(x2, part, info, y1, y2, final_w, final):
    T, D = x2.shape
    tp = T // COMBINE_PARTS
    tm = min(TOK_TILE, tp)
    nb = tp // tm
    x_spec = pl.BlockSpec((tm, D), lambda i: (part * nb + i, 0))
    half_spec = pl.BlockSpec((tm, D // 2), lambda i: (i, 0))
    return pl.pallas_call(
        functools.partial(_combine_kernel, final=final),
        grid=(nb,),
        in_specs=[x_spec, pl.BlockSpec((tm, LANES), lambda i: (part * nb + i, 0)),
                  _const_spec((1, D)), half_spec, half_spec],
        out_specs=x_spec,
        out_shape=jax.ShapeDtypeStruct((T, D), F32),
        input_output_aliases={0: 0},
        compiler_params=pltpu.CompilerParams(
            dimension_semantics=("arbitrary",), vmem_limit_bytes=VMEM_LIMIT),
        name="moe_combine",
    )(x2, info, final_w[None, :], y1, y2)


def _hier_moe(x2, norm_w, w_group, b_group, w_expert, b_expert, w1, w3, w2, layer, final_w, final):
    T, D = x2.shape
    nt = (2 * T) // ROW_TILE + MOE_EXPERTS
    hnp, info, info_t, cnt = _router(x2, norm_w, w_group, b_group, w_expert, b_expert)
    pos, tile_expert, next_expert, n_valid = _plan(cnt[0].astype(jnp.int32), info_t, nt)
    xs = _dispatch(hnp, pos[0], pos[1], nt * ROW_TILE)
    ys = _experts(xs, tile_expert, next_expert, n_valid, w1, w3, w2, layer)
    tp = T // COMBINE_PARTS
    gathered = [(_gather_rows(ys, pos[0, p * tp:(p + 1) * tp]), _gather_rows(ys, pos[1, p * tp:(p + 1) * tp]))
                for p in range(COMBINE_PARTS)]
    out = x2
    for part, (y1, y2) in enumerate(gathered):
        out = _combine(out, part, info, y1, y2, final_w, final)
    return out


def kernel(x, norm_mix, norm_ffn, norm_final, even_w_in, hgrn_lb_logits, hgrn_norm_w, gla_w_up,
           gla_b_up, gla_norm_w, even_w_out, odd_w_in, ret_norm_w, conv_w, conv_b, rglru_w_r,
           rglru_b_r, rglru_w_i, rglru_b_i, rglru_lambda, odd_w_out, moe_w_group, moe_b_group,
           moe_w_expert, moe_b_expert, moe_w1, moe_w3, moe_w2):
    B, L, D = x.shape
    depth = norm_mix.shape[0]
    lb_table = jnp.cumsum(jax.nn.softmax(hgrn_lb_logits.astype(F32), axis=0), axis=0)
    w1 = moe_w1.reshape((-1,) + moe_w1.shape[2:])
    w3 = moe_w3.reshape((-1,) + moe_w3.shape[2:])
    w2 = moe_w2.reshape((-1,) + moe_w2.shape[2:])
    for l in range(depth):
        j = l // 2
        if l % 2 == 0:
            x = _even_mixer(x, norm_mix[l], even_w_in[j], lb_table[l], gla_w_up[j], gla_b_up[j],
                            hgrn_norm_w[j], gla_norm_w[j], even_w_out[j])
        else:
            x = _odd_mixer(x, norm_mix[l], odd_w_in[j], ret_norm_w[j], conv_w[j], conv_b[j],
                           rglru_w_r[j], rglru_b_r[j], rglru_w_i[j], rglru_b_i[j], rglru_lambda[j],
                           odd_w_out[j])
        x = _hier_moe(x.reshape(B * L, D), norm_ffn[l], moe_w_group[l], moe_b_group[l],
                      moe_w_expert[l], moe_b_expert[l], w1, w3, w2, l,
                      norm_final, l == depth - 1).reshape(B, L, D)
    return x
```

```python
import functools

import numpy as np
import jax
import jax.numpy as jnp
from jax import lax
from jax.experimental import pallas as pl
from jax.experimental.pallas import tpu as pltpu
from jax.experimental.pallas import tpu_sc as plsc

F32 = jnp.float32
BF16 = jnp.bfloat16

EPS = 1e-6
CHUNK = 64
GROUP = 8
LOG2E = 1.4426950408889634
LANES = 128
D_MODEL = 1024
MIX_WIDTH = 1024
N_HEADS = 8
GLA_RANK = 16
GLA_TAU = 16.0
RET_HEADS = 4
RET_DK = 64
ROPE_BASE = 10000.0
LRU_WIDTH = 512
LRU_BLOCKS = 8
LRU_CONV = 4
LRU_C = 8.0
MOE_GROUPS = 4
MOE_PER_GROUP = 8
MOE_EXPERTS = MOE_GROUPS * MOE_PER_GROUP
MOE_DFF = 512
NEG_BIG = -1e30

MIX_TILE = 512
EVEN_SPLIT = 2
ODD_SPLIT = 2
TOK_TILE = 512
ROW_TILE = 512
COMBINE_PARTS = 4
SC_CORES = 2
SC_WORKERS = 32
SC_ROWS = 128
VMEM_LIMIT = 56 * 1024 * 1024


def _dot(a, b):
    return jnp.dot(a.astype(BF16), b.astype(BF16), preferred_element_type=F32)


def _dot_nt(a, b):
    return lax.dot_general(a.astype(BF16), b.astype(BF16), (((1,), (1,)), ((), ())),
                           preferred_element_type=F32)


def _dot_tn(a, b):
    return lax.dot_general(a.astype(BF16), b.astype(BF16), (((0,), (0,)), ((), ())),
                           preferred_element_type=F32)


def _split2(x):
    hi = x.astype(BF16)
    lo = (x - hi.astype(F32)).astype(BF16)
    return hi, lo


def _split3(x):
    hi = x.astype(BF16)
    r = x - hi.astype(F32)
    mid = r.astype(BF16)
    lo = (r - mid.astype(F32)).astype(BF16)
    return hi, mid, lo


def _dot_x3(a, b):
    ah, al = _split2(a)
    bh, bl = _split2(b)
    d = functools.partial(jnp.dot, preferred_element_type=F32)
    return d(ah, bh) + (d(ah, bl) + d(al, bh))


def _sigmoid(x):
    return 1.0 / (1.0 + jnp.exp(-x))


def _softplus(x):
    return jnp.maximum(x, 0.0) + jnp.log(1.0 + jnp.exp(-jnp.abs(x)))


def _rms(x, w):
    return x * lax.rsqrt(jnp.mean(x * x, axis=-1, keepdims=True) + EPS) * w


def _tri(n):
    r = lax.broadcasted_iota(jnp.int32, (n, n), 0)
    c = lax.broadcasted_iota(jnp.int32, (n, n), 1)
    return jnp.where(r >= c, 1.0, 0.0).astype(BF16)


def _cumsum_rows(tri3, g):
    return jnp.dot(tri3, jnp.concatenate(_split3(g), axis=0), preferred_element_type=F32)


_E_AQ, _E_AF, _E_AI, _E_AG = 0, 512, 1024, 1536
_E_BQ, _E_BK, _E_BV, _E_BR, _E_LOW = 2048, 2304, 2560, 3072, 3584
EVEN_COLS = 3712
_QK_COLS = 768


_LEVEL_BLOCKS = (32, 16, 8)


def _level_tables():
    t = np.arange(CHUNK)
    out = []
    for blk in _LEVEL_BLOCKS:
        odd = (t // blk) % 2 == 1
        out.append((np.where(odd, (t // blk) * blk - 1, -1), np.where(~odd, (t // blk) * blk + blk - 1, -1)))
    return out


def _gla_chunk(rows, q_ref, k_ref, v_ref, b2, o_ref, st_ref, a_ref):
    n_units = _QK_COLS // LANES
    lane = lax.broadcasted_iota(jnp.int32, (1, LANES), 1)
    trow = lax.broadcasted_iota(jnp.int32, (GROUP, LANES), 0)
    unit_heads = [((u, None),) for u in range(4)] + [((4, 0), (5, 64)), ((6, 0), (7, 64))]
    qs = [q_ref[rows, u * LANES:(u + 1) * LANES] for u in range(n_units)]
    ks = [k_ref[rows, u * LANES:(u + 1) * LANES] for u in range(n_units)]
    bs = [b2[:, u * LANES:(u + 1) * LANES] for u in range(n_units)]
    vs = [v_ref[rows, h * LANES:(h + 1) * LANES] for h in range(N_HEADS)]

    for u in range(n_units):
        for j in range(CHUNK // GROUP):
            qj = qs[u][j * GROUP:(j + 1) * GROUP]
            bj = bs[u][j * GROUP:(j + 1) * GROUP]
            for s in range(0, GROUP, 2):
                pair = [qj * jnp.exp2(jnp.where(trow >= s + i, bj - bj[s + i:s + i + 1, :], NEG_BIG))
                        for i in (0, 1)]
                r0 = (j * GROUP + s) * GROUP
                a_ref[r0:r0 + 2 * GROUP, u * LANES:(u + 1) * LANES] = (
                    jnp.concatenate(pair, axis=0).astype(BF16))

    zk = jnp.zeros((CHUNK, LANES), F32)
    scs = []
    for pr in range(n_units // 2):
        u0, u1 = 2 * pr, 2 * pr + 1
        if u0 < 4:
            keys = jnp.concatenate([jnp.concatenate([ks[u0], zk], axis=1),
                                    jnp.concatenate([zk, ks[u1]], axis=1)], axis=0)
        else:
            lo0, hi0 = jnp.where(lane < 64, ks[u0], 0.0), jnp.where(lane >= 64, ks[u0], 0.0)
            lo1, hi1 = jnp.where(lane < 64, ks[u1], 0.0), jnp.where(lane >= 64, ks[u1], 0.0)
            keys = jnp.concatenate([jnp.concatenate([lo0, zk], axis=1), jnp.concatenate([hi0, zk], axis=1),
                                    jnp.concatenate([zk, lo1], axis=1), jnp.concatenate([zk, hi1], axis=1)],
                                   axis=0)
        scs.append(_dot_nt(a_ref[:, u0 * LANES:(u1 + 1) * LANES], keys))

    tables = _level_tables()
    tq = lax.broadcasted_iota(jnp.int32, (CHUNK, CHUNK), 0)
    ts = lax.broadcasted_iota(jnp.int32, (CHUNK, CHUNK), 1)
    covers = [((tq // blk) % 2 == 1) & ((tq // blk) == (ts // blk) + 1) & ((ts // blk) % 2 == 0)
              for blk in _LEVEL_BLOCKS]
    s_offs, o_inter = {}, {}
    for u in range(n_units):
        q, k, b = qs[u], ks[u], bs[u]
        heads = unit_heads[u]
        if len(heads) == 2:
            qm = jnp.concatenate([jnp.where((lane & 64) == lo, q, 0.0) for _, lo in heads], axis=0)
            bm = jnp.concatenate([b, b], axis=0)
        else:
            qm, bm = q, b
        s_off = None
        for (qref, kref), cover in zip(tables, covers):
            qrows, krows = [], []
            for g in range(CHUNK // GROUP):
                sl = slice(g * GROUP, (g + 1) * GROUP)
                rq, rk = int(qref[g * GROUP]), int(kref[g * GROUP])
                qrows.append(jnp.exp2(b[sl] - b[rq:rq + 1, :]) if rq >= 0 else jnp.zeros((GROUP, LANES), F32))
                krows.append(k[sl] * jnp.exp2(b[rk:rk + 1, :] - b[sl]) if rk >= 0
                             else jnp.zeros((GROUP, LANES), F32))
            qfac = jnp.concatenate(qrows, axis=0)
            if len(heads) == 2:
                qfac = jnp.concatenate([qfac, qfac], axis=0)
                cov = jnp.concatenate([cover, cover], axis=0)
            else:
                cov = cover
            part = _dot_nt(qm * qfac, jnp.concatenate(krows, axis=0))
            s_off = jnp.where(cov, part, 0.0 if s_off is None else s_off)
        eb = jnp.exp2(bm)
        b_last = b[CHUNK - 1:CHUNK, :]
        kd = (k * jnp.exp2(b_last - b)).astype(BF16)
        dec = jnp.exp2(b_last)
        qe = qm * eb
        for i, (head, lo) in enumerate(heads):
            sl = slice(i * CHUNK, (i + 1) * CHUNK)
            st = st_ref[head]
            o_inter[head] = _dot_nt(qe[sl], st)
            st_ref[head] = st * dec + _dot_tn(vs[head], kd)
            s_offs[head] = s_off[sl]

    for u in range(n_units):
        sc = scs[u // 2]
        paired = len(unit_heads[u]) == 2
        width = LANES if paired else CHUNK
        c0 = (u % 2) * width
        scol = (lane & 63) if paired else lane[:, :CHUNK]
        groups = []
        for j in range(CHUNK // GROUP):
            acc = jnp.zeros((GROUP, width), F32)
            for s in range(GROUP):
                r0 = (j * GROUP + s) * GROUP
                acc = jnp.where(scol == j * GROUP + s, sc[r0:r0 + GROUP, c0:c0 + width], acc)
            groups.append(acc)
        p = jnp.concatenate(groups, axis=0)
        if paired:
            vstack = jnp.concatenate([vs[h] for h, _ in unit_heads[u]], axis=0)
        for head, lo in unit_heads[u]:
            if lo is None:
                o = o_inter[head] + _dot(s_offs[head] + p, vs[head])
            else:
                o = (o_inter[head] + _dot(s_offs[head], vs[head])
                     + _dot(jnp.where((lane & 64) == lo, p, 0.0), vstack))
            o_ref[rows, head * LANES:(head + 1) * LANES] = o


def _even_kernel(x_ref, nw_ref, win_ref, lb_ref, wup_ref, bup_ref, hw_ref, wout_ref, out_ref,
                 q_ref, k_ref, g_ref, v_ref, gate_ref, o_ref, st_ref, a_ref):
    tl = x_ref.shape[1]
    parts = [(r0, tl // EVEN_SPLIT) for r0 in range(0, tl, tl // EVEN_SPLIT)]

    @pl.when(pl.program_id(1) == 0)
    def _():
        st_ref[...] = jnp.zeros_like(st_ref)

    def project(r0, n):
        rs = slice(r0, r0 + n)
        hn = _rms(x_ref[0, rs, :], nw_ref[...]).astype(BF16)

        def proj(c0, w):
            return jnp.dot(hn, win_ref[:, c0:c0 + w], preferred_element_type=F32)

        q_ref[rs, 0:512] = proj(_E_AQ, 512)
        lb = lb_ref[...]
        f = lb + (1.0 - lb) * _sigmoid(proj(_E_AF, 512))
        k_ref[rs, 0:512] = 1.0 - f
        g_ref[rs, 0:512] = jnp.log(f)
        v_ref[rs, 0:512] = proj(_E_AI, 512)
        ag = proj(_E_AG, 512)
        gate_ref[rs, 0:512] = ag * _sigmoid(ag)
        q_ref[rs, 512:768] = proj(_E_BQ, 256) * (64.0 ** -0.5)
        k_ref[rs, 512:768] = proj(_E_BK, 256)
        v_ref[rs, 512:1024] = proj(_E_BV, 512)
        br = proj(_E_BR, 512)
        gate_ref[rs, 512:1024] = br * _sigmoid(br)
        z = _dot_x3(proj(_E_LOW, LANES), wup_ref[...]) + bup_ref[...]
        g_ref[rs, 512:768] = -_softplus(-z) * (1.0 / GLA_TAU)

    tri = _tri(CHUNK)
    tri3 = jnp.concatenate([tri, tri, tri], axis=1)

    def finish(r0, n):
        rs = slice(r0, r0 + n)
        hw = hw_ref[...]
        ys = []
        for head in range(N_HEADS):
            sl = slice(head * LANES, (head + 1) * LANES)
            oh = o_ref[rs, sl]
            ys.append(oh * lax.rsqrt(jnp.mean(oh * oh, axis=-1, keepdims=True) + EPS)
                      * hw[:, sl] * gate_ref[rs, sl])
        y = jnp.concatenate(ys, axis=1).astype(BF16)
        out_ref[0, rs, :] = x_ref[0, rs, :] + jnp.dot(y, wout_ref[...], preferred_element_type=F32)

    for r0, n in parts:
        project(r0, n)
    for r0, n in parts:
        for c in range(r0 // CHUNK, (r0 + n) // CHUNK):
            rows = slice(c * CHUNK, (c + 1) * CHUNK)
            b2 = _cumsum_rows(tri3, g_ref[rows, :]) * LOG2E
            _gla_chunk(rows, q_ref, k_ref, v_ref, b2, o_ref, st_ref, a_ref.at[c % 2])
    for r0, n in parts:
        finish(r0, n)


def _const_spec(shape):
    nd = len(shape)
    return pl.BlockSpec(shape, lambda *_: (0,) * nd)


def _even_mixer(x, norm_w, w_in, lb, w_up, b_up, hgrn_norm_w, gla_norm_w, w_out):
    B, L, D = x.shape
    tl = min(MIX_TILE, L)
    s = np.cumsum([0, 512, 512, 512, 512, 256, 256, 512, GLA_RANK, 512])
    cols = [w_in[:, s[i]:s[i + 1]] for i in range(9)]
    low = jnp.pad(cols[7], ((0, 0), (0, LANES - GLA_RANK)))
    w_in_r = jnp.concatenate(cols[:7] + [cols[8], low], axis=1).astype(BF16)
    w_up_p = jnp.pad(w_up, ((0, LANES - GLA_RANK), (0, 0)))
    hw = jnp.concatenate([hgrn_norm_w, gla_norm_w])[None, :]
    return pl.pallas_call(
        _even_kernel,
        grid=(B, L // tl),
        in_specs=[
            pl.BlockSpec((1, tl, D), lambda b, l: (b, l, 0)),
            _const_spec((1, D)),
            _const_spec((D, EVEN_COLS)),
            _const_spec((1, 512)),
            _const_spec((LANES, 256)),
            _const_spec((1, 256)),
            _const_spec((1, MIX_WIDTH)),
            _const_spec((MIX_WIDTH, D)),
        ],
        out_specs=pl.BlockSpec((1, tl, D), lambda b, l: (b, l, 0)),
        out_shape=jax.ShapeDtypeStruct((B, L, D), F32),
        scratch_shapes=[
            pltpu.VMEM((tl, _QK_COLS), F32),
            pltpu.VMEM((tl, _QK_COLS), F32),
            pltpu.VMEM((tl, _QK_COLS), F32),
            pltpu.VMEM((tl, MIX_WIDTH), F32),
            pltpu.VMEM((tl, MIX_WIDTH), F32),
            pltpu.VMEM((tl, MIX_WIDTH), F32),
            pltpu.VMEM((N_HEADS, LANES, LANES), F32),
            pltpu.VMEM((2, CHUNK * GROUP, _QK_COLS), BF16),
        ],
        compiler_params=pltpu.CompilerParams(
            dimension_semantics=("arbitrary", "arbitrary"), vmem_limit_bytes=VMEM_LIMIT),
        name="even_mixer",
    )(x, norm_w[None, :], w_in_r, lb[None, :], w_up_p, b_up[None, :], hw, w_out.astype(BF16))


_O_Q, _O_QR, _O_K, _O_KR, _O_V, _O_G, _O_X, _O_GATE = 0, 256, 512, 768, 1024, 1536, 2048, 2560
ODD_COLS = 3072


def _odd_kernel(x_ref, nw_ref, win_ref, cos_ref, sin_ref, dmat_ref, qdec_ref, kdec_ref, sdec_ref,
                rw_ref, cw_ref, cb_ref, wri_ref, bri_ref, lam_ref, wout_ref, out_ref,
                q_ref, k_ref, v_ref, gate_ref, o_ref, xe_ref, a_ref, u_ref, st_ref, h_ref):
    tl = x_ref.shape[1]
    parts = [(r0, tl // ODD_SPLIT) for r0 in range(0, tl, tl // ODD_SPLIT)]

    @pl.when(pl.program_id(1) == 0)
    def _():
        st_ref[...] = jnp.zeros_like(st_ref)
        h_ref[...] = jnp.zeros_like(h_ref)
        xe_ref[0:8, :] = jnp.zeros((8, LRU_WIDTH), F32)

    def project(r0, n):
        rs = slice(r0, r0 + n)
        hn = _rms(x_ref[0, rs, :], nw_ref[...]).astype(BF16)

        def proj(c0, w):
            return jnp.dot(hn, win_ref[:, c0:c0 + w], preferred_element_type=F32)

        cos = cos_ref[rs, :]
        sin = sin_ref[rs, :]
        q_ref[rs, :] = proj(_O_Q, 256) * cos + proj(_O_QR, 256) * sin
        k_ref[rs, :] = (proj(_O_K, 256) * cos + proj(_O_KR, 256) * sin) * (RET_DK ** -0.5)
        v_ref[rs, :] = proj(_O_V, 512)
        cg = proj(_O_G, 512)
        gate_ref[rs, 0:512] = cg * _sigmoid(cg)
        dg = proj(_O_GATE, LRU_WIDTH)
        gate_ref[rs, 512:1024] = 0.5 * dg * (
            1.0 + jnp.tanh(np.sqrt(2.0 / np.pi) * (dg + 0.044715 * (dg * dg * dg))))

        xe_ref[8 + r0:8 + r0 + n, :] = proj(_O_X, LRU_WIDTH)
        cw = cw_ref[...]
        xc = cb_ref[...]
        for j in range(LRU_CONV):
            xc = xc + xe_ref[5 + j + r0:5 + j + r0 + n, :] * cw[j:j + 1, :]
        ri = jnp.dot(xc.astype(BF16), wri_ref[...], preferred_element_type=F32) + bri_ref[...]
        r = _sigmoid(ri[:, :LRU_WIDTH])
        i = _sigmoid(ri[:, LRU_WIDTH:])
        a = jnp.exp(-LRU_C * r * _softplus(-lam_ref[...]))
        a_ref[rs, :] = a
        u_ref[rs, :] = jnp.sqrt(1.0 - a * a) * (i * xc)

    lane = lax.broadcasted_iota(jnp.int32, (1, LANES), 1)
    crow = lax.broadcasted_iota(jnp.int32, (CHUNK, LRU_WIDTH), 0)

    def recur(r0, n):
        rs = slice(r0, r0 + n)
        for head in range(RET_HEADS):
            unit, half = head // 2, head % 2
            ul = slice(unit * LANES, (unit + 1) * LANES)
            vl = slice(head * LANES, (head + 1) * LANES)
            own = (lane < 64) if half == 0 else (lane >= 64)
            q = jnp.where(own, q_ref[rs, ul], 0.0)
            k = k_ref[rs, ul]
            v = v_ref[rs, vl]
            st = st_ref[head]
            o = _dot_nt(q, st) * qdec_ref[head]
            o = o + _dot(_dot_nt(q, k) * dmat_ref[head], v)
            st_ref[head] = st * sdec_ref[head] + _dot_tn(v, k * kdec_ref[head])
            o_ref[rs, vl] = o
        for c in range(r0 // CHUNK, (r0 + n) // CHUNK):
            rows = slice(c * CHUNK, (c + 1) * CHUNK)
            ca = a_ref[rows, :]
            ch = u_ref[rows, :]
            d = 1
            while d < CHUNK:
                keep = crow >= d
                sa = jnp.where(keep, pltpu.roll(ca, d, axis=0), 1.0)
                sh = jnp.where(keep, pltpu.roll(ch, d, axis=0), 0.0)
                ch = ca * sh + ch
                ca = ca * sa
                d *= 2
            ch = ch + ca * h_ref[...]
            h_ref[...] = ch[CHUNK - 1:CHUNK, :]
            o_ref[rows, 512:1024] = ch

    def finish(r0, n):
        rs = slice(r0, r0 + n)
        rw = rw_ref[...]
        ys = []
        for head in range(RET_HEADS):
            sl = slice(head * LANES, (head + 1) * LANES)
            oh = o_ref[rs, sl]
            oh = oh - jnp.mean(oh, axis=-1, keepdims=True)
            ys.append(oh * lax.rsqrt(jnp.mean(oh * oh, axis=-1, keepdims=True) + EPS) * rw[:, sl])
        ys.append(o_ref[rs, 512:1024])
        y = (jnp.concatenate(ys, axis=1) * gate_ref[rs, :]).astype(BF16)
        out_ref[0, rs, :] = x_ref[0, rs, :] + jnp.dot(y, wout_ref[...], preferred_element_type=F32)

    for r0, n in parts:
        project(r0, n)
    xe_ref[0:8, :] = xe_ref[tl:tl + 8, :]
    for r0, n in parts:
        recur(r0, n)
    for r0, n in parts:
        finish(r0, n)


def _rot_cols(w):
    d = w.shape[0]
    w4 = w.reshape(d, RET_HEADS, 2, RET_DK // 2)
    return jnp.stack([-w4[:, :, 1], w4[:, :, 0]], axis=2).reshape(d, RET_HEADS * RET_DK)


def _odd_mixer(x, norm_w, w_in, ret_norm_w, conv_w, conv_b, w_r, b_r, w_i, b_i, lam, w_out):
    B, L, D = x.shape
    tl = min(MIX_TILE, L)
    s = np.cumsum([0, 256, 256, 512, 512, 512, 512])
    wq, wk, wv, wg, wx, wgate = [w_in[:, s[i]:s[i + 1]] for i in range(6)]
    w_in_r = jnp.concatenate([wq, _rot_cols(wq), wk, _rot_cols(wk), wv, wg, wx, wgate],
                             axis=1).astype(BF16)
    half = RET_DK // 2
    inv = ROPE_BASE ** (-jnp.arange(half, dtype=F32) / half)
    ang = jnp.arange(L, dtype=F32)[:, None] * inv[None, :]
    cos = jnp.tile(jnp.cos(ang), (1, 2 * RET_HEADS))
    sin = jnp.tile(jnp.sin(ang), (1, 2 * RET_HEADS))
    lg = jnp.log1p(-jnp.exp2(-5.0 - jnp.arange(RET_HEADS, dtype=F32)))[:, None, None]
    rc = tl // ODD_SPLIT
    t = jnp.arange(rc, dtype=F32)
    rel = t[:, None] - t[None, :]
    dmat = jnp.where(rel >= 0, jnp.exp(lg * jnp.maximum(rel, 0.0)), 0.0)
    qdec = jnp.broadcast_to(jnp.exp(lg * (t[None, :, None] + 1.0)), (RET_HEADS, rc, LANES))
    kdec = jnp.broadcast_to(jnp.exp(lg * (rc - 1.0 - t[None, :, None])), (RET_HEADS, rc, LANES))
    sdec = jnp.broadcast_to(jnp.exp(lg * float(rc)), (RET_HEADS, 1, LANES))
    eye = jnp.eye(LRU_BLOCKS, dtype=F32)
    bd = LRU_WIDTH // LRU_BLOCKS

    def blockdiag(w):
        return (eye[:, None, :, None] * w[:, :, None, :]).reshape(LRU_WIDTH, LRU_WIDTH)

    wri = jnp.concatenate([blockdiag(w_r), blockdiag(w_i)], axis=1).astype(BF16)
    bri = jnp.concatenate([b_r, b_i])[None, :]
    return pl.pallas_call(
        _odd_kernel,
        grid=(B, L // tl),
        in_specs=[
            pl.BlockSpec((1, tl, D), lambda b, l: (b, l, 0)),
            _const_spec((1, D)),
            _const_spec((D, ODD_COLS)),
            pl.BlockSpec((tl, 256), lambda b, l: (l, 0)),
            pl.BlockSpec((tl, 256), lambda b, l: (l, 0)),
            _const_spec((RET_HEADS, rc, rc)),
            _const_spec((RET_HEADS, rc, LANES)),
            _const_spec((RET_HEADS, rc, LANES)),
            _const_spec((RET_HEADS, 1, LANES)),
            _const_spec((1, 512)),
            _const_spec((LRU_CONV, LRU_WIDTH)),
            _const_spec((1, LRU_WIDTH)),
            _const_spec((LRU_WIDTH, 2 * LRU_WIDTH)),
            _const_spec((1, 2 * LRU_WIDTH)),
            _const_spec((1, LRU_WIDTH)),
            _const_spec((MIX_WIDTH, D)),
        ],
        out_specs=pl.BlockSpec((1, tl, D), lambda b, l: (b, l, 0)),
        out_shape=jax.ShapeDtypeStruct((B, L, D), F32),
        scratch_shapes=[
            pltpu.VMEM((tl, 256), F32),
            pltpu.VMEM((tl, 256), F32),
            pltpu.VMEM((tl, 512), F32),
            pltpu.VMEM((tl, MIX_WIDTH), F32),
            pltpu.VMEM((tl, MIX_WIDTH), F32),
            pltpu.VMEM((tl + 8, LRU_WIDTH), F32),
            pltpu.VMEM((tl, LRU_WIDTH), F32),
            pltpu.VMEM((tl, LRU_WIDTH), F32),
            pltpu.VMEM((RET_HEADS, LANES, LANES), F32),
            pltpu.VMEM((1, LRU_WIDTH), F32),
        ],
        compiler_params=pltpu.CompilerParams(
            dimension_semantics=("arbitrary", "arbitrary"), vmem_limit_bytes=VMEM_LIMIT),
        name="odd_mixer",
    )(x, norm_w[None, :], w_in_r, cos, sin, dmat, qdec, kdec, sdec, ret_norm_w[None, :],
      conv_w, conv_b[None, :], wri, bri, lam[None, :], w_out.astype(BF16))


_I_E1, _I_E2, _I_G1, _I_G2, _I_R1, _I_R2 = 0, 1, 2, 3, 4, 5
_LOGIT_E0 = MOE_GROUPS


def _pack_bf16_pairs(lo, hi):
    lo_b = pltpu.bitcast(lo.astype(BF16).astype(F32), jnp.uint32)
    hi_b = pltpu.bitcast(hi.astype(BF16).astype(F32), jnp.uint32)
    return (lo_b >> 16) | (hi_b & jnp.uint32(0xFFFF0000))


def _unpack_pairs_f32(w):
    return pltpu.bitcast(w << 16, F32), pltpu.bitcast(w & jnp.uint32(0xFFFF0000), F32)


def _unpack_bf16_pairs(w):
    lo, hi = _unpack_pairs_f32(w)
    return lo.astype(BF16), hi.astype(BF16)


def _router_kernel(x_ref, nw_ref, wr_ref, br_ref, hnp_ref, info_ref, info_t_ref, cnt_ref):
    tm = x_ref.shape[0]

    @pl.when(pl.program_id(0) == 0)
    def _():
        cnt_ref[...] = jnp.zeros_like(cnt_ref)

    hn = _rms(x_ref[...], nw_ref[...])
    half = D_MODEL // 2
    hnp_ref[...] = _pack_bf16_pairs(hn[:, :half], hn[:, half:])

    logits = _dot_x3(hn, wr_ref[...]) + br_ref[...]
    lane = lax.broadcasted_iota(jnp.int32, (tm, LANES), 1).astype(F32)

    def first_max(vals):
        m = jnp.max(vals, axis=-1, keepdims=True)
        idx = jnp.min(jnp.where(vals == m, lane, float(LANES)), axis=-1, keepdims=True)
        return m, idx

    gl = jnp.where(lane < MOE_GROUPS, logits, NEG_BIG)
    gmax, gidx = first_max(gl)
    g_w = 1.0 / jnp.sum(jnp.exp(gl - gmax), axis=-1, keepdims=True)
    e0 = _LOGIT_E0 + MOE_PER_GROUP * gidx
    el = jnp.where((lane >= e0) & (lane < e0 + MOE_PER_GROUP), logits, NEG_BIG)
    m1, i1 = first_max(el)
    m2, i2 = first_max(jnp.where(lane == i1, NEG_BIG, el))
    p2 = jnp.exp(m2 - m1)
    w1 = 1.0 / (1.0 + p2)
    e1 = i1 - _LOGIT_E0
    e2 = i2 - _LOGIT_E0

    oh1 = jnp.where(lane == e1, 1.0, 0.0)
    oh2 = jnp.where(lane == e2, 1.0, 0.0)
    tri = _tri(tm)
    pre1 = jnp.dot(tri, oh1.astype(BF16), preferred_element_type=F32)
    pre2 = jnp.dot(tri, oh2.astype(BF16), preferred_element_type=F32)
    carry = cnt_ref[...]
    c1 = pre1[tm - 1:tm, :]
    r1 = jnp.sum(oh1 * (pre1 + carry), axis=-1, keepdims=True) - 1.0
    r2 = jnp.sum(oh2 * (pre2 + (carry + c1)), axis=-1, keepdims=True) - 1.0
    cnt_ref[...] = carry + c1 + pre2[tm - 1:tm, :]

    info = jnp.zeros((tm, LANES), F32)
    for ln, val in ((_I_E1, e1), (_I_E2, e2), (_I_G1, g_w * w1), (_I_G2, g_w * (w1 * p2)),
                    (_I_R1, r1), (_I_R2, r2)):
        info = jnp.where(lane == float(ln), val, info)
    info_ref[...] = info
    info_t_ref[...] = info.T[0:GROUP, :]


def _router(x2, norm_w, w_group, b_group, w_expert, b_expert):
    T, D = x2.shape
    tm = min(TOK_TILE, T)
    pad = LANES - MOE_GROUPS - MOE_EXPERTS
    wr = jnp.pad(jnp.concatenate([w_group, w_expert], axis=1), ((0, 0), (0, pad)))
    br = jnp.pad(jnp.concatenate([b_group, b_expert]), (0, pad))[None, :]
    return pl.pallas_call(
        _router_kernel,
        grid=(T // tm,),
        in_specs=[
            pl.BlockSpec((tm, D), lambda i: (i, 0)),
            _const_spec((1, D)),
            _const_spec((D, LANES)),
            _const_spec((1, LANES)),
        ],
        out_specs=[
            pl.BlockSpec((tm, D // 2), lambda i: (i, 0)),
            pl.BlockSpec((tm, LANES), lambda i: (i, 0)),
            pl.BlockSpec((GROUP, tm), lambda i: (0, i)),
            _const_spec((1, LANES)),
        ],
        out_shape=[
            jax.ShapeDtypeStruct((T, D // 2), jnp.uint32),
            jax.ShapeDtypeStruct((T, LANES), F32),
            jax.ShapeDtypeStruct((GROUP, T), F32),
            jax.ShapeDtypeStruct((1, LANES), F32),
        ],
        compiler_params=pltpu.CompilerParams(
            dimension_semantics=("arbitrary",), vmem_limit_bytes=VMEM_LIMIT),
        name="moe_router",
    )(x2, norm_w[None, :], wr, br)


def _plan_kernel(cnt_ref, info_ref, pos_ref, te_ref, nx_ref, nv_ref):
    nt = te_ref.shape[0]
    e1 = info_ref[_I_E1:_I_E1 + 1, :]
    e2 = info_ref[_I_E2:_I_E2 + 1, :]
    s1 = jnp.zeros_like(e1)
    s2 = jnp.zeros_like(e2)
    tiles = [lax.div(cnt_ref[e] + (ROW_TILE - 1), jnp.int32(ROW_TILE)) for e in range(MOE_EXPERTS)]
    following = [None] * MOE_EXPERTS
    nxt = jnp.int32(-1)
    for e in reversed(range(MOE_EXPERTS)):
        following[e] = nxt
        nxt = jnp.where(tiles[e] > 0, jnp.int32(e), nxt)
    start = jnp.int32(0)
    for e in range(MOE_EXPERTS):
        seg = (start * ROW_TILE).astype(F32)
        s1 = jnp.where(e1 == float(e), seg, s1)
        s2 = jnp.where(e2 == float(e), seg, s2)

        def fill(i, carry, e=e):
            te_ref[i] = jnp.int32(e)
            nx_ref[i] = following[e]
            return carry

        lax.fori_loop(start, start + tiles[e], fill, 0)
        start = start + tiles[e]
    nv_ref[0] = start

    def fill_tail(i, carry):
        te_ref[i] = jnp.int32(MOE_EXPERTS - 1)
        nx_ref[i] = jnp.int32(-1)
        return carry

    lax.fori_loop(start, nt, fill_tail, 0)
    pos_ref[0:1, :] = (s1 + info_ref[_I_R1:_I_R1 + 1, :]).astype(jnp.int32)
    pos_ref[1:2, :] = (s2 + info_ref[_I_R2:_I_R2 + 1, :]).astype(jnp.int32)


def _plan(cnt, info_t, nt):
    T = info_t.shape[1]
    smem = pl.BlockSpec(memory_space=pltpu.SMEM)
    return pl.pallas_call(
        _plan_kernel,
        in_specs=[smem, pl.BlockSpec(memory_space=pltpu.VMEM)],
        out_specs=[pl.BlockSpec(memory_space=pltpu.VMEM), smem, smem, smem],
        out_shape=[
            jax.ShapeDtypeStruct((2, T), jnp.int32),
            jax.ShapeDtypeStruct((nt,), jnp.int32),
            jax.ShapeDtypeStruct((nt,), jnp.int32),
            jax.ShapeDtypeStruct((1,), jnp.int32),
        ],
        name="moe_plan",
    )(cnt, info_t)


def _sc_worker():
    return lax.axis_index("s") * SC_CORES + lax.axis_index("c")


def _dispatch(hnp, pos1, pos2, n_rows):
    T, W = hnp.shape
    per_w = T // SC_WORKERS
    n_chunks = per_w // SC_ROWS
    mesh = plsc.VectorSubcoreMesh(core_axis_name="c", subcore_axis_name="s")

    @functools.partial(
        pl.kernel, mesh=mesh,
        out_type=jax.ShapeDtypeStruct((n_rows, W), jnp.uint32),
        scratch_types=[
            pltpu.VMEM((SC_ROWS,), jnp.int32),
            pltpu.VMEM((SC_ROWS,), jnp.int32),
            pltpu.VMEM((SC_ROWS, W), jnp.uint32),
        ],
        name="moe_dispatch_sc",
    )
    def scatter(hnp_hbm, p1_hbm, p2_hbm, xs_hbm, i1_v, i2_v, rows_v):
        base0 = _sc_worker() * per_w

        @pl.loop(0, n_chunks)
        def _(j):
            base = pl.multiple_of(base0 + j * SC_ROWS, SC_ROWS)
            pltpu.sync_copy(p1_hbm.at[pl.ds(base, SC_ROWS)], i1_v)
            pltpu.sync_copy(p2_hbm.at[pl.ds(base, SC_ROWS)], i2_v)
            pltpu.sync_copy(hnp_hbm.at[pl.ds(base, SC_ROWS)], rows_v)
            pltpu.sync_copy(rows_v, xs_hbm.at[i1_v])
            pltpu.sync_copy(rows_v, xs_hbm.at[i2_v])

    return scatter(hnp, pos1, pos2)


def _gather_rows(table, idx):
    T = idx.shape[0]
    W = table.shape[1]
    per_w = T // SC_WORKERS
    rows = SC_ROWS
    n_chunks = per_w // rows
    mesh = plsc.VectorSubcoreMesh(core_axis_name="c", subcore_axis_name="s")

    @functools.partial(
        pl.kernel, mesh=mesh,
        out_type=jax.ShapeDtypeStruct((T, W), table.dtype),
        scratch_types=[
            pltpu.VMEM((rows,), jnp.int32),
            pltpu.VMEM((rows, W), table.dtype),
        ],
        name="moe_gather_sc",
    )
    def gather(table_hbm, idx_hbm, out_hbm, i_v, rows_v):
        base0 = _sc_worker() * per_w

        @pl.loop(0, n_chunks)
        def _(j):
            base = pl.multiple_of(base0 + j * rows, rows)
            pltpu.sync_copy(idx_hbm.at[pl.ds(base, rows)], i_v)
            pltpu.sync_copy(table_hbm.at[i_v], rows_v)
            pltpu.sync_copy(rows_v, out_hbm.at[pl.ds(base, rows)])

    return gather(table, idx)


def _experts_kernel(te_ref, nx_ref, nv_ref, xs_ref, w1_hbm, w3_hbm, w2_hbm, ys_ref,
                    w1f, w3f, w2f, w1b, w3b, w2b, sem, *, layer):
    i = pl.program_id(0)

    def fetch(e):
        row = e + layer * MOE_EXPERTS
        return [pltpu.make_async_copy(w1_hbm.at[row], w1f, sem),
                pltpu.make_async_copy(w3_hbm.at[row], w3f, sem),
                pltpu.make_async_copy(w2_hbm.at[row], w2f, sem)]

    @pl.when(i < nv_ref[0])
    def _():
        @pl.when(i == 0)
        def _():
            for cp in fetch(te_ref[0]):
                cp.start()

        @pl.when((i == 0) | (te_ref[i] != te_ref[jnp.maximum(i - 1, 0)]))
        def _():
            for cp in fetch(te_ref[i]):
                cp.wait()
            w1b[...] = w1f[...].astype(BF16)
            w3b[...] = w3f[...].astype(BF16)
            w2b[...] = w2f[...].astype(BF16)

            @pl.when(nx_ref[i] >= 0)
            def _():
                for cp in fetch(nx_ref[i]):
                    cp.start()

        lo, hi = _unpack_bf16_pairs(xs_ref[...])
        half = D_MODEL // 2
        d = functools.partial(jnp.dot, preferred_element_type=F32)
        h1 = d(lo, w1b[0:half, :]) + d(hi, w1b[half:, :])
        h3 = d(lo, w3b[0:half, :]) + d(hi, w3b[half:, :])
        hid = (h1 * _sigmoid(h1)) * h3
        y = d(hid.astype(BF16), w2b[...])
        ys_ref[...] = _pack_bf16_pairs(y[:, :half], y[:, half:])

    @pl.when(i >= nv_ref[0])
    def _():
        ys_ref[...] = jnp.zeros_like(ys_ref)


def _experts(xs, tile_expert, next_expert, n_valid, w1, w3, w2, layer):
    n_rows, W = xs.shape
    nt = n_rows // ROW_TILE
    D, F = w1.shape[1], w1.shape[2]

    def row_map(i, te, nx, nv):
        return (i, 0)

    hbm = pl.BlockSpec(memory_space=pl.ANY)
    return pl.pallas_call(
        functools.partial(_experts_kernel, layer=layer),
        grid_spec=pltpu.PrefetchScalarGridSpec(
            num_scalar_prefetch=3,
            grid=(nt,),
            in_specs=[pl.BlockSpec((ROW_TILE, W), row_map), hbm, hbm, hbm],
            out_specs=pl.BlockSpec((ROW_TILE, D // 2), row_map),
            scratch_shapes=[
                pltpu.VMEM((D, F), F32),
                pltpu.VMEM((D, F), F32),
                pltpu.VMEM((F, D), F32),
                pltpu.VMEM((D, F), BF16),
                pltpu.VMEM((D, F), BF16),
                pltpu.VMEM((F, D), BF16),
                pltpu.SemaphoreType.DMA(()),
            ],
        ),
        out_shape=jax.ShapeDtypeStruct((n_rows, D // 2), jnp.uint32),
        compiler_params=pltpu.CompilerParams(
            dimension_semantics=("arbitrary",), vmem_limit_bytes=VMEM_LIMIT),
        name="moe_experts",
    )(tile_expert, next_expert, n_valid, xs, w1, w3, w2)


def _combine_kernel(x_ref, info_ref, fw_ref, y1_ref, y2_ref, out_ref, *, final):
    info = info_ref[...]
    g1 = info[:, _I_G1:_I_G1 + 1]
    g2 = info[:, _I_G2:_I_G2 + 1]
    y1 = jnp.concatenate(_unpack_pairs_f32(y1_ref[...]), axis=1)
    y2 = jnp.concatenate(_unpack_pairs_f32(y2_ref[...]), axis=1)
    y = x_ref[...] + (g1 * y1 + g2 * y2)
    if final:
        y = _rms(y, fw_ref[...])
    out_ref[...] = y


def _combine(x2, part, info, y1, y2, final_w, final):
    T, D = x2.shape
    tp = T // COMBINE_PARTS
    tm = min(TOK_TILE, tp)
    nb = tp // tm
    x_spec = pl.BlockSpec((tm, D), lambda i: (part * nb + i, 0))
    half_spec = pl.BlockSpec((tm, D // 2), lambda i: (i, 0))
    return pl.pallas_call(
        functools.partial(_combine_kernel, final=final),
        grid=(nb,),
        in_specs=[x_spec, pl.BlockSpec((tm, LANES), lambda i: (part * nb + i, 0)),
                  _const_spec((1, D)), half_spec, half_spec],
        out_specs=x_spec,
        out_shape=jax.ShapeDtypeStruct((T, D), F32),
        input_output_aliases={0: 0},
        compiler_params=pltpu.CompilerParams(
            dimension_semantics=("arbitrary",), vmem_limit_bytes=VMEM_LIMIT),
        name="moe_combine",
    )(x2, info, final_w[None, :], y1, y2)


def _hier_moe(x2, norm_w, w_group, b_group, w_expert, b_expert, w1, w3, w2, layer, final_w, final):
    T, D = x2.shape
    nt = (2 * T) // ROW_TILE + MOE_EXPERTS
    hnp, info, info_t, cnt = _router(x2, norm_w, w_group, b_group, w_expert, b_expert)
    pos, tile_expert, next_expert, n_valid = _plan(cnt[0].astype(jnp.int32), info_t, nt)
    xs = _dispatch(hnp, pos[0], pos[1], nt * ROW_TILE)
    ys = _experts(xs, tile_expert, next_expert, n_valid, w1, w3, w2, layer)
    tp = T // COMBINE_PARTS
    gathered = [(_gather_rows(ys, pos[0, p * tp:(p + 1) * tp]), _gather_rows(ys, pos[1, p * tp:(p + 1) * tp]))
                for p in range(COMBINE_PARTS)]
    out = x2
    for part, (y1, y2) in enumerate(gathered):
        out = _combine(out, part, info, y1, y2, final_w, final)
    return out


def kernel(x, norm_mix, norm_ffn, norm_final, even_w_in, hgrn_lb_logits, hgrn_norm_w, gla_w_up,
           gla_b_up, gla_norm_w, even_w_out, odd_w_in, ret_norm_w, conv_w, conv_b, rglru_w_r,
           rglru_b_r, rglru_w_i, rglru_b_i, rglru_lambda, odd_w_out, moe_w_group, moe_b_group,
           moe_w_expert, moe_b_expert, moe_w1, moe_w3, moe_w2):
    B, L, D = x.shape
    depth = norm_mix.shape[0]
    lb_table = jnp.cumsum(jax.nn.softmax(hgrn_lb_logits.astype(F32), axis=0), axis=0)
    w1 = moe_w1.reshape((-1,) + moe_w1.shape[2:])
    w3 = moe_w3.reshape((-1,) + moe_w3.shape[2:])
    w2 = moe_w2.reshape((-1,) + moe_w2.shape[2:])
    for l in range(depth):
        j = l // 2
        if l % 2 == 0:
            x = _even_mixer(x, norm_mix[l], even_w_in[j], lb_table[l], gla_w_up[j], gla_b_up[j],
                            hgrn_norm_w[j], gla_norm_w[j], even_w_out[j])
        else:
            x = _odd_mixer(x, norm_mix[l], odd_w_in[j], ret_norm_w[j], conv_w[j], conv_b[j],
                           rglru_w_r[j], rglru_b_r[j], rglru_w_i[j], rglru_b_i[j], rglru_lambda[j],
                           odd_w_out[j])
        x = _hier_moe(x.reshape(B * L, D), norm_ffn[l], moe_w_group[l], moe_b_group[l],
                      moe_w_expert[l], moe_b_expert[l], w1, w3, w2, l,
                      norm_final, l == depth - 1).reshape(B, L, D)
    return x
```

```python
import functools

import numpy as np
import jax
import jax.numpy as jnp
from jax import lax
from jax.experimental import pallas as pl
from jax.experimental.pallas import tpu as pltpu
from jax.experimental.pallas import tpu_sc as plsc

F32 = jnp.float32
BF16 = jnp.bfloat16

EPS = 1e-6
CHUNK = 64
GROUP = 8
LOG2E = 1.4426950408889634
LANES = 128
D_MODEL = 1024
MIX_WIDTH = 1024
N_HEADS = 8
GLA_RANK = 16
GLA_TAU = 16.0
RET_HEADS = 4
RET_DK = 64
ROPE_BASE = 10000.0
LRU_WIDTH = 512
LRU_BLOCKS = 8
LRU_CONV = 4
LRU_C = 8.0
MOE_GROUPS = 4
MOE_PER_GROUP = 8
MOE_EXPERTS = MOE_GROUPS * MOE_PER_GROUP
MOE_DFF = 512
NEG_BIG = -1e30

MIX_TILE = 512
EVEN_SPLIT = 2
ODD_SPLIT = 2
TOK_TILE = 1024
ROW_TILE = 512
COMBINE_PARTS = 4
SC_CORES = 2
SC_WORKERS = 32
SC_ROWS = 128
VMEM_LIMIT = 56 * 1024 * 1024


def _dot(a, b):
    return jnp.dot(a.astype(BF16), b.astype(BF16), preferred_element_type=F32)


def _dot_nt(a, b):
    return lax.dot_general(a.astype(BF16), b.astype(BF16), (((1,), (1,)), ((), ())),
                           preferred_element_type=F32)


def _dot_tn(a, b):
    return lax.dot_general(a.astype(BF16), b.astype(BF16), (((0,), (0,)), ((), ())),
                           preferred_element_type=F32)


def _split2(x):
    hi = x.astype(BF16)
    lo = (x - hi.astype(F32)).astype(BF16)
    return hi, lo


def _split3(x):
    hi = x.astype(BF16)
    r = x - hi.astype(F32)
    mid = r.astype(BF16)
    lo = (r - mid.astype(F32)).astype(BF16)
    return hi, mid, lo


def _dot_x3(a, b):
    ah, al = _split2(a)
    bh, bl = _split2(b)
    d = functools.partial(jnp.dot, preferred_element_type=F32)
    return d(ah, bh) + (d(ah, bl) + d(al, bh))


def _sigmoid(x):
    return 1.0 / (1.0 + jnp.exp(-x))


def _softplus(x):
    return jnp.maximum(x, 0.0) + jnp.log(1.0 + jnp.exp(-jnp.abs(x)))


def _rms(x, w):
    return x * lax.rsqrt(jnp.mean(x * x, axis=-1, keepdims=True) + EPS) * w


def _tri(n):
    r = lax.broadcasted_iota(jnp.int32, (n, n), 0)
    c = lax.broadcasted_iota(jnp.int32, (n, n), 1)
    return jnp.where(r >= c, 1.0, 0.0).astype(BF16)


def _cumsum_rows(tri3, g):
    return jnp.dot(tri3, jnp.concatenate(_split3(g), axis=0), preferred_element_type=F32)


_E_AQ, _E_AF, _E_AI, _E_AG = 0, 512, 1024, 1536
_E_BQ, _E_BK, _E_BV, _E_BR, _E_LOW = 2048, 2304, 2560, 3072, 3584
EVEN_COLS = 3712
_QK_COLS = 768


_LEVEL_BLOCKS = (32, 16, 8)


def _level_tables():
    t = np.arange(CHUNK)
    out = []
    for blk in _LEVEL_BLOCKS:
        odd = (t // blk) % 2 == 1
        out.append((np.where(odd, (t // blk) * blk - 1, -1), np.where(~odd, (t // blk) * blk + blk - 1, -1)))
    return out


def _gla_chunk(rows, q_ref, k_ref, v_ref, b2, o_ref, st_ref, a_ref):
    n_units = _QK_COLS // LANES
    lane = lax.broadcasted_iota(jnp.int32, (1, LANES), 1)
    trow = lax.broadcasted_iota(jnp.int32, (GROUP, LANES), 0)
    unit_heads = [((u, None),) for u in range(4)] + [((4, 0), (5, 64)), ((6, 0), (7, 64))]
    qs = [q_ref[rows, u * LANES:(u + 1) * LANES] for u in range(n_units)]
    ks = [k_ref[rows, u * LANES:(u + 1) * LANES] for u in range(n_units)]
    bs = [b2[:, u * LANES:(u + 1) * LANES] for u in range(n_units)]
    vs = [v_ref[rows, h * LANES:(h + 1) * LANES] for h in range(N_HEADS)]

    for u in range(n_units):
        for j in range(CHUNK // GROUP):
            qj = qs[u][j * GROUP:(j + 1) * GROUP]
            bj = bs[u][j * GROUP:(j + 1) * GROUP]
            for s in range(0, GROUP, 2):
                pair = [qj * jnp.exp2(jnp.where(trow >= s + i, bj - bj[s + i:s + i + 1, :], NEG_BIG))
                        for i in (0, 1)]
                r0 = (j * GROUP + s) * GROUP
                a_ref[r0:r0 + 2 * GROUP, u * LANES:(u + 1) * LANES] = (
                    jnp.concatenate(pair, axis=0).astype(BF16))

    zk = jnp.zeros((CHUNK, LANES), F32)
    scs = []
    for pr in range(n_units // 2):
        u0, u1 = 2 * pr, 2 * pr + 1
        if u0 < 4:
            keys = jnp.concatenate([jnp.concatenate([ks[u0], zk], axis=1),
                                    jnp.concatenate([zk, ks[u1]], axis=1)], axis=0)
        else:
            lo0, hi0 = jnp.where(lane < 64, ks[u0], 0.0), jnp.where(lane >= 64, ks[u0], 0.0)
            lo1, hi1 = jnp.where(lane < 64, ks[u1], 0.0), jnp.where(lane >= 64, ks[u1], 0.0)
            keys = jnp.concatenate([jnp.concatenate([lo0, zk], axis=1), jnp.concatenate([hi0, zk], axis=1),
                                    jnp.concatenate([zk, lo1], axis=1), jnp.concatenate([zk, hi1], axis=1)],
                                   axis=0)
        scs.append(_dot_nt(a_ref[:, u0 * LANES:(u1 + 1) * LANES], keys))

    tables = _level_tables()
    tq = lax.broadcasted_iota(jnp.int32, (CHUNK, CHUNK), 0)
    ts = lax.broadcasted_iota(jnp.int32, (CHUNK, CHUNK), 1)
    covers = [((tq // blk) % 2 == 1) & ((tq // blk) == (ts // blk) + 1) & ((ts // blk) % 2 == 0)
              for blk in _LEVEL_BLOCKS]
    s_offs, o_inter = {}, {}
    for u in range(n_units):
        q, k, b = qs[u], ks[u], bs[u]
        heads = unit_heads[u]
        if len(heads) == 2:
            qm = jnp.concatenate([jnp.where((lane & 64) == lo, q, 0.0) for _, lo in heads], axis=0)
            bm = jnp.concatenate([b, b], axis=0)
        else:
            qm, bm = q, b
        s_off = None
        for (qref, kref), cover in zip(tables, covers):
            qrows, krows = [], []
            for g in range(CHUNK // GROUP):
                sl = slice(g * GROUP, (g + 1) * GROUP)
                rq, rk = int(qref[g * GROUP]), int(kref[g * GROUP])
                qrows.append(jnp.exp2(b[sl] - b[rq:rq + 1, :]) if rq >= 0 else jnp.zeros((GROUP, LANES), F32))
                krows.append(k[sl] * jnp.exp2(b[rk:rk + 1, :] - b[sl]) if rk >= 0
                             else jnp.zeros((GROUP, LANES), F32))
            qfac = jnp.concatenate(qrows, axis=0)
            if len(heads) == 2:
                qfac = jnp.concatenate([qfac, qfac], axis=0)
                cov = jnp.concatenate([cover, cover], axis=0)
            else:
                cov = cover
            part = _dot_nt(qm * qfac, jnp.concatenate(krows, axis=0))
            s_off = jnp.where(cov, part, 0.0 if s_off is None else s_off)
        eb = jnp.exp2(bm)
        b_last = b[CHUNK - 1:CHUNK, :]
        kd = (k * jnp.exp2(b_last - b)).astype(BF16)
        dec = jnp.exp2(b_last)
        qe = qm * eb
        for i, (head, lo) in enumerate(heads):
            sl = slice(i * CHUNK, (i + 1) * CHUNK)
            st = st_ref[head]
            o_inter[head] = _dot_nt(qe[sl], st)
            st_ref[head] = st * dec + _dot_tn(vs[head], kd)
            s_offs[head] = s_off[sl]

    for u in range(n_units):
        sc = scs[u // 2]
        paired = len(unit_heads[u]) == 2
        width = LANES if paired else CHUNK
        c0 = (u % 2) * width
        scol = (lane & 63) if paired else lane[:, :CHUNK]
        groups = []
        for j in range(CHUNK // GROUP):
            acc = jnp.zeros((GROUP, width), F32)
            for s in range(GROUP):
                r0 = (j * GROUP + s) * GROUP
                acc = jnp.where(scol == j * GROUP + s, sc[r0:r0 + GROUP, c0:c0 + width], acc)
            groups.append(acc)
        p = jnp.concatenate(groups, axis=0)
        if paired:
            vstack = jnp.concatenate([vs[h] for h, _ in unit_heads[u]], axis=0)
        for head, lo in unit_heads[u]:
            if lo is None:
                o = o_inter[head] + _dot(s_offs[head] + p, vs[head])
            else:
                o = (o_inter[head] + _dot(s_offs[head], vs[head])
                     + _dot(jnp.where((lane & 64) == lo, p, 0.0), vstack))
            o_ref[rows, head * LANES:(head + 1) * LANES] = o


def _even_kernel(x_ref, nw_ref, win_ref, lb_ref, wup_ref, bup_ref, hw_ref, wout_ref, out_ref,
                 q_ref, k_ref, g_ref, v_ref, gate_ref, o_ref, st_ref, a_ref):
    tl = x_ref.shape[1]
    parts = [(r0, tl // EVEN_SPLIT) for r0 in range(0, tl, tl // EVEN_SPLIT)]

    @pl.when(pl.program_id(1) == 0)
    def _():
        st_ref[...] = jnp.zeros_like(st_ref)

    def project(r0, n):
        rs = slice(r0, r0 + n)
        hn = _rms(x_ref[0, rs, :], nw_ref[...]).astype(BF16)

        def proj(c0, w):
            return jnp.dot(hn, win_ref[:, c0:c0 + w], preferred_element_type=F32)

        q_ref[rs, 0:512] = proj(_E_AQ, 512)
        lb = lb_ref[...]
        f = lb + (1.0 - lb) * _sigmoid(proj(_E_AF, 512))
        k_ref[rs, 0:512] = 1.0 - f
        g_ref[rs, 0:512] = jnp.log(f)
        v_ref[rs, 0:512] = proj(_E_AI, 512)
        ag = proj(_E_AG, 512)
        gate_ref[rs, 0:512] = ag * _sigmoid(ag)
        q_ref[rs, 512:768] = proj(_E_BQ, 256) * (64.0 ** -0.5)
        k_ref[rs, 512:768] = proj(_E_BK, 256)
        v_ref[rs, 512:1024] = proj(_E_BV, 512)
        br = proj(_E_BR, 512)
        gate_ref[rs, 512:1024] = br * _sigmoid(br)
        z = _dot_x3(proj(_E_LOW, LANES), wup_ref[...]) + bup_ref[...]
        g_ref[rs, 512:768] = -_softplus(-z) * (1.0 / GLA_TAU)

    tri = _tri(CHUNK)
    tri3 = jnp.concatenate([tri, tri, tri], axis=1)

    def finish(r0, n):
        rs = slice(r0, r0 + n)
        hw = hw_ref[...]
        ys = []
        for head in range(N_HEADS):
            sl = slice(head * LANES, (head + 1) * LANES)
            oh = o_ref[rs, sl]
            ys.append(oh * lax.rsqrt(jnp.mean(oh * oh, axis=-1, keepdims=True) + EPS)
                      * hw[:, sl] * gate_ref[rs, sl])
        y = jnp.concatenate(ys, axis=1).astype(BF16)
        out_ref[0, rs, :] = x_ref[0, rs, :] + jnp.dot(y, wout_ref[...], preferred_element_type=F32)

    for r0, n in parts:
        project(r0, n)
    for r0, n in parts:
        for c in range(r0 // CHUNK, (r0 + n) // CHUNK):
            rows = slice(c * CHUNK, (c + 1) * CHUNK)
            b2 = _cumsum_rows(tri3, g_ref[rows, :]) * LOG2E
            _gla_chunk(rows, q_ref, k_ref, v_ref, b2, o_ref, st_ref, a_ref.at[c % 2])
    for r0, n in parts:
        finish(r0, n)


def _const_spec(shape):
    nd = len(shape)
    return pl.BlockSpec(shape, lambda *_: (0,) * nd)


def _even_mixer(x, norm_w, w_in, lb, w_up, b_up, hgrn_norm_w, gla_norm_w, w_out):
    B, L, D = x.shape
    tl = min(MIX_TILE, L)
    s = np.cumsum([0, 512, 512, 512, 512, 256, 256, 512, GLA_RANK, 512])
    cols = [w_in[:, s[i]:s[i + 1]] for i in range(9)]
    low = jnp.pad(cols[7], ((0, 0), (0, LANES - GLA_RANK)))
    w_in_r = jnp.concatenate(cols[:7] + [cols[8], low], axis=1).astype(BF16)
    w_up_p = jnp.pad(w_up, ((0, LANES - GLA_RANK), (0, 0)))
    hw = jnp.concatenate([hgrn_norm_w, gla_norm_w])[None, :]
    return pl.pallas_call(
        _even_kernel,
        grid=(B, L // tl),
        in_specs=[
            pl.BlockSpec((1, tl, D), lambda b, l: (b, l, 0)),
            _const_spec((1, D)),
            _const_spec((D, EVEN_COLS)),
            _const_spec((1, 512)),
            _const_spec((LANES, 256)),
            _const_spec((1, 256)),
            _const_spec((1, MIX_WIDTH)),
            _const_spec((MIX_WIDTH, D)),
        ],
        out_specs=pl.BlockSpec((1, tl, D), lambda b, l: (b, l, 0)),
        out_shape=jax.ShapeDtypeStruct((B, L, D), F32),
        scratch_shapes=[
            pltpu.VMEM((tl, _QK_COLS), F32),
            pltpu.VMEM((tl, _QK_COLS), F32),
            pltpu.VMEM((tl, _QK_COLS), F32),
            pltpu.VMEM((tl, MIX_WIDTH), F32),
            pltpu.VMEM((tl, MIX_WIDTH), F32),
            pltpu.VMEM((tl, MIX_WIDTH), F32),
            pltpu.VMEM((N_HEADS, LANES, LANES), F32),
            pltpu.VMEM((2, CHUNK * GROUP, _QK_COLS), BF16),
        ],
        compiler_params=pltpu.CompilerParams(
            dimension_semantics=("arbitrary", "arbitrary"), vmem_limit_bytes=VMEM_LIMIT),
        name="even_mixer",
    )(x, norm_w[None, :], w_in_r, lb[None, :], w_up_p, b_up[None, :], hw, w_out.astype(BF16))


_O_Q, _O_QR, _O_K, _O_KR, _O_V, _O_G, _O_X, _O_GATE = 0, 256, 512, 768, 1024, 1536, 2048, 2560
ODD_COLS = 3072


def _odd_kernel(x_ref, nw_ref, win_ref, cos_ref, sin_ref, dmat_ref, qdec_ref, kdec_ref, sdec_ref,
                rw_ref, cw_ref, cb_ref, wri_ref, bri_ref, lam_ref, wout_ref, out_ref,
                q_ref, k_ref, v_ref, gate_ref, o_ref, xe_ref, a_ref, u_ref, st_ref, h_ref):
    tl = x_ref.shape[1]
    parts = [(r0, tl // ODD_SPLIT) for r0 in range(0, tl, tl // ODD_SPLIT)]

    @pl.when(pl.program_id(1) == 0)
    def _():
        st_ref[...] = jnp.zeros_like(st_ref)
        h_ref[...] = jnp.zeros_like(h_ref)
        xe_ref[0:8, :] = jnp.zeros((8, LRU_WIDTH), F32)

    def project(r0, n):
        rs = slice(r0, r0 + n)
        hn = _rms(x_ref[0, rs, :], nw_ref[...]).astype(BF16)

        def proj(c0, w):
            return jnp.dot(hn, win_ref[:, c0:c0 + w], preferred_element_type=F32)

        cos = cos_ref[rs, :]
        sin = sin_ref[rs, :]
        q_ref[rs, :] = proj(_O_Q, 256) * cos + proj(_O_QR, 256) * sin
        k_ref[rs, :] = (proj(_O_K, 256) * cos + proj(_O_KR, 256) * sin) * (RET_DK ** -0.5)
        v_ref[rs, :] = proj(_O_V, 512)
        cg = proj(_O_G, 512)
        gate_ref[rs, 0:512] = cg * _sigmoid(cg)
        dg = proj(_O_GATE, LRU_WIDTH)
        gate_ref[rs, 512:1024] = 0.5 * dg * (
            1.0 + jnp.tanh(np.sqrt(2.0 / np.pi) * (dg + 0.044715 * (dg * dg * dg))))

        xe_ref[8 + r0:8 + r0 + n, :] = proj(_O_X, LRU_WIDTH)
        cw = cw_ref[...]
        xc = cb_ref[...]
        for j in range(LRU_CONV):
            xc = xc + xe_ref[5 + j + r0:5 + j + r0 + n, :] * cw[j:j + 1, :]
        ri = jnp.dot(xc.astype(BF16), wri_ref[...], preferred_element_type=F32) + bri_ref[...]
        r = _sigmoid(ri[:, :LRU_WIDTH])
        i = _sigmoid(ri[:, LRU_WIDTH:])
        a = jnp.exp(-LRU_C * r * _softplus(-lam_ref[...]))
        a_ref[rs, :] = a
        u_ref[rs, :] = jnp.sqrt(1.0 - a * a) * (i * xc)

    lane = lax.broadcasted_iota(jnp.int32, (1, LANES), 1)
    crow = lax.broadcasted_iota(jnp.int32, (CHUNK, LRU_WIDTH), 0)

    def recur(r0, n):
        rs = slice(r0, r0 + n)
        for head in range(RET_HEADS):
            unit, half = head // 2, head % 2
            ul = slice(unit * LANES, (unit + 1) * LANES)
            vl = slice(head * LANES, (head + 1) * LANES)
            own = (lane < 64) if half == 0 else (lane >= 64)
            q = jnp.where(own, q_ref[rs, ul], 0.0)
            k = k_ref[rs, ul]
            v = v_ref[rs, vl]
            st = st_ref[head]
            o = _dot_nt(q, st) * qdec_ref[head]
            o = o + _dot(_dot_nt(q, k) * dmat_ref[head], v)
            st_ref[head] = st * sdec_ref[head] + _dot_tn(v, k * kdec_ref[head])
            o_ref[rs, vl] = o
        for c in range(r0 // CHUNK, (r0 + n) // CHUNK):
            rows = slice(c * CHUNK, (c + 1) * CHUNK)
            ca = a_ref[rows, :]
            ch = u_ref[rows, :]
            d = 1
            while d < CHUNK:
                keep = crow >= d
                sa = jnp.where(keep, pltpu.roll(ca, d, axis=0), 1.0)
                sh = jnp.where(keep, pltpu.roll(ch, d, axis=0), 0.0)
                ch = ca * sh + ch
                ca = ca * sa
                d *= 2
            ch = ch + ca * h_ref[...]
            h_ref[...] = ch[CHUNK - 1:CHUNK, :]
            o_ref[rows, 512:1024] = ch

    def finish(r0, n):
        rs = slice(r0, r0 + n)
        rw = rw_ref[...]
        ys = []
        for head in range(RET_HEADS):
            sl = slice(head * LANES, (head + 1) * LANES)
            oh = o_ref[rs, sl]
            oh = oh - jnp.mean(oh, axis=-1, keepdims=True)
            ys.append(oh * lax.rsqrt(jnp.mean(oh * oh, axis=-1, keepdims=True) + EPS) * rw[:, sl])
        ys.append(o_ref[rs, 512:1024])
        y = (jnp.concatenate(ys, axis=1) * gate_ref[rs, :]).astype(BF16)
        out_ref[0, rs, :] = x_ref[0, rs, :] + jnp.dot(y, wout_ref[...], preferred_element_type=F32)

    for r0, n in parts:
        project(r0, n)
    xe_ref[0:8, :] = xe_ref[tl:tl + 8, :]
    for r0, n in parts:
        recur(r0, n)
    for r0, n in parts:
        finish(r0, n)


def _rot_cols(w):
    d = w.shape[0]
    w4 = w.reshape(d, RET_HEADS, 2, RET_DK // 2)
    return jnp.stack([-w4[:, :, 1], w4[:, :, 0]], axis=2).reshape(d, RET_HEADS * RET_DK)


def _odd_mixer(x, norm_w, w_in, ret_norm_w, conv_w, conv_b, w_r, b_r, w_i, b_i, lam, w_out):
    B, L, D = x.shape
    tl = min(MIX_TILE, L)
    s = np.cumsum([0, 256, 256, 512, 512, 512, 512])
    wq, wk, wv, wg, wx, wgate = [w_in[:, s[i]:s[i + 1]] for i in range(6)]
    w_in_r = jnp.concatenate([wq, _rot_cols(wq), wk, _rot_cols(wk), wv, wg, wx, wgate],
                             axis=1).astype(BF16)
    half = RET_DK // 2
    inv = ROPE_BASE ** (-jnp.arange(half, dtype=F32) / half)
    ang = jnp.arange(L, dtype=F32)[:, None] * inv[None, :]
    cos = jnp.tile(jnp.cos(ang), (1, 2 * RET_HEADS))
    sin = jnp.tile(jnp.sin(ang), (1, 2 * RET_HEADS))
    lg = jnp.log1p(-jnp.exp2(-5.0 - jnp.arange(RET_HEADS, dtype=F32)))[:, None, None]
    rc = tl // ODD_SPLIT
    t = jnp.arange(rc, dtype=F32)
    rel = t[:, None] - t[None, :]
    dmat = jnp.where(rel >= 0, jnp.exp(lg * jnp.maximum(rel, 0.0)), 0.0)
    qdec = jnp.broadcast_to(jnp.exp(lg * (t[None, :, None] + 1.0)), (RET_HEADS, rc, LANES))
    kdec = jnp.broadcast_to(jnp.exp(lg * (rc - 1.0 - t[None, :, None])), (RET_HEADS, rc, LANES))
    sdec = jnp.broadcast_to(jnp.exp(lg * float(rc)), (RET_HEADS, 1, LANES))
    eye = jnp.eye(LRU_BLOCKS, dtype=F32)
    bd = LRU_WIDTH // LRU_BLOCKS

    def blockdiag(w):
        return (eye[:, None, :, None] * w[:, :, None, :]).reshape(LRU_WIDTH, LRU_WIDTH)

    wri = jnp.concatenate([blockdiag(w_r), blockdiag(w_i)], axis=1).astype(BF16)
    bri = jnp.concatenate([b_r, b_i])[None, :]
    return pl.pallas_call(
        _odd_kernel,
        grid=(B, L // tl),
        in_specs=[
            pl.BlockSpec((1, tl, D), lambda b, l: (b, l, 0)),
            _const_spec((1, D)),
            _const_spec((D, ODD_COLS)),
            pl.BlockSpec((tl, 256), lambda b, l: (l, 0)),
            pl.BlockSpec((tl, 256), lambda b, l: (l, 0)),
            _const_spec((RET_HEADS, rc, rc)),
            _const_spec((RET_HEADS, rc, LANES)),
            _const_spec((RET_HEADS, rc, LANES)),
            _const_spec((RET_HEADS, 1, LANES)),
            _const_spec((1, 512)),
            _const_spec((LRU_CONV, LRU_WIDTH)),
            _const_spec((1, LRU_WIDTH)),
            _const_spec((LRU_WIDTH, 2 * LRU_WIDTH)),
            _const_spec((1, 2 * LRU_WIDTH)),
            _const_spec((1, LRU_WIDTH)),
            _const_spec((MIX_WIDTH, D)),
        ],
        out_specs=pl.BlockSpec((1, tl, D), lambda b, l: (b, l, 0)),
        out_shape=jax.ShapeDtypeStruct((B, L, D), F32),
        scratch_shapes=[
            pltpu.VMEM((tl, 256), F32),
            pltpu.VMEM((tl, 256), F32),
            pltpu.VMEM((tl, 512), F32),
            pltpu.VMEM((tl, MIX_WIDTH), F32),
            pltpu.VMEM((tl, MIX_WIDTH), F32),
            pltpu.VMEM((tl + 8, LRU_WIDTH), F32),
            pltpu.VMEM((tl, LRU_WIDTH), F32),
            pltpu.VMEM((tl, LRU_WIDTH), F32),
            pltpu.VMEM((RET_HEADS, LANES, LANES), F32),
            pltpu.VMEM((1, LRU_WIDTH), F32),
        ],
        compiler_params=pltpu.CompilerParams(
            dimension_semantics=("arbitrary", "arbitrary"), vmem_limit_bytes=VMEM_LIMIT),
        name="odd_mixer",
    )(x, norm_w[None, :], w_in_r, cos, sin, dmat, qdec, kdec, sdec, ret_norm_w[None, :],
      conv_w, conv_b[None, :], wri, bri, lam[None, :], w_out.astype(BF16))


_I_E1, _I_E2, _I_G1, _I_G2, _I_R1, _I_R2 = 0, 1, 2, 3, 4, 5
_LOGIT_E0 = MOE_GROUPS
ROUTE_ROWS = 40


def _pack_bf16_pairs(lo, hi):
    lo_b = pltpu.bitcast(lo.astype(BF16).astype(F32), jnp.uint32)
    hi_b = pltpu.bitcast(hi.astype(BF16).astype(F32), jnp.uint32)
    return (lo_b >> 16) | (hi_b & jnp.uint32(0xFFFF0000))


def _unpack_pairs_f32(w):
    return pltpu.bitcast(w << 16, F32), pltpu.bitcast(w & jnp.uint32(0xFFFF0000), F32)


def _unpack_bf16_pairs(w):
    lo, hi = _unpack_pairs_f32(w)
    return lo.astype(BF16), hi.astype(BF16)


def _router_kernel(x_ref, nw_ref, wrt_ref, brt_ref, hnp_ref, info_ref, info_t_ref, cnt_ref):
    tm = x_ref.shape[0]

    @pl.when(pl.program_id(0) == 0)
    def _():
        cnt_ref[...] = jnp.zeros_like(cnt_ref)

    hn = _rms(x_ref[...], nw_ref[...])
    half = D_MODEL // 2
    hn_hi = hn.astype(BF16)
    hn_hif = hn_hi.astype(F32)
    hn_lo = (hn - hn_hif).astype(BF16)
    bits = pltpu.bitcast(hn_hif, jnp.uint32)
    hnp_ref[...] = (bits[:, :half] >> 16) | (bits[:, half:] & jnp.uint32(0xFFFF0000))

    w_hi, w_lo = _split2(wrt_ref[...])
    both = _dot_nt(jnp.concatenate([w_hi, w_lo], axis=0), hn_hi)
    bias = jnp.concatenate([brt_ref[...]] * (tm // LANES), axis=1)
    lt = (both[:LANES] + (both[LANES:] + _dot_nt(w_hi, hn_lo)) + bias)[0:ROUTE_ROWS]
    row = lax.broadcasted_iota(jnp.int32, (ROUTE_ROWS, tm), 0).astype(F32)

    def first_max(vals):
        m = jnp.max(vals, axis=0, keepdims=True)
        idx = jnp.min(jnp.where(vals == m, row, float(LANES)), axis=0, keepdims=True)
        return m, idx

    gl = jnp.where(row < MOE_GROUPS, lt, NEG_BIG)
    gmax, gidx = first_max(gl)
    g_w = 1.0 / jnp.sum(jnp.exp(gl - gmax), axis=0, keepdims=True)
    e0 = _LOGIT_E0 + MOE_PER_GROUP * gidx
    el = jnp.where((row >= e0) & (row < e0 + MOE_PER_GROUP), lt, NEG_BIG)
    m1, i1 = first_max(el)
    m2, i2 = first_max(jnp.where(row == i1, NEG_BIG, el))
    p2 = jnp.exp(m2 - m1)
    w1 = 1.0 / (1.0 + p2)
    e1 = i1 - _LOGIT_E0
    e2 = i2 - _LOGIT_E0

    erow = lax.broadcasted_iota(jnp.int32, (MOE_EXPERTS, tm), 0).astype(F32)
    oh1 = jnp.where(erow == e1, 1.0, 0.0)
    oh2 = jnp.where(erow == e2, 1.0, 0.0)
    r_i = lax.broadcasted_iota(jnp.int32, (tm, tm), 0)
    c_i = lax.broadcasted_iota(jnp.int32, (tm, tm), 1)
    triu = jnp.where(r_i <= c_i, 1.0, 0.0).astype(BF16)
    pre = jnp.dot(jnp.concatenate([oh1, oh2], axis=0).astype(BF16), triu, preferred_element_type=F32)
    pre1, pre2 = pre[:MOE_EXPERTS], pre[MOE_EXPERTS:]
    carry = cnt_ref[0:MOE_EXPERTS, :]
    tot = jnp.dot(jnp.concatenate([oh1, oh2], axis=0).astype(BF16), jnp.ones((tm, LANES), BF16),
                  preferred_element_type=F32)
    c1, c2 = tot[:MOE_EXPERTS], tot[MOE_EXPERTS:]
    rep = tm // LANES
    r1 = jnp.sum(oh1 * (pre1 + jnp.concatenate([carry] * rep, axis=1)), axis=0, keepdims=True) - 1.0
    r2 = jnp.sum(oh2 * (pre2 + jnp.concatenate([carry + c1] * rep, axis=1)), axis=0, keepdims=True) - 1.0
    cnt_ref[0:MOE_EXPERTS, :] = carry + (c1 + c2)

    row8 = lax.broadcasted_iota(jnp.int32, (GROUP, tm), 0).astype(F32)
    info_t = jnp.zeros((GROUP, tm), F32)
    for k, val in ((_I_E1, e1), (_I_E2, e2), (_I_G1, g_w * w1), (_I_G2, g_w * (w1 * p2)),
                   (_I_R1, r1), (_I_R2, r2)):
        info_t = jnp.where(row8 == float(k), val, info_t)
    info_t_ref[...] = info_t
    info_ref[...] = jnp.concatenate([info_t, jnp.zeros((LANES - GROUP, tm), F32)], axis=0).T


def _router(x2, norm_w, w_group, b_group, w_expert, b_expert):
    T, D = x2.shape
    tm = min(TOK_TILE, T)
    pad = LANES - MOE_GROUPS - MOE_EXPERTS
    wrt = jnp.pad(jnp.concatenate([w_group, w_expert], axis=1), ((0, 0), (0, pad))).T
    brt = jnp.broadcast_to(jnp.pad(jnp.concatenate([b_group, b_expert]), (0, pad))[:, None], (LANES, LANES))
    return pl.pallas_call(
        _router_kernel,
        grid=(T // tm,),
        in_specs=[
            pl.BlockSpec((tm, D), lambda i: (i, 0)),
            _const_spec((1, D)),
            _const_spec((LANES, D)),
            _const_spec((LANES, LANES)),
        ],
        out_specs=[
            pl.BlockSpec((tm, D // 2), lambda i: (i, 0)),
            pl.BlockSpec((tm, LANES), lambda i: (i, 0)),
            pl.BlockSpec((GROUP, tm), lambda i: (0, i)),
            _const_spec((LANES, LANES)),
        ],
        out_shape=[
            jax.ShapeDtypeStruct((T, D // 2), jnp.uint32),
            jax.ShapeDtypeStruct((T, LANES), F32),
            jax.ShapeDtypeStruct((GROUP, T), F32),
            jax.ShapeDtypeStruct((LANES, LANES), F32),
        ],
        compiler_params=pltpu.CompilerParams(
            dimension_semantics=("arbitrary",), vmem_limit_bytes=VMEM_LIMIT),
        name="moe_router",
    )(x2, norm_w[None, :], wrt, brt)


def _plan_kernel(cnt_ref, info_ref, pos_ref, te_ref, nx_ref, nv_ref):
    nt = te_ref.shape[0]
    e1 = info_ref[_I_E1:_I_E1 + 1, :]
    e2 = info_ref[_I_E2:_I_E2 + 1, :]
    s1 = jnp.zeros_like(e1)
    s2 = jnp.zeros_like(e2)
    tiles = [lax.div(cnt_ref[e] + (ROW_TILE - 1), jnp.int32(ROW_TILE)) for e in range(MOE_EXPERTS)]
    following = [None] * MOE_EXPERTS
    nxt = jnp.int32(-1)
    for e in reversed(range(MOE_EXPERTS)):
        following[e] = nxt
        nxt = jnp.where(tiles[e] > 0, jnp.int32(e), nxt)
    start = jnp.int32(0)
    for e in range(MOE_EXPERTS):
        seg = (start * ROW_TILE).astype(F32)
        s1 = jnp.where(e1 == float(e), seg, s1)
        s2 = jnp.where(e2 == float(e), seg, s2)

        def fill(i, carry, e=e):
            te_ref[i] = jnp.int32(e)
            nx_ref[i] = following[e]
            return carry

        lax.fori_loop(start, start + tiles[e], fill, 0)
        start = start + tiles[e]
    nv_ref[0] = start

    def fill_tail(i, carry):
        te_ref[i] = jnp.int32(MOE_EXPERTS - 1)
        nx_ref[i] = jnp.int32(-1)
        return carry

    lax.fori_loop(start, nt, fill_tail, 0)
    pos_ref[0:1, :] = (s1 + info_ref[_I_R1:_I_R1 + 1, :]).astype(jnp.int32)
    pos_ref[1:2, :] = (s2 + info_ref[_I_R2:_I_R2 + 1, :]).astype(jnp.int32)


def _plan(cnt, info_t, nt):
    T = info_t.shape[1]
    smem = pl.BlockSpec(memory_space=pltpu.SMEM)
    return pl.pallas_call(
        _plan_kernel,
        in_specs=[smem, pl.BlockSpec(memory_space=pltpu.VMEM)],
        out_specs=[pl.BlockSpec(memory_space=pltpu.VMEM), smem, smem, smem],
        out_shape=[
            jax.ShapeDtypeStruct((2, T), jnp.int32),
            jax.ShapeDtypeStruct((nt,), jnp.int32),
            jax.ShapeDtypeStruct((nt,), jnp.int32),
            jax.ShapeDtypeStruct((1,), jnp.int32),
        ],
        name="moe_plan",
    )(cnt, info_t)


def _sc_worker():
    return lax.axis_index("s") * SC_CORES + lax.axis_index("c")


def _dispatch(hnp, pos1, pos2, n_rows):
    T, W = hnp.shape
    per_w = T // SC_WORKERS
    n_chunks = per_w // SC_ROWS
    mesh = plsc.VectorSubcoreMesh(core_axis_name="c", subcore_axis_name="s")

    @functools.partial(
        pl.kernel, mesh=mesh,
        out_type=jax.ShapeDtypeStruct((n_rows, W), jnp.uint32),
        scratch_types=[
            pltpu.VMEM((SC_ROWS,), jnp.int32),
            pltpu.VMEM((SC_ROWS,), jnp.int32),
            pltpu.VMEM((SC_ROWS, W), jnp.uint32),
        ],
        name="moe_dispatch_sc",
    )
    def scatter(hnp_hbm, p1_hbm, p2_hbm, xs_hbm, i1_v, i2_v, rows_v):
        base0 = _sc_worker() * per_w

        @pl.loop(0, n_chunks)
        def _(j):
            base = pl.multiple_of(base0 + j * SC_ROWS, SC_ROWS)
            pltpu.sync_copy(p1_hbm.at[pl.ds(base, SC_ROWS)], i1_v)
            pltpu.sync_copy(p2_hbm.at[pl.ds(base, SC_ROWS)], i2_v)
            pltpu.sync_copy(hnp_hbm.at[pl.ds(base, SC_ROWS)], rows_v)
            pltpu.sync_copy(rows_v, xs_hbm.at[i1_v])
            pltpu.sync_copy(rows_v, xs_hbm.at[i2_v])

    return scatter(hnp, pos1, pos2)


def _gather_rows(table, idx):
    T = idx.shape[0]
    W = table.shape[1]
    per_w = T // SC_WORKERS
    rows = SC_ROWS
    n_chunks = per_w // rows
    mesh = plsc.VectorSubcoreMesh(core_axis_name="c", subcore_axis_name="s")

    @functools.partial(
        pl.kernel, mesh=mesh,
        out_type=jax.ShapeDtypeStruct((T, W), table.dtype),
        scratch_types=[
            pltpu.VMEM((rows,), jnp.int32),
            pltpu.VMEM((rows, W), table.dtype),
        ],
        name="moe_gather_sc",
    )
    def gather(table_hbm, idx_hbm, out_hbm, i_v, rows_v):
        base0 = _sc_worker() * per_w

        @pl.loop(0, n_chunks)
        def _(j):
            base = pl.multiple_of(base0 + j * rows, rows)
            pltpu.sync_copy(idx_hbm.at[pl.ds(base, rows)], i_v)
            pltpu.sync_copy(table_hbm.at[i_v], rows_v)
            pltpu.sync_copy(rows_v, out_hbm.at[pl.ds(base, rows)])

    return gather(table, idx)


def _experts_kernel(te_ref, nx_ref, nv_ref, xs_ref, w1_hbm, w3_hbm, w2_hbm, ys_ref,
                    w1f, w3f, w2f, w1b, w3b, w2b, sem, *, layer):
    i = pl.program_id(0)

    def fetch(e):
        row = e + layer * MOE_EXPERTS
        return [pltpu.make_async_copy(w1_hbm.at[row], w1f, sem),
                pltpu.make_async_copy(w3_hbm.at[row], w3f, sem),
                pltpu.make_async_copy(w2_hbm.at[row], w2f, sem)]

    @pl.when(i < nv_ref[0])
    def _():
        @pl.when(i == 0)
        def _():
            for cp in fetch(te_ref[0]):
                cp.start()

        @pl.when((i == 0) | (te_ref[i] != te_ref[jnp.maximum(i - 1, 0)]))
        def _():
            for cp in fetch(te_ref[i]):
                cp.wait()
            w1b[...] = w1f[...].astype(BF16)
            w3b[...] = w3f[...].astype(BF16)
            w2b[...] = w2f[...].astype(BF16)

            @pl.when(nx_ref[i] >= 0)
            def _():
                for cp in fetch(nx_ref[i]):
                    cp.start()

        lo, hi = _unpack_bf16_pairs(xs_ref[...])
        half = D_MODEL // 2
        d = functools.partial(jnp.dot, preferred_element_type=F32)
        h1 = d(lo, w1b[0:half, :]) + d(hi, w1b[half:, :])
        h3 = d(lo, w3b[0:half, :]) + d(hi, w3b[half:, :])
        hid = (h1 * _sigmoid(h1)) * h3
        y = d(hid.astype(BF16), w2b[...])
        ys_ref[...] = _pack_bf16_pairs(y[:, :half], y[:, half:])

    @pl.when(i >= nv_ref[0])
    def _():
        ys_ref[...] = jnp.zeros_like(ys_ref)


def _experts(xs, tile_expert, next_expert, n_valid, w1, w3, w2, layer):
    n_rows, W = xs.shape
    nt = n_rows // ROW_TILE
    D, F = w1.shape[1], w1.shape[2]

    def row_map(i, te, nx, nv):
        return (i, 0)

    hbm = pl.BlockSpec(memory_space=pl.ANY)
    return pl.pallas_call(
        functools.partial(_experts_kernel, layer=layer),
        grid_spec=pltpu.PrefetchScalarGridSpec(
            num_scalar_prefetch=3,
            grid=(nt,),
            in_specs=[pl.BlockSpec((ROW_TILE, W), row_map), hbm, hbm, hbm],
            out_specs=pl.BlockSpec((ROW_TILE, D // 2), row_map),
            scratch_shapes=[
                pltpu.VMEM((D, F), F32),
                pltpu.VMEM((D, F), F32),
                pltpu.VMEM((F, D), F32),
                pltpu.VMEM((D, F), BF16),
                pltpu.VMEM((D, F), BF16),
                pltpu.VMEM((F, D), BF16),
                pltpu.SemaphoreType.DMA(()),
            ],
        ),
        out_shape=jax.ShapeDtypeStruct((n_rows, D // 2), jnp.uint32),
        compiler_params=pltpu.CompilerParams(
            dimension_semantics=("arbitrary",), vmem_limit_bytes=VMEM_LIMIT),
        name="moe_experts",
    )(tile_expert, next_expert, n_valid, xs, w1, w3, w2)


def _combine_kernel(x_ref, info_ref, fw_ref, y1_ref, y2_ref, out_ref, *, final):
    info = info_ref[...]
    g1 = info[:, _I_G1:_I_G1 + 1]
    g2 = info[:, _I_G2:_I_G2 + 1]
    y1 = jnp.concatenate(_unpack_pairs_f32(y1_ref[...]), axis=1)
    y2 = jnp.concatenate(_unpack_pairs_f32(y2_ref[...]), axis=1)
    y = x_ref[...] + (g1 * y1 + g2 * y2)
    if final:
        y = _rms(y, fw_ref[...])
    out_ref[...] = y


def _combine(x2, part, info, y1, y2, final_w, final):
    T, D = x2.shape
    tp = T // COMBINE_PARTS
    tm = min(TOK_TILE, tp)
    nb = tp // tm
    x_spec = pl.BlockSpec((tm, D), lambda i: (part * nb + i, 0))
    half_spec = pl.BlockSpec((tm, D // 2), lambda i: (i, 0))
    return pl.pallas_call(
        functools.partial(_combine_kernel, final=final),
        grid=(nb,),
        in_specs=[x_spec, pl.BlockSpec((tm, LANES), lambda i: (part * nb + i, 0)),
                  _const_spec((1, D)), half_spec, half_spec],
        out_specs=x_spec,
        out_shape=jax.ShapeDtypeStruct((T, D), F32),
        input_output_aliases={0: 0},
        compiler_params=pltpu.CompilerParams(
            dimension_semantics=("arbitrary",), vmem_limit_bytes=VMEM_LIMIT),
        name="moe_combine",
    )(x2, info, final_w[None, :], y1, y2)


def _hier_moe(x2, norm_w, w_group, b_group, w_expert, b_expert, w1, w3, w2, layer, final_w, final):
    T, D = x2.shape
    nt = (2 * T) // ROW_TILE + MOE_EXPERTS
    hnp, info, info_t, cnt = _router(x2, norm_w, w_group, b_group, w_expert, b_expert)
    pos, tile_expert, next_expert, n_valid = _plan(cnt[:, 0].astype(jnp.int32), info_t, nt)
    xs = _dispatch(hnp, pos[0], pos[1], nt * ROW_TILE)
    ys = _experts(xs, tile_expert, next_expert, n_valid, w1, w3, w2, layer)
    tp = T // COMBINE_PARTS
    gathered = [(_gather_rows(ys, pos[0, p * tp:(p + 1) * tp]), _gather_rows(ys, pos[1, p * tp:(p + 1) * tp]))
                for p in range(COMBINE_PARTS)]
    out = x2
    for part, (y1, y2) in enumerate(gathered):
        out = _combine(out, part, info, y1, y2, final_w, final)
    return out


def kernel(x, norm_mix, norm_ffn, norm_final, even_w_in, hgrn_lb_logits, hgrn_norm_w, gla_w_up,
           gla_b_up, gla_norm_w, even_w_out, odd_w_in, ret_norm_w, conv_w, conv_b, rglru_w_r,
           rglru_b_r, rglru_w_i, rglru_b_i, rglru_lambda, odd_w_out, moe_w_group, moe_b_group,
           moe_w_expert, moe_b_expert, moe_w1, moe_w3, moe_w2):
    B, L, D = x.shape
    depth = norm_mix.shape[0]
    lb_table = jnp.cumsum(jax.nn.softmax(hgrn_lb_logits.astype(F32), axis=0), axis=0)
    w1 = moe_w1.reshape((-1,) + moe_w1.shape[2:])
    w3 = moe_w3.reshape((-1,) + moe_w3.shape[2:])
    w2 = moe_w2.reshape((-1,) + moe_w2.shape[2:])
    for l in range(depth):
        j = l // 2
        if l % 2 == 0:
            x = _even_mixer(x, norm_mix[l], even_w_in[j], lb_table[l], gla_w_up[j], gla_b_up[j],
                            hgrn_norm_w[j], gla_norm_w[j], even_w_out[j])
        else:
            x = _odd_mixer(x, norm_mix[l], odd_w_in[j], ret_norm_w[j], conv_w[j], conv_b[j],
                           rglru_w_r[j], rglru_b_r[j], rglru_w_i[j], rglru_b_i[j], rglru_lambda[j],
                           odd_w_out[j])
        x = _hier_moe(x.reshape(B * L, D), norm_ffn[l], moe_w_group[l], moe_b_group[l],
                      moe_w_expert[l], moe_b_expert[l], w1, w3, w2, l,
                      norm_final, l == depth - 1).reshape(B, L, D)
    return x
```

```python
import functools

import numpy as np
import jax
import jax.numpy as jnp
from jax import lax
from jax.experimental import pallas as pl
from jax.experimental.pallas import tpu as pltpu
from jax.experimental.pallas import tpu_sc as plsc

F32 = jnp.float32
BF16 = jnp.bfloat16

EPS = 1e-6
CHUNK = 64
GROUP = 8
LOG2E = 1.4426950408889634
LANES = 128
D_MODEL = 1024
MIX_WIDTH = 1024
N_HEADS = 8
GLA_RANK = 16
GLA_TAU = 16.0
RET_HEADS = 4
RET_DK = 64
ROPE_BASE = 10000.0
LRU_WIDTH = 512
LRU_BLOCKS = 8
LRU_CONV = 4
LRU_C = 8.0
MOE_GROUPS = 4
MOE_PER_GROUP = 8
MOE_EXPERTS = MOE_GROUPS * MOE_PER_GROUP
MOE_DFF = 512
NEG_BIG = -1e30

MIX_TILE = 512
EVEN_SPLIT = 2
ODD_SPLIT = 2
TOK_TILE = 1024
ROW_TILE = 512
COMBINE_PARTS = 4
SC_CORES = 2
SC_WORKERS = 32
SC_ROWS = 128
VMEM_LIMIT = 56 * 1024 * 1024


def _dot(a, b):
    return jnp.dot(a.astype(BF16), b.astype(BF16), preferred_element_type=F32)


def _dot_nt(a, b):
    return lax.dot_general(a.astype(BF16), b.astype(BF16), (((1,), (1,)), ((), ())),
                           preferred_element_type=F32)


def _dot_tn(a, b):
    return lax.dot_general(a.astype(BF16), b.astype(BF16), (((0,), (0,)), ((), ())),
                           preferred_element_type=F32)


def _split2(x):
    hi = x.astype(BF16)
    lo = (x - hi.astype(F32)).astype(BF16)
    return hi, lo


def _split3(x):
    hi = x.astype(BF16)
    r = x - hi.astype(F32)
    mid = r.astype(BF16)
    lo = (r - mid.astype(F32)).astype(BF16)
    return hi, mid, lo


def _dot_x3(a, b):
    ah, al = _split2(a)
    bh, bl = _split2(b)
    d = functools.partial(jnp.dot, preferred_element_type=F32)
    return d(ah, bh) + (d(ah, bl) + d(al, bh))


def _sigmoid(x):
    return 1.0 / (1.0 + jnp.exp(-x))


def _softplus(x):
    return jnp.maximum(x, 0.0) + jnp.log(1.0 + jnp.exp(-jnp.abs(x)))


def _rms(x, w):
    return x * lax.rsqrt(jnp.mean(x * x, axis=-1, keepdims=True) + EPS) * w


def _tri(n):
    r = lax.broadcasted_iota(jnp.int32, (n, n), 0)
    c = lax.broadcasted_iota(jnp.int32, (n, n), 1)
    return jnp.where(r >= c, 1.0, 0.0).astype(BF16)


def _cumsum_rows(tri3, g):
    return jnp.dot(tri3, jnp.concatenate(_split3(g), axis=0), preferred_element_type=F32)


_E_AQ, _E_AF, _E_AI, _E_AG = 0, 512, 1024, 1536
_E_BQ, _E_BK, _E_BV, _E_BR, _E_LOW = 2048, 2304, 2560, 3072, 3584
EVEN_COLS = 3712
_QK_COLS = 768


_LEVEL_BLOCKS = (32, 16, 8)


def _level_tables():
    t = np.arange(CHUNK)
    out = []
    for blk in _LEVEL_BLOCKS:
        odd = (t // blk) % 2 == 1
        out.append((np.where(odd, (t // blk) * blk - 1, -1), np.where(~odd, (t // blk) * blk + blk - 1, -1)))
    return out


def _gla_chunk(rows, q_ref, k_ref, v_ref, b2, o_ref, st_ref, a_ref):
    n_units = _QK_COLS // LANES
    lane = lax.broadcasted_iota(jnp.int32, (1, LANES), 1)
    trow = lax.broadcasted_iota(jnp.int32, (GROUP, LANES), 0)
    unit_heads = [((u, None),) for u in range(4)] + [((4, 0), (5, 64)), ((6, 0), (7, 64))]
    qs = [q_ref[rows, u * LANES:(u + 1) * LANES] for u in range(n_units)]
    ks = [k_ref[rows, u * LANES:(u + 1) * LANES] for u in range(n_units)]
    bs = [b2[:, u * LANES:(u + 1) * LANES] for u in range(n_units)]
    vs = [v_ref[rows, h * LANES:(h + 1) * LANES] for h in range(N_HEADS)]

    for u in range(n_units):
        for j in range(CHUNK // GROUP):
            qj = qs[u][j * GROUP:(j + 1) * GROUP]
            bj = bs[u][j * GROUP:(j + 1) * GROUP]
            for s in range(0, GROUP, 2):
                pair = [qj * jnp.exp2(jnp.where(trow >= s + i, bj - bj[s + i:s + i + 1, :], NEG_BIG))
                        for i in (0, 1)]
                r0 = (j * GROUP + s) * GROUP
                a_ref[r0:r0 + 2 * GROUP, u * LANES:(u + 1) * LANES] = (
                    jnp.concatenate(pair, axis=0).astype(BF16))

    zk = jnp.zeros((CHUNK, LANES), F32)
    scs = []
    for pr in range(n_units // 2):
        u0, u1 = 2 * pr, 2 * pr + 1
        if u0 < 4:
            keys = jnp.concatenate([jnp.concatenate([ks[u0], zk], axis=1),
                                    jnp.concatenate([zk, ks[u1]], axis=1)], axis=0)
        else:
            lo0, hi0 = jnp.where(lane < 64, ks[u0], 0.0), jnp.where(lane >= 64, ks[u0], 0.0)
            lo1, hi1 = jnp.where(lane < 64, ks[u1], 0.0), jnp.where(lane >= 64, ks[u1], 0.0)
            keys = jnp.concatenate([jnp.concatenate([lo0, zk], axis=1), jnp.concatenate([hi0, zk], axis=1),
                                    jnp.concatenate([zk, lo1], axis=1), jnp.concatenate([zk, hi1], axis=1)],
                                   axis=0)
        scs.append(_dot_nt(a_ref[:, u0 * LANES:(u1 + 1) * LANES], keys))

    tables = _level_tables()
    tq = lax.broadcasted_iota(jnp.int32, (CHUNK, CHUNK), 0)
    ts = lax.broadcasted_iota(jnp.int32, (CHUNK, CHUNK), 1)
    covers = [((tq // blk) % 2 == 1) & ((tq // blk) == (ts // blk) + 1) & ((ts // blk) % 2 == 0)
              for blk in _LEVEL_BLOCKS]
    s_offs, o_inter = {}, {}
    for u in range(n_units):
        q, k, b = qs[u], ks[u], bs[u]
        heads = unit_heads[u]
        if len(heads) == 2:
            qm = jnp.concatenate([jnp.where((lane & 64) == lo, q, 0.0) for _, lo in heads], axis=0)
            bm = jnp.concatenate([b, b], axis=0)
        else:
            qm, bm = q, b
        s_off = None
        for (qref, kref), cover in zip(tables, covers):
            qrows, krows = [], []
            for g in range(CHUNK // GROUP):
                sl = slice(g * GROUP, (g + 1) * GROUP)
                rq, rk = int(qref[g * GROUP]), int(kref[g * GROUP])
                qrows.append(jnp.exp2(b[sl] - b[rq:rq + 1, :]) if rq >= 0 else jnp.zeros((GROUP, LANES), F32))
                krows.append(k[sl] * jnp.exp2(b[rk:rk + 1, :] - b[sl]) if rk >= 0
                             else jnp.zeros((GROUP, LANES), F32))
            qfac = jnp.concatenate(qrows, axis=0)
            if len(heads) == 2:
                qfac = jnp.concatenate([qfac, qfac], axis=0)
                cov = jnp.concatenate([cover, cover], axis=0)
            else:
                cov = cover
            part = _dot_nt(qm * qfac, jnp.concatenate(krows, axis=0))
            s_off = jnp.where(cov, part, 0.0 if s_off is None else s_off)
        eb = jnp.exp2(bm)
        b_last = b[CHUNK - 1:CHUNK, :]
        kd = (k * jnp.exp2(b_last - b)).astype(BF16)
        dec = jnp.exp2(b_last)
        qe = qm * eb
        for i, (head, lo) in enumerate(heads):
            sl = slice(i * CHUNK, (i + 1) * CHUNK)
            st = st_ref[head]
            o_inter[head] = _dot_nt(qe[sl], st)
            st_ref[head] = st * dec + _dot_tn(vs[head], kd)
            s_offs[head] = s_off[sl]

    for u in range(n_units):
        sc = scs[u // 2]
        paired = len(unit_heads[u]) == 2
        width = LANES if paired else CHUNK
        c0 = (u % 2) * width
        scol = (lane & 63) if paired else lane[:, :CHUNK]
        groups = []
        for j in range(CHUNK // GROUP):
            acc = jnp.zeros((GROUP, width), F32)
            for s in range(GROUP):
                r0 = (j * GROUP + s) * GROUP
                acc = jnp.where(scol == j * GROUP + s, sc[r0:r0 + GROUP, c0:c0 + width], acc)
            groups.append(acc)
        p = jnp.concatenate(groups, axis=0)
        if paired:
            vstack = jnp.concatenate([vs[h] for h, _ in unit_heads[u]], axis=0)
        for head, lo in unit_heads[u]:
            if lo is None:
                o = o_inter[head] + _dot(s_offs[head] + p, vs[head])
            else:
                o = (o_inter[head] + _dot(s_offs[head], vs[head])
                     + _dot(jnp.where((lane & 64) == lo, p, 0.0), vstack))
            o_ref[rows, head * LANES:(head + 1) * LANES] = o


def _even_kernel(x_ref, nw_ref, win_ref, lb_ref, wup_ref, bup_ref, hw_ref, wout_ref, out_ref,
                 q_ref, k_ref, g_ref, v_ref, gate_ref, o_ref, st_ref, a_ref):
    tl = x_ref.shape[1]
    parts = [(r0, tl // EVEN_SPLIT) for r0 in range(0, tl, tl // EVEN_SPLIT)]

    @pl.when(pl.program_id(1) == 0)
    def _():
        st_ref[...] = jnp.zeros_like(st_ref)

    def project(r0, n):
        rs = slice(r0, r0 + n)
        hn = _rms(x_ref[0, rs, :], nw_ref[...]).astype(BF16)

        def proj(c0, w):
            return jnp.dot(hn, win_ref[:, c0:c0 + w], preferred_element_type=F32)

        q_ref[rs, 0:512] = proj(_E_AQ, 512)
        lb = lb_ref[...]
        f = lb + (1.0 - lb) * _sigmoid(proj(_E_AF, 512))
        k_ref[rs, 0:512] = 1.0 - f
        g_ref[rs, 0:512] = jnp.log(f)
        v_ref[rs, 0:512] = proj(_E_AI, 512)
        ag = proj(_E_AG, 512)
        gate_ref[rs, 0:512] = ag * _sigmoid(ag)
        q_ref[rs, 512:768] = proj(_E_BQ, 256) * (64.0 ** -0.5)
        k_ref[rs, 512:768] = proj(_E_BK, 256)
        v_ref[rs, 512:1024] = proj(_E_BV, 512)
        br = proj(_E_BR, 512)
        gate_ref[rs, 512:1024] = br * _sigmoid(br)
        z = _dot_x3(proj(_E_LOW, LANES), wup_ref[...]) + bup_ref[...]
        g_ref[rs, 512:768] = -_softplus(-z) * (1.0 / GLA_TAU)

    tri = _tri(CHUNK)
    tri3 = jnp.concatenate([tri, tri, tri], axis=1)

    def finish(r0, n):
        rs = slice(r0, r0 + n)
        hw = hw_ref[...]
        ys = []
        for head in range(N_HEADS):
            sl = slice(head * LANES, (head + 1) * LANES)
            oh = o_ref[rs, sl]
            ys.append(oh * lax.rsqrt(jnp.mean(oh * oh, axis=-1, keepdims=True) + EPS)
                      * hw[:, sl] * gate_ref[rs, sl])
        y = jnp.concatenate(ys, axis=1).astype(BF16)
        out_ref[0, rs, :] = x_ref[0, rs, :] + jnp.dot(y, wout_ref[...], preferred_element_type=F32)

    for r0, n in parts:
        project(r0, n)
    for r0, n in parts:
        for c in range(r0 // CHUNK, (r0 + n) // CHUNK):
            rows = slice(c * CHUNK, (c + 1) * CHUNK)
            b2 = _cumsum_rows(tri3, g_ref[rows, :]) * LOG2E
            _gla_chunk(rows, q_ref, k_ref, v_ref, b2, o_ref, st_ref, a_ref.at[c % 2])
    for r0, n in parts:
        finish(r0, n)


def _const_spec(shape):
    nd = len(shape)
    return pl.BlockSpec(shape, lambda *_: (0,) * nd)


def _even_mixer(x, norm_w, w_in, lb, w_up, b_up, hgrn_norm_w, gla_norm_w, w_out):
    B, L, D = x.shape
    tl = min(MIX_TILE, L)
    s = np.cumsum([0, 512, 512, 512, 512, 256, 256, 512, GLA_RANK, 512])
    cols = [w_in[:, s[i]:s[i + 1]] for i in range(9)]
    low = jnp.pad(cols[7], ((0, 0), (0, LANES - GLA_RANK)))
    w_in_r = jnp.concatenate(cols[:7] + [cols[8], low], axis=1).astype(BF16)
    w_up_p = jnp.pad(w_up, ((0, LANES - GLA_RANK), (0, 0)))
    hw = jnp.concatenate([hgrn_norm_w, gla_norm_w])[None, :]
    return pl.pallas_call(
        _even_kernel,
        grid=(B, L // tl),
        in_specs=[
            pl.BlockSpec((1, tl, D), lambda b, l: (b, l, 0)),
            _const_spec((1, D)),
            _const_spec((D, EVEN_COLS)),
            _const_spec((1, 512)),
            _const_spec((LANES, 256)),
            _const_spec((1, 256)),
            _const_spec((1, MIX_WIDTH)),
            _const_spec((MIX_WIDTH, D)),
        ],
        out_specs=pl.BlockSpec((1, tl, D), lambda b, l: (b, l, 0)),
        out_shape=jax.ShapeDtypeStruct((B, L, D), F32),
        scratch_shapes=[
            pltpu.VMEM((tl, _QK_COLS), F32),
            pltpu.VMEM((tl, _QK_COLS), F32),
            pltpu.VMEM((tl, _QK_COLS), F32),
            pltpu.VMEM((tl, MIX_WIDTH), F32),
            pltpu.VMEM((tl, MIX_WIDTH), F32),
            pltpu.VMEM((tl, MIX_WIDTH), F32),
            pltpu.VMEM((N_HEADS, LANES, LANES), F32),
            pltpu.VMEM((2, CHUNK * GROUP, _QK_COLS), BF16),
        ],
        compiler_params=pltpu.CompilerParams(
            dimension_semantics=("arbitrary", "arbitrary"), vmem_limit_bytes=VMEM_LIMIT),
        name="even_mixer",
    )(x, norm_w[None, :], w_in_r, lb[None, :], w_up_p, b_up[None, :], hw, w_out.astype(BF16))


_O_Q, _O_QR, _O_K, _O_KR, _O_V, _O_G, _O_X, _O_GATE = 0, 256, 512, 768, 1024, 1536, 2048, 2560
ODD_COLS = 3072


def _odd_kernel(x_ref, nw_ref, win_ref, cos_ref, sin_ref, dmat_ref, qdec_ref, kdec_ref, sdec_ref,
                rw_ref, cw_ref, cb_ref, wri_ref, bri_ref, lam_ref, wout_ref, out_ref,
                q_ref, k_ref, v_ref, gate_ref, o_ref, xe_ref, a_ref, u_ref, st_ref, h_ref):
    tl = x_ref.shape[1]
    parts = [(r0, tl // ODD_SPLIT) for r0 in range(0, tl, tl // ODD_SPLIT)]

    @pl.when(pl.program_id(1) == 0)
    def _():
        st_ref[...] = jnp.zeros_like(st_ref)
        h_ref[...] = jnp.zeros_like(h_ref)
        xe_ref[0:8, :] = jnp.zeros((8, LRU_WIDTH), F32)

    def project(r0, n):
        rs = slice(r0, r0 + n)
        hn = _rms(x_ref[0, rs, :], nw_ref[...]).astype(BF16)

        def proj(c0, w):
            return jnp.dot(hn, win_ref[:, c0:c0 + w], preferred_element_type=F32)

        cos = jnp.concatenate([cos_ref[rs, :]] * 2, axis=1)
        sin = jnp.concatenate([sin_ref[rs, :]] * 2, axis=1)
        q_ref[rs, :] = proj(_O_Q, 256) * cos + proj(_O_QR, 256) * sin
        k_ref[rs, :] = (proj(_O_K, 256) * cos + proj(_O_KR, 256) * sin) * (RET_DK ** -0.5)
        v_ref[rs, :] = proj(_O_V, 512)
        cg = proj(_O_G, 512)
        gate_ref[rs, 0:512] = cg * _sigmoid(cg)
        dg = proj(_O_GATE, LRU_WIDTH)
        gate_ref[rs, 512:1024] = 0.5 * dg * (
            1.0 + jnp.tanh(np.sqrt(2.0 / np.pi) * (dg + 0.044715 * (dg * dg * dg))))

        xe_ref[8 + r0:8 + r0 + n, :] = proj(_O_X, LRU_WIDTH)
        cw = cw_ref[...]
        xc = cb_ref[...]
        for j in range(LRU_CONV):
            xc = xc + xe_ref[5 + j + r0:5 + j + r0 + n, :] * cw[j:j + 1, :]
        ri = jnp.dot(xc.astype(BF16), wri_ref[...], preferred_element_type=F32) + bri_ref[...]
        r = _sigmoid(ri[:, :LRU_WIDTH])
        i = _sigmoid(ri[:, LRU_WIDTH:])
        a = jnp.exp(-LRU_C * r * _softplus(-lam_ref[...]))
        a_ref[rs, :] = a
        u_ref[rs, :] = jnp.sqrt(1.0 - a * a) * (i * xc)

    lane = lax.broadcasted_iota(jnp.int32, (1, LANES), 1)
    crow = lax.broadcasted_iota(jnp.int32, (CHUNK, LRU_WIDTH), 0)

    def recur(r0, n):
        rs = slice(r0, r0 + n)
        for head in range(RET_HEADS):
            unit, half = head // 2, head % 2
            ul = slice(unit * LANES, (unit + 1) * LANES)
            vl = slice(head * LANES, (head + 1) * LANES)
            own = (lane < 64) if half == 0 else (lane >= 64)
            q = jnp.where(own, q_ref[rs, ul], 0.0)
            k = k_ref[rs, ul]
            v = v_ref[rs, vl]
            st = st_ref[head]
            o = _dot_nt(q, st) * qdec_ref[head]
            o = o + _dot(_dot_nt(q, k) * dmat_ref[head], v)
            st_ref[head] = st * sdec_ref[head] + _dot_tn(v, k * kdec_ref[head])
            o_ref[rs, vl] = o
        for c in range(r0 // CHUNK, (r0 + n) // CHUNK):
            rows = slice(c * CHUNK, (c + 1) * CHUNK)
            ca = a_ref[rows, :]
            ch = u_ref[rows, :]
            d = 1
            while d < CHUNK:
                keep = crow >= d
                sa = jnp.where(keep, pltpu.roll(ca, d, axis=0), 1.0)
                sh = jnp.where(keep, pltpu.roll(ch, d, axis=0), 0.0)
                ch = ca * sh + ch
                ca = ca * sa
                d *= 2
            ch = ch + ca * h_ref[...]
            h_ref[...] = ch[CHUNK - 1:CHUNK, :]
            o_ref[rows, 512:1024] = ch

    def finish(r0, n):
        rs = slice(r0, r0 + n)
        rw = rw_ref[...]
        ys = []
        for head in range(RET_HEADS):
            sl = slice(head * LANES, (head + 1) * LANES)
            oh = o_ref[rs, sl]
            oh = oh - jnp.mean(oh, axis=-1, keepdims=True)
            ys.append(oh * lax.rsqrt(jnp.mean(oh * oh, axis=-1, keepdims=True) + EPS) * rw[:, sl])
        ys.append(o_ref[rs, 512:1024])
        y = (jnp.concatenate(ys, axis=1) * gate_ref[rs, :]).astype(BF16)
        out_ref[0, rs, :] = x_ref[0, rs, :] + jnp.dot(y, wout_ref[...], preferred_element_type=F32)

    for r0, n in parts:
        project(r0, n)
    xe_ref[0:8, :] = xe_ref[tl:tl + 8, :]
    for r0, n in parts:
        recur(r0, n)
    for r0, n in parts:
        finish(r0, n)


def _rot_cols(w):
    d = w.shape[0]
    w4 = w.reshape(d, RET_HEADS, 2, RET_DK // 2)
    return jnp.stack([-w4[:, :, 1], w4[:, :, 0]], axis=2).reshape(d, RET_HEADS * RET_DK)


def _odd_mixer(x, norm_w, w_in, ret_norm_w, conv_w, conv_b, w_r, b_r, w_i, b_i, lam, w_out):
    B, L, D = x.shape
    tl = min(MIX_TILE, L)
    s = np.cumsum([0, 256, 256, 512, 512, 512, 512])
    wq, wk, wv, wg, wx, wgate = [w_in[:, s[i]:s[i + 1]] for i in range(6)]
    w_in_r = jnp.concatenate([wq, _rot_cols(wq), wk, _rot_cols(wk), wv, wg, wx, wgate],
                             axis=1).astype(BF16)
    half = RET_DK // 2
    inv = ROPE_BASE ** (-jnp.arange(half, dtype=F32) / half)
    ang = jnp.arange(L, dtype=F32)[:, None] * inv[None, :]
    cos = jnp.tile(jnp.cos(ang), (1, LANES // half))
    sin = jnp.tile(jnp.sin(ang), (1, LANES // half))
    lg = jnp.log1p(-jnp.exp2(-5.0 - jnp.arange(RET_HEADS, dtype=F32)))[:, None, None]
    rc = tl // ODD_SPLIT
    t = jnp.arange(rc, dtype=F32)
    rel = t[:, None] - t[None, :]
    dmat = jnp.where(rel >= 0, jnp.exp(lg * jnp.maximum(rel, 0.0)), 0.0)
    qdec = jnp.broadcast_to(jnp.exp(lg * (t[None, :, None] + 1.0)), (RET_HEADS, rc, LANES))
    kdec = jnp.broadcast_to(jnp.exp(lg * (rc - 1.0 - t[None, :, None])), (RET_HEADS, rc, LANES))
    sdec = jnp.broadcast_to(jnp.exp(lg * float(rc)), (RET_HEADS, 1, LANES))
    eye = jnp.eye(LRU_BLOCKS, dtype=F32)
    bd = LRU_WIDTH // LRU_BLOCKS

    def blockdiag(w):
        return (eye[:, None, :, None] * w[:, :, None, :]).reshape(LRU_WIDTH, LRU_WIDTH)

    wri = jnp.concatenate([blockdiag(w_r), blockdiag(w_i)], axis=1).astype(BF16)
    bri = jnp.concatenate([b_r, b_i])[None, :]
    return pl.pallas_call(
        _odd_kernel,
        grid=(B, L // tl),
        in_specs=[
            pl.BlockSpec((1, tl, D), lambda b, l: (b, l, 0)),
            _const_spec((1, D)),
            _const_spec((D, ODD_COLS)),
            pl.BlockSpec((tl, LANES), lambda b, l: (l, 0)),
            pl.BlockSpec((tl, LANES), lambda b, l: (l, 0)),
            _const_spec((RET_HEADS, rc, rc)),
            _const_spec((RET_HEADS, rc, LANES)),
            _const_spec((RET_HEADS, rc, LANES)),
            _const_spec((RET_HEADS, 1, LANES)),
            _const_spec((1, 512)),
            _const_spec((LRU_CONV, LRU_WIDTH)),
            _const_spec((1, LRU_WIDTH)),
            _const_spec((LRU_WIDTH, 2 * LRU_WIDTH)),
            _const_spec((1, 2 * LRU_WIDTH)),
            _const_spec((1, LRU_WIDTH)),
            _const_spec((MIX_WIDTH, D)),
        ],
        out_specs=pl.BlockSpec((1, tl, D), lambda b, l: (b, l, 0)),
        out_shape=jax.ShapeDtypeStruct((B, L, D), F32),
        scratch_shapes=[
            pltpu.VMEM((tl, 256), F32),
            pltpu.VMEM((tl, 256), F32),
            pltpu.VMEM((tl, 512), F32),
            pltpu.VMEM((tl, MIX_WIDTH), F32),
            pltpu.VMEM((tl, MIX_WIDTH), F32),
            pltpu.VMEM((tl + 8, LRU_WIDTH), F32),
            pltpu.VMEM((tl, LRU_WIDTH), F32),
            pltpu.VMEM((tl, LRU_WIDTH), F32),
            pltpu.VMEM((RET_HEADS, LANES, LANES), F32),
            pltpu.VMEM((1, LRU_WIDTH), F32),
        ],
        compiler_params=pltpu.CompilerParams(
            dimension_semantics=("arbitrary", "arbitrary"), vmem_limit_bytes=VMEM_LIMIT),
        name="odd_mixer",
    )(x, norm_w[None, :], w_in_r, cos, sin, dmat, qdec, kdec, sdec, ret_norm_w[None, :],
      conv_w, conv_b[None, :], wri, bri, lam[None, :], w_out.astype(BF16))


_I_E1, _I_E2, _I_G1, _I_G2, _I_R1, _I_R2 = 0, 1, 2, 3, 4, 5
_LOGIT_E0 = MOE_GROUPS
ROUTE_ROWS = 40


def _pack_bf16_pairs(lo, hi):
    lo_b = pltpu.bitcast(lo.astype(BF16).astype(F32), jnp.uint32)
    hi_b = pltpu.bitcast(hi.astype(BF16).astype(F32), jnp.uint32)
    return (lo_b >> 16) | (hi_b & jnp.uint32(0xFFFF0000))


def _unpack_pairs_f32(w):
    return pltpu.bitcast(w << 16, F32), pltpu.bitcast(w & jnp.uint32(0xFFFF0000), F32)


def _unpack_bf16_pairs(w):
    lo, hi = _unpack_pairs_f32(w)
    return lo.astype(BF16), hi.astype(BF16)


def _router_kernel(x_ref, nw_ref, wrt_ref, brt_ref, hnp_ref, info_ref, info_t_ref, cnt_ref):
    tm = x_ref.shape[0]

    @pl.when(pl.program_id(0) == 0)
    def _():
        cnt_ref[...] = jnp.zeros_like(cnt_ref)

    hn = _rms(x_ref[...], nw_ref[...])
    half = D_MODEL // 2
    hn_hi = hn.astype(BF16)
    hn_hif = hn_hi.astype(F32)
    hn_lo = (hn - hn_hif).astype(BF16)
    bits = pltpu.bitcast(hn_hif, jnp.uint32)
    hnp_ref[...] = (bits[:, :half] >> 16) | (bits[:, half:] & jnp.uint32(0xFFFF0000))

    w_hi, w_lo = _split2(wrt_ref[...])
    both = _dot_nt(jnp.concatenate([w_hi, w_lo], axis=0), hn_hi)
    bias = jnp.concatenate([brt_ref[...]] * (tm // LANES), axis=1)
    lt = (both[:LANES] + (both[LANES:] + _dot_nt(w_hi, hn_lo)) + bias)[0:ROUTE_ROWS]
    row = lax.broadcasted_iota(jnp.int32, (ROUTE_ROWS, tm), 0).astype(F32)

    def first_max(vals):
        m = jnp.max(vals, axis=0, keepdims=True)
        idx = jnp.min(jnp.where(vals == m, row, float(LANES)), axis=0, keepdims=True)
        return m, idx

    gl = jnp.where(row < MOE_GROUPS, lt, NEG_BIG)
    gmax, gidx = first_max(gl)
    g_w = 1.0 / jnp.sum(jnp.exp(gl - gmax), axis=0, keepdims=True)
    e0 = _LOGIT_E0 + MOE_PER_GROUP * gidx
    el = jnp.where((row >= e0) & (row < e0 + MOE_PER_GROUP), lt, NEG_BIG)
    m1, i1 = first_max(el)
    m2, i2 = first_max(jnp.where(row == i1, NEG_BIG, el))
    p2 = jnp.exp(m2 - m1)
    w1 = 1.0 / (1.0 + p2)
    e1 = i1 - _LOGIT_E0
    e2 = i2 - _LOGIT_E0

    erow = lax.broadcasted_iota(jnp.int32, (MOE_EXPERTS, tm), 0).astype(F32)
    oh1 = jnp.where(erow == e1, 1.0, 0.0)
    oh2 = jnp.where(erow == e2, 1.0, 0.0)
    r_i = lax.broadcasted_iota(jnp.int32, (tm, tm), 0)
    c_i = lax.broadcasted_iota(jnp.int32, (tm, tm), 1)
    triu = jnp.where(r_i <= c_i, 1.0, 0.0).astype(BF16)
    pre = jnp.dot(jnp.concatenate([oh1, oh2], axis=0).astype(BF16), triu, preferred_element_type=F32)
    pre1, pre2 = pre[:MOE_EXPERTS], pre[MOE_EXPERTS:]
    carry = cnt_ref[0:MOE_EXPERTS, :]
    tot = jnp.dot(jnp.concatenate([oh1, oh2], axis=0).astype(BF16), jnp.ones((tm, LANES), BF16),
                  preferred_element_type=F32)
    c1, c2 = tot[:MOE_EXPERTS], tot[MOE_EXPERTS:]
    rep = tm // LANES
    r1 = jnp.sum(oh1 * (pre1 + jnp.concatenate([carry] * rep, axis=1)), axis=0, keepdims=True) - 1.0
    r2 = jnp.sum(oh2 * (pre2 + jnp.concatenate([carry + c1] * rep, axis=1)), axis=0, keepdims=True) - 1.0
    cnt_ref[0:MOE_EXPERTS, :] = carry + (c1 + c2)

    row8 = lax.broadcasted_iota(jnp.int32, (GROUP, tm), 0).astype(F32)
    info_t = jnp.zeros((GROUP, tm), F32)
    for k, val in ((_I_E1, e1), (_I_E2, e2), (_I_G1, g_w * w1), (_I_G2, g_w * (w1 * p2)),
                   (_I_R1, r1), (_I_R2, r2)):
        info_t = jnp.where(row8 == float(k), val, info_t)
    info_t_ref[...] = info_t
    info_ref[...] = jnp.concatenate([info_t, jnp.zeros((LANES - GROUP, tm), F32)], axis=0).T


def _router(x2, norm_w, w_group, b_group, w_expert, b_expert):
    T, D = x2.shape
    tm = min(TOK_TILE, T)
    pad = LANES - MOE_GROUPS - MOE_EXPERTS
    wrt = jnp.pad(jnp.concatenate([w_group, w_expert], axis=1), ((0, 0), (0, pad))).T
    brt = jnp.broadcast_to(jnp.pad(jnp.concatenate([b_group, b_expert]), (0, pad))[:, None], (LANES, LANES))
    return pl.pallas_call(
        _router_kernel,
        grid=(T // tm,),
        in_specs=[
            pl.BlockSpec((tm, D), lambda i: (i, 0)),
            _const_spec((1, D)),
            _const_spec((LANES, D)),
            _const_spec((LANES, LANES)),
        ],
        out_specs=[
            pl.BlockSpec((tm, D // 2), lambda i: (i, 0)),
            pl.BlockSpec((tm, LANES), lambda i: (i, 0)),
            pl.BlockSpec((GROUP, tm), lambda i: (0, i)),
            _const_spec((LANES, LANES)),
        ],
        out_shape=[
            jax.ShapeDtypeStruct((T, D // 2), jnp.uint32),
            jax.ShapeDtypeStruct((T, LANES), F32),
            jax.ShapeDtypeStruct((GROUP, T), F32),
            jax.ShapeDtypeStruct((LANES, LANES), F32),
        ],
        compiler_params=pltpu.CompilerParams(
            dimension_semantics=("arbitrary",), vmem_limit_bytes=VMEM_LIMIT),
        name="moe_router",
    )(x2, norm_w[None, :], wrt, brt)


def _plan_kernel(cnt_ref, info_ref, pos_ref, te_ref, nx_ref, nv_ref):
    nt = te_ref.shape[0]
    e1 = info_ref[_I_E1:_I_E1 + 1, :]
    e2 = info_ref[_I_E2:_I_E2 + 1, :]
    s1 = jnp.zeros_like(e1)
    s2 = jnp.zeros_like(e2)
    tiles = [lax.div(cnt_ref[e] + (ROW_TILE - 1), jnp.int32(ROW_TILE)) for e in range(MOE_EXPERTS)]
    following = [None] * MOE_EXPERTS
    nxt = jnp.int32(-1)
    for e in reversed(range(MOE_EXPERTS)):
        following[e] = nxt
        nxt = jnp.where(tiles[e] > 0, jnp.int32(e), nxt)
    start = jnp.int32(0)
    for e in range(MOE_EXPERTS):
        seg = (start * ROW_TILE).astype(F32)
        s1 = jnp.where(e1 == float(e), seg, s1)
        s2 = jnp.where(e2 == float(e), seg, s2)

        def fill(i, carry, e=e):
            te_ref[i] = jnp.int32(e)
            nx_ref[i] = following[e]
            return carry

        lax.fori_loop(start, start + tiles[e], fill, 0)
        start = start + tiles[e]
    nv_ref[0] = start

    def fill_tail(i, carry):
        te_ref[i] = jnp.int32(MOE_EXPERTS - 1)
        nx_ref[i] = jnp.int32(-1)
        return carry

    lax.fori_loop(start, nt, fill_tail, 0)
    pos_ref[0:1, :] = (s1 + info_ref[_I_R1:_I_R1 + 1, :]).astype(jnp.int32)
    pos_ref[1:2, :] = (s2 + info_ref[_I_R2:_I_R2 + 1, :]).astype(jnp.int32)


def _plan(cnt, info_t, nt):
    T = info_t.shape[1]
    smem = pl.BlockSpec(memory_space=pltpu.SMEM)
    return pl.pallas_call(
        _plan_kernel,
        in_specs=[smem, pl.BlockSpec(memory_space=pltpu.VMEM)],
        out_specs=[pl.BlockSpec(memory_space=pltpu.VMEM), smem, smem, smem],
        out_shape=[
            jax.ShapeDtypeStruct((2, T), jnp.int32),
            jax.ShapeDtypeStruct((nt,), jnp.int32),
            jax.ShapeDtypeStruct((nt,), jnp.int32),
            jax.ShapeDtypeStruct((1,), jnp.int32),
        ],
        name="moe_plan",
    )(cnt, info_t)


def _sc_worker():
    return lax.axis_index("s") * SC_CORES + lax.axis_index("c")


def _dispatch(hnp, pos1, pos2, n_rows):
    T, W = hnp.shape
    per_w = T // SC_WORKERS
    n_chunks = per_w // SC_ROWS
    mesh = plsc.VectorSubcoreMesh(core_axis_name="c", subcore_axis_name="s")

    @functools.partial(
        pl.kernel, mesh=mesh,
        out_type=jax.ShapeDtypeStruct((n_rows, W), jnp.uint32),
        scratch_types=[
            pltpu.VMEM((SC_ROWS,), jnp.int32),
            pltpu.VMEM((SC_ROWS,), jnp.int32),
            pltpu.VMEM((SC_ROWS, W), jnp.uint32),
        ],
        name="moe_dispatch_sc",
    )
    def scatter(hnp_hbm, p1_hbm, p2_hbm, xs_hbm, i1_v, i2_v, rows_v):
        base0 = _sc_worker() * per_w

        @pl.loop(0, n_chunks)
        def _(j):
            base = pl.multiple_of(base0 + j * SC_ROWS, SC_ROWS)
            pltpu.sync_copy(p1_hbm.at[pl.ds(base, SC_ROWS)], i1_v)
            pltpu.sync_copy(p2_hbm.at[pl.ds(base, SC_ROWS)], i2_v)
            pltpu.sync_copy(hnp_hbm.at[pl.ds(base, SC_ROWS)], rows_v)
            pltpu.sync_copy(rows_v, xs_hbm.at[i1_v])
            pltpu.sync_copy(rows_v, xs_hbm.at[i2_v])

    return scatter(hnp, pos1, pos2)


def _gather_rows(table, idx):
    T = idx.shape[0]
    W = table.shape[1]
    per_w = T // SC_WORKERS
    rows = SC_ROWS
    n_chunks = per_w // rows
    mesh = plsc.VectorSubcoreMesh(core_axis_name="c", subcore_axis_name="s")

    @functools.partial(
        pl.kernel, mesh=mesh,
        out_type=jax.ShapeDtypeStruct((T, W), table.dtype),
        scratch_types=[
            pltpu.VMEM((rows,), jnp.int32),
            pltpu.VMEM((rows, W), table.dtype),
        ],
        name="moe_gather_sc",
    )
    def gather(table_hbm, idx_hbm, out_hbm, i_v, rows_v):
        base0 = _sc_worker() * per_w

        @pl.loop(0, n_chunks)
        def _(j):
            base = pl.multiple_of(base0 + j * rows, rows)
            pltpu.sync_copy(idx_hbm.at[pl.ds(base, rows)], i_v)
            pltpu.sync_copy(table_hbm.at[i_v], rows_v)
            pltpu.sync_copy(rows_v, out_hbm.at[pl.ds(base, rows)])

    return gather(table, idx)


def _experts_kernel(te_ref, nx_ref, nv_ref, xs_ref, w1_hbm, w3_hbm, w2_hbm, ys_ref,
                    w1f, w3f, w2f, w1b, w3b, w2b, sem, *, layer):
    i = pl.program_id(0)

    def fetch(e):
        row = e + layer * MOE_EXPERTS
        return [pltpu.make_async_copy(w1_hbm.at[row], w1f, sem),
                pltpu.make_async_copy(w3_hbm.at[row], w3f, sem),
                pltpu.make_async_copy(w2_hbm.at[row], w2f, sem)]

    @pl.when(i < nv_ref[0])
    def _():
        @pl.when(i == 0)
        def _():
            for cp in fetch(te_ref[0]):
                cp.start()

        @pl.when((i == 0) | (te_ref[i] != te_ref[jnp.maximum(i - 1, 0)]))
        def _():
            for cp in fetch(te_ref[i]):
                cp.wait()
            w1b[...] = w1f[...].astype(BF16)
            w3b[...] = w3f[...].astype(BF16)
            w2b[...] = w2f[...].astype(BF16)

            @pl.when(nx_ref[i] >= 0)
            def _():
                for cp in fetch(nx_ref[i]):
                    cp.start()

        lo, hi = _unpack_bf16_pairs(xs_ref[...])
        half = D_MODEL // 2
        d = functools.partial(jnp.dot, preferred_element_type=F32)
        h1 = d(lo, w1b[0:half, :]) + d(hi, w1b[half:, :])
        h3 = d(lo, w3b[0:half, :]) + d(hi, w3b[half:, :])
        hid = (h1 * _sigmoid(h1)) * h3
        y = d(hid.astype(BF16), w2b[...])
        ys_ref[...] = _pack_bf16_pairs(y[:, :half], y[:, half:])

    @pl.when(i >= nv_ref[0])
    def _():
        ys_ref[...] = jnp.zeros_like(ys_ref)


def _experts(xs, tile_expert, next_expert, n_valid, w1, w3, w2, layer):
    n_rows, W = xs.shape
    nt = n_rows // ROW_TILE
    D, F = w1.shape[1], w1.shape[2]

    def row_map(i, te, nx, nv):
        return (i, 0)

    hbm = pl.BlockSpec(memory_space=pl.ANY)
    return pl.pallas_call(
        functools.partial(_experts_kernel, layer=layer),
        grid_spec=pltpu.PrefetchScalarGridSpec(
            num_scalar_prefetch=3,
            grid=(nt,),
            in_specs=[pl.BlockSpec((ROW_TILE, W), row_map), hbm, hbm, hbm],
            out_specs=pl.BlockSpec((ROW_TILE, D // 2), row_map),
            scratch_shapes=[
                pltpu.VMEM((D, F), F32),
                pltpu.VMEM((D, F), F32),
                pltpu.VMEM((F, D), F32),
                pltpu.VMEM((D, F), BF16),
                pltpu.VMEM((D, F), BF16),
                pltpu.VMEM((F, D), BF16),
                pltpu.SemaphoreType.DMA(()),
            ],
        ),
        out_shape=jax.ShapeDtypeStruct((n_rows, D // 2), jnp.uint32),
        compiler_params=pltpu.CompilerParams(
            dimension_semantics=("arbitrary",), vmem_limit_bytes=VMEM_LIMIT),
        name="moe_experts",
    )(tile_expert, next_expert, n_valid, xs, w1, w3, w2)


def _combine_kernel(x_ref, info_ref, fw_ref, y1_ref, y2_ref, out_ref, *, final):
    info = info_ref[...]
    g1 = info[:, _I_G1:_I_G1 + 1]
    g2 = info[:, _I_G2:_I_G2 + 1]
    y1 = jnp.concatenate(_unpack_pairs_f32(y1_ref[...]), axis=1)
    y2 = jnp.concatenate(_unpack_pairs_f32(y2_ref[...]), axis=1)
    y = x_ref[...] + (g1 * y1 + g2 * y2)
    if final:
        y = _rms(y, fw_ref[...])
    out_ref[...] = y


def _combine(x2, part, info, y1, y2, final_w, final):
    T, D = x2.shape
    tp = T // COMBINE_PARTS
    tm = min(TOK_TILE, tp)
    nb = tp // tm
    x_spec = pl.BlockSpec((tm, D), lambda i: (part * nb + i, 0))
    half_spec = pl.BlockSpec((tm, D // 2), lambda i: (i, 0))
    return pl.pallas_call(
        functools.partial(_combine_kernel, final=final),
        grid=(nb,),
        in_specs=[x_spec, pl.BlockSpec((tm, LANES), lambda i: (part * nb + i, 0)),
                  _const_spec((1, D)), half_spec, half_spec],
        out_specs=x_spec,
        out_shape=jax.ShapeDtypeStruct((T, D), F32),
        input_output_aliases={0: 0},
        compiler_params=pltpu.CompilerParams(
            dimension_semantics=("arbitrary",), vmem_limit_bytes=VMEM_LIMIT),
        name="moe_combine",
    )(x2, info, final_w[None, :], y1, y2)


def _hier_moe(x2, norm_w, w_group, b_group, w_expert, b_expert, w1, w3, w2, layer, final_w, final):
    T, D = x2.shape
    nt = (2 * T) // ROW_TILE + MOE_EXPERTS
    hnp, info, info_t, cnt = _router(x2, norm_w, w_group, b_group, w_expert, b_expert)
    pos, tile_expert, next_expert, n_valid = _plan(cnt[:, 0].astype(jnp.int32), info_t, nt)
    xs = _dispatch(hnp, pos[0], pos[1], nt * ROW_TILE)
    ys = _experts(xs, tile_expert, next_expert, n_valid, w1, w3, w2, layer)
    tp = T // COMBINE_PARTS
    gathered = [(_gather_rows(ys, pos[0, p * tp:(p + 1) * tp]), _gather_rows(ys, pos[1, p * tp:(p + 1) * tp]))
                for p in range(COMBINE_PARTS)]
    out = x2
    for part, (y1, y2) in enumerate(gathered):
        out = _combine(out, part, info, y1, y2, final_w, final)
    return out


def kernel(x, norm_mix, norm_ffn, norm_final, even_w_in, hgrn_lb_logits, hgrn_norm_w, gla_w_up,
           gla_b_up, gla_norm_w, even_w_out, odd_w_in, ret_norm_w, conv_w, conv_b, rglru_w_r,
           rglru_b_r, rglru_w_i, rglru_b_i, rglru_lambda, odd_w_out, moe_w_group, moe_b_group,
           moe_w_expert, moe_b_expert, moe_w1, moe_w3, moe_w2):
    B, L, D = x.shape
    depth = norm_mix.shape[0]
    lb_table = jnp.cumsum(jax.nn.softmax(hgrn_lb_logits.astype(F32), axis=0), axis=0)
    w1 = moe_w1.reshape((-1,) + moe_w1.shape[2:])
    w3 = moe_w3.reshape((-1,) + moe_w3.shape[2:])
    w2 = moe_w2.reshape((-1,) + moe_w2.shape[2:])
    for l in range(depth):
        j = l // 2
        if l % 2 == 0:
            x = _even_mixer(x, norm_mix[l], even_w_in[j], lb_table[l], gla_w_up[j], gla_b_up[j],
                            hgrn_norm_w[j], gla_norm_w[j], even_w_out[j])
        else:
            x = _odd_mixer(x, norm_mix[l], odd_w_in[j], ret_norm_w[j], conv_w[j], conv_b[j],
                           rglru_w_r[j], rglru_b_r[j], rglru_w_i[j], rglru_b_i[j], rglru_lambda[j],
                           odd_w_out[j])
        x = _hier_moe(x.reshape(B * L, D), norm_ffn[l], moe_w_group[l], moe_b_group[l],
                      moe_w_expert[l], moe_b_expert[l], w1, w3, w2, l,
                      norm_final, l == depth - 1).reshape(B, L, D)
    return x
```

```python
import functools

import numpy as np
import jax
import jax.numpy as jnp
from jax import lax
from jax.experimental import pallas as pl
from jax.experimental.pallas import tpu as pltpu
from jax.experimental.pallas import tpu_sc as plsc

F32 = jnp.float32
BF16 = jnp.bfloat16

EPS = 1e-6
CHUNK = 64
GROUP = 8
LOG2E = 1.4426950408889634
LANES = 128
D_MODEL = 1024
MIX_WIDTH = 1024
N_HEADS = 8
GLA_RANK = 16
GLA_TAU = 16.0
RET_HEADS = 4
RET_DK = 64
ROPE_BASE = 10000.0
LRU_WIDTH = 512
LRU_BLOCKS = 8
LRU_CONV = 4
LRU_C = 8.0
MOE_GROUPS = 4
MOE_PER_GROUP = 8
MOE_EXPERTS = MOE_GROUPS * MOE_PER_GROUP
NEG_BIG = -1e30

MIX_TILE = 512
EVEN_SPLIT = 2
ODD_SPLIT = 2
TOK_TILE = 1024
ROW_TILE = 512
COMBINE_PARTS = 4
SC_CORES = 2
SC_WORKERS = 32
SC_ROWS = 128
VMEM_LIMIT = 56 * 1024 * 1024


def _dot(a, b):
    return jnp.dot(a.astype(BF16), b.astype(BF16), preferred_element_type=F32)


def _dot_nt(a, b):
    return lax.dot_general(a.astype(BF16), b.astype(BF16), (((1,), (1,)), ((), ())),
                           preferred_element_type=F32)


def _dot_tn(a, b):
    return lax.dot_general(a.astype(BF16), b.astype(BF16), (((0,), (0,)), ((), ())),
                           preferred_element_type=F32)


def _split2(x):
    hi = x.astype(BF16)
    lo = (x - hi.astype(F32)).astype(BF16)
    return hi, lo


def _split3(x):
    hi = x.astype(BF16)
    r = x - hi.astype(F32)
    mid = r.astype(BF16)
    lo = (r - mid.astype(F32)).astype(BF16)
    return hi, mid, lo


def _dot_x3(a, b):
    ah, al = _split2(a)
    bh, bl = _split2(b)
    d = functools.partial(jnp.dot, preferred_element_type=F32)
    return d(ah, bh) + (d(ah, bl) + d(al, bh))


def _sigmoid(x):
    return 1.0 / (1.0 + jnp.exp(-x))


def _softplus(x):
    return jnp.maximum(x, 0.0) + jnp.log(1.0 + jnp.exp(-jnp.abs(x)))


def _rms(x, w):
    return x * lax.rsqrt(jnp.mean(x * x, axis=-1, keepdims=True) + EPS) * w


def _tri(n):
    r = lax.broadcasted_iota(jnp.int32, (n, n), 0)
    c = lax.broadcasted_iota(jnp.int32, (n, n), 1)
    return jnp.where(r >= c, 1.0, 0.0).astype(BF16)


def _cumsum_rows(tri3, g):
    return jnp.dot(tri3, jnp.concatenate(_split3(g), axis=0), preferred_element_type=F32)


_E_AQ, _E_AF, _E_AI, _E_AG = 0, 512, 1024, 1536
_E_BQ, _E_BK, _E_BV, _E_BR, _E_LOW = 2048, 2304, 2560, 3072, 3584
EVEN_COLS = 3712
_QK_COLS = 768


_LEVEL_BLOCKS = (32, 16, 8)


def _level_tables():
    t = np.arange(CHUNK)
    out = []
    for blk in _LEVEL_BLOCKS:
        odd = (t // blk) % 2 == 1
        out.append((np.where(odd, (t // blk) * blk - 1, -1), np.where(~odd, (t // blk) * blk + blk - 1, -1)))
    return out


def _gla_chunk(rows, q_ref, k_ref, v_ref, b2, o_ref, st_ref, a_ref):
    n_units = _QK_COLS // LANES
    lane = lax.broadcasted_iota(jnp.int32, (1, LANES), 1)
    trow = lax.broadcasted_iota(jnp.int32, (GROUP, LANES), 0)
    unit_heads = [((u, None),) for u in range(4)] + [((4, 0), (5, 64)), ((6, 0), (7, 64))]
    qs = [q_ref[rows, u * LANES:(u + 1) * LANES] for u in range(n_units)]
    ks = [k_ref[rows, u * LANES:(u + 1) * LANES] for u in range(n_units)]
    bs = [b2[:, u * LANES:(u + 1) * LANES] for u in range(n_units)]
    vs = [v_ref[rows, h * LANES:(h + 1) * LANES] for h in range(N_HEADS)]

    for u in range(n_units):
        for j in range(CHUNK // GROUP):
            qj = qs[u][j * GROUP:(j + 1) * GROUP]
            bj = bs[u][j * GROUP:(j + 1) * GROUP]
            for s in range(0, GROUP, 2):
                pair = [qj * jnp.exp2(jnp.where(trow >= s + i, bj - bj[s + i:s + i + 1, :], NEG_BIG))
                        for i in (0, 1)]
                r0 = (j * GROUP + s) * GROUP
                a_ref[r0:r0 + 2 * GROUP, u * LANES:(u + 1) * LANES] = (
                    jnp.concatenate(pair, axis=0).astype(BF16))

    zk = jnp.zeros((CHUNK, LANES), F32)
    scs = []
    for pr in range(n_units // 2):
        u0, u1 = 2 * pr, 2 * pr + 1
        if u0 < 4:
            keys = jnp.concatenate([jnp.concatenate([ks[u0], zk], axis=1),
                                    jnp.concatenate([zk, ks[u1]], axis=1)], axis=0)
        else:
            lo0, hi0 = jnp.where(lane < 64, ks[u0], 0.0), jnp.where(lane >= 64, ks[u0], 0.0)
            lo1, hi1 = jnp.where(lane < 64, ks[u1], 0.0), jnp.where(lane >= 64, ks[u1], 0.0)
            keys = jnp.concatenate([jnp.concatenate([lo0, zk], axis=1), jnp.concatenate([hi0, zk], axis=1),
                                    jnp.concatenate([zk, lo1], axis=1), jnp.concatenate([zk, hi1], axis=1)],
                                   axis=0)
        scs.append(_dot_nt(a_ref[:, u0 * LANES:(u1 + 1) * LANES], keys))

    tables = _level_tables()
    tq = lax.broadcasted_iota(jnp.int32, (CHUNK, CHUNK), 0)
    ts = lax.broadcasted_iota(jnp.int32, (CHUNK, CHUNK), 1)
    covers = [((tq // blk) % 2 == 1) & ((tq // blk) == (ts // blk) + 1) & ((ts // blk) % 2 == 0)
              for blk in _LEVEL_BLOCKS]
    s_offs, o_inter, qes, kds, decs = {}, {}, {}, {}, {}
    for u in range(n_units):
        q, k, b = qs[u], ks[u], bs[u]
        heads = unit_heads[u]
        if len(heads) == 2:
            qm = jnp.concatenate([jnp.where((lane & 64) == lo, q, 0.0) for _, lo in heads], axis=0)
            bm = jnp.concatenate([b, b], axis=0)
        else:
            qm, bm = q, b
        s_off = None
        for (qref, kref), cover in zip(tables, covers):
            qrows, krows = [], []
            for g in range(CHUNK // GROUP):
                sl = slice(g * GROUP, (g + 1) * GROUP)
                rq, rk = int(qref[g * GROUP]), int(kref[g * GROUP])
                qrows.append(jnp.exp2(b[sl] - b[rq:rq + 1, :]) if rq >= 0 else jnp.zeros((GROUP, LANES), F32))
                krows.append(k[sl] * jnp.exp2(b[rk:rk + 1, :] - b[sl]) if rk >= 0
                             else jnp.zeros((GROUP, LANES), F32))
            qfac = jnp.concatenate(qrows, axis=0)
            if len(heads) == 2:
                qfac = jnp.concatenate([qfac, qfac], axis=0)
                cov = jnp.concatenate([cover, cover], axis=0)
            else:
                cov = cover
            part = _dot_nt(qm * qfac, jnp.concatenate(krows, axis=0))
            s_off = jnp.where(cov, part, 0.0 if s_off is None else s_off)
        eb = jnp.exp2(bm)
        b_last = b[CHUNK - 1:CHUNK, :]
        kd = (k * jnp.exp2(b_last - b)).astype(BF16)
        dec = jnp.exp2(b_last)
        qe = (qm * eb).astype(BF16)
        for i, (head, lo) in enumerate(heads):
            sl = slice(i * CHUNK, (i + 1) * CHUNK)
            qes[head], kds[head], decs[head] = qe[sl], kd, dec
            s_offs[head] = s_off[sl]

    zs = jnp.zeros((LANES, LANES), BF16)
    zk = jnp.zeros((CHUNK, LANES), BF16)
    for h0 in range(0, N_HEADS, 2):
        h1 = h0 + 1
        st0, st1 = st_ref[h0], st_ref[h1]
        states = jnp.concatenate([jnp.concatenate([st0.astype(BF16), zs], axis=1),
                                  jnp.concatenate([zs, st1.astype(BF16)], axis=1)], axis=0)
        oi = _dot_nt(jnp.concatenate([qes[h0], qes[h1]], axis=1), states)
        o_inter[h0], o_inter[h1] = oi[:, :LANES], oi[:, LANES:]
        keys2 = jnp.concatenate([jnp.concatenate([kds[h0], zk], axis=1),
                                 jnp.concatenate([zk, kds[h1]], axis=1)], axis=0)
        upd = _dot_tn(jnp.concatenate([vs[h0], vs[h1]], axis=0), keys2)
        st_ref[h0] = st0 * decs[h0] + upd[:, :LANES]
        st_ref[h1] = st1 * decs[h1] + upd[:, LANES:]

    for u in range(n_units):
        sc = scs[u // 2]
        paired = len(unit_heads[u]) == 2
        width = LANES if paired else CHUNK
        c0 = (u % 2) * width
        scol = (lane & 63) if paired else lane[:, :CHUNK]
        groups = []
        for j in range(CHUNK // GROUP):
            acc = jnp.zeros((GROUP, width), F32)
            for s in range(GROUP):
                r0 = (j * GROUP + s) * GROUP
                acc = jnp.where(scol == j * GROUP + s, sc[r0:r0 + GROUP, c0:c0 + width], acc)
            groups.append(acc)
        p = jnp.concatenate(groups, axis=0)
        if paired:
            vstack = jnp.concatenate([vs[h] for h, _ in unit_heads[u]], axis=0)
        for head, lo in unit_heads[u]:
            if lo is None:
                o = o_inter[head] + _dot(s_offs[head] + p, vs[head])
            else:
                o = (o_inter[head] + _dot(s_offs[head], vs[head])
                     + _dot(jnp.where((lane & 64) == lo, p, 0.0), vstack))
            o_ref[rows, head * LANES:(head + 1) * LANES] = o


def _even_kernel(x_ref, nw_ref, win_ref, lb_ref, wup_ref, bup_ref, hw_ref, wout_ref, out_ref,
                 q_ref, k_ref, g_ref, v_ref, gate_ref, o_ref, st_ref, a_ref):
    tl = x_ref.shape[1]
    parts = [(r0, tl // EVEN_SPLIT) for r0 in range(0, tl, tl // EVEN_SPLIT)]

    @pl.when(pl.program_id(1) == 0)
    def _():
        st_ref[...] = jnp.zeros_like(st_ref)

    def project(r0, n):
        rs = slice(r0, r0 + n)
        hn = _rms(x_ref[0, rs, :], nw_ref[...]).astype(BF16)

        def proj(c0, w):
            return jnp.dot(hn, win_ref[:, c0:c0 + w], preferred_element_type=F32)

        q_ref[rs, 0:512] = proj(_E_AQ, 512)
        lb = lb_ref[...]
        f = lb + (1.0 - lb) * _sigmoid(proj(_E_AF, 512))
        k_ref[rs, 0:512] = 1.0 - f
        g_ref[rs, 0:512] = jnp.log(f)
        v_ref[rs, 0:512] = proj(_E_AI, 512)
        ag = proj(_E_AG, 512)
        gate_ref[rs, 0:512] = ag * _sigmoid(ag)
        q_ref[rs, 512:768] = proj(_E_BQ, 256) * (64.0 ** -0.5)
        k_ref[rs, 512:768] = proj(_E_BK, 256)
        v_ref[rs, 512:1024] = proj(_E_BV, 512)
        br = proj(_E_BR, 512)
        gate_ref[rs, 512:1024] = br * _sigmoid(br)
        z = _dot_x3(proj(_E_LOW, LANES), wup_ref[...]) + bup_ref[...]
        g_ref[rs, 512:768] = -_softplus(-z) * (1.0 / GLA_TAU)

    tri = _tri(CHUNK)
    tri3 = jnp.concatenate([tri, tri, tri], axis=1)

    def finish(r0, n):
        rs = slice(r0, r0 + n)
        hw = hw_ref[...]
        ys = []
        for head in range(N_HEADS):
            sl = slice(head * LANES, (head + 1) * LANES)
            oh = o_ref[rs, sl]
            ys.append(oh * lax.rsqrt(jnp.mean(oh * oh, axis=-1, keepdims=True) + EPS)
                      * hw[:, sl] * gate_ref[rs, sl])
        y = jnp.concatenate(ys, axis=1).astype(BF16)
        out_ref[0, rs, :] = x_ref[0, rs, :] + jnp.dot(y, wout_ref[...], preferred_element_type=F32)

    for r0, n in parts:
        project(r0, n)
    for r0, n in parts:
        for c in range(r0 // CHUNK, (r0 + n) // CHUNK):
            rows = slice(c * CHUNK, (c + 1) * CHUNK)
            b2 = _cumsum_rows(tri3, g_ref[rows, :]) * LOG2E
            _gla_chunk(rows, q_ref, k_ref, v_ref, b2, o_ref, st_ref, a_ref.at[c % 2])
    for r0, n in parts:
        finish(r0, n)


def _const_spec(shape):
    nd = len(shape)
    return pl.BlockSpec(shape, lambda *_: (0,) * nd)


def _even_mixer(x, norm_w, w_in, lb, w_up, b_up, hgrn_norm_w, gla_norm_w, w_out):
    B, L, D = x.shape
    tl = min(MIX_TILE, L)
    s = np.cumsum([0, 512, 512, 512, 512, 256, 256, 512, GLA_RANK, 512])
    cols = [w_in[:, s[i]:s[i + 1]] for i in range(9)]
    low = jnp.pad(cols[7], ((0, 0), (0, LANES - GLA_RANK)))
    w_in_r = jnp.concatenate(cols[:7] + [cols[8], low], axis=1).astype(BF16)
    w_up_p = jnp.pad(w_up, ((0, LANES - GLA_RANK), (0, 0)))
    hw = jnp.concatenate([hgrn_norm_w, gla_norm_w])[None, :]
    return pl.pallas_call(
        _even_kernel,
        grid=(B, L // tl),
        in_specs=[
            pl.BlockSpec((1, tl, D), lambda b, l: (b, l, 0)),
            _const_spec((1, D)),
            _const_spec((D, EVEN_COLS)),
            _const_spec((1, 512)),
            _const_spec((LANES, 256)),
            _const_spec((1, 256)),
            _const_spec((1, MIX_WIDTH)),
            _const_spec((MIX_WIDTH, D)),
        ],
        out_specs=pl.BlockSpec((1, tl, D), lambda b, l: (b, l, 0)),
        out_shape=jax.ShapeDtypeStruct((B, L, D), F32),
        scratch_shapes=[
            pltpu.VMEM((tl, _QK_COLS), F32),
            pltpu.VMEM((tl, _QK_COLS), F32),
            pltpu.VMEM((tl, _QK_COLS), F32),
            pltpu.VMEM((tl, MIX_WIDTH), F32),
            pltpu.VMEM((tl, MIX_WIDTH), F32),
            pltpu.VMEM((tl, MIX_WIDTH), F32),
            pltpu.VMEM((N_HEADS, LANES, LANES), F32),
            pltpu.VMEM((2, CHUNK * GROUP, _QK_COLS), BF16),
        ],
        compiler_params=pltpu.CompilerParams(
            dimension_semantics=("arbitrary", "arbitrary"), vmem_limit_bytes=VMEM_LIMIT),
        name="even_mixer",
    )(x, norm_w[None, :], w_in_r, lb[None, :], w_up_p, b_up[None, :], hw, w_out.astype(BF16))


_O_Q, _O_QR, _O_K, _O_KR, _O_V, _O_G, _O_X, _O_GATE = 0, 256, 512, 768, 1024, 1536, 2048, 2560
ODD_COLS = 3072


def _odd_kernel(x_ref, nw_ref, win_ref, cos_ref, sin_ref, dmat_ref, qdec_ref, kdec_ref, sdec_ref,
                rw_ref, cw_ref, cb_ref, wri_ref, bri_ref, lam_ref, wout_ref, out_ref,
                q_ref, k_ref, v_ref, gate_ref, o_ref, xe_ref, a_ref, u_ref, st_ref, h_ref):
    tl = x_ref.shape[1]
    parts = [(r0, tl // ODD_SPLIT) for r0 in range(0, tl, tl // ODD_SPLIT)]

    @pl.when(pl.program_id(1) == 0)
    def _():
        st_ref[...] = jnp.zeros_like(st_ref)
        h_ref[...] = jnp.zeros_like(h_ref)
        xe_ref[0:8, :] = jnp.zeros((8, LRU_WIDTH), F32)

    def project(r0, n):
        rs = slice(r0, r0 + n)
        hn = _rms(x_ref[0, rs, :], nw_ref[...]).astype(BF16)

        def proj(c0, w):
            return jnp.dot(hn, win_ref[:, c0:c0 + w], preferred_element_type=F32)

        cos = jnp.concatenate([cos_ref[rs, :]] * 2, axis=1)
        sin = jnp.concatenate([sin_ref[rs, :]] * 2, axis=1)
        q_ref[rs, :] = proj(_O_Q, 256) * cos + proj(_O_QR, 256) * sin
        k_ref[rs, :] = (proj(_O_K, 256) * cos + proj(_O_KR, 256) * sin) * (RET_DK ** -0.5)
        v_ref[rs, :] = proj(_O_V, 512)
        cg = proj(_O_G, 512)
        gate_ref[rs, 0:512] = cg * _sigmoid(cg)
        dg = proj(_O_GATE, LRU_WIDTH)
        gate_ref[rs, 512:1024] = 0.5 * dg * (
            1.0 + jnp.tanh(np.sqrt(2.0 / np.pi) * (dg + 0.044715 * (dg * dg * dg))))

        xe_ref[8 + r0:8 + r0 + n, :] = proj(_O_X, LRU_WIDTH)
        cw = cw_ref[...]
        xc = cb_ref[...]
        for j in range(LRU_CONV):
            xc = xc + xe_ref[5 + j + r0:5 + j + r0 + n, :] * cw[j:j + 1, :]
        ri = jnp.dot(xc.astype(BF16), wri_ref[...], preferred_element_type=F32) + bri_ref[...]
        r = _sigmoid(ri[:, :LRU_WIDTH])
        i = _sigmoid(ri[:, LRU_WIDTH:])
        a = jnp.exp(-LRU_C * r * _softplus(-lam_ref[...]))
        a_ref[rs, :] = a
        u_ref[rs, :] = jnp.sqrt(1.0 - a * a) * (i * xc)

    lane = lax.broadcasted_iota(jnp.int32, (1, LANES), 1)
    crow = lax.broadcasted_iota(jnp.int32, (CHUNK, LRU_WIDTH), 0)

    def recur(r0, n):
        rs = slice(r0, r0 + n)
        for head in range(RET_HEADS):
            unit, half = head // 2, head % 2
            ul = slice(unit * LANES, (unit + 1) * LANES)
            vl = slice(head * LANES, (head + 1) * LANES)
            own = (lane < 64) if half == 0 else (lane >= 64)
            q = jnp.where(own, q_ref[rs, ul], 0.0)
            k = k_ref[rs, ul]
            v = v_ref[rs, vl]
            st = st_ref[head]
            o = _dot_nt(q, st) * qdec_ref[head]
            o = o + _dot(_dot_nt(q, k) * dmat_ref[head], v)
            st_ref[head] = st * sdec_ref[head] + _dot_tn(v, k * kdec_ref[head])
            o_ref[rs, vl] = o
        for c in range(r0 // CHUNK, (r0 + n) // CHUNK):
            rows = slice(c * CHUNK, (c + 1) * CHUNK)
            ca = a_ref[rows, :]
            ch = u_ref[rows, :]
            d = 1
            while d < CHUNK:
                keep = crow >= d
                sa = jnp.where(keep, pltpu.roll(ca, d, axis=0), 1.0)
                sh = jnp.where(keep, pltpu.roll(ch, d, axis=0), 0.0)
                ch = ca * sh + ch
                ca = ca * sa
                d *= 2
            ch = ch + ca * h_ref[...]
            h_ref[...] = ch[CHUNK - 1:CHUNK, :]
            o_ref[rows, 512:1024] = ch

    def finish(r0, n):
        rs = slice(r0, r0 + n)
        rw = rw_ref[...]
        ys = []
        for head in range(RET_HEADS):
            sl = slice(head * LANES, (head + 1) * LANES)
            oh = o_ref[rs, sl]
            oh = oh - jnp.mean(oh, axis=-1, keepdims=True)
            ys.append(oh * lax.rsqrt(jnp.mean(oh * oh, axis=-1, keepdims=True) + EPS) * rw[:, sl])
        ys.append(o_ref[rs, 512:1024])
        y = (jnp.concatenate(ys, axis=1) * gate_ref[rs, :]).astype(BF16)
        out_ref[0, rs, :] = x_ref[0, rs, :] + jnp.dot(y, wout_ref[...], preferred_element_type=F32)

    for r0, n in parts:
        project(r0, n)
    xe_ref[0:8, :] = xe_ref[tl:tl + 8, :]
    for r0, n in parts:
        recur(r0, n)
    for r0, n in parts:
        finish(r0, n)


def _rot_cols(w):
    d = w.shape[0]
    w4 = w.reshape(d, RET_HEADS, 2, RET_DK // 2)
    return jnp.stack([-w4[:, :, 1], w4[:, :, 0]], axis=2).reshape(d, RET_HEADS * RET_DK)


def _odd_mixer(x, norm_w, w_in, ret_norm_w, conv_w, conv_b, w_r, b_r, w_i, b_i, lam, w_out):
    B, L, D = x.shape
    tl = min(MIX_TILE, L)
    s = np.cumsum([0, 256, 256, 512, 512, 512, 512])
    wq, wk, wv, wg, wx, wgate = [w_in[:, s[i]:s[i + 1]] for i in range(6)]
    w_in_r = jnp.concatenate([wq, _rot_cols(wq), wk, _rot_cols(wk), wv, wg, wx, wgate],
                             axis=1).astype(BF16)
    half = RET_DK // 2
    inv = ROPE_BASE ** (-jnp.arange(half, dtype=F32) / half)
    ang = jnp.arange(L, dtype=F32)[:, None] * inv[None, :]
    cos = jnp.tile(jnp.cos(ang), (1, LANES // half))
    sin = jnp.tile(jnp.sin(ang), (1, LANES // half))
    lg = jnp.log1p(-jnp.exp2(-5.0 - jnp.arange(RET_HEADS, dtype=F32)))[:, None, None]
    rc = tl // ODD_SPLIT
    t = jnp.arange(rc, dtype=F32)
    rel = t[:, None] - t[None, :]
    dmat = jnp.where(rel >= 0, jnp.exp(lg * jnp.maximum(rel, 0.0)), 0.0)
    qdec = jnp.broadcast_to(jnp.exp(lg * (t[None, :, None] + 1.0)), (RET_HEADS, rc, LANES))
    kdec = jnp.broadcast_to(jnp.exp(lg * (rc - 1.0 - t[None, :, None])), (RET_HEADS, rc, LANES))
    sdec = jnp.broadcast_to(jnp.exp(lg * float(rc)), (RET_HEADS, 1, LANES))
    eye = jnp.eye(LRU_BLOCKS, dtype=F32)
    bd = LRU_WIDTH // LRU_BLOCKS

    def blockdiag(w):
        return (eye[:, None, :, None] * w[:, :, None, :]).reshape(LRU_WIDTH, LRU_WIDTH)

    wri = jnp.concatenate([blockdiag(w_r), blockdiag(w_i)], axis=1).astype(BF16)
    bri = jnp.concatenate([b_r, b_i])[None, :]
    return pl.pallas_call(
        _odd_kernel,
        grid=(B, L // tl),
        in_specs=[
            pl.BlockSpec((1, tl, D), lambda b, l: (b, l, 0)),
            _const_spec((1, D)),
            _const_spec((D, ODD_COLS)),
            pl.BlockSpec((tl, LANES), lambda b, l: (l, 0)),
            pl.BlockSpec((tl, LANES), lambda b, l: (l, 0)),
            _const_spec((RET_HEADS, rc, rc)),
            _const_spec((RET_HEADS, rc, LANES)),
            _const_spec((RET_HEADS, rc, LANES)),
            _const_spec((RET_HEADS, 1, LANES)),
            _const_spec((1, 512)),
            _const_spec((LRU_CONV, LRU_WIDTH)),
            _const_spec((1, LRU_WIDTH)),
            _const_spec((LRU_WIDTH, 2 * LRU_WIDTH)),
            _const_spec((1, 2 * LRU_WIDTH)),
            _const_spec((1, LRU_WIDTH)),
            _const_spec((MIX_WIDTH, D)),
        ],
        out_specs=pl.BlockSpec((1, tl, D), lambda b, l: (b, l, 0)),
        out_shape=jax.ShapeDtypeStruct((B, L, D), F32),
        scratch_shapes=[
            pltpu.VMEM((tl, 256), F32),
            pltpu.VMEM((tl, 256), F32),
            pltpu.VMEM((tl, 512), F32),
            pltpu.VMEM((tl, MIX_WIDTH), F32),
            pltpu.VMEM((tl, MIX_WIDTH), F32),
            pltpu.VMEM((tl + 8, LRU_WIDTH), F32),
            pltpu.VMEM((tl, LRU_WIDTH), F32),
            pltpu.VMEM((tl, LRU_WIDTH), F32),
            pltpu.VMEM((RET_HEADS, LANES, LANES), F32),
            pltpu.VMEM((1, LRU_WIDTH), F32),
        ],
        compiler_params=pltpu.CompilerParams(
            dimension_semantics=("arbitrary", "arbitrary"), vmem_limit_bytes=VMEM_LIMIT),
        name="odd_mixer",
    )(x, norm_w[None, :], w_in_r, cos, sin, dmat, qdec, kdec, sdec, ret_norm_w[None, :],
      conv_w, conv_b[None, :], wri, bri, lam[None, :], w_out.astype(BF16))


_I_E1, _I_E2, _I_G1, _I_G2, _I_R1, _I_R2 = 0, 1, 2, 3, 4, 5
_LOGIT_E0 = MOE_GROUPS
ROUTE_ROWS = 40


def _pack_bf16_pairs(lo, hi):
    lo_b = pltpu.bitcast(lo.astype(BF16).astype(F32), jnp.uint32)
    hi_b = pltpu.bitcast(hi.astype(BF16).astype(F32), jnp.uint32)
    return (lo_b >> 16) | (hi_b & jnp.uint32(0xFFFF0000))


def _unpack_pairs_f32(w):
    return pltpu.bitcast(w << 16, F32), pltpu.bitcast(w & jnp.uint32(0xFFFF0000), F32)


def _unpack_bf16_pairs(w):
    lo, hi = _unpack_pairs_f32(w)
    return lo.astype(BF16), hi.astype(BF16)


def _router_kernel(x_ref, nw_ref, wrt_ref, brt_ref, hnp_ref, info_ref, info_t_ref, cnt_ref):
    tm = x_ref.shape[0]

    @pl.when(pl.program_id(0) == 0)
    def _():
        cnt_ref[...] = jnp.zeros_like(cnt_ref)

    hn = _rms(x_ref[...], nw_ref[...])
    half = D_MODEL // 2
    hn_hi = hn.astype(BF16)
    hn_hif = hn_hi.astype(F32)
    hn_lo = (hn - hn_hif).astype(BF16)
    bits = pltpu.bitcast(hn_hif, jnp.uint32)
    hnp_ref[...] = (bits[:, :half] >> 16) | (bits[:, half:] & jnp.uint32(0xFFFF0000))

    w_hi, w_lo = _split2(wrt_ref[...])
    both = _dot_nt(jnp.concatenate([w_hi, w_lo], axis=0), hn_hi)
    bias = jnp.concatenate([brt_ref[...]] * (tm // LANES), axis=1)
    lt = (both[:LANES] + (both[LANES:] + _dot_nt(w_hi, hn_lo)) + bias)[0:ROUTE_ROWS]
    row = lax.broadcasted_iota(jnp.int32, (ROUTE_ROWS, tm), 0).astype(F32)

    def first_max(vals):
        m = jnp.max(vals, axis=0, keepdims=True)
        idx = jnp.min(jnp.where(vals == m, row, float(LANES)), axis=0, keepdims=True)
        return m, idx

    gl = jnp.where(row < MOE_GROUPS, lt, NEG_BIG)
    gmax, gidx = first_max(gl)
    g_w = 1.0 / jnp.sum(jnp.exp(gl - gmax), axis=0, keepdims=True)
    e0 = _LOGIT_E0 + MOE_PER_GROUP * gidx
    el = jnp.where((row >= e0) & (row < e0 + MOE_PER_GROUP), lt, NEG_BIG)
    m1, i1 = first_max(el)
    m2, i2 = first_max(jnp.where(row == i1, NEG_BIG, el))
    p2 = jnp.exp(m2 - m1)
    w1 = 1.0 / (1.0 + p2)
    e1 = i1 - _LOGIT_E0
    e2 = i2 - _LOGIT_E0

    erow = lax.broadcasted_iota(jnp.int32, (MOE_EXPERTS, tm), 0).astype(F32)
    oh1 = jnp.where(erow == e1, 1.0, 0.0)
    oh2 = jnp.where(erow == e2, 1.0, 0.0)
    r_i = lax.broadcasted_iota(jnp.int32, (tm, tm), 0)
    c_i = lax.broadcasted_iota(jnp.int32, (tm, tm), 1)
    triu = jnp.where(r_i <= c_i, 1.0, 0.0).astype(BF16)
    pre = jnp.dot(jnp.concatenate([oh1, oh2], axis=0).astype(BF16), triu, preferred_element_type=F32)
    pre1, pre2 = pre[:MOE_EXPERTS], pre[MOE_EXPERTS:]
    carry = cnt_ref[0:MOE_EXPERTS, :]
    tot = jnp.dot(jnp.concatenate([oh1, oh2], axis=0).astype(BF16), jnp.ones((tm, LANES), BF16),
                  preferred_element_type=F32)
    c1, c2 = tot[:MOE_EXPERTS], tot[MOE_EXPERTS:]
    rep = tm // LANES
    r1 = jnp.sum(oh1 * (pre1 + jnp.concatenate([carry] * rep, axis=1)), axis=0, keepdims=True) - 1.0
    r2 = jnp.sum(oh2 * (pre2 + jnp.concatenate([carry + c1] * rep, axis=1)), axis=0, keepdims=True) - 1.0
    cnt_ref[0:MOE_EXPERTS, :] = carry + (c1 + c2)

    row8 = lax.broadcasted_iota(jnp.int32, (GROUP, tm), 0).astype(F32)
    info_t = jnp.zeros((GROUP, tm), F32)
    for k, val in ((_I_E1, e1), (_I_E2, e2), (_I_G1, g_w * w1), (_I_G2, g_w * (w1 * p2)),
                   (_I_R1, r1), (_I_R2, r2)):
        info_t = jnp.where(row8 == float(k), val, info_t)
    info_t_ref[...] = info_t
    info_ref[...] = jnp.concatenate([info_t, jnp.zeros((LANES - GROUP, tm), F32)], axis=0).T


def _router(x2, norm_w, w_group, b_group, w_expert, b_expert):
    T, D = x2.shape
    tm = min(TOK_TILE, T)
    pad = LANES - MOE_GROUPS - MOE_EXPERTS
    wrt = jnp.pad(jnp.concatenate([w_group, w_expert], axis=1), ((0, 0), (0, pad))).T
    brt = jnp.broadcast_to(jnp.pad(jnp.concatenate([b_group, b_expert]), (0, pad))[:, None], (LANES, LANES))
    return pl.pallas_call(
        _router_kernel,
        grid=(T // tm,),
        in_specs=[
            pl.BlockSpec((tm, D), lambda i: (i, 0)),
            _const_spec((1, D)),
            _const_spec((LANES, D)),
            _const_spec((LANES, LANES)),
        ],
        out_specs=[
            pl.BlockSpec((tm, D // 2), lambda i: (i, 0)),
            pl.BlockSpec((tm, LANES), lambda i: (i, 0)),
            pl.BlockSpec((GROUP, tm), lambda i: (0, i)),
            _const_spec((LANES, LANES)),
        ],
        out_shape=[
            jax.ShapeDtypeStruct((T, D // 2), jnp.uint32),
            jax.ShapeDtypeStruct((T, LANES), F32),
            jax.ShapeDtypeStruct((GROUP, T), F32),
            jax.ShapeDtypeStruct((LANES, LANES), F32),
        ],
        compiler_params=pltpu.CompilerParams(
            dimension_semantics=("arbitrary",), vmem_limit_bytes=VMEM_LIMIT),
        name="moe_router",
    )(x2, norm_w[None, :], wrt, brt)


def _plan_kernel(cnt_ref, info_ref, pos_ref, te_ref, nx_ref, hf_ref, nv_ref):
    nt = te_ref.shape[0]
    e1 = info_ref[_I_E1:_I_E1 + 1, :]
    e2 = info_ref[_I_E2:_I_E2 + 1, :]
    s1 = jnp.zeros_like(e1)
    s2 = jnp.zeros_like(e2)
    tiles = [lax.div(cnt_ref[e] + (ROW_TILE - 1), jnp.int32(ROW_TILE)) for e in range(MOE_EXPERTS)]
    following = [None] * MOE_EXPERTS
    nxt = jnp.int32(-1)
    for e in reversed(range(MOE_EXPERTS)):
        following[e] = nxt
        nxt = jnp.where(tiles[e] > 0, jnp.int32(e), nxt)
    start = jnp.int32(0)
    for e in range(MOE_EXPERTS):
        seg = (start * ROW_TILE).astype(F32)
        s1 = jnp.where(e1 == float(e), seg, s1)
        s2 = jnp.where(e2 == float(e), seg, s2)

        end = start + tiles[e]
        last_rows = cnt_ref[e] - (tiles[e] - 1) * ROW_TILE

        def fill(i, carry, e=e, end=end, last_rows=last_rows):
            te_ref[i] = jnp.int32(e)
            nx_ref[i] = following[e]
            hf_ref[i] = jnp.where((i == end - 1) & (last_rows <= ROW_TILE // 2), 1, 0).astype(jnp.int32)
            return carry

        lax.fori_loop(start, end, fill, 0)
        start = end
    nv_ref[0] = start

    def fill_tail(i, carry):
        te_ref[i] = jnp.int32(MOE_EXPERTS - 1)
        nx_ref[i] = jnp.int32(-1)
        hf_ref[i] = jnp.int32(0)
        return carry

    lax.fori_loop(start, nt, fill_tail, 0)
    pos_ref[0:1, :] = (s1 + info_ref[_I_R1:_I_R1 + 1, :]).astype(jnp.int32)
    pos_ref[1:2, :] = (s2 + info_ref[_I_R2:_I_R2 + 1, :]).astype(jnp.int32)


def _plan(cnt, info_t, nt):
    T = info_t.shape[1]
    smem = pl.BlockSpec(memory_space=pltpu.SMEM)
    return pl.pallas_call(
        _plan_kernel,
        in_specs=[smem, pl.BlockSpec(memory_space=pltpu.VMEM)],
        out_specs=[pl.BlockSpec(memory_space=pltpu.VMEM), smem, smem, smem, smem],
        out_shape=[
            jax.ShapeDtypeStruct((2, T), jnp.int32),
            jax.ShapeDtypeStruct((nt,), jnp.int32),
            jax.ShapeDtypeStruct((nt,), jnp.int32),
            jax.ShapeDtypeStruct((nt,), jnp.int32),
            jax.ShapeDtypeStruct((1,), jnp.int32),
        ],
        name="moe_plan",
    )(cnt, info_t)


def _sc_worker():
    return lax.axis_index("s") * SC_CORES + lax.axis_index("c")


def _dispatch(hnp, pos1, pos2, n_rows):
    T, W = hnp.shape
    per_w = T // SC_WORKERS
    n_chunks = per_w // SC_ROWS
    mesh = plsc.VectorSubcoreMesh(core_axis_name="c", subcore_axis_name="s")

    @functools.partial(
        pl.kernel, mesh=mesh,
        out_type=jax.ShapeDtypeStruct((n_rows, W), jnp.uint32),
        scratch_types=[
            pltpu.VMEM((SC_ROWS,), jnp.int32),
            pltpu.VMEM((SC_ROWS,), jnp.int32),
            pltpu.VMEM((SC_ROWS, W), jnp.uint32),
        ],
        name="moe_dispatch_sc",
    )
    def scatter(hnp_hbm, p1_hbm, p2_hbm, xs_hbm, i1_v, i2_v, rows_v):
        base0 = _sc_worker() * per_w

        @pl.loop(0, n_chunks)
        def _(j):
            base = pl.multiple_of(base0 + j * SC_ROWS, SC_ROWS)
            pltpu.sync_copy(p1_hbm.at[pl.ds(base, SC_ROWS)], i1_v)
            pltpu.sync_copy(p2_hbm.at[pl.ds(base, SC_ROWS)], i2_v)
            pltpu.sync_copy(hnp_hbm.at[pl.ds(base, SC_ROWS)], rows_v)
            pltpu.sync_copy(rows_v, xs_hbm.at[i1_v])
            pltpu.sync_copy(rows_v, xs_hbm.at[i2_v])

    return scatter(hnp, pos1, pos2)


def _gather_rows(table, idx):
    T = idx.shape[0]
    W = table.shape[1]
    per_w = T // SC_WORKERS
    rows = SC_ROWS
    n_chunks = per_w // rows
    mesh = plsc.VectorSubcoreMesh(core_axis_name="c", subcore_axis_name="s")

    @functools.partial(
        pl.kernel, mesh=mesh,
        out_type=jax.ShapeDtypeStruct((T, W), table.dtype),
        scratch_types=[
            pltpu.VMEM((rows,), jnp.int32),
            pltpu.VMEM((rows, W), table.dtype),
        ],
        name="moe_gather_sc",
    )
    def gather(table_hbm, idx_hbm, out_hbm, i_v, rows_v):
        base0 = _sc_worker() * per_w

        @pl.loop(0, n_chunks)
        def _(j):
            base = pl.multiple_of(base0 + j * rows, rows)
            pltpu.sync_copy(idx_hbm.at[pl.ds(base, rows)], i_v)
            pltpu.sync_copy(table_hbm.at[i_v], rows_v)
            pltpu.sync_copy(rows_v, out_hbm.at[pl.ds(base, rows)])

    return gather(table, idx)


def _experts_kernel(te_ref, nx_ref, hf_ref, nv_ref, xs_ref, w1_hbm, w3_hbm, w2_hbm, ys_ref,
                    w1f, w3f, w2f, w1b, w3b, w2b, sem, *, layer):
    i = pl.program_id(0)

    def fetch(e):
        row = e + layer * MOE_EXPERTS
        return [pltpu.make_async_copy(w1_hbm.at[row], w1f, sem),
                pltpu.make_async_copy(w3_hbm.at[row], w3f, sem),
                pltpu.make_async_copy(w2_hbm.at[row], w2f, sem)]

    @pl.when(i < nv_ref[0])
    def _():
        @pl.when(i == 0)
        def _():
            for cp in fetch(te_ref[0]):
                cp.start()

        @pl.when((i == 0) | (te_ref[i] != te_ref[jnp.maximum(i - 1, 0)]))
        def _():
            for cp in fetch(te_ref[i]):
                cp.wait()
            w1b[...] = w1f[...].astype(BF16)
            w3b[...] = w3f[...].astype(BF16)
            w2b[...] = w2f[...].astype(BF16)

            @pl.when(nx_ref[i] >= 0)
            def _():
                for cp in fetch(nx_ref[i]):
                    cp.start()

        def mlp(rows):
            x = jnp.concatenate(_unpack_bf16_pairs(xs_ref[0:rows, :]), axis=1)
            half = D_MODEL // 2
            d = functools.partial(jnp.dot, preferred_element_type=F32)
            h1 = d(x, w1b[...])
            h3 = d(x, w3b[...])
            hid = (h1 * _sigmoid(h1)) * h3
            y = d(hid.astype(BF16), w2b[...])
            ys_ref[0:rows, :] = _pack_bf16_pairs(y[:, :half], y[:, half:])

        @pl.when(hf_ref[i] == 0)
        def _():
            mlp(ROW_TILE)

        @pl.when(hf_ref[i] != 0)
        def _():
            mlp(ROW_TILE // 2)
            ys_ref[ROW_TILE // 2:, :] = jnp.zeros((ROW_TILE // 2, ys_ref.shape[1]), ys_ref.dtype)

    @pl.when(i >= nv_ref[0])
    def _():
        ys_ref[...] = jnp.zeros_like(ys_ref)


def _experts(xs, tile_expert, next_expert, half_tile, n_valid, w1, w3, w2, layer):
    n_rows, W = xs.shape
    nt = n_rows // ROW_TILE
    D, F = w1.shape[1], w1.shape[2]

    def row_map(i, te, nx, hf, nv):
        return (i, 0)

    hbm = pl.BlockSpec(memory_space=pl.ANY)
    return pl.pallas_call(
        functools.partial(_experts_kernel, layer=layer),
        grid_spec=pltpu.PrefetchScalarGridSpec(
            num_scalar_prefetch=4,
            grid=(nt,),
            in_specs=[pl.BlockSpec((ROW_TILE, W), row_map), hbm, hbm, hbm],
            out_specs=pl.BlockSpec((ROW_TILE, D // 2), row_map),
            scratch_shapes=[
                pltpu.VMEM((D, F), F32),
                pltpu.VMEM((D, F), F32),
                pltpu.VMEM((F, D), F32),
                pltpu.VMEM((D, F), BF16),
                pltpu.VMEM((D, F), BF16),
                pltpu.VMEM((F, D), BF16),
                pltpu.SemaphoreType.DMA(()),
            ],
        ),
        out_shape=jax.ShapeDtypeStruct((n_rows, D // 2), jnp.uint32),
        compiler_params=pltpu.CompilerParams(
            dimension_semantics=("arbitrary",), vmem_limit_bytes=VMEM_LIMIT),
        name="moe_experts",
    )(tile_expert, next_expert, half_tile, n_valid, xs, w1, w3, w2)


def _combine_kernel(x_ref, info_ref, fw_ref, y1_ref, y2_ref, out_ref, *, final):
    info = info_ref[...]
    g1 = info[:, _I_G1:_I_G1 + 1]
    g2 = info[:, _I_G2:_I_G2 + 1]
    y1 = jnp.concatenate(_unpack_pairs_f32(y1_ref[...]), axis=1)
    y2 = jnp.concatenate(_unpack_pairs_f32(y2_ref[...]), axis=1)
    y = x_ref[...] + (g1 * y1 + g2 * y2)
    if final:
        y = _rms(y, fw_ref[...])
    out_ref[...] = y


def _combine(x2, part, info, y1, y2, final_w, final):
    T, D = x2.shape
    tp = T // COMBINE_PARTS
    tm = min(TOK_TILE, tp)
    nb = tp // tm
    x_spec = pl.BlockSpec((tm, D), lambda i: (part * nb + i, 0))
    half_spec = pl.BlockSpec((tm, D // 2), lambda i: (i, 0))
    return pl.pallas_call(
        functools.partial(_combine_kernel, final=final),
        grid=(nb,),
        in_specs=[x_spec, pl.BlockSpec((tm, LANES), lambda i: (part * nb + i, 0)),
                  _const_spec((1, D)), half_spec, half_spec],
        out_specs=x_spec,
        out_shape=jax.ShapeDtypeStruct((T, D), F32),
        input_output_aliases={0: 0},
        compiler_params=pltpu.CompilerParams(
            dimension_semantics=("arbitrary",), vmem_limit_bytes=VMEM_LIMIT),
        name="moe_combine",
    )(x2, info, final_w[None, :], y1, y2)


def _hier_moe(x2, norm_w, w_group, b_group, w_expert, b_expert, w1, w3, w2, layer, final_w, final):
    T, D = x2.shape
    nt = (2 * T) // ROW_TILE + MOE_EXPERTS
    hnp, info, info_t, cnt = _router(x2, norm_w, w_group, b_group, w_expert, b_expert)
    pos, tile_expert, next_expert, half_tile, n_valid = _plan(cnt[:, 0].astype(jnp.int32), info_t, nt)
    xs = _dispatch(hnp, pos[0], pos[1], nt * ROW_TILE)
    ys = _experts(xs, tile_expert, next_expert, half_tile, n_valid, w1, w3, w2, layer)
    tp = T // COMBINE_PARTS
    gathered = [(_gather_rows(ys, pos[0, p * tp:(p + 1) * tp]), _gather_rows(ys, pos[1, p * tp:(p + 1) * tp]))
                for p in range(COMBINE_PARTS)]
    out = x2
    for part, (y1, y2) in enumerate(gathered):
        out = _combine(out, part, info, y1, y2, final_w, final)
    return out


def kernel(x, norm_mix, norm_ffn, norm_final, even_w_in, hgrn_lb_logits, hgrn_norm_w, gla_w_up,
           gla_b_up, gla_norm_w, even_w_out, odd_w_in, ret_norm_w, conv_w, conv_b, rglru_w_r,
           rglru_b_r, rglru_w_i, rglru_b_i, rglru_lambda, odd_w_out, moe_w_group, moe_b_group,
           moe_w_expert, moe_b_expert, moe_w1, moe_w3, moe_w2):
    B, L, D = x.shape
    depth = norm_mix.shape[0]
    lb_table = jnp.cumsum(jax.nn.softmax(hgrn_lb_logits.astype(F32), axis=0), axis=0)
    w1 = moe_w1.reshape((-1,) + moe_w1.shape[2:])
    w3 = moe_w3.reshape((-1,) + moe_w3.shape[2:])
    w2 = moe_w2.reshape((-1,) + moe_w2.shape[2:])
    for l in range(depth):
        j = l // 2
        if l % 2 == 0:
            x = _even_mixer(x, norm_mix[l], even_w_in[j], lb_table[l], gla_w_up[j], gla_b_up[j],
                            hgrn_norm_w[j], gla_norm_w[j], even_w_out[j])
        else:
            x = _odd_mixer(x, norm_mix[l], odd_w_in[j], ret_norm_w[j], conv_w[j], conv_b[j],
                           rglru_w_r[j], rglru_b_r[j], rglru_w_i[j], rglru_b_i[j], rglru_lambda[j],
                           odd_w_out[j])
        x = _hier_moe(x.reshape(B * L, D), norm_ffn[l], moe_w_group[l], moe_b_group[l],
                      moe_w_expert[l], moe_b_expert[l], w1, w3, w2, l,
                      norm_final, l == depth - 1).reshape(B, L, D)
    return x
```

```python
import functools

import numpy as np
import jax
import jax.numpy as jnp
from jax import lax
from jax.experimental import pallas as pl
from jax.experimental.pallas import tpu as pltpu
from jax.experimental.pallas import tpu_sc as plsc

F32 = jnp.float32
BF16 = jnp.bfloat16

EPS = 1e-6
CHUNK = 64
GROUP = 8
LOG2E = 1.4426950408889634
LANES = 128
D_MODEL = 1024
MIX_WIDTH = 1024
N_HEADS = 8
GLA_RANK = 16
GLA_TAU = 16.0
RET_HEADS = 4
RET_DK = 64
ROPE_BASE = 10000.0
LRU_WIDTH = 512
LRU_BLOCKS = 8
LRU_CONV = 4
LRU_C = 8.0
MOE_GROUPS = 4
MOE_PER_GROUP = 8
MOE_EXPERTS = MOE_GROUPS * MOE_PER_GROUP
NEG_BIG = -1e30

MIX_TILE = 512
EVEN_SPLIT = 2
ODD_SPLIT = 2
TOK_TILE = 1024
ROW_TILE = 512
COMBINE_PARTS = 4
SC_CORES = 2
SC_WORKERS = 32
SC_ROWS = 128
VMEM_LIMIT = 56 * 1024 * 1024


def _dot(a, b):
    return jnp.dot(a.astype(BF16), b.astype(BF16), preferred_element_type=F32)


def _dot_nt(a, b):
    return lax.dot_general(a.astype(BF16), b.astype(BF16), (((1,), (1,)), ((), ())),
                           preferred_element_type=F32)


def _dot_tn(a, b):
    return lax.dot_general(a.astype(BF16), b.astype(BF16), (((0,), (0,)), ((), ())),
                           preferred_element_type=F32)


def _split2(x):
    hi = x.astype(BF16)
    lo = (x - hi.astype(F32)).astype(BF16)
    return hi, lo


def _split3(x):
    hi = x.astype(BF16)
    r = x - hi.astype(F32)
    mid = r.astype(BF16)
    lo = (r - mid.astype(F32)).astype(BF16)
    return hi, mid, lo


def _dot_x3(a, b):
    ah, al = _split2(a)
    bh, bl = _split2(b)
    d = functools.partial(jnp.dot, preferred_element_type=F32)
    return d(ah, bh) + (d(ah, bl) + d(al, bh))


def _sigmoid(x):
    return 1.0 / (1.0 + jnp.exp(-x))


def _softplus(x):
    return jnp.maximum(x, 0.0) + jnp.log(1.0 + jnp.exp(-jnp.abs(x)))


def _rms(x, w):
    return x * lax.rsqrt(jnp.mean(x * x, axis=-1, keepdims=True) + EPS) * w


def _tri(n):
    r = lax.broadcasted_iota(jnp.int32, (n, n), 0)
    c = lax.broadcasted_iota(jnp.int32, (n, n), 1)
    return jnp.where(r >= c, 1.0, 0.0).astype(BF16)


def _cumsum_rows(tri3, g):
    return jnp.dot(tri3, jnp.concatenate(_split3(g), axis=0), preferred_element_type=F32)


_E_AQ, _E_AF, _E_AI, _E_AG = 0, 512, 1024, 1536
_E_BQ, _E_BK, _E_BV, _E_BR, _E_LOW = 2048, 2304, 2560, 3072, 3584
EVEN_COLS = 3712
_QK_COLS = 768


_LEVEL_BLOCKS = (32, 16, 8)


def _level_tables():
    t = np.arange(CHUNK)
    out = []
    for blk in _LEVEL_BLOCKS:
        odd = (t // blk) % 2 == 1
        out.append((np.where(odd, (t // blk) * blk - 1, -1), np.where(~odd, (t // blk) * blk + blk - 1, -1)))
    return out


def _gla_chunk(rows, q_ref, k_ref, v_ref, b2, o_ref, st_ref, a_ref):
    n_units = _QK_COLS // LANES
    lane = lax.broadcasted_iota(jnp.int32, (1, LANES), 1)
    trow = lax.broadcasted_iota(jnp.int32, (GROUP, LANES), 0)
    unit_heads = [((u, None),) for u in range(4)] + [((4, 0), (5, 64)), ((6, 0), (7, 64))]
    qs = [q_ref[rows, u * LANES:(u + 1) * LANES] for u in range(n_units)]
    ks = [k_ref[rows, u * LANES:(u + 1) * LANES] for u in range(n_units)]
    bs = [b2[:, u * LANES:(u + 1) * LANES] for u in range(n_units)]
    vs = [v_ref[rows, h * LANES:(h + 1) * LANES] for h in range(N_HEADS)]

    for u in range(n_units):
        for j in range(CHUNK // GROUP):
            qj = qs[u][j * GROUP:(j + 1) * GROUP]
            bj = bs[u][j * GROUP:(j + 1) * GROUP]
            for s in range(0, GROUP, 2):
                pair = [qj * jnp.exp2(jnp.where(trow >= s + i, bj - bj[s + i:s + i + 1, :], NEG_BIG))
                        for i in (0, 1)]
                r0 = (j * GROUP + s) * GROUP
                a_ref[r0:r0 + 2 * GROUP, u * LANES:(u + 1) * LANES] = (
                    jnp.concatenate(pair, axis=0).astype(BF16))

    zk = jnp.zeros((CHUNK, LANES), F32)
    scs = []
    for pr in range(n_units // 2):
        u0, u1 = 2 * pr, 2 * pr + 1
        if u0 < 4:
            keys = jnp.concatenate([jnp.concatenate([ks[u0], zk], axis=1),
                                    jnp.concatenate([zk, ks[u1]], axis=1)], axis=0)
        else:
            lo0, hi0 = jnp.where(lane < 64, ks[u0], 0.0), jnp.where(lane >= 64, ks[u0], 0.0)
            lo1, hi1 = jnp.where(lane < 64, ks[u1], 0.0), jnp.where(lane >= 64, ks[u1], 0.0)
            keys = jnp.concatenate([jnp.concatenate([lo0, zk], axis=1), jnp.concatenate([hi0, zk], axis=1),
                                    jnp.concatenate([zk, lo1], axis=1), jnp.concatenate([zk, hi1], axis=1)],
                                   axis=0)
        scs.append(_dot_nt(a_ref[:, u0 * LANES:(u1 + 1) * LANES], keys))

    tables = _level_tables()
    tq = lax.broadcasted_iota(jnp.int32, (CHUNK, CHUNK), 0)
    ts = lax.broadcasted_iota(jnp.int32, (CHUNK, CHUNK), 1)
    covers = [((tq // blk) % 2 == 1) & ((tq // blk) == (ts // blk) + 1) & ((ts // blk) % 2 == 0)
              for blk in _LEVEL_BLOCKS]
    s_offs, o_inter, qes, kds, decs = {}, {}, {}, {}, {}
    for u in range(n_units):
        q, k, b = qs[u], ks[u], bs[u]
        heads = unit_heads[u]
        if len(heads) == 2:
            qm = jnp.concatenate([jnp.where((lane & 64) == lo, q, 0.0) for _, lo in heads], axis=0)
            bm = jnp.concatenate([b, b], axis=0)
        else:
            qm, bm = q, b
        s_off = None
        for (qref, kref), cover in zip(tables, covers):
            qrows, krows = [], []
            for g in range(CHUNK // GROUP):
                sl = slice(g * GROUP, (g + 1) * GROUP)
                rq, rk = int(qref[g * GROUP]), int(kref[g * GROUP])
                qrows.append(jnp.exp2(b[sl] - b[rq:rq + 1, :]) if rq >= 0 else jnp.zeros((GROUP, LANES), F32))
                krows.append(k[sl] * jnp.exp2(b[rk:rk + 1, :] - b[sl]) if rk >= 0
                             else jnp.zeros((GROUP, LANES), F32))
            qfac = jnp.concatenate(qrows, axis=0)
            if len(heads) == 2:
                qfac = jnp.concatenate([qfac, qfac], axis=0)
                cov = jnp.concatenate([cover, cover], axis=0)
            else:
                cov = cover
            part = _dot_nt(qm * qfac, jnp.concatenate(krows, axis=0))
            s_off = jnp.where(cov, part, 0.0 if s_off is None else s_off)
        eb = jnp.exp2(bm)
        b_last = b[CHUNK - 1:CHUNK, :]
        kd = (k * jnp.exp2(b_last - b)).astype(BF16)
        dec = jnp.exp2(b_last)
        qe = (qm * eb).astype(BF16)
        for i, (head, lo) in enumerate(heads):
            sl = slice(i * CHUNK, (i + 1) * CHUNK)
            qes[head], kds[head], decs[head] = qe[sl], kd, dec
            s_offs[head] = s_off[sl]

    zs = jnp.zeros((LANES, LANES), BF16)
    zk = jnp.zeros((CHUNK, LANES), BF16)
    for h0 in range(0, N_HEADS, 2):
        h1 = h0 + 1
        st0, st1 = st_ref[h0], st_ref[h1]
        states = jnp.concatenate([jnp.concatenate([st0.astype(BF16), zs], axis=1),
                                  jnp.concatenate([zs, st1.astype(BF16)], axis=1)], axis=0)
        oi = _dot_nt(jnp.concatenate([qes[h0], qes[h1]], axis=1), states)
        o_inter[h0], o_inter[h1] = oi[:, :LANES], oi[:, LANES:]
        keys2 = jnp.concatenate([jnp.concatenate([kds[h0], zk], axis=1),
                                 jnp.concatenate([zk, kds[h1]], axis=1)], axis=0)
        upd = _dot_tn(jnp.concatenate([vs[h0], vs[h1]], axis=0), keys2)
        st_ref[h0] = st0 * decs[h0] + upd[:, :LANES]
        st_ref[h1] = st1 * decs[h1] + upd[:, LANES:]

    for u in range(n_units):
        sc = scs[u // 2]
        paired = len(unit_heads[u]) == 2
        width = LANES if paired else CHUNK
        c0 = (u % 2) * width
        scol = (lane & 63) if paired else lane[:, :CHUNK]
        groups = []
        for j in range(CHUNK // GROUP):
            acc = jnp.zeros((GROUP, width), F32)
            for s in range(GROUP):
                r0 = (j * GROUP + s) * GROUP
                acc = jnp.where(scol == j * GROUP + s, sc[r0:r0 + GROUP, c0:c0 + width], acc)
            groups.append(acc)
        p = jnp.concatenate(groups, axis=0)
        if paired:
            vstack = jnp.concatenate([vs[h] for h, _ in unit_heads[u]], axis=0)
        for head, lo in unit_heads[u]:
            if lo is None:
                o = o_inter[head] + _dot(s_offs[head] + p, vs[head])
            else:
                o = (o_inter[head] + _dot(s_offs[head], vs[head])
                     + _dot(jnp.where((lane & 64) == lo, p, 0.0), vstack))
            o_ref[rows, head * LANES:(head + 1) * LANES] = o


def _even_kernel(x_ref, nw_ref, win_ref, lb_ref, wup_ref, bup_ref, hw_ref, wout_ref, out_ref,
                 q_ref, k_ref, g_ref, v_ref, gate_ref, o_ref, st_ref, a_ref):
    tl = x_ref.shape[1]
    parts = [(r0, tl // EVEN_SPLIT) for r0 in range(0, tl, tl // EVEN_SPLIT)]

    @pl.when(pl.program_id(1) == 0)
    def _():
        st_ref[...] = jnp.zeros_like(st_ref)

    def project(r0, n):
        rs = slice(r0, r0 + n)
        hn = _rms(x_ref[0, rs, :], nw_ref[...]).astype(BF16)

        def proj(c0, w):
            return jnp.dot(hn, win_ref[:, c0:c0 + w], preferred_element_type=F32)

        q_ref[rs, 0:512] = proj(_E_AQ, 512)
        lb = lb_ref[...]
        f = lb + (1.0 - lb) * _sigmoid(proj(_E_AF, 512))
        k_ref[rs, 0:512] = 1.0 - f
        g_ref[rs, 0:512] = jnp.log(f)
        v_ref[rs, 0:512] = proj(_E_AI, 512)
        ag = proj(_E_AG, 512)
        gate_ref[rs, 0:512] = ag * _sigmoid(ag)
        q_ref[rs, 512:768] = proj(_E_BQ, 256) * (64.0 ** -0.5)
        k_ref[rs, 512:768] = proj(_E_BK, 256)
        v_ref[rs, 512:1024] = proj(_E_BV, 512)
        br = proj(_E_BR, 512)
        gate_ref[rs, 512:1024] = br * _sigmoid(br)
        z = _dot_x3(proj(_E_LOW, LANES), wup_ref[...]) + bup_ref[...]
        g_ref[rs, 512:768] = -_softplus(-z) * (1.0 / GLA_TAU)

    tri = _tri(CHUNK)
    tri3 = jnp.concatenate([tri, tri, tri], axis=1)

    def finish(r0, n):
        rs = slice(r0, r0 + n)
        hw = hw_ref[...]
        ys = []
        for head in range(N_HEADS):
            sl = slice(head * LANES, (head + 1) * LANES)
            oh = o_ref[rs, sl]
            ys.append(oh * lax.rsqrt(jnp.mean(oh * oh, axis=-1, keepdims=True) + EPS)
                      * hw[:, sl] * gate_ref[rs, sl])
        y = jnp.concatenate(ys, axis=1).astype(BF16)
        out_ref[0, rs, :] = x_ref[0, rs, :] + jnp.dot(y, wout_ref[...], preferred_element_type=F32)

    for r0, n in parts:
        project(r0, n)
    for r0, n in parts:
        for c in range(r0 // CHUNK, (r0 + n) // CHUNK):
            rows = slice(c * CHUNK, (c + 1) * CHUNK)
            b2 = _cumsum_rows(tri3, g_ref[rows, :]) * LOG2E
            _gla_chunk(rows, q_ref, k_ref, v_ref, b2, o_ref, st_ref, a_ref.at[c % 2])
    for r0, n in parts:
        finish(r0, n)


def _const_spec(shape):
    nd = len(shape)
    return pl.BlockSpec(shape, lambda *_: (0,) * nd)


def _even_mixer(x, norm_w, w_in, lb, w_up, b_up, hgrn_norm_w, gla_norm_w, w_out):
    B, L, D = x.shape
    tl = min(MIX_TILE, L)
    s = np.cumsum([0, 512, 512, 512, 512, 256, 256, 512, GLA_RANK, 512])
    cols = [w_in[:, s[i]:s[i + 1]] for i in range(9)]
    low = jnp.pad(cols[7], ((0, 0), (0, LANES - GLA_RANK)))
    w_in_r = jnp.concatenate(cols[:7] + [cols[8], low], axis=1).astype(BF16)
    w_up_p = jnp.pad(w_up, ((0, LANES - GLA_RANK), (0, 0)))
    hw = jnp.concatenate([hgrn_norm_w, gla_norm_w])[None, :]
    return pl.pallas_call(
        _even_kernel,
        grid=(B, L // tl),
        in_specs=[
            pl.BlockSpec((1, tl, D), lambda b, l: (b, l, 0)),
            _const_spec((1, D)),
            _const_spec((D, EVEN_COLS)),
            _const_spec((1, 512)),
            _const_spec((LANES, 256)),
            _const_spec((1, 256)),
            _const_spec((1, MIX_WIDTH)),
            _const_spec((MIX_WIDTH, D)),
        ],
        out_specs=pl.BlockSpec((1, tl, D), lambda b, l: (b, l, 0)),
        out_shape=jax.ShapeDtypeStruct((B, L, D), F32),
        scratch_shapes=[
            pltpu.VMEM((tl, _QK_COLS), F32),
            pltpu.VMEM((tl, _QK_COLS), F32),
            pltpu.VMEM((tl, _QK_COLS), F32),
            pltpu.VMEM((tl, MIX_WIDTH), F32),
            pltpu.VMEM((tl, MIX_WIDTH), F32),
            pltpu.VMEM((tl, MIX_WIDTH), F32),
            pltpu.VMEM((N_HEADS, LANES, LANES), F32),
            pltpu.VMEM((2, CHUNK * GROUP, _QK_COLS), BF16),
        ],
        compiler_params=pltpu.CompilerParams(
            dimension_semantics=("arbitrary", "arbitrary"), vmem_limit_bytes=VMEM_LIMIT),
        name="even_mixer",
    )(x, norm_w[None, :], w_in_r, lb[None, :], w_up_p, b_up[None, :], hw, w_out.astype(BF16))


_O_Q, _O_K, _O_V, _O_G, _O_X, _O_GATE = 0, 256, 512, 1024, 1536, 2048
ODD_COLS = 2560


def _odd_kernel(x_ref, nw_ref, win_ref, cos_ref, sin_ref, dmat_ref, qdec_ref, kdec_ref, sdec_ref,
                rw_ref, cw_ref, cb_ref, wri_ref, bri_ref, lam_ref, wout_ref, out_ref,
                q_ref, k_ref, v_ref, gate_ref, o_ref, xe_ref, a_ref, u_ref, st_ref, h_ref):
    tl = x_ref.shape[1]
    parts = [(r0, tl // ODD_SPLIT) for r0 in range(0, tl, tl // ODD_SPLIT)]

    @pl.when(pl.program_id(1) == 0)
    def _():
        st_ref[...] = jnp.zeros_like(st_ref)
        h_ref[...] = jnp.zeros_like(h_ref)
        xe_ref[0:8, :] = jnp.zeros((8, LRU_WIDTH), F32)

    def project(r0, n):
        rs = slice(r0, r0 + n)
        hn = _rms(x_ref[0, rs, :], nw_ref[...]).astype(BF16)

        def proj(c0, w):
            return jnp.dot(hn, win_ref[:, c0:c0 + w], preferred_element_type=F32)

        cos = jnp.concatenate([cos_ref[rs, :]] * 2, axis=1)
        sin = jnp.concatenate([sin_ref[rs, :]] * 2, axis=1)
        lane_qk = lax.broadcasted_iota(jnp.int32, (1, RET_HEADS * RET_DK), 1)
        first_half = (lane_qk & (RET_DK // 2)) == 0

        def rotate_half(t):
            w = t.shape[1]
            return jnp.where(first_half, -pltpu.roll(t, w - RET_DK // 2, axis=1),
                             pltpu.roll(t, RET_DK // 2, axis=1))

        qp = proj(_O_Q, 256)
        kp = proj(_O_K, 256)
        q_ref[rs, :] = qp * cos + rotate_half(qp) * sin
        k_ref[rs, :] = (kp * cos + rotate_half(kp) * sin) * (RET_DK ** -0.5)
        v_ref[rs, :] = proj(_O_V, 512)
        cg = proj(_O_G, 512)
        gate_ref[rs, 0:512] = cg * _sigmoid(cg)
        dg = proj(_O_GATE, LRU_WIDTH)
        gate_ref[rs, 512:1024] = 0.5 * dg * (
            1.0 + jnp.tanh(np.sqrt(2.0 / np.pi) * (dg + 0.044715 * (dg * dg * dg))))

        xe_ref[8 + r0:8 + r0 + n, :] = proj(_O_X, LRU_WIDTH)
        cw = cw_ref[...]
        xc = cb_ref[...]
        for j in range(LRU_CONV):
            xc = xc + xe_ref[5 + j + r0:5 + j + r0 + n, :] * cw[j:j + 1, :]
        ri = jnp.dot(xc.astype(BF16), wri_ref[...], preferred_element_type=F32) + bri_ref[...]
        r = _sigmoid(ri[:, :LRU_WIDTH])
        i = _sigmoid(ri[:, LRU_WIDTH:])
        a = jnp.exp(-LRU_C * r * _softplus(-lam_ref[...]))
        a_ref[rs, :] = a
        u_ref[rs, :] = jnp.sqrt(1.0 - a * a) * (i * xc)

    lane = lax.broadcasted_iota(jnp.int32, (1, LANES), 1)
    crow = lax.broadcasted_iota(jnp.int32, (CHUNK, LRU_WIDTH), 0)

    def recur(r0, n):
        rs = slice(r0, r0 + n)
        for head in range(RET_HEADS):
            unit, half = head // 2, head % 2
            ul = slice(unit * LANES, (unit + 1) * LANES)
            vl = slice(head * LANES, (head + 1) * LANES)
            own = (lane < 64) if half == 0 else (lane >= 64)
            q = jnp.where(own, q_ref[rs, ul], 0.0)
            k = k_ref[rs, ul]
            v = v_ref[rs, vl]
            st = st_ref[head]
            o = _dot_nt(q, st) * qdec_ref[head]
            o = o + _dot(_dot_nt(q, k) * dmat_ref[head], v)
            st_ref[head] = st * sdec_ref[head] + _dot_tn(v, k * kdec_ref[head])
            o_ref[rs, vl] = o
        for c in range(r0 // CHUNK, (r0 + n) // CHUNK):
            rows = slice(c * CHUNK, (c + 1) * CHUNK)
            ca = a_ref[rows, :]
            ch = u_ref[rows, :]
            d = 1
            while d < CHUNK:
                keep = crow >= d
                sa = jnp.where(keep, pltpu.roll(ca, d, axis=0), 1.0)
                sh = jnp.where(keep, pltpu.roll(ch, d, axis=0), 0.0)
                ch = ca * sh + ch
                ca = ca * sa
                d *= 2
            ch = ch + ca * h_ref[...]
            h_ref[...] = ch[CHUNK - 1:CHUNK, :]
            o_ref[rows, 512:1024] = ch

    def finish(r0, n):
        rs = slice(r0, r0 + n)
        rw = rw_ref[...]
        ys = []
        for head in range(RET_HEADS):
            sl = slice(head * LANES, (head + 1) * LANES)
            oh = o_ref[rs, sl]
            oh = oh - jnp.mean(oh, axis=-1, keepdims=True)
            ys.append(oh * lax.rsqrt(jnp.mean(oh * oh, axis=-1, keepdims=True) + EPS) * rw[:, sl])
        ys.append(o_ref[rs, 512:1024])
        y = (jnp.concatenate(ys, axis=1) * gate_ref[rs, :]).astype(BF16)
        out_ref[0, rs, :] = x_ref[0, rs, :] + jnp.dot(y, wout_ref[...], preferred_element_type=F32)

    for r0, n in parts:
        project(r0, n)
    xe_ref[0:8, :] = xe_ref[tl:tl + 8, :]
    for r0, n in parts:
        recur(r0, n)
    for r0, n in parts:
        finish(r0, n)


def _odd_mixer(x, norm_w, w_in, ret_norm_w, conv_w, conv_b, w_r, b_r, w_i, b_i, lam, w_out):
    B, L, D = x.shape
    tl = min(MIX_TILE, L)
    w_in_r = w_in.astype(BF16)
    half = RET_DK // 2
    inv = ROPE_BASE ** (-jnp.arange(half, dtype=F32) / half)
    ang = jnp.arange(L, dtype=F32)[:, None] * inv[None, :]
    cos = jnp.tile(jnp.cos(ang), (1, LANES // half))
    sin = jnp.tile(jnp.sin(ang), (1, LANES // half))
    lg = jnp.log1p(-jnp.exp2(-5.0 - jnp.arange(RET_HEADS, dtype=F32)))[:, None, None]
    rc = tl // ODD_SPLIT
    t = jnp.arange(rc, dtype=F32)
    rel = t[:, None] - t[None, :]
    dmat = jnp.where(rel >= 0, jnp.exp(lg * jnp.maximum(rel, 0.0)), 0.0)
    qdec = jnp.broadcast_to(jnp.exp(lg * (t[None, :, None] + 1.0)), (RET_HEADS, rc, LANES))
    kdec = jnp.broadcast_to(jnp.exp(lg * (rc - 1.0 - t[None, :, None])), (RET_HEADS, rc, LANES))
    sdec = jnp.broadcast_to(jnp.exp(lg * float(rc)), (RET_HEADS, 1, LANES))
    eye = jnp.eye(LRU_BLOCKS, dtype=F32)
    bd = LRU_WIDTH // LRU_BLOCKS

    def blockdiag(w):
        return (eye[:, None, :, None] * w[:, :, None, :]).reshape(LRU_WIDTH, LRU_WIDTH)

    wri = jnp.concatenate([blockdiag(w_r), blockdiag(w_i)], axis=1).astype(BF16)
    bri = jnp.concatenate([b_r, b_i])[None, :]
    return pl.pallas_call(
        _odd_kernel,
        grid=(B, L // tl),
        in_specs=[
            pl.BlockSpec((1, tl, D), lambda b, l: (b, l, 0)),
            _const_spec((1, D)),
            _const_spec((D, ODD_COLS)),
            pl.BlockSpec((tl, LANES), lambda b, l: (l, 0)),
            pl.BlockSpec((tl, LANES), lambda b, l: (l, 0)),
            _const_spec((RET_HEADS, rc, rc)),
            _const_spec((RET_HEADS, rc, LANES)),
            _const_spec((RET_HEADS, rc, LANES)),
            _const_spec((RET_HEADS, 1, LANES)),
            _const_spec((1, 512)),
            _const_spec((LRU_CONV, LRU_WIDTH)),
            _const_spec((1, LRU_WIDTH)),
            _const_spec((LRU_WIDTH, 2 * LRU_WIDTH)),
            _const_spec((1, 2 * LRU_WIDTH)),
            _const_spec((1, LRU_WIDTH)),
            _const_spec((MIX_WIDTH, D)),
        ],
        out_specs=pl.BlockSpec((1, tl, D), lambda b, l: (b, l, 0)),
        out_shape=jax.ShapeDtypeStruct((B, L, D), F32),
        scratch_shapes=[
            pltpu.VMEM((tl, 256), F32),
            pltpu.VMEM((tl, 256), F32),
            pltpu.VMEM((tl, 512), F32),
            pltpu.VMEM((tl, MIX_WIDTH), F32),
            pltpu.VMEM((tl, MIX_WIDTH), F32),
            pltpu.VMEM((tl + 8, LRU_WIDTH), F32),
            pltpu.VMEM((tl, LRU_WIDTH), F32),
            pltpu.VMEM((tl, LRU_WIDTH), F32),
            pltpu.VMEM((RET_HEADS, LANES, LANES), F32),
            pltpu.VMEM((1, LRU_WIDTH), F32),
        ],
        compiler_params=pltpu.CompilerParams(
            dimension_semantics=("arbitrary", "arbitrary"), vmem_limit_bytes=VMEM_LIMIT),
        name="odd_mixer",
    )(x, norm_w[None, :], w_in_r, cos, sin, dmat, qdec, kdec, sdec, ret_norm_w[None, :],
      conv_w, conv_b[None, :], wri, bri, lam[None, :], w_out.astype(BF16))


_I_E1, _I_E2, _I_G1, _I_G2, _I_R1, _I_R2 = 0, 1, 2, 3, 4, 5
_LOGIT_E0 = MOE_GROUPS
ROUTE_ROWS = 40


def _pack_bf16_pairs(lo, hi):
    lo_b = pltpu.bitcast(lo.astype(BF16).astype(F32), jnp.uint32)
    hi_b = pltpu.bitcast(hi.astype(BF16).astype(F32), jnp.uint32)
    return (lo_b >> 16) | (hi_b & jnp.uint32(0xFFFF0000))


def _unpack_pairs_f32(w):
    return pltpu.bitcast(w << 16, F32), pltpu.bitcast(w & jnp.uint32(0xFFFF0000), F32)


def _unpack_bf16_pairs(w):
    lo, hi = _unpack_pairs_f32(w)
    return lo.astype(BF16), hi.astype(BF16)


def _router_kernel(x_ref, nw_ref, wrt_ref, brt_ref, hnp_ref, info_ref, info_t_ref, cnt_ref):
    tm = x_ref.shape[0]

    @pl.when(pl.program_id(0) == 0)
    def _():
        cnt_ref[...] = jnp.zeros_like(cnt_ref)

    hn = _rms(x_ref[...], nw_ref[...])
    half = D_MODEL // 2
    hn_hi = hn.astype(BF16)
    hn_hif = hn_hi.astype(F32)
    hn_lo = (hn - hn_hif).astype(BF16)
    bits = pltpu.bitcast(hn_hif, jnp.uint32)
    hnp_ref[...] = (bits[:, :half] >> 16) | (bits[:, half:] & jnp.uint32(0xFFFF0000))

    w_hi, w_lo = _split2(wrt_ref[...])
    both = _dot_nt(jnp.concatenate([w_hi, w_lo], axis=0), hn_hi)
    bias = jnp.concatenate([brt_ref[...]] * (tm // LANES), axis=1)
    lt = (both[:LANES] + (both[LANES:] + _dot_nt(w_hi, hn_lo)) + bias)[0:ROUTE_ROWS]
    row = lax.broadcasted_iota(jnp.int32, (ROUTE_ROWS, tm), 0).astype(F32)

    def first_max(vals):
        m = jnp.max(vals, axis=0, keepdims=True)
        idx = jnp.min(jnp.where(vals == m, row, float(LANES)), axis=0, keepdims=True)
        return m, idx

    gl = jnp.where(row < MOE_GROUPS, lt, NEG_BIG)
    gmax, gidx = first_max(gl)
    g_w = 1.0 / jnp.sum(jnp.exp(gl - gmax), axis=0, keepdims=True)
    e0 = _LOGIT_E0 + MOE_PER_GROUP * gidx
    el = jnp.where((row >= e0) & (row < e0 + MOE_PER_GROUP), lt, NEG_BIG)
    m1, i1 = first_max(el)
    m2, i2 = first_max(jnp.where(row == i1, NEG_BIG, el))
    p2 = jnp.exp(m2 - m1)
    w1 = 1.0 / (1.0 + p2)
    e1 = i1 - _LOGIT_E0
    e2 = i2 - _LOGIT_E0

    erow = lax.broadcasted_iota(jnp.int32, (MOE_EXPERTS, tm), 0).astype(F32)
    oh1 = jnp.where(erow == e1, 1.0, 0.0)
    oh2 = jnp.where(erow == e2, 1.0, 0.0)
    r_i = lax.broadcasted_iota(jnp.int32, (tm, tm), 0)
    c_i = lax.broadcasted_iota(jnp.int32, (tm, tm), 1)
    triu = jnp.where(r_i <= c_i, 1.0, 0.0).astype(BF16)
    pre = jnp.dot(jnp.concatenate([oh1, oh2], axis=0).astype(BF16), triu, preferred_element_type=F32)
    pre1, pre2 = pre[:MOE_EXPERTS], pre[MOE_EXPERTS:]
    carry = cnt_ref[0:MOE_EXPERTS, :]
    tot = jnp.dot(jnp.concatenate([oh1, oh2], axis=0).astype(BF16), jnp.ones((tm, LANES), BF16),
                  preferred_element_type=F32)
    c1, c2 = tot[:MOE_EXPERTS], tot[MOE_EXPERTS:]
    rep = tm // LANES
    r1 = jnp.sum(oh1 * (pre1 + jnp.concatenate([carry] * rep, axis=1)), axis=0, keepdims=True) - 1.0
    r2 = jnp.sum(oh2 * (pre2 + jnp.concatenate([carry + c1] * rep, axis=1)), axis=0, keepdims=True) - 1.0
    cnt_ref[0:MOE_EXPERTS, :] = carry + (c1 + c2)

    row8 = lax.broadcasted_iota(jnp.int32, (GROUP, tm), 0).astype(F32)
    info_t = jnp.zeros((GROUP, tm), F32)
    for k, val in ((_I_E1, e1), (_I_E2, e2), (_I_G1, g_w * w1), (_I_G2, g_w * (w1 * p2)),
                   (_I_R1, r1), (_I_R2, r2)):
        info_t = jnp.where(row8 == float(k), val, info_t)
    info_t_ref[...] = info_t
    info_ref[...] = jnp.concatenate([info_t, jnp.zeros((LANES - GROUP, tm), F32)], axis=0).T


def _router(x2, norm_w, w_group, b_group, w_expert, b_expert):
    T, D = x2.shape
    tm = min(TOK_TILE, T)
    pad = LANES - MOE_GROUPS - MOE_EXPERTS
    wrt = jnp.pad(jnp.concatenate([w_group, w_expert], axis=1), ((0, 0), (0, pad))).T
    brt = jnp.broadcast_to(jnp.pad(jnp.concatenate([b_group, b_expert]), (0, pad))[:, None], (LANES, LANES))
    return pl.pallas_call(
        _router_kernel,
        grid=(T // tm,),
        in_specs=[
            pl.BlockSpec((tm, D), lambda i: (i, 0)),
            _const_spec((1, D)),
            _const_spec((LANES, D)),
            _const_spec((LANES, LANES)),
        ],
        out_specs=[
            pl.BlockSpec((tm, D // 2), lambda i: (i, 0)),
            pl.BlockSpec((tm, LANES), lambda i: (i, 0)),
            pl.BlockSpec((GROUP, tm), lambda i: (0, i)),
            _const_spec((LANES, LANES)),
        ],
        out_shape=[
            jax.ShapeDtypeStruct((T, D // 2), jnp.uint32),
            jax.ShapeDtypeStruct((T, LANES), F32),
            jax.ShapeDtypeStruct((GROUP, T), F32),
            jax.ShapeDtypeStruct((LANES, LANES), F32),
        ],
        compiler_params=pltpu.CompilerParams(
            dimension_semantics=("arbitrary",), vmem_limit_bytes=VMEM_LIMIT),
        name="moe_router",
    )(x2, norm_w[None, :], wrt, brt)


def _plan_kernel(cnt_ref, info_ref, pos_ref, te_ref, nx_ref, hf_ref, nv_ref):
    nt = te_ref.shape[0]
    e1 = info_ref[_I_E1:_I_E1 + 1, :]
    e2 = info_ref[_I_E2:_I_E2 + 1, :]
    s1 = jnp.zeros_like(e1)
    s2 = jnp.zeros_like(e2)
    tiles = [lax.div(cnt_ref[e] + (ROW_TILE - 1), jnp.int32(ROW_TILE)) for e in range(MOE_EXPERTS)]
    following = [None] * MOE_EXPERTS
    nxt = jnp.int32(-1)
    for e in reversed(range(MOE_EXPERTS)):
        following[e] = nxt
        nxt = jnp.where(tiles[e] > 0, jnp.int32(e), nxt)
    start = jnp.int32(0)
    for e in range(MOE_EXPERTS):
        seg = (start * ROW_TILE).astype(F32)
        s1 = jnp.where(e1 == float(e), seg, s1)
        s2 = jnp.where(e2 == float(e), seg, s2)

        end = start + tiles[e]
        last_rows = cnt_ref[e] - (tiles[e] - 1) * ROW_TILE

        def fill(i, carry, e=e, end=end, last_rows=last_rows):
            te_ref[i] = jnp.int32(e)
            nx_ref[i] = following[e]
            hf_ref[i] = jnp.where((i == end - 1) & (last_rows <= ROW_TILE // 2), 1, 0).astype(jnp.int32)
            return carry

        lax.fori_loop(start, end, fill, 0)
        start = end
    nv_ref[0] = start

    def fill_tail(i, carry):
        te_ref[i] = jnp.int32(MOE_EXPERTS - 1)
        nx_ref[i] = jnp.int32(-1)
        hf_ref[i] = jnp.int32(0)
        return carry

    lax.fori_loop(start, nt, fill_tail, 0)
    pos_ref[0:1, :] = (s1 + info_ref[_I_R1:_I_R1 + 1, :]).astype(jnp.int32)
    pos_ref[1:2, :] = (s2 + info_ref[_I_R2:_I_R2 + 1, :]).astype(jnp.int32)


def _plan(cnt, info_t, nt):
    T = info_t.shape[1]
    smem = pl.BlockSpec(memory_space=pltpu.SMEM)
    return pl.pallas_call(
        _plan_kernel,
        in_specs=[smem, pl.BlockSpec(memory_space=pltpu.VMEM)],
        out_specs=[pl.BlockSpec(memory_space=pltpu.VMEM), smem, smem, smem, smem],
        out_shape=[
            jax.ShapeDtypeStruct((2, T), jnp.int32),
            jax.ShapeDtypeStruct((nt,), jnp.int32),
            jax.ShapeDtypeStruct((nt,), jnp.int32),
            jax.ShapeDtypeStruct((nt,), jnp.int32),
            jax.ShapeDtypeStruct((1,), jnp.int32),
        ],
        name="moe_plan",
    )(cnt, info_t)


def _sc_worker():
    return lax.axis_index("s") * SC_CORES + lax.axis_index("c")


def _dispatch(hnp, pos1, pos2, n_rows):
    T, W = hnp.shape
    per_w = T // SC_WORKERS
    n_chunks = per_w // SC_ROWS
    mesh = plsc.VectorSubcoreMesh(core_axis_name="c", subcore_axis_name="s")

    @functools.partial(
        pl.kernel, mesh=mesh,
        out_type=jax.ShapeDtypeStruct((n_rows, W), jnp.uint32),
        scratch_types=[
            pltpu.VMEM((SC_ROWS,), jnp.int32),
            pltpu.VMEM((SC_ROWS,), jnp.int32),
            pltpu.VMEM((SC_ROWS, W), jnp.uint32),
        ],
        name="moe_dispatch_sc",
    )
    def scatter(hnp_hbm, p1_hbm, p2_hbm, xs_hbm, i1_v, i2_v, rows_v):
        base0 = _sc_worker() * per_w

        @pl.loop(0, n_chunks)
        def _(j):
            base = pl.multiple_of(base0 + j * SC_ROWS, SC_ROWS)
            pltpu.sync_copy(p1_hbm.at[pl.ds(base, SC_ROWS)], i1_v)
            pltpu.sync_copy(p2_hbm.at[pl.ds(base, SC_ROWS)], i2_v)
            pltpu.sync_copy(hnp_hbm.at[pl.ds(base, SC_ROWS)], rows_v)
            pltpu.sync_copy(rows_v, xs_hbm.at[i1_v])
            pltpu.sync_copy(rows_v, xs_hbm.at[i2_v])

    return scatter(hnp, pos1, pos2)


def _gather_rows(table, idx):
    T = idx.shape[0]
    W = table.shape[1]
    per_w = T // SC_WORKERS
    rows = SC_ROWS
    n_chunks = per_w // rows
    mesh = plsc.VectorSubcoreMesh(core_axis_name="c", subcore_axis_name="s")

    @functools.partial(
        pl.kernel, mesh=mesh,
        out_type=jax.ShapeDtypeStruct((T, W), table.dtype),
        scratch_types=[
            pltpu.VMEM((rows,), jnp.int32),
            pltpu.VMEM((rows, W), table.dtype),
        ],
        name="moe_gather_sc",
    )
    def gather(table_hbm, idx_hbm, out_hbm, i_v, rows_v):
        base0 = _sc_worker() * per_w

        @pl.loop(0, n_chunks)
        def _(j):
            base = pl.multiple_of(base0 + j * rows, rows)
            pltpu.sync_copy(idx_hbm.at[pl.ds(base, rows)], i_v)
            pltpu.sync_copy(table_hbm.at[i_v], rows_v)
            pltpu.sync_copy(rows_v, out_hbm.at[pl.ds(base, rows)])

    return gather(table, idx)


def _experts_kernel(te_ref, nx_ref, hf_ref, nv_ref, xs_ref, w1_hbm, w3_hbm, w2_hbm, ys_ref,
                    w1f, w3f, w2f, w1b, w3b, w2b, sem, *, layer):
    i = pl.program_id(0)

    def fetch(e):
        row = e + layer * MOE_EXPERTS
        return [pltpu.make_async_copy(w1_hbm.at[row], w1f, sem),
                pltpu.make_async_copy(w3_hbm.at[row], w3f, sem),
                pltpu.make_async_copy(w2_hbm.at[row], w2f, sem)]

    @pl.when(i < nv_ref[0])
    def _():
        @pl.when(i == 0)
        def _():
            for cp in fetch(te_ref[0]):
                cp.start()

        @pl.when((i == 0) | (te_ref[i] != te_ref[jnp.maximum(i - 1, 0)]))
        def _():
            for cp in fetch(te_ref[i]):
                cp.wait()
            w1b[...] = w1f[...].astype(BF16)
            w3b[...] = w3f[...].astype(BF16)
            w2b[...] = w2f[...].astype(BF16)

            @pl.when(nx_ref[i] >= 0)
            def _():
                for cp in fetch(nx_ref[i]):
                    cp.start()

        def mlp(rows):
            x = jnp.concatenate(_unpack_bf16_pairs(xs_ref[0:rows, :]), axis=1)
            half = D_MODEL // 2
            d = functools.partial(jnp.dot, preferred_element_type=F32)
            h1 = d(x, w1b[...])
            h3 = d(x, w3b[...])
            hid = (h1 * _sigmoid(h1)) * h3
            y = d(hid.astype(BF16), w2b[...])
            ys_ref[0:rows, :] = _pack_bf16_pairs(y[:, :half], y[:, half:])

        @pl.when(hf_ref[i] == 0)
        def _():
            mlp(ROW_TILE)

        @pl.when(hf_ref[i] != 0)
        def _():
            mlp(ROW_TILE // 2)
            ys_ref[ROW_TILE // 2:, :] = jnp.zeros((ROW_TILE // 2, ys_ref.shape[1]), ys_ref.dtype)

    @pl.when(i >= nv_ref[0])
    def _():
        ys_ref[...] = jnp.zeros_like(ys_ref)


def _experts(xs, tile_expert, next_expert, half_tile, n_valid, w1, w3, w2, layer):
    n_rows, W = xs.shape
    nt = n_rows // ROW_TILE
    D, F = w1.shape[1], w1.shape[2]

    def row_map(i, te, nx, hf, nv):
        return (i, 0)

    hbm = pl.BlockSpec(memory_space=pl.ANY)
    return pl.pallas_call(
        functools.partial(_experts_kernel, layer=layer),
        grid_spec=pltpu.PrefetchScalarGridSpec(
            num_scalar_prefetch=4,
            grid=(nt,),
            in_specs=[pl.BlockSpec((ROW_TILE, W), row_map), hbm, hbm, hbm],
            out_specs=pl.BlockSpec((ROW_TILE, D // 2), row_map),
            scratch_shapes=[
                pltpu.VMEM((D, F), F32),
                pltpu.VMEM((D, F), F32),
                pltpu.VMEM((F, D), F32),
                pltpu.VMEM((D, F), BF16),
                pltpu.VMEM((D, F), BF16),
                pltpu.VMEM((F, D), BF16),
                pltpu.SemaphoreType.DMA(()),
            ],
        ),
        out_shape=jax.ShapeDtypeStruct((n_rows, D // 2), jnp.uint32),
        compiler_params=pltpu.CompilerParams(
            dimension_semantics=("arbitrary",), vmem_limit_bytes=VMEM_LIMIT),
        name="moe_experts",
    )(tile_expert, next_expert, half_tile, n_valid, xs, w1, w3, w2)


def _combine_kernel(x_ref, info_ref, fw_ref, y1_ref, y2_ref, out_ref, *, final):
    info = info_ref[...]
    g1 = info[:, _I_G1:_I_G1 + 1]
    g2 = info[:, _I_G2:_I_G2 + 1]
    y1 = jnp.concatenate(_unpack_pairs_f32(y1_ref[...]), axis=1)
    y2 = jnp.concatenate(_unpack_pairs_f32(y2_ref[...]), axis=1)
    y = x_ref[...] + (g1 * y1 + g2 * y2)
    if final:
        y = _rms(y, fw_ref[...])
    out_ref[...] = y


def _combine(x2, part, info, y1, y2, final_w, final):
    T, D = x2.shape
    tp = T // COMBINE_PARTS
    tm = min(TOK_TILE, tp)
    nb = tp // tm
    x_spec = pl.BlockSpec((tm, D), lambda i: (part * nb + i, 0))
    half_spec = pl.BlockSpec((tm, D // 2), lambda i: (i, 0))
    return pl.pallas_call(
        functools.partial(_combine_kernel, final=final),
        grid=(nb,),
        in_specs=[x_spec, pl.BlockSpec((tm, LANES), lambda i: (part * nb + i, 0)),
                  _const_spec((1, D)), half_spec, half_spec],
        out_specs=x_spec,
        out_shape=jax.ShapeDtypeStruct((T, D), F32),
        input_output_aliases={0: 0},
        compiler_params=pltpu.CompilerParams(
            dimension_semantics=("arbitrary",), vmem_limit_bytes=VMEM_LIMIT),
        name="moe_combine",
    )(x2, info, final_w[None, :], y1, y2)


def _hier_moe(x2, norm_w, w_group, b_group, w_expert, b_expert, w1, w3, w2, layer, final_w, final):
    T, D = x2.shape
    nt = (2 * T) // ROW_TILE + MOE_EXPERTS
    hnp, info, info_t, cnt = _router(x2, norm_w, w_group, b_group, w_expert, b_expert)
    pos, tile_expert, next_expert, half_tile, n_valid = _plan(cnt[:, 0].astype(jnp.int32), info_t, nt)
    xs = _dispatch(hnp, pos[0], pos[1], nt * ROW_TILE)
    ys = _experts(xs, tile_expert, next_expert, half_tile, n_valid, w1, w3, w2, layer)
    tp = T // COMBINE_PARTS
    gathered = [(_gather_rows(ys, pos[0, p * tp:(p + 1) * tp]), _gather_rows(ys, pos[1, p * tp:(p + 1) * tp]))
                for p in range(COMBINE_PARTS)]
    out = x2
    for part, (y1, y2) in enumerate(gathered):
        out = _combine(out, part, info, y1, y2, final_w, final)
    return out


def kernel(x, norm_mix, norm_ffn, norm_final, even_w_in, hgrn_lb_logits, hgrn_norm_w, gla_w_up,
           gla_b_up, gla_norm_w, even_w_out, odd_w_in, ret_norm_w, conv_w, conv_b, rglru_w_r,
           rglru_b_r, rglru_w_i, rglru_b_i, rglru_lambda, odd_w_out, moe_w_group, moe_b_group,
           moe_w_expert, moe_b_expert, moe_w1, moe_w3, moe_w2):
    B, L, D = x.shape
    depth = norm_mix.shape[0]
    lb_table = jnp.cumsum(jax.nn.softmax(hgrn_lb_logits.astype(F32), axis=0), axis=0)
    w1 = moe_w1.reshape((-1,) + moe_w1.shape[2:])
    w3 = moe_w3.reshape((-1,) + moe_w3.shape[2:])
    w2 = moe_w2.reshape((-1,) + moe_w2.shape[2:])
    for l in range(depth):
        j = l // 2
        if l % 2 == 0:
            x = _even_mixer(x, norm_mix[l], even_w_in[j], lb_table[l], gla_w_up[j], gla_b_up[j],
                            hgrn_norm_w[j], gla_norm_w[j], even_w_out[j])
        else:
            x = _odd_mixer(x, norm_mix[l], odd_w_in[j], ret_norm_w[j], conv_w[j], conv_b[j],
                           rglru_w_r[j], rglru_b_r[j], rglru_w_i[j], rglru_b_i[j], rglru_lambda[j],
                           odd_w_out[j])
        x = _hier_moe(x.reshape(B * L, D), norm_ffn[l], moe_w_group[l], moe_b_group[l],
                      moe_w_expert[l], moe_b_expert[l], w1, w3, w2, l,
                      norm_final, l == depth - 1).reshape(B, L, D)
    return x
```

```python
import functools

import numpy as np
import jax
import jax.numpy as jnp
from jax import lax
from jax.experimental import pallas as pl
from jax.experimental.pallas import tpu as pltpu
from jax.experimental.pallas import tpu_sc as plsc

F32 = jnp.float32
BF16 = jnp.bfloat16

EPS = 1e-6
CHUNK = 64
GROUP = 8
LOG2E = 1.4426950408889634
LANES = 128
D_MODEL = 1024
MIX_WIDTH = 1024
N_HEADS = 8
GLA_RANK = 16
GLA_TAU = 16.0
RET_HEADS = 4
RET_DK = 64
ROPE_BASE = 10000.0
LRU_WIDTH = 512
LRU_BLOCKS = 8
LRU_CONV = 4
LRU_C = 8.0
MOE_GROUPS = 4
MOE_PER_GROUP = 8
MOE_EXPERTS = MOE_GROUPS * MOE_PER_GROUP
NEG_BIG = -1e30

MIX_TILE = 512
EVEN_SPLIT = 2
ODD_SPLIT = 2
TOK_TILE = 1024
ROW_TILE = 512
COMBINE_PARTS = 4
SC_CORES = 2
SC_WORKERS = 32
SC_ROWS = 128
VMEM_LIMIT = 56 * 1024 * 1024


def _dot(a, b):
    return jnp.dot(a.astype(BF16), b.astype(BF16), preferred_element_type=F32)


def _dot_nt(a, b):
    return lax.dot_general(a.astype(BF16), b.astype(BF16), (((1,), (1,)), ((), ())),
                           preferred_element_type=F32)


def _dot_tn(a, b):
    return lax.dot_general(a.astype(BF16), b.astype(BF16), (((0,), (0,)), ((), ())),
                           preferred_element_type=F32)


def _split2(x):
    hi = x.astype(BF16)
    lo = (x - hi.astype(F32)).astype(BF16)
    return hi, lo


def _split3(x):
    hi = x.astype(BF16)
    r = x - hi.astype(F32)
    mid = r.astype(BF16)
    lo = (r - mid.astype(F32)).astype(BF16)
    return hi, mid, lo


def _dot_x3(a, b):
    ah, al = _split2(a)
    bh, bl = _split2(b)
    d = functools.partial(jnp.dot, preferred_element_type=F32)
    return d(ah, bh) + (d(ah, bl) + d(al, bh))


def _sigmoid(x):
    return 1.0 / (1.0 + jnp.exp(-x))


def _softplus(x):
    return jnp.maximum(x, 0.0) + jnp.log(1.0 + jnp.exp(-jnp.abs(x)))


def _rms(x, w):
    return x * lax.rsqrt(jnp.mean(x * x, axis=-1, keepdims=True) + EPS) * w


def _tri(n):
    r = lax.broadcasted_iota(jnp.int32, (n, n), 0)
    c = lax.broadcasted_iota(jnp.int32, (n, n), 1)
    return jnp.where(r >= c, 1.0, 0.0).astype(BF16)


def _cumsum_rows(tri3, g):
    return jnp.dot(tri3, jnp.concatenate(_split3(g), axis=0), preferred_element_type=F32)


_E_AQ, _E_AF, _E_AI, _E_AG = 0, 512, 1024, 1536
_E_BQ, _E_BK, _E_BV, _E_BR, _E_LOW = 2048, 2304, 2560, 3072, 3584
EVEN_COLS = 3712
_QK_COLS = 768


_LEVEL_BLOCKS = (32, 16, 8)


def _level_tables():
    t = np.arange(CHUNK)
    out = []
    for blk in _LEVEL_BLOCKS:
        odd = (t // blk) % 2 == 1
        out.append((np.where(odd, (t // blk) * blk - 1, -1), np.where(~odd, (t // blk) * blk + blk - 1, -1)))
    return out


def _gla_chunk(rows, q_ref, k_ref, v_ref, b2, o_ref, st_ref, a_ref, b_ref):
    n_units = _QK_COLS // LANES
    lane = lax.broadcasted_iota(jnp.int32, (1, LANES), 1)
    trow = lax.broadcasted_iota(jnp.int32, (GROUP, LANES), 0)
    unit_heads = [((u, None),) for u in range(4)] + [((4, 0), (5, 64)), ((6, 0), (7, 64))]
    qs = [q_ref[rows, u * LANES:(u + 1) * LANES] for u in range(n_units)]
    ks = [k_ref[rows, u * LANES:(u + 1) * LANES] for u in range(n_units)]
    b_ref[...] = b2
    bs = [b2[:, u * LANES:(u + 1) * LANES] for u in range(n_units)]
    vs = [v_ref[rows, h * LANES:(h + 1) * LANES] for h in range(N_HEADS)]

    for u in range(n_units):
        for j in range(CHUNK // GROUP):
            qj = qs[u][j * GROUP:(j + 1) * GROUP]
            bj = bs[u][j * GROUP:(j + 1) * GROUP]
            for s in range(0, GROUP, 2):
                pair = [qj * jnp.exp2(jnp.where(
                    trow >= s + i,
                    bj - b_ref[j * GROUP + s + i:j * GROUP + s + i + 1, u * LANES:(u + 1) * LANES], NEG_BIG))
                        for i in (0, 1)]
                r0 = (j * GROUP + s) * GROUP
                a_ref[r0:r0 + 2 * GROUP, u * LANES:(u + 1) * LANES] = (
                    jnp.concatenate(pair, axis=0).astype(BF16))

    zk = jnp.zeros((CHUNK, LANES), F32)
    scs = []
    for pr in range(n_units // 2):
        u0, u1 = 2 * pr, 2 * pr + 1
        if u0 < 4:
            keys = jnp.concatenate([jnp.concatenate([ks[u0], zk], axis=1),
                                    jnp.concatenate([zk, ks[u1]], axis=1)], axis=0)
        else:
            lo0, hi0 = jnp.where(lane < 64, ks[u0], 0.0), jnp.where(lane >= 64, ks[u0], 0.0)
            lo1, hi1 = jnp.where(lane < 64, ks[u1], 0.0), jnp.where(lane >= 64, ks[u1], 0.0)
            keys = jnp.concatenate([jnp.concatenate([lo0, zk], axis=1), jnp.concatenate([hi0, zk], axis=1),
                                    jnp.concatenate([zk, lo1], axis=1), jnp.concatenate([zk, hi1], axis=1)],
                                   axis=0)
        scs.append(_dot_nt(a_ref[:, u0 * LANES:(u1 + 1) * LANES], keys))

    tables = _level_tables()
    tq = lax.broadcasted_iota(jnp.int32, (CHUNK, CHUNK), 0)
    ts = lax.broadcasted_iota(jnp.int32, (CHUNK, CHUNK), 1)
    covers = [((tq // blk) % 2 == 1) & ((tq // blk) == (ts // blk) + 1) & ((ts // blk) % 2 == 0)
              for blk in _LEVEL_BLOCKS]
    s_offs, o_inter, qes, kds, decs = {}, {}, {}, {}, {}
    for u in range(n_units):
        q, k, b = qs[u], ks[u], bs[u]
        heads = unit_heads[u]
        if len(heads) == 2:
            qm = jnp.concatenate([jnp.where((lane & 64) == lo, q, 0.0) for _, lo in heads], axis=0)
            bm = jnp.concatenate([b, b], axis=0)
        else:
            qm, bm = q, b
        s_off = None
        for (qref, kref), cover in zip(tables, covers):
            qrows, krows = [], []
            for g in range(CHUNK // GROUP):
                sl = slice(g * GROUP, (g + 1) * GROUP)
                rq, rk = int(qref[g * GROUP]), int(kref[g * GROUP])
                qrows.append(jnp.exp2(b[sl] - b[rq:rq + 1, :]) if rq >= 0 else jnp.zeros((GROUP, LANES), F32))
                krows.append(k[sl] * jnp.exp2(b[rk:rk + 1, :] - b[sl]) if rk >= 0
                             else jnp.zeros((GROUP, LANES), F32))
            qfac = jnp.concatenate(qrows, axis=0)
            if len(heads) == 2:
                qfac = jnp.concatenate([qfac, qfac], axis=0)
                cov = jnp.concatenate([cover, cover], axis=0)
            else:
                cov = cover
            part = _dot_nt(qm * qfac, jnp.concatenate(krows, axis=0))
            s_off = jnp.where(cov, part, 0.0 if s_off is None else s_off)
        eb = jnp.exp2(bm)
        b_last = b[CHUNK - 1:CHUNK, :]
        kd = (k * jnp.exp2(b_last - b)).astype(BF16)
        dec = jnp.exp2(b_last)
        qe = (qm * eb).astype(BF16)
        for i, (head, lo) in enumerate(heads):
            sl = slice(i * CHUNK, (i + 1) * CHUNK)
            qes[head], kds[head], decs[head] = qe[sl], kd, dec
            s_offs[head] = s_off[sl]

    zs = jnp.zeros((LANES, LANES), BF16)
    zk = jnp.zeros((CHUNK, LANES), BF16)
    for h0 in range(0, N_HEADS, 2):
        h1 = h0 + 1
        st0, st1 = st_ref[h0], st_ref[h1]
        states = jnp.concatenate([jnp.concatenate([st0.astype(BF16), zs], axis=1),
                                  jnp.concatenate([zs, st1.astype(BF16)], axis=1)], axis=0)
        oi = _dot_nt(jnp.concatenate([qes[h0], qes[h1]], axis=1), states)
        o_inter[h0], o_inter[h1] = oi[:, :LANES], oi[:, LANES:]
        keys2 = jnp.concatenate([jnp.concatenate([kds[h0], zk], axis=1),
                                 jnp.concatenate([zk, kds[h1]], axis=1)], axis=0)
        upd = _dot_tn(jnp.concatenate([vs[h0], vs[h1]], axis=0), keys2)
        st_ref[h0] = st0 * decs[h0] + upd[:, :LANES]
        st_ref[h1] = st1 * decs[h1] + upd[:, LANES:]

    for u in range(n_units):
        sc = scs[u // 2]
        paired = len(unit_heads[u]) == 2
        width = LANES if paired else CHUNK
        c0 = (u % 2) * width
        scol = (lane & 63) if paired else lane[:, :CHUNK]
        groups = []
        for j in range(CHUNK // GROUP):
            acc = jnp.zeros((GROUP, width), F32)
            for s in range(GROUP):
                r0 = (j * GROUP + s) * GROUP
                acc = jnp.where(scol == j * GROUP + s, sc[r0:r0 + GROUP, c0:c0 + width], acc)
            groups.append(acc)
        p = jnp.concatenate(groups, axis=0)
        if paired:
            vstack = jnp.concatenate([vs[h] for h, _ in unit_heads[u]], axis=0)
        for head, lo in unit_heads[u]:
            if lo is None:
                o = o_inter[head] + _dot(s_offs[head] + p, vs[head])
            else:
                o = (o_inter[head] + _dot(s_offs[head], vs[head])
                     + _dot(jnp.where((lane & 64) == lo, p, 0.0), vstack))
            o_ref[rows, head * LANES:(head + 1) * LANES] = o


def _even_kernel(x_ref, nw_ref, win_ref, lb_ref, wup_ref, bup_ref, hw_ref, wout_ref, out_ref,
                 q_ref, k_ref, g_ref, v_ref, gate_ref, o_ref, st_ref, a_ref, b_ref):
    tl = x_ref.shape[1]
    parts = [(r0, tl // EVEN_SPLIT) for r0 in range(0, tl, tl // EVEN_SPLIT)]

    @pl.when(pl.program_id(1) == 0)
    def _():
        st_ref[...] = jnp.zeros_like(st_ref)

    def project(r0, n):
        rs = slice(r0, r0 + n)
        hn = _rms(x_ref[0, rs, :], nw_ref[...]).astype(BF16)

        def proj(c0, w):
            return jnp.dot(hn, win_ref[:, c0:c0 + w], preferred_element_type=F32)

        q_ref[rs, 0:512] = proj(_E_AQ, 512)
        lb = lb_ref[...]
        f = lb + (1.0 - lb) * _sigmoid(proj(_E_AF, 512))
        k_ref[rs, 0:512] = 1.0 - f
        g_ref[rs, 0:512] = jnp.log(f)
        v_ref[rs, 0:512] = proj(_E_AI, 512)
        ag = proj(_E_AG, 512)
        gate_ref[rs, 0:512] = ag * _sigmoid(ag)
        q_ref[rs, 512:768] = proj(_E_BQ, 256) * (64.0 ** -0.5)
        k_ref[rs, 512:768] = proj(_E_BK, 256)
        v_ref[rs, 512:1024] = proj(_E_BV, 512)
        br = proj(_E_BR, 512)
        gate_ref[rs, 512:1024] = br * _sigmoid(br)
        z = _dot_x3(proj(_E_LOW, LANES), wup_ref[...]) + bup_ref[...]
        g_ref[rs, 512:768] = -_softplus(-z) * (1.0 / GLA_TAU)

    tri = _tri(CHUNK)
    tri3 = jnp.concatenate([tri, tri, tri], axis=1)

    def finish(r0, n):
        rs = slice(r0, r0 + n)
        hw = hw_ref[...]
        ys = []
        for head in range(N_HEADS):
            sl = slice(head * LANES, (head + 1) * LANES)
            oh = o_ref[rs, sl]
            ys.append(oh * lax.rsqrt(jnp.mean(oh * oh, axis=-1, keepdims=True) + EPS)
                      * hw[:, sl] * gate_ref[rs, sl])
        y = jnp.concatenate(ys, axis=1).astype(BF16)
        out_ref[0, rs, :] = x_ref[0, rs, :] + jnp.dot(y, wout_ref[...], preferred_element_type=F32)

    for r0, n in parts:
        project(r0, n)
    for r0, n in parts:
        for c in range(r0 // CHUNK, (r0 + n) // CHUNK):
            rows = slice(c * CHUNK, (c + 1) * CHUNK)
            b2 = _cumsum_rows(tri3, g_ref[rows, :]) * LOG2E
            _gla_chunk(rows, q_ref, k_ref, v_ref, b2, o_ref, st_ref, a_ref.at[c % 2], b_ref.at[c % 2])
    for r0, n in parts:
        finish(r0, n)


def _const_spec(shape):
    nd = len(shape)
    return pl.BlockSpec(shape, lambda *_: (0,) * nd)


def _even_mixer(x, norm_w, w_in, lb, w_up, b_up, hgrn_norm_w, gla_norm_w, w_out):
    B, L, D = x.shape
    tl = min(MIX_TILE, L)
    s = np.cumsum([0, 512, 512, 512, 512, 256, 256, 512, GLA_RANK, 512])
    cols = [w_in[:, s[i]:s[i + 1]] for i in range(9)]
    low = jnp.pad(cols[7], ((0, 0), (0, LANES - GLA_RANK)))
    w_in_r = jnp.concatenate(cols[:7] + [cols[8], low], axis=1).astype(BF16)
    w_up_p = jnp.pad(w_up, ((0, LANES - GLA_RANK), (0, 0)))
    hw = jnp.concatenate([hgrn_norm_w, gla_norm_w])[None, :]
    return pl.pallas_call(
        _even_kernel,
        grid=(B, L // tl),
        in_specs=[
            pl.BlockSpec((1, tl, D), lambda b, l: (b, l, 0)),
            _const_spec((1, D)),
            _const_spec((D, EVEN_COLS)),
            _const_spec((1, 512)),
            _const_spec((LANES, 256)),
            _const_spec((1, 256)),
            _const_spec((1, MIX_WIDTH)),
            _const_spec((MIX_WIDTH, D)),
        ],
        out_specs=pl.BlockSpec((1, tl, D), lambda b, l: (b, l, 0)),
        out_shape=jax.ShapeDtypeStruct((B, L, D), F32),
        scratch_shapes=[
            pltpu.VMEM((tl, _QK_COLS), F32),
            pltpu.VMEM((tl, _QK_COLS), F32),
            pltpu.VMEM((tl, _QK_COLS), F32),
            pltpu.VMEM((tl, MIX_WIDTH), F32),
            pltpu.VMEM((tl, MIX_WIDTH), F32),
            pltpu.VMEM((tl, MIX_WIDTH), F32),
            pltpu.VMEM((N_HEADS, LANES, LANES), F32),
            pltpu.VMEM((2, CHUNK * GROUP, _QK_COLS), BF16),
            pltpu.VMEM((2, CHUNK, _QK_COLS), F32),
        ],
        compiler_params=pltpu.CompilerParams(
            dimension_semantics=("arbitrary", "arbitrary"), vmem_limit_bytes=VMEM_LIMIT),
        name="even_mixer",
    )(x, norm_w[None, :], w_in_r, lb[None, :], w_up_p, b_up[None, :], hw, w_out.astype(BF16))


_O_Q, _O_K, _O_V, _O_G, _O_X, _O_GATE = 0, 256, 512, 1024, 1536, 2048
ODD_COLS = 2560


def _odd_kernel(x_ref, nw_ref, win_ref, cos_ref, sin_ref, dmat_ref, qdec_ref, kdec_ref, sdec_ref,
                rw_ref, cw_ref, cb_ref, wri_ref, bri_ref, lam_ref, wout_ref, out_ref,
                q_ref, k_ref, v_ref, gate_ref, o_ref, xe_ref, a_ref, u_ref, st_ref, h_ref):
    tl = x_ref.shape[1]
    parts = [(r0, tl // ODD_SPLIT) for r0 in range(0, tl, tl // ODD_SPLIT)]

    @pl.when(pl.program_id(1) == 0)
    def _():
        st_ref[...] = jnp.zeros_like(st_ref)
        h_ref[...] = jnp.zeros_like(h_ref)
        xe_ref[0:8, :] = jnp.zeros((8, LRU_WIDTH), F32)

    def project(r0, n):
        rs = slice(r0, r0 + n)
        hn = _rms(x_ref[0, rs, :], nw_ref[...]).astype(BF16)

        def proj(c0, w):
            return jnp.dot(hn, win_ref[:, c0:c0 + w], preferred_element_type=F32)

        cos = jnp.concatenate([cos_ref[rs, :]] * 2, axis=1)
        sin = jnp.concatenate([sin_ref[rs, :]] * 2, axis=1)
        lane_qk = lax.broadcasted_iota(jnp.int32, (1, RET_HEADS * RET_DK), 1)
        first_half = (lane_qk & (RET_DK // 2)) == 0

        def rotate_half(t):
            w = t.shape[1]
            return jnp.where(first_half, -pltpu.roll(t, w - RET_DK // 2, axis=1),
                             pltpu.roll(t, RET_DK // 2, axis=1))

        qp = proj(_O_Q, 256)
        kp = proj(_O_K, 256)
        q_ref[rs, :] = qp * cos + rotate_half(qp) * sin
        k_ref[rs, :] = (kp * cos + rotate_half(kp) * sin) * (RET_DK ** -0.5)
        v_ref[rs, :] = proj(_O_V, 512)
        cg = proj(_O_G, 512)
        gate_ref[rs, 0:512] = cg * _sigmoid(cg)
        dg = proj(_O_GATE, LRU_WIDTH)
        gate_ref[rs, 512:1024] = 0.5 * dg * (
            1.0 + jnp.tanh(np.sqrt(2.0 / np.pi) * (dg + 0.044715 * (dg * dg * dg))))

        xe_ref[8 + r0:8 + r0 + n, :] = proj(_O_X, LRU_WIDTH)
        cw = cw_ref[...]
        xc = cb_ref[...]
        for j in range(LRU_CONV):
            xc = xc + xe_ref[5 + j + r0:5 + j + r0 + n, :] * cw[j:j + 1, :]
        ri = jnp.dot(xc.astype(BF16), wri_ref[...], preferred_element_type=F32) + bri_ref[...]
        r = _sigmoid(ri[:, :LRU_WIDTH])
        i = _sigmoid(ri[:, LRU_WIDTH:])
        a = jnp.exp(-LRU_C * r * _softplus(-lam_ref[...]))
        a_ref[rs, :] = a
        u_ref[rs, :] = jnp.sqrt(1.0 - a * a) * (i * xc)

    lane = lax.broadcasted_iota(jnp.int32, (1, LANES), 1)
    crow = lax.broadcasted_iota(jnp.int32, (CHUNK, LRU_WIDTH), 0)

    def recur(r0, n):
        rs = slice(r0, r0 + n)
        for head in range(RET_HEADS):
            unit, half = head // 2, head % 2
            ul = slice(unit * LANES, (unit + 1) * LANES)
            vl = slice(head * LANES, (head + 1) * LANES)
            own = (lane < 64) if half == 0 else (lane >= 64)
            q = jnp.where(own, q_ref[rs, ul], 0.0)
            k = k_ref[rs, ul]
            v = v_ref[rs, vl]
            st = st_ref[head]
            o = _dot_nt(q, st) * qdec_ref[head]
            o = o + _dot(_dot_nt(q, k) * dmat_ref[head], v)
            st_ref[head] = st * sdec_ref[head] + _dot_tn(v, k * kdec_ref[head])
            o_ref[rs, vl] = o
        for c in range(r0 // CHUNK, (r0 + n) // CHUNK):
            rows = slice(c * CHUNK, (c + 1) * CHUNK)
            ca = a_ref[rows, :]
            ch = u_ref[rows, :]
            d = 1
            while d < CHUNK:
                keep = crow >= d
                sa = jnp.where(keep, pltpu.roll(ca, d, axis=0), 1.0)
                sh = jnp.where(keep, pltpu.roll(ch, d, axis=0), 0.0)
                ch = ca * sh + ch
                ca = ca * sa
                d *= 2
            ch = ch + ca * h_ref[...]
            h_ref[...] = ch[CHUNK - 1:CHUNK, :]
            o_ref[rows, 512:1024] = ch

    def finish(r0, n):
        rs = slice(r0, r0 + n)
        rw = rw_ref[...]
        ys = []
        for head in range(RET_HEADS):
            sl = slice(head * LANES, (head + 1) * LANES)
            oh = o_ref[rs, sl]
            oh = oh - jnp.mean(oh, axis=-1, keepdims=True)
            ys.append(oh * lax.rsqrt(jnp.mean(oh * oh, axis=-1, keepdims=True) + EPS) * rw[:, sl])
        ys.append(o_ref[rs, 512:1024])
        y = (jnp.concatenate(ys, axis=1) * gate_ref[rs, :]).astype(BF16)
        out_ref[0, rs, :] = x_ref[0, rs, :] + jnp.dot(y, wout_ref[...], preferred_element_type=F32)

    for r0, n in parts:
        project(r0, n)
    xe_ref[0:8, :] = xe_ref[tl:tl + 8, :]
    for r0, n in parts:
        recur(r0, n)
    for r0, n in parts:
        finish(r0, n)


def _odd_mixer(x, norm_w, w_in, ret_norm_w, conv_w, conv_b, w_r, b_r, w_i, b_i, lam, w_out):
    B, L, D = x.shape
    tl = min(MIX_TILE, L)
    w_in_r = w_in.astype(BF16)
    half = RET_DK // 2
    inv = ROPE_BASE ** (-jnp.arange(half, dtype=F32) / half)
    ang = jnp.arange(L, dtype=F32)[:, None] * inv[None, :]
    cos = jnp.tile(jnp.cos(ang), (1, LANES // half))
    sin = jnp.tile(jnp.sin(ang), (1, LANES // half))
    lg = jnp.log1p(-jnp.exp2(-5.0 - jnp.arange(RET_HEADS, dtype=F32)))[:, None, None]
    rc = tl // ODD_SPLIT
    t = jnp.arange(rc, dtype=F32)
    rel = t[:, None] - t[None, :]
    dmat = jnp.where(rel >= 0, jnp.exp(lg * jnp.maximum(rel, 0.0)), 0.0)
    qdec = jnp.broadcast_to(jnp.exp(lg * (t[None, :, None] + 1.0)), (RET_HEADS, rc, LANES))
    kdec = jnp.broadcast_to(jnp.exp(lg * (rc - 1.0 - t[None, :, None])), (RET_HEADS, rc, LANES))
    sdec = jnp.broadcast_to(jnp.exp(lg * float(rc)), (RET_HEADS, 1, LANES))
    eye = jnp.eye(LRU_BLOCKS, dtype=F32)
    bd = LRU_WIDTH // LRU_BLOCKS

    def blockdiag(w):
        return (eye[:, None, :, None] * w[:, :, None, :]).reshape(LRU_WIDTH, LRU_WIDTH)

    wri = jnp.concatenate([blockdiag(w_r), blockdiag(w_i)], axis=1).astype(BF16)
    bri = jnp.concatenate([b_r, b_i])[None, :]
    return pl.pallas_call(
        _odd_kernel,
        grid=(B, L // tl),
        in_specs=[
            pl.BlockSpec((1, tl, D), lambda b, l: (b, l, 0)),
            _const_spec((1, D)),
            _const_spec((D, ODD_COLS)),
            pl.BlockSpec((tl, LANES), lambda b, l: (l, 0)),
            pl.BlockSpec((tl, LANES), lambda b, l: (l, 0)),
            _const_spec((RET_HEADS, rc, rc)),
            _const_spec((RET_HEADS, rc, LANES)),
            _const_spec((RET_HEADS, rc, LANES)),
            _const_spec((RET_HEADS, 1, LANES)),
            _const_spec((1, 512)),
            _const_spec((LRU_CONV, LRU_WIDTH)),
            _const_spec((1, LRU_WIDTH)),
            _const_spec((LRU_WIDTH, 2 * LRU_WIDTH)),
            _const_spec((1, 2 * LRU_WIDTH)),
            _const_spec((1, LRU_WIDTH)),
            _const_spec((MIX_WIDTH, D)),
        ],
        out_specs=pl.BlockSpec((1, tl, D), lambda b, l: (b, l, 0)),
        out_shape=jax.ShapeDtypeStruct((B, L, D), F32),
        scratch_shapes=[
            pltpu.VMEM((tl, 256), F32),
            pltpu.VMEM((tl, 256), F32),
            pltpu.VMEM((tl, 512), F32),
            pltpu.VMEM((tl, MIX_WIDTH), F32),
            pltpu.VMEM((tl, MIX_WIDTH), F32),
            pltpu.VMEM((tl + 8, LRU_WIDTH), F32),
            pltpu.VMEM((tl, LRU_WIDTH), F32),
            pltpu.VMEM((tl, LRU_WIDTH), F32),
            pltpu.VMEM((RET_HEADS, LANES, LANES), F32),
            pltpu.VMEM((1, LRU_WIDTH), F32),
        ],
        compiler_params=pltpu.CompilerParams(
            dimension_semantics=("arbitrary", "arbitrary"), vmem_limit_bytes=VMEM_LIMIT),
        name="odd_mixer",
    )(x, norm_w[None, :], w_in_r, cos, sin, dmat, qdec, kdec, sdec, ret_norm_w[None, :],
      conv_w, conv_b[None, :], wri, bri, lam[None, :], w_out.astype(BF16))


_I_E1, _I_E2, _I_G1, _I_G2, _I_R1, _I_R2 = 0, 1, 2, 3, 4, 5
_LOGIT_E0 = MOE_GROUPS
ROUTE_ROWS = 40


def _pack_bf16_pairs(lo, hi):
    lo_b = pltpu.bitcast(lo.astype(BF16).astype(F32), jnp.uint32)
    hi_b = pltpu.bitcast(hi.astype(BF16).astype(F32), jnp.uint32)
    return (lo_b >> 16) | (hi_b & jnp.uint32(0xFFFF0000))


def _unpack_pairs_f32(w):
    return pltpu.bitcast(w << 16, F32), pltpu.bitcast(w & jnp.uint32(0xFFFF0000), F32)


def _unpack_bf16_pairs(w):
    lo, hi = _unpack_pairs_f32(w)
    return lo.astype(BF16), hi.astype(BF16)


def _router_kernel(x_ref, nw_ref, wrt_ref, brt_ref, hnp_ref, info_ref, info_t_ref, cnt_ref):
    tm = x_ref.shape[0]

    @pl.when(pl.program_id(0) == 0)
    def _():
        cnt_ref[...] = jnp.zeros_like(cnt_ref)

    hn = _rms(x_ref[...], nw_ref[...])
    half = D_MODEL // 2
    hn_hi = hn.astype(BF16)
    hn_hif = hn_hi.astype(F32)
    hn_lo = (hn - hn_hif).astype(BF16)
    bits = pltpu.bitcast(hn_hif, jnp.uint32)
    hnp_ref[...] = (bits[:, :half] >> 16) | (bits[:, half:] & jnp.uint32(0xFFFF0000))

    w_hi, w_lo = _split2(wrt_ref[...])
    both = _dot_nt(jnp.concatenate([w_hi, w_lo], axis=0), hn_hi)
    bias = jnp.concatenate([brt_ref[...]] * (tm // LANES), axis=1)
    lt = (both[:LANES] + (both[LANES:] + _dot_nt(w_hi, hn_lo)) + bias)[0:ROUTE_ROWS]
    row = lax.broadcasted_iota(jnp.int32, (ROUTE_ROWS, tm), 0).astype(F32)

    def first_max(vals):
        m = jnp.max(vals, axis=0, keepdims=True)
        idx = jnp.min(jnp.where(vals == m, row, float(LANES)), axis=0, keepdims=True)
        return m, idx

    gl = jnp.where(row < MOE_GROUPS, lt, NEG_BIG)
    gmax, gidx = first_max(gl)
    g_w = 1.0 / jnp.sum(jnp.exp(gl - gmax), axis=0, keepdims=True)
    e0 = _LOGIT_E0 + MOE_PER_GROUP * gidx
    el = jnp.where((row >= e0) & (row < e0 + MOE_PER_GROUP), lt, NEG_BIG)
    m1, i1 = first_max(el)
    m2, i2 = first_max(jnp.where(row == i1, NEG_BIG, el))
    p2 = jnp.exp(m2 - m1)
    w1 = 1.0 / (1.0 + p2)
    e1 = i1 - _LOGIT_E0
    e2 = i2 - _LOGIT_E0

    erow = lax.broadcasted_iota(jnp.int32, (MOE_EXPERTS, tm), 0).astype(F32)
    oh1 = jnp.where(erow == e1, 1.0, 0.0)
    oh2 = jnp.where(erow == e2, 1.0, 0.0)
    r_i = lax.broadcasted_iota(jnp.int32, (tm, tm), 0)
    c_i = lax.broadcasted_iota(jnp.int32, (tm, tm), 1)
    triu = jnp.where(r_i <= c_i, 1.0, 0.0).astype(BF16)
    pre = jnp.dot(jnp.concatenate([oh1, oh2], axis=0).astype(BF16), triu, preferred_element_type=F32)
    pre1, pre2 = pre[:MOE_EXPERTS], pre[MOE_EXPERTS:]
    carry = cnt_ref[0:MOE_EXPERTS, :]
    tot = jnp.dot(jnp.concatenate([oh1, oh2], axis=0).astype(BF16), jnp.ones((tm, LANES), BF16),
                  preferred_element_type=F32)
    c1, c2 = tot[:MOE_EXPERTS], tot[MOE_EXPERTS:]
    rep = tm // LANES
    r1 = jnp.sum(oh1 * (pre1 + jnp.concatenate([carry] * rep, axis=1)), axis=0, keepdims=True) - 1.0
    r2 = jnp.sum(oh2 * (pre2 + jnp.concatenate([carry + c1] * rep, axis=1)), axis=0, keepdims=True) - 1.0
    cnt_ref[0:MOE_EXPERTS, :] = carry + (c1 + c2)

    row8 = lax.broadcasted_iota(jnp.int32, (GROUP, tm), 0).astype(F32)
    info_t = jnp.zeros((GROUP, tm), F32)
    for k, val in ((_I_E1, e1), (_I_E2, e2), (_I_G1, g_w * w1), (_I_G2, g_w * (w1 * p2)),
                   (_I_R1, r1), (_I_R2, r2)):
        info_t = jnp.where(row8 == float(k), val, info_t)
    info_t_ref[...] = info_t
    info_ref[...] = jnp.concatenate([info_t, jnp.zeros((LANES - GROUP, tm), F32)], axis=0).T


def _router(x2, norm_w, w_group, b_group, w_expert, b_expert):
    T, D = x2.shape
    tm = min(TOK_TILE, T)
    pad = LANES - MOE_GROUPS - MOE_EXPERTS
    wrt = jnp.pad(jnp.concatenate([w_group, w_expert], axis=1), ((0, 0), (0, pad))).T
    brt = jnp.broadcast_to(jnp.pad(jnp.concatenate([b_group, b_expert]), (0, pad))[:, None], (LANES, LANES))
    return pl.pallas_call(
        _router_kernel,
        grid=(T // tm,),
        in_specs=[
            pl.BlockSpec((tm, D), lambda i: (i, 0)),
            _const_spec((1, D)),
            _const_spec((LANES, D)),
            _const_spec((LANES, LANES)),
        ],
        out_specs=[
            pl.BlockSpec((tm, D // 2), lambda i: (i, 0)),
            pl.BlockSpec((tm, LANES), lambda i: (i, 0)),
            pl.BlockSpec((GROUP, tm), lambda i: (0, i)),
            _const_spec((LANES, LANES)),
        ],
        out_shape=[
            jax.ShapeDtypeStruct((T, D // 2), jnp.uint32),
            jax.ShapeDtypeStruct((T, LANES), F32),
            jax.ShapeDtypeStruct((GROUP, T), F32),
            jax.ShapeDtypeStruct((LANES, LANES), F32),
        ],
        compiler_params=pltpu.CompilerParams(
            dimension_semantics=("arbitrary",), vmem_limit_bytes=VMEM_LIMIT),
        name="moe_router",
    )(x2, norm_w[None, :], wrt, brt)


def _plan_kernel(cnt_ref, info_ref, pos_ref, te_ref, nx_ref, hf_ref, nv_ref):
    nt = te_ref.shape[0]
    e1 = info_ref[_I_E1:_I_E1 + 1, :]
    e2 = info_ref[_I_E2:_I_E2 + 1, :]
    s1 = jnp.zeros_like(e1)
    s2 = jnp.zeros_like(e2)
    tiles = [lax.div(cnt_ref[e] + (ROW_TILE - 1), jnp.int32(ROW_TILE)) for e in range(MOE_EXPERTS)]
    following = [None] * MOE_EXPERTS
    nxt = jnp.int32(-1)
    for e in reversed(range(MOE_EXPERTS)):
        following[e] = nxt
        nxt = jnp.where(tiles[e] > 0, jnp.int32(e), nxt)
    start = jnp.int32(0)
    for e in range(MOE_EXPERTS):
        seg = (start * ROW_TILE).astype(F32)
        s1 = jnp.where(e1 == float(e), seg, s1)
        s2 = jnp.where(e2 == float(e), seg, s2)

        end = start + tiles[e]
        last_rows = cnt_ref[e] - (tiles[e] - 1) * ROW_TILE

        def fill(i, carry, e=e, end=end, last_rows=last_rows):
            te_ref[i] = jnp.int32(e)
            nx_ref[i] = following[e]
            hf_ref[i] = jnp.where((i == end - 1) & (last_rows <= ROW_TILE // 2), 1, 0).astype(jnp.int32)
            return carry

        lax.fori_loop(start, end, fill, 0)
        start = end
    nv_ref[0] = start

    def fill_tail(i, carry):
        te_ref[i] = jnp.int32(MOE_EXPERTS - 1)
        nx_ref[i] = jnp.int32(-1)
        hf_ref[i] = jnp.int32(0)
        return carry

    lax.fori_loop(start, nt, fill_tail, 0)
    pos_ref[0:1, :] = (s1 + info_ref[_I_R1:_I_R1 + 1, :]).astype(jnp.int32)
    pos_ref[1:2, :] = (s2 + info_ref[_I_R2:_I_R2 + 1, :]).astype(jnp.int32)


def _plan(cnt, info_t, nt):
    T = info_t.shape[1]
    smem = pl.BlockSpec(memory_space=pltpu.SMEM)
    return pl.pallas_call(
        _plan_kernel,
        in_specs=[smem, pl.BlockSpec(memory_space=pltpu.VMEM)],
        out_specs=[pl.BlockSpec(memory_space=pltpu.VMEM), smem, smem, smem, smem],
        out_shape=[
            jax.ShapeDtypeStruct((2, T), jnp.int32),
            jax.ShapeDtypeStruct((nt,), jnp.int32),
            jax.ShapeDtypeStruct((nt,), jnp.int32),
            jax.ShapeDtypeStruct((nt,), jnp.int32),
            jax.ShapeDtypeStruct((1,), jnp.int32),
        ],
        name="moe_plan",
    )(cnt, info_t)


def _sc_worker():
    return lax.axis_index("s") * SC_CORES + lax.axis_index("c")


def _dispatch(hnp, pos1, pos2, n_rows):
    T, W = hnp.shape
    per_w = T // SC_WORKERS
    n_chunks = per_w // SC_ROWS
    mesh = plsc.VectorSubcoreMesh(core_axis_name="c", subcore_axis_name="s")

    @functools.partial(
        pl.kernel, mesh=mesh,
        out_type=jax.ShapeDtypeStruct((n_rows, W), jnp.uint32),
        scratch_types=[
            pltpu.VMEM((SC_ROWS,), jnp.int32),
            pltpu.VMEM((SC_ROWS,), jnp.int32),
            pltpu.VMEM((SC_ROWS, W), jnp.uint32),
        ],
        name="moe_dispatch_sc",
    )
    def scatter(hnp_hbm, p1_hbm, p2_hbm, xs_hbm, i1_v, i2_v, rows_v):
        base0 = _sc_worker() * per_w

        @pl.loop(0, n_chunks)
        def _(j):
            base = pl.multiple_of(base0 + j * SC_ROWS, SC_ROWS)
            pltpu.sync_copy(p1_hbm.at[pl.ds(base, SC_ROWS)], i1_v)
            pltpu.sync_copy(p2_hbm.at[pl.ds(base, SC_ROWS)], i2_v)
            pltpu.sync_copy(hnp_hbm.at[pl.ds(base, SC_ROWS)], rows_v)
            pltpu.sync_copy(rows_v, xs_hbm.at[i1_v])
            pltpu.sync_copy(rows_v, xs_hbm.at[i2_v])

    return scatter(hnp, pos1, pos2)


def _gather_rows(table, idx):
    T = idx.shape[0]
    W = table.shape[1]
    per_w = T // SC_WORKERS
    rows = SC_ROWS
    n_chunks = per_w // rows
    mesh = plsc.VectorSubcoreMesh(core_axis_name="c", subcore_axis_name="s")

    @functools.partial(
        pl.kernel, mesh=mesh,
        out_type=jax.ShapeDtypeStruct((T, W), table.dtype),
        scratch_types=[
            pltpu.VMEM((rows,), jnp.int32),
            pltpu.VMEM((rows, W), table.dtype),
        ],
        name="moe_gather_sc",
    )
    def gather(table_hbm, idx_hbm, out_hbm, i_v, rows_v):
        base0 = _sc_worker() * per_w

        @pl.loop(0, n_chunks)
        def _(j):
            base = pl.multiple_of(base0 + j * rows, rows)
            pltpu.sync_copy(idx_hbm.at[pl.ds(base, rows)], i_v)
            pltpu.sync_copy(table_hbm.at[i_v], rows_v)
            pltpu.sync_copy(rows_v, out_hbm.at[pl.ds(base, rows)])

    return gather(table, idx)


def _experts_kernel(te_ref, nx_ref, hf_ref, nv_ref, xs_ref, w1_hbm, w3_hbm, w2_hbm, ys_ref,
                    w1f, w3f, w2f, w1b, w3b, w2b, sem, *, layer):
    i = pl.program_id(0)

    def fetch(e):
        row = e + layer * MOE_EXPERTS
        return [pltpu.make_async_copy(w1_hbm.at[row], w1f, sem),
                pltpu.make_async_copy(w3_hbm.at[row], w3f, sem),
                pltpu.make_async_copy(w2_hbm.at[row], w2f, sem)]

    @pl.when(i < nv_ref[0])
    def _():
        @pl.when(i == 0)
        def _():
            for cp in fetch(te_ref[0]):
                cp.start()

        @pl.when((i == 0) | (te_ref[i] != te_ref[jnp.maximum(i - 1, 0)]))
        def _():
            for cp in fetch(te_ref[i]):
                cp.wait()
            w1b[...] = w1f[...].astype(BF16)
            w3b[...] = w3f[...].astype(BF16)
            w2b[...] = w2f[...].astype(BF16)

            @pl.when(nx_ref[i] >= 0)
            def _():
                for cp in fetch(nx_ref[i]):
                    cp.start()

        def mlp(rows):
            x = jnp.concatenate(_unpack_bf16_pairs(xs_ref[0:rows, :]), axis=1)
            half = D_MODEL // 2
            d = functools.partial(jnp.dot, preferred_element_type=F32)
            h1 = d(x, w1b[...])
            h3 = d(x, w3b[...])
            hid = (h1 * _sigmoid(h1)) * h3
            y = d(hid.astype(BF16), w2b[...])
            ys_ref[0:rows, :] = _pack_bf16_pairs(y[:, :half], y[:, half:])

        @pl.when(hf_ref[i] == 0)
        def _():
            mlp(ROW_TILE)

        @pl.when(hf_ref[i] != 0)
        def _():
            mlp(ROW_TILE // 2)
            ys_ref[ROW_TILE // 2:, :] = jnp.zeros((ROW_TILE // 2, ys_ref.shape[1]), ys_ref.dtype)

    @pl.when(i >= nv_ref[0])
    def _():
        ys_ref[...] = jnp.zeros_like(ys_ref)


def _experts(xs, tile_expert, next_expert, half_tile, n_valid, w1, w3, w2, layer):
    n_rows, W = xs.shape
    nt = n_rows // ROW_TILE
    D, F = w1.shape[1], w1.shape[2]

    def row_map(i, te, nx, hf, nv):
        return (i, 0)

    hbm = pl.BlockSpec(memory_space=pl.ANY)
    return pl.pallas_call(
        functools.partial(_experts_kernel, layer=layer),
        grid_spec=pltpu.PrefetchScalarGridSpec(
            num_scalar_prefetch=4,
            grid=(nt,),
            in_specs=[pl.BlockSpec((ROW_TILE, W), row_map), hbm, hbm, hbm],
            out_specs=pl.BlockSpec((ROW_TILE, D // 2), row_map),
            scratch_shapes=[
                pltpu.VMEM((D, F), F32),
                pltpu.VMEM((D, F), F32),
                pltpu.VMEM((F, D), F32),
                pltpu.VMEM((D, F), BF16),
                pltpu.VMEM((D, F), BF16),
                pltpu.VMEM((F, D), BF16),
                pltpu.SemaphoreType.DMA(()),
            ],
        ),
        out_shape=jax.ShapeDtypeStruct((n_rows, D // 2), jnp.uint32),
        compiler_params=pltpu.CompilerParams(
            dimension_semantics=("arbitrary",), vmem_limit_bytes=VMEM_LIMIT),
        name="moe_experts",
    )(tile_expert, next_expert, half_tile, n_valid, xs, w1, w3, w2)


def _combine_kernel(x_ref, info_ref, fw_ref, y1_ref, y2_ref, out_ref, *, final):
    info = info_ref[...]
    g1 = info[:, _I_G1:_I_G1 + 1]
    g2 = info[:, _I_G2:_I_G2 + 1]
    y1 = jnp.concatenate(_unpack_pairs_f32(y1_ref[...]), axis=1)
    y2 = jnp.concatenate(_unpack_pairs_f32(y2_ref[...]), axis=1)
    y = x_ref[...] + (g1 * y1 + g2 * y2)
    if final:
        y = _rms(y, fw_ref[...])
    out_ref[...] = y


def _combine(x2, part, info, y1, y2, final_w, final):
    T, D = x2.shape
    tp = T // COMBINE_PARTS
    tm = min(TOK_TILE, tp)
    nb = tp // tm
    x_spec = pl.BlockSpec((tm, D), lambda i: (part * nb + i, 0))
    half_spec = pl.BlockSpec((tm, D // 2), lambda i: (i, 0))
    return pl.pallas_call(
        functools.partial(_combine_kernel, final=final),
        grid=(nb,),
        in_specs=[x_spec, pl.BlockSpec((tm, LANES), lambda i: (part * nb + i, 0)),
                  _const_spec((1, D)), half_spec, half_spec],
        out_specs=x_spec,
        out_shape=jax.ShapeDtypeStruct((T, D), F32),
        input_output_aliases={0: 0},
        compiler_params=pltpu.CompilerParams(
            dimension_semantics=("arbitrary",), vmem_limit_bytes=VMEM_LIMIT),
        name="moe_combine",
    )(x2, info, final_w[None, :], y1, y2)


def _hier_moe(x2, norm_w, w_group, b_group, w_expert, b_expert, w1, w3, w2, layer, final_w, final):
    T, D = x2.shape
    nt = (2 * T) // ROW_TILE + MOE_EXPERTS
    hnp, info, info_t, cnt = _router(x2, norm_w, w_group, b_group, w_expert, b_expert)
    pos, tile_expert, next_expert, half_tile, n_valid = _plan(cnt[:, 0].astype(jnp.int32), info_t, nt)
    xs = _dispatch(hnp, pos[0], pos[1], nt * ROW_TILE)
    ys = _experts(xs, tile_expert, next_expert, half_tile, n_valid, w1, w3, w2, layer)
    tp = T // COMBINE_PARTS
    gathered = [(_gather_rows(ys, pos[0, p * tp:(p + 1) * tp]), _gather_rows(ys, pos[1, p * tp:(p + 1) * tp]))
                for p in range(COMBINE_PARTS)]
    out = x2
    for part, (y1, y2) in enumerate(gathered):
        out = _combine(out, part, info, y1, y2, final_w, final)
    return out


def kernel(x, norm_mix, norm_ffn, norm_final, even_w_in, hgrn_lb_logits, hgrn_norm_w, gla_w_up,
           gla_b_up, gla_norm_w, even_w_out, odd_w_in, ret_norm_w, conv_w, conv_b, rglru_w_r,
           rglru_b_r, rglru_w_i, rglru_b_i, rglru_lambda, odd_w_out, moe_w_group, moe_b_group,
           moe_w_expert, moe_b_expert, moe_w1, moe_w3, moe_w2):
    B, L, D = x.shape
    depth = norm_mix.shape[0]
    lb_table = jnp.cumsum(jax.nn.softmax(hgrn_lb_logits.astype(F32), axis=0), axis=0)
    w1 = moe_w1.reshape((-1,) + moe_w1.shape[2:])
    w3 = moe_w3.reshape((-1,) + moe_w3.shape[2:])
    w2 = moe_w2.reshape((-1,) + moe_w2.shape[2:])
    for l in range(depth):
        j = l // 2
        if l % 2 == 0:
            x = _even_mixer(x, norm_mix[l], even_w_in[j], lb_table[l], gla_w_up[j], gla_b_up[j],
                            hgrn_norm_w[j], gla_norm_w[j], even_w_out[j])
        else:
            x = _odd_mixer(x, norm_mix[l], odd_w_in[j], ret_norm_w[j], conv_w[j], conv_b[j],
                           rglru_w_r[j], rglru_b_r[j], rglru_w_i[j], rglru_b_i[j], rglru_lambda[j],
                           odd_w_out[j])
        x = _hier_moe(x.reshape(B * L, D), norm_ffn[l], moe_w_group[l], moe_b_group[l],
                      moe_w_expert[l], moe_b_expert[l], w1, w3, w2, l,
                      norm_final, l == depth - 1).reshape(B, L, D)
    return x
```

```python
import functools

import numpy as np
import jax
import jax.numpy as jnp
from jax import lax
from jax.experimental import pallas as pl
from jax.experimental.pallas import tpu as pltpu
from jax.experimental.pallas import tpu_sc as plsc

F32 = jnp.float32
BF16 = jnp.bfloat16

EPS = 1e-6
CHUNK = 64
GROUP = 8
LOG2E = 1.4426950408889634
LANES = 128
D_MODEL = 1024
MIX_WIDTH = 1024
N_HEADS = 8
GLA_RANK = 16
GLA_TAU = 16.0
RET_HEADS = 4
RET_DK = 64
ROPE_BASE = 10000.0
LRU_WIDTH = 512
LRU_BLOCKS = 8
LRU_CONV = 4
LRU_C = 8.0
MOE_GROUPS = 4
MOE_PER_GROUP = 8
MOE_EXPERTS = MOE_GROUPS * MOE_PER_GROUP
NEG_BIG = -1e30

MIX_TILE = 512
EVEN_SPLIT = 2
ODD_SPLIT = 2
TOK_TILE = 1024
ROW_TILE = 512
COMBINE_PARTS = 4
SC_CORES = 2
SC_WORKERS = 32
SC_ROWS = 128
VMEM_LIMIT = 56 * 1024 * 1024


def _dot(a, b):
    return jnp.dot(a.astype(BF16), b.astype(BF16), preferred_element_type=F32)


def _dot_nt(a, b):
    return lax.dot_general(a.astype(BF16), b.astype(BF16), (((1,), (1,)), ((), ())),
                           preferred_element_type=F32)


def _dot_tn(a, b):
    return lax.dot_general(a.astype(BF16), b.astype(BF16), (((0,), (0,)), ((), ())),
                           preferred_element_type=F32)


def _split2(x):
    hi = x.astype(BF16)
    lo = (x - hi.astype(F32)).astype(BF16)
    return hi, lo


def _split3(x):
    hi = x.astype(BF16)
    r = x - hi.astype(F32)
    mid = r.astype(BF16)
    lo = (r - mid.astype(F32)).astype(BF16)
    return hi, mid, lo


def _dot_x3(a, b):
    ah, al = _split2(a)
    bh, bl = _split2(b)
    d = functools.partial(jnp.dot, preferred_element_type=F32)
    return d(ah, bh) + (d(ah, bl) + d(al, bh))


def _sigmoid(x):
    return 1.0 / (1.0 + jnp.exp(-x))


def _softplus(x):
    return jnp.maximum(x, 0.0) + jnp.log(1.0 + jnp.exp(-jnp.abs(x)))


def _rms(x, w):
    return x * lax.rsqrt(jnp.mean(x * x, axis=-1, keepdims=True) + EPS) * w


def _tri(n):
    r = lax.broadcasted_iota(jnp.int32, (n, n), 0)
    c = lax.broadcasted_iota(jnp.int32, (n, n), 1)
    return jnp.where(r >= c, 1.0, 0.0).astype(BF16)


def _cumsum_rows(tri3, g):
    return jnp.dot(tri3, jnp.concatenate(_split3(g), axis=0), preferred_element_type=F32)


_E_AQ, _E_AF, _E_AI, _E_AG = 0, 512, 1024, 1536
_E_BQ, _E_BK, _E_BV, _E_BR, _E_LOW = 2048, 2304, 2560, 3072, 3584
EVEN_COLS = 3712
_QK_COLS = 768


_LEVEL_BLOCKS = (32, 16, 8)


def _level_tables():
    t = np.arange(CHUNK)
    out = []
    for blk in _LEVEL_BLOCKS:
        odd = (t // blk) % 2 == 1
        out.append((np.where(odd, (t // blk) * blk - 1, -1), np.where(~odd, (t // blk) * blk + blk - 1, -1)))
    return out


def _gla_chunk(rows, q_ref, k_ref, v_ref, b2, o_ref, st_ref, a_ref, b_ref):
    n_units = _QK_COLS // LANES
    lane = lax.broadcasted_iota(jnp.int32, (1, LANES), 1)
    trow = lax.broadcasted_iota(jnp.int32, (GROUP, LANES), 0)
    unit_heads = [((u, None),) for u in range(4)] + [((4, 0), (5, 64)), ((6, 0), (7, 64))]
    qs = [q_ref[rows, u * LANES:(u + 1) * LANES] for u in range(n_units)]
    ks = [k_ref[rows, u * LANES:(u + 1) * LANES] for u in range(n_units)]
    b_ref[...] = b2
    bs = [b2[:, u * LANES:(u + 1) * LANES] for u in range(n_units)]
    vs = [v_ref[rows, h * LANES:(h + 1) * LANES] for h in range(N_HEADS)]

    for u in range(n_units):
        for j in range(CHUNK // GROUP):
            qj = qs[u][j * GROUP:(j + 1) * GROUP]
            bj = bs[u][j * GROUP:(j + 1) * GROUP]
            for s in range(0, GROUP, 2):
                pair = [qj * jnp.exp2(jnp.where(
                    trow >= s + i,
                    bj - b_ref[j * GROUP + s + i:j * GROUP + s + i + 1, u * LANES:(u + 1) * LANES], NEG_BIG))
                        for i in (0, 1)]
                r0 = (j * GROUP + s) * GROUP
                a_ref[r0:r0 + 2 * GROUP, u * LANES:(u + 1) * LANES] = (
                    jnp.concatenate(pair, axis=0).astype(BF16))

    zk = jnp.zeros((CHUNK, LANES), F32)
    scs = []
    for pr in range(n_units // 2):
        u0, u1 = 2 * pr, 2 * pr + 1
        if u0 < 4:
            keys = jnp.concatenate([jnp.concatenate([ks[u0], zk], axis=1),
                                    jnp.concatenate([zk, ks[u1]], axis=1)], axis=0)
        else:
            lo0, hi0 = jnp.where(lane < 64, ks[u0], 0.0), jnp.where(lane >= 64, ks[u0], 0.0)
            lo1, hi1 = jnp.where(lane < 64, ks[u1], 0.0), jnp.where(lane >= 64, ks[u1], 0.0)
            keys = jnp.concatenate([jnp.concatenate([lo0, zk], axis=1), jnp.concatenate([hi0, zk], axis=1),
                                    jnp.concatenate([zk, lo1], axis=1), jnp.concatenate([zk, hi1], axis=1)],
                                   axis=0)
        scs.append(_dot_nt(a_ref[:, u0 * LANES:(u1 + 1) * LANES], keys))

    tables = _level_tables()
    tq = lax.broadcasted_iota(jnp.int32, (CHUNK, CHUNK), 0)
    ts = lax.broadcasted_iota(jnp.int32, (CHUNK, CHUNK), 1)
    covers = [((tq // blk) % 2 == 1) & ((tq // blk) == (ts // blk) + 1) & ((ts // blk) % 2 == 0)
              for blk in _LEVEL_BLOCKS]
    s_offs, o_inter, qes, kds, decs = {}, {}, {}, {}, {}
    for u in range(n_units):
        q, k, b = qs[u], ks[u], bs[u]
        heads = unit_heads[u]
        if len(heads) == 2:
            qm = jnp.concatenate([jnp.where((lane & 64) == lo, q, 0.0) for _, lo in heads], axis=0)
            bm = jnp.concatenate([b, b], axis=0)
        else:
            qm, bm = q, b
        s_off = None
        for (qref, kref), cover in zip(tables, covers):
            qrows, krows = [], []
            for g in range(CHUNK // GROUP):
                sl = slice(g * GROUP, (g + 1) * GROUP)
                rq, rk = int(qref[g * GROUP]), int(kref[g * GROUP])
                qrows.append(jnp.exp2(b[sl] - b[rq:rq + 1, :]) if rq >= 0 else jnp.zeros((GROUP, LANES), F32))
                krows.append(k[sl] * jnp.exp2(b[rk:rk + 1, :] - b[sl]) if rk >= 0
                             else jnp.zeros((GROUP, LANES), F32))
            qfac = jnp.concatenate(qrows, axis=0)
            if len(heads) == 2:
                qfac = jnp.concatenate([qfac, qfac], axis=0)
                cov = jnp.concatenate([cover, cover], axis=0)
            else:
                cov = cover
            part = _dot_nt(qm * qfac, jnp.concatenate(krows, axis=0))
            s_off = jnp.where(cov, part, 0.0 if s_off is None else s_off)
        eb = jnp.exp2(bm)
        b_last = b[CHUNK - 1:CHUNK, :]
        kd = (k * jnp.exp2(b_last - b)).astype(BF16)
        dec = jnp.exp2(b_last)
        qe = (qm * eb).astype(BF16)
        for i, (head, lo) in enumerate(heads):
            sl = slice(i * CHUNK, (i + 1) * CHUNK)
            qes[head], kds[head], decs[head] = qe[sl], kd, dec
            s_offs[head] = s_off[sl]

    zs = jnp.zeros((LANES, LANES), BF16)
    zk = jnp.zeros((CHUNK, LANES), BF16)
    for h0 in range(0, N_HEADS, 2):
        h1 = h0 + 1
        st0, st1 = st_ref[h0], st_ref[h1]
        states = jnp.concatenate([jnp.concatenate([st0.astype(BF16), zs], axis=1),
                                  jnp.concatenate([zs, st1.astype(BF16)], axis=1)], axis=0)
        oi = _dot_nt(jnp.concatenate([qes[h0], qes[h1]], axis=1), states)
        o_inter[h0], o_inter[h1] = oi[:, :LANES], oi[:, LANES:]
        keys2 = jnp.concatenate([jnp.concatenate([kds[h0], zk], axis=1),
                                 jnp.concatenate([zk, kds[h1]], axis=1)], axis=0)
        upd = _dot_tn(jnp.concatenate([vs[h0], vs[h1]], axis=0), keys2)
        st_ref[h0] = st0 * decs[h0] + upd[:, :LANES]
        st_ref[h1] = st1 * decs[h1] + upd[:, LANES:]

    for u in range(n_units):
        sc = scs[u // 2]
        paired = len(unit_heads[u]) == 2
        width = LANES if paired else CHUNK
        c0 = (u % 2) * width
        scol = (lane & 63) if paired else lane[:, :CHUNK]
        groups = []
        for j in range(CHUNK // GROUP):
            acc = jnp.zeros((GROUP, width), F32)
            for s in range(GROUP):
                r0 = (j * GROUP + s) * GROUP
                acc = jnp.where(scol == j * GROUP + s, sc[r0:r0 + GROUP, c0:c0 + width], acc)
            groups.append(acc)
        p = jnp.concatenate(groups, axis=0)
        if paired:
            vstack = jnp.concatenate([vs[h] for h, _ in unit_heads[u]], axis=0)
        for head, lo in unit_heads[u]:
            if lo is None:
                o = o_inter[head] + _dot(s_offs[head] + p, vs[head])
            else:
                o = (o_inter[head] + _dot(s_offs[head], vs[head])
                     + _dot(jnp.where((lane & 64) == lo, p, 0.0), vstack))
            o_ref[rows, head * LANES:(head + 1) * LANES] = o


def _even_kernel(x_ref, nw_ref, win_ref, lb_ref, wup_ref, bup_ref, hw_ref, wout_ref, out_ref,
                 q_ref, k_ref, g_ref, v_ref, gate_ref, o_ref, st_ref, a_ref, b_ref):
    tl = x_ref.shape[1]
    parts = [(r0, tl // EVEN_SPLIT) for r0 in range(0, tl, tl // EVEN_SPLIT)]

    @pl.when(pl.program_id(1) == 0)
    def _():
        st_ref[...] = jnp.zeros_like(st_ref)

    def project(r0, n):
        rs = slice(r0, r0 + n)
        hn = _rms(x_ref[0, rs, :], nw_ref[...]).astype(BF16)

        def proj(c0, w):
            return jnp.dot(hn, win_ref[:, c0:c0 + w], preferred_element_type=F32)

        q_ref[rs, 0:512] = proj(_E_AQ, 512)
        lb = lb_ref[...]
        f = lb + (1.0 - lb) * _sigmoid(proj(_E_AF, 512))
        k_ref[rs, 0:512] = 1.0 - f
        g_ref[rs, 0:512] = jnp.log(f)
        v_ref[rs, 0:512] = proj(_E_AI, 512)
        ag = proj(_E_AG, 512)
        gate_ref[rs, 0:512] = ag * _sigmoid(ag)
        q_ref[rs, 512:768] = proj(_E_BQ, 256) * (64.0 ** -0.5)
        k_ref[rs, 512:768] = proj(_E_BK, 256)
        v_ref[rs, 512:1024] = proj(_E_BV, 512)
        br = proj(_E_BR, 512)
        gate_ref[rs, 512:1024] = br * _sigmoid(br)
        z = _dot_x3(proj(_E_LOW, LANES), wup_ref[...]) + bup_ref[...]
        g_ref[rs, 512:768] = -_softplus(-z) * (1.0 / GLA_TAU)

    tri = _tri(CHUNK)
    tri3 = jnp.concatenate([tri, tri, tri], axis=1)

    def finish(r0, n):
        rs = slice(r0, r0 + n)
        hw = hw_ref[...]
        ys = []
        for head in range(N_HEADS):
            sl = slice(head * LANES, (head + 1) * LANES)
            oh = o_ref[rs, sl]
            ys.append(oh * lax.rsqrt(jnp.mean(oh * oh, axis=-1, keepdims=True) + EPS)
                      * hw[:, sl] * gate_ref[rs, sl])
        y = jnp.concatenate(ys, axis=1).astype(BF16)
        out_ref[0, rs, :] = x_ref[0, rs, :] + jnp.dot(y, wout_ref[...], preferred_element_type=F32)

    for r0, n in parts:
        project(r0, n)
    for r0, n in parts:
        for c in range(r0 // CHUNK, (r0 + n) // CHUNK):
            rows = slice(c * CHUNK, (c + 1) * CHUNK)
            b2 = _cumsum_rows(tri3, g_ref[rows, :]) * LOG2E
            _gla_chunk(rows, q_ref, k_ref, v_ref, b2, o_ref, st_ref, a_ref.at[c % 2], b_ref.at[c % 2])
    for r0, n in parts:
        finish(r0, n)


def _const_spec(shape):
    nd = len(shape)
    return pl.BlockSpec(shape, lambda *_: (0,) * nd)


def _even_mixer(x, norm_w, w_in, lb, w_up, b_up, hgrn_norm_w, gla_norm_w, w_out):
    B, L, D = x.shape
    tl = min(MIX_TILE, L)
    s = np.cumsum([0, 512, 512, 512, 512, 256, 256, 512, GLA_RANK, 512])
    cols = [w_in[:, s[i]:s[i + 1]] for i in range(9)]
    low = jnp.pad(cols[7], ((0, 0), (0, LANES - GLA_RANK)))
    w_in_r = jnp.concatenate(cols[:7] + [cols[8], low], axis=1).astype(BF16)
    w_up_p = jnp.pad(w_up, ((0, LANES - GLA_RANK), (0, 0)))
    hw = jnp.concatenate([hgrn_norm_w, gla_norm_w])[None, :]
    return pl.pallas_call(
        _even_kernel,
        grid=(B, L // tl),
        in_specs=[
            pl.BlockSpec((1, tl, D), lambda b, l: (b, l, 0)),
            _const_spec((1, D)),
            _const_spec((D, EVEN_COLS)),
            _const_spec((1, 512)),
            _const_spec((LANES, 256)),
            _const_spec((1, 256)),
            _const_spec((1, MIX_WIDTH)),
            _const_spec((MIX_WIDTH, D)),
        ],
        out_specs=pl.BlockSpec((1, tl, D), lambda b, l: (b, l, 0)),
        out_shape=jax.ShapeDtypeStruct((B, L, D), F32),
        scratch_shapes=[
            pltpu.VMEM((tl, _QK_COLS), F32),
            pltpu.VMEM((tl, _QK_COLS), F32),
            pltpu.VMEM((tl, _QK_COLS), F32),
            pltpu.VMEM((tl, MIX_WIDTH), F32),
            pltpu.VMEM((tl, MIX_WIDTH), F32),
            pltpu.VMEM((tl, MIX_WIDTH), F32),
            pltpu.VMEM((N_HEADS, LANES, LANES), F32),
            pltpu.VMEM((2, CHUNK * GROUP, _QK_COLS), BF16),
            pltpu.VMEM((2, CHUNK, _QK_COLS), F32),
        ],
        compiler_params=pltpu.CompilerParams(
            dimension_semantics=("arbitrary", "arbitrary"), vmem_limit_bytes=VMEM_LIMIT),
        name="even_mixer",
    )(x, norm_w[None, :], w_in_r, lb[None, :], w_up_p, b_up[None, :], hw, w_out.astype(BF16))


_O_Q, _O_K, _O_V, _O_G, _O_X, _O_GATE = 0, 256, 512, 1024, 1536, 2048
ODD_COLS = 2560


def _odd_kernel(x_ref, nw_ref, win_ref, cos_ref, sin_ref, dmat_ref, qdec_ref, kdec_ref, sdec_ref,
                rw_ref, cw_ref, cb_ref, wri_ref, bri_ref, lam_ref, wout_ref, out_ref,
                q_ref, k_ref, v_ref, gate_ref, o_ref, xe_ref, a_ref, u_ref, st_ref, h_ref):
    tl = x_ref.shape[1]
    parts = [(r0, tl // ODD_SPLIT) for r0 in range(0, tl, tl // ODD_SPLIT)]

    @pl.when(pl.program_id(1) == 0)
    def _():
        st_ref[...] = jnp.zeros_like(st_ref)
        h_ref[...] = jnp.zeros_like(h_ref)
        xe_ref[0:8, :] = jnp.zeros((8, LRU_WIDTH), F32)

    def project(r0, n):
        rs = slice(r0, r0 + n)
        hn = _rms(x_ref[0, rs, :], nw_ref[...]).astype(BF16)

        def proj(c0, w):
            return jnp.dot(hn, win_ref[:, c0:c0 + w], preferred_element_type=F32)

        cos = jnp.concatenate([cos_ref[rs, :]] * 2, axis=1)
        sin = jnp.concatenate([sin_ref[rs, :]] * 2, axis=1)
        lane_qk = lax.broadcasted_iota(jnp.int32, (1, RET_HEADS * RET_DK), 1)
        first_half = (lane_qk & (RET_DK // 2)) == 0

        def rotate_half(t):
            w = t.shape[1]
            return jnp.where(first_half, -pltpu.roll(t, w - RET_DK // 2, axis=1),
                             pltpu.roll(t, RET_DK // 2, axis=1))

        qp = proj(_O_Q, 256)
        kp = proj(_O_K, 256)
        q_ref[rs, :] = qp * cos + rotate_half(qp) * sin
        k_ref[rs, :] = (kp * cos + rotate_half(kp) * sin) * (RET_DK ** -0.5)
        v_ref[rs, :] = proj(_O_V, 512)
        cg = proj(_O_G, 512)
        gate_ref[rs, 0:512] = cg * _sigmoid(cg)
        dg = proj(_O_GATE, LRU_WIDTH)
        gate_ref[rs, 512:1024] = 0.5 * dg * (
            1.0 + jnp.tanh(np.sqrt(2.0 / np.pi) * (dg + 0.044715 * (dg * dg * dg))))

        xe_ref[8 + r0:8 + r0 + n, :] = proj(_O_X, LRU_WIDTH)
        cw = cw_ref[...]
        xc = cb_ref[...]
        for j in range(LRU_CONV):
            xc = xc + xe_ref[5 + j + r0:5 + j + r0 + n, :] * cw[j:j + 1, :]
        ri = jnp.dot(xc.astype(BF16), wri_ref[...], preferred_element_type=F32) + bri_ref[...]
        r = _sigmoid(ri[:, :LRU_WIDTH])
        i = _sigmoid(ri[:, LRU_WIDTH:])
        a = jnp.exp(-LRU_C * r * _softplus(-lam_ref[...]))
        a_ref[rs, :] = a
        u_ref[rs, :] = jnp.sqrt(1.0 - a * a) * (i * xc)

    lane = lax.broadcasted_iota(jnp.int32, (1, LANES), 1)
    crow = lax.broadcasted_iota(jnp.int32, (CHUNK, LRU_WIDTH), 0)

    def recur(r0, n):
        rs = slice(r0, r0 + n)
        for head in range(RET_HEADS):
            unit, half = head // 2, head % 2
            ul = slice(unit * LANES, (unit + 1) * LANES)
            vl = slice(head * LANES, (head + 1) * LANES)
            own = (lane < 64) if half == 0 else (lane >= 64)
            q = jnp.where(own, q_ref[rs, ul], 0.0)
            k = k_ref[rs, ul]
            v = v_ref[rs, vl]
            st = st_ref[head]
            o = _dot_nt(q, st) * qdec_ref[head]
            o = o + _dot(_dot_nt(q, k) * dmat_ref[head], v)
            st_ref[head] = st * sdec_ref[head] + _dot_tn(v, k * kdec_ref[head])
            o_ref[rs, vl] = o
        for c in range(r0 // CHUNK, (r0 + n) // CHUNK):
            rows = slice(c * CHUNK, (c + 1) * CHUNK)
            ca = a_ref[rows, :]
            ch = u_ref[rows, :]
            d = 1
            while d < CHUNK:
                keep = crow >= d
                sa = jnp.where(keep, pltpu.roll(ca, d, axis=0), 1.0)
                sh = jnp.where(keep, pltpu.roll(ch, d, axis=0), 0.0)
                ch = ca * sh + ch
                ca = ca * sa
                d *= 2
            ch = ch + ca * h_ref[...]
            h_ref[...] = ch[CHUNK - 1:CHUNK, :]
            o_ref[rows, 512:1024] = ch

    def finish(r0, n):
        rs = slice(r0, r0 + n)
        rw = rw_ref[...]
        ys = []
        for head in range(RET_HEADS):
            sl = slice(head * LANES, (head + 1) * LANES)
            oh = o_ref[rs, sl]
            oh = oh - jnp.mean(oh, axis=-1, keepdims=True)
            ys.append(oh * lax.rsqrt(jnp.mean(oh * oh, axis=-1, keepdims=True) + EPS) * rw[:, sl])
        ys.append(o_ref[rs, 512:1024])
        y = (jnp.concatenate(ys, axis=1) * gate_ref[rs, :]).astype(BF16)
        out_ref[0, rs, :] = x_ref[0, rs, :] + jnp.dot(y, wout_ref[...], preferred_element_type=F32)

    for r0, n in parts:
        project(r0, n)
    xe_ref[0:8, :] = xe_ref[tl:tl + 8, :]
    for r0, n in parts:
        recur(r0, n)
    for r0, n in parts:
        finish(r0, n)


def _odd_mixer(x, norm_w, w_in, ret_norm_w, conv_w, conv_b, w_r, b_r, w_i, b_i, lam, w_out):
    B, L, D = x.shape
    tl = min(MIX_TILE, L)
    w_in_r = w_in.astype(BF16)
    half = RET_DK // 2
    inv = ROPE_BASE ** (-jnp.arange(half, dtype=F32) / half)
    ang = jnp.arange(L, dtype=F32)[:, None] * inv[None, :]
    cos = jnp.tile(jnp.cos(ang), (1, LANES // half))
    sin = jnp.tile(jnp.sin(ang), (1, LANES // half))
    lg = jnp.log1p(-jnp.exp2(-5.0 - jnp.arange(RET_HEADS, dtype=F32)))[:, None, None]
    rc = tl // ODD_SPLIT
    t = jnp.arange(rc, dtype=F32)
    rel = t[:, None] - t[None, :]
    dmat = jnp.where(rel >= 0, jnp.exp(lg * jnp.maximum(rel, 0.0)), 0.0)
    qdec = jnp.broadcast_to(jnp.exp(lg * (t[None, :, None] + 1.0)), (RET_HEADS, rc, LANES))
    kdec = jnp.broadcast_to(jnp.exp(lg * (rc - 1.0 - t[None, :, None])), (RET_HEADS, rc, LANES))
    sdec = jnp.broadcast_to(jnp.exp(lg * float(rc)), (RET_HEADS, 1, LANES))
    eye = jnp.eye(LRU_BLOCKS, dtype=F32)
    bd = LRU_WIDTH // LRU_BLOCKS

    def blockdiag(w):
        return (eye[:, None, :, None] * w[:, :, None, :]).reshape(LRU_WIDTH, LRU_WIDTH)

    wri = jnp.concatenate([blockdiag(w_r), blockdiag(w_i)], axis=1).astype(BF16)
    bri = jnp.concatenate([b_r, b_i])[None, :]
    return pl.pallas_call(
        _odd_kernel,
        grid=(B, L // tl),
        in_specs=[
            pl.BlockSpec((1, tl, D), lambda b, l: (b, l, 0)),
            _const_spec((1, D)),
            _const_spec((D, ODD_COLS)),
            pl.BlockSpec((tl, LANES), lambda b, l: (l, 0)),
            pl.BlockSpec((tl, LANES), lambda b, l: (l, 0)),
            _const_spec((RET_HEADS, rc, rc)),
            _const_spec((RET_HEADS, rc, LANES)),
            _const_spec((RET_HEADS, rc, LANES)),
            _const_spec((RET_HEADS, 1, LANES)),
            _const_spec((1, 512)),
            _const_spec((LRU_CONV, LRU_WIDTH)),
            _const_spec((1, LRU_WIDTH)),
            _const_spec((LRU_WIDTH, 2 * LRU_WIDTH)),
            _const_spec((1, 2 * LRU_WIDTH)),
            _const_spec((1, LRU_WIDTH)),
            _const_spec((MIX_WIDTH, D)),
        ],
        out_specs=pl.BlockSpec((1, tl, D), lambda b, l: (b, l, 0)),
        out_shape=jax.ShapeDtypeStruct((B, L, D), F32),
        scratch_shapes=[
            pltpu.VMEM((tl, 256), F32),
            pltpu.VMEM((tl, 256), F32),
            pltpu.VMEM((tl, 512), F32),
            pltpu.VMEM((tl, MIX_WIDTH), F32),
            pltpu.VMEM((tl, MIX_WIDTH), F32),
            pltpu.VMEM((tl + 8, LRU_WIDTH), F32),
            pltpu.VMEM((tl, LRU_WIDTH), F32),
            pltpu.VMEM((tl, LRU_WIDTH), F32),
            pltpu.VMEM((RET_HEADS, LANES, LANES), F32),
            pltpu.VMEM((1, LRU_WIDTH), F32),
        ],
        compiler_params=pltpu.CompilerParams(
            dimension_semantics=("arbitrary", "arbitrary"), vmem_limit_bytes=VMEM_LIMIT),
        name="odd_mixer",
    )(x, norm_w[None, :], w_in_r, cos, sin, dmat, qdec, kdec, sdec, ret_norm_w[None, :],
      conv_w, conv_b[None, :], wri, bri, lam[None, :], w_out.astype(BF16))


_I_E1, _I_E2, _I_G1, _I_G2, _I_R1, _I_R2 = 0, 1, 2, 3, 4, 5
_LOGIT_E0 = MOE_GROUPS
ROUTE_ROWS = 40


def _pack_bf16_pairs(lo, hi):
    lo_b = pltpu.bitcast(lo.astype(BF16).astype(F32), jnp.uint32)
    hi_b = pltpu.bitcast(hi.astype(BF16).astype(F32), jnp.uint32)
    return (lo_b >> 16) | (hi_b & jnp.uint32(0xFFFF0000))


def _unpack_pairs_f32(w):
    return pltpu.bitcast(w << 16, F32), pltpu.bitcast(w & jnp.uint32(0xFFFF0000), F32)


def _unpack_bf16_pairs(w):
    lo, hi = _unpack_pairs_f32(w)
    return lo.astype(BF16), hi.astype(BF16)


def _router_kernel(x_ref, nw_ref, wrt_ref, brt_ref, hnp_ref, info_ref, info_t_ref, cnt_ref):
    tm = x_ref.shape[0]

    @pl.when(pl.program_id(0) == 0)
    def _():
        cnt_ref[...] = jnp.zeros_like(cnt_ref)

    hn = _rms(x_ref[...], nw_ref[...])
    half = D_MODEL // 2
    hn_hi = hn.astype(BF16)
    hn_hif = hn_hi.astype(F32)
    hn_lo = (hn - hn_hif).astype(BF16)
    bits = pltpu.bitcast(hn_hif, jnp.uint32)
    hnp_ref[...] = (bits[:, :half] >> 16) | (bits[:, half:] & jnp.uint32(0xFFFF0000))

    w_hi, w_lo = _split2(wrt_ref[...])
    both = _dot_nt(jnp.concatenate([w_hi, w_lo], axis=0), hn_hi)
    bias = jnp.concatenate([brt_ref[...]] * (tm // LANES), axis=1)
    lt = (both[:LANES] + (both[LANES:] + _dot_nt(w_hi, hn_lo)) + bias)[0:ROUTE_ROWS]
    row = lax.broadcasted_iota(jnp.int32, (ROUTE_ROWS, tm), 0).astype(F32)

    def first_max(vals):
        m = jnp.max(vals, axis=0, keepdims=True)
        idx = jnp.min(jnp.where(vals == m, row, float(LANES)), axis=0, keepdims=True)
        return m, idx

    gl = jnp.where(row < MOE_GROUPS, lt, NEG_BIG)
    gmax, gidx = first_max(gl)
    g_w = 1.0 / jnp.sum(jnp.exp(gl - gmax), axis=0, keepdims=True)
    e0 = _LOGIT_E0 + MOE_PER_GROUP * gidx
    el = jnp.where((row >= e0) & (row < e0 + MOE_PER_GROUP), lt, NEG_BIG)
    m1, i1 = first_max(el)
    m2, i2 = first_max(jnp.where(row == i1, NEG_BIG, el))
    p2 = jnp.exp(m2 - m1)
    w1 = 1.0 / (1.0 + p2)
    e1 = i1 - _LOGIT_E0
    e2 = i2 - _LOGIT_E0

    erow = lax.broadcasted_iota(jnp.int32, (MOE_EXPERTS, tm), 0).astype(F32)
    oh1 = jnp.where(erow == e1, 1.0, 0.0)
    oh2 = jnp.where(erow == e2, 1.0, 0.0)
    r_i = lax.broadcasted_iota(jnp.int32, (tm, tm), 0)
    c_i = lax.broadcasted_iota(jnp.int32, (tm, tm), 1)
    triu = jnp.where(r_i <= c_i, 1.0, 0.0).astype(BF16)
    pre = jnp.dot(jnp.concatenate([oh1, oh2], axis=0).astype(BF16), triu, preferred_element_type=F32)
    pre1, pre2 = pre[:MOE_EXPERTS], pre[MOE_EXPERTS:]
    carry = cnt_ref[0:MOE_EXPERTS, :]
    tot = jnp.dot(jnp.concatenate([oh1, oh2], axis=0).astype(BF16), jnp.ones((tm, LANES), BF16),
                  preferred_element_type=F32)
    c1, c2 = tot[:MOE_EXPERTS], tot[MOE_EXPERTS:]
    rep = tm // LANES
    r1 = jnp.sum(oh1 * (pre1 + jnp.concatenate([carry] * rep, axis=1)), axis=0, keepdims=True) - 1.0
    r2 = jnp.sum(oh2 * (pre2 + jnp.concatenate([carry + c1] * rep, axis=1)), axis=0, keepdims=True) - 1.0
    cnt_ref[0:MOE_EXPERTS, :] = carry + (c1 + c2)

    row8 = lax.broadcasted_iota(jnp.int32, (GROUP, tm), 0).astype(F32)
    info_t = jnp.zeros((GROUP, tm), F32)
    for k, val in ((_I_E1, e1), (_I_E2, e2), (_I_G1, g_w * w1), (_I_G2, g_w * (w1 * p2)),
                   (_I_R1, r1), (_I_R2, r2)):
        info_t = jnp.where(row8 == float(k), val, info_t)
    info_t_ref[...] = info_t
    info_ref[...] = jnp.concatenate([info_t, jnp.zeros((LANES - GROUP, tm), F32)], axis=0).T


def _router(x2, norm_w, w_group, b_group, w_expert, b_expert):
    T, D = x2.shape
    tm = min(TOK_TILE, T)
    pad = LANES - MOE_GROUPS - MOE_EXPERTS
    wrt = jnp.pad(jnp.concatenate([w_group, w_expert], axis=1), ((0, 0), (0, pad))).T
    brt = jnp.broadcast_to(jnp.pad(jnp.concatenate([b_group, b_expert]), (0, pad))[:, None], (LANES, LANES))
    return pl.pallas_call(
        _router_kernel,
        grid=(T // tm,),
        in_specs=[
            pl.BlockSpec((tm, D), lambda i: (i, 0)),
            _const_spec((1, D)),
            _const_spec((LANES, D)),
            _const_spec((LANES, LANES)),
        ],
        out_specs=[
            pl.BlockSpec((tm, D // 2), lambda i: (i, 0)),
            pl.BlockSpec((tm, LANES), lambda i: (i, 0)),
            pl.BlockSpec((GROUP, tm), lambda i: (0, i)),
            _const_spec((LANES, LANES)),
        ],
        out_shape=[
            jax.ShapeDtypeStruct((T, D // 2), jnp.uint32),
            jax.ShapeDtypeStruct((T, LANES), F32),
            jax.ShapeDtypeStruct((GROUP, T), F32),
            jax.ShapeDtypeStruct((LANES, LANES), F32),
        ],
        compiler_params=pltpu.CompilerParams(
            dimension_semantics=("arbitrary",), vmem_limit_bytes=VMEM_LIMIT),
        name="moe_router",
    )(x2, norm_w[None, :], wrt, brt)


def _plan_kernel(cnt_ref, info_ref, pos_ref, te_ref, nx_ref, hf_ref, nv_ref):
    nt = te_ref.shape[0]
    e1 = info_ref[_I_E1:_I_E1 + 1, :]
    e2 = info_ref[_I_E2:_I_E2 + 1, :]
    s1 = jnp.zeros_like(e1)
    s2 = jnp.zeros_like(e2)
    tiles = [lax.div(cnt_ref[e] + (ROW_TILE - 1), jnp.int32(ROW_TILE)) for e in range(MOE_EXPERTS)]
    following = [None] * MOE_EXPERTS
    nxt = jnp.int32(-1)
    for e in reversed(range(MOE_EXPERTS)):
        following[e] = nxt
        nxt = jnp.where(tiles[e] > 0, jnp.int32(e), nxt)
    start = jnp.int32(0)
    for e in range(MOE_EXPERTS):
        seg = (start * ROW_TILE).astype(F32)
        s1 = jnp.where(e1 == float(e), seg, s1)
        s2 = jnp.where(e2 == float(e), seg, s2)

        end = start + tiles[e]
        last_rows = cnt_ref[e] - (tiles[e] - 1) * ROW_TILE

        def fill(i, carry, e=e, end=end, last_rows=last_rows):
            te_ref[i] = jnp.int32(e)
            nx_ref[i] = following[e]
            hf_ref[i] = jnp.where((i == end - 1) & (last_rows <= ROW_TILE // 2), 1, 0).astype(jnp.int32)
            return carry

        lax.fori_loop(start, end, fill, 0)
        start = end
    nv_ref[0] = start

    def fill_tail(i, carry):
        te_ref[i] = jnp.int32(MOE_EXPERTS - 1)
        nx_ref[i] = jnp.int32(-1)
        hf_ref[i] = jnp.int32(0)
        return carry

    lax.fori_loop(start, nt, fill_tail, 0)
    pos_ref[0:1, :] = (s1 + info_ref[_I_R1:_I_R1 + 1, :]).astype(jnp.int32)
    pos_ref[1:2, :] = (s2 + info_ref[_I_R2:_I_R2 + 1, :]).astype(jnp.int32)


def _plan(cnt, info_t, nt):
    T = info_t.shape[1]
    smem = pl.BlockSpec(memory_space=pltpu.SMEM)
    return pl.pallas_call(
        _plan_kernel,
        in_specs=[smem, pl.BlockSpec(memory_space=pltpu.VMEM)],
        out_specs=[pl.BlockSpec(memory_space=pltpu.VMEM), smem, smem, smem, smem],
        out_shape=[
            jax.ShapeDtypeStruct((2, T), jnp.int32),
            jax.ShapeDtypeStruct((nt,), jnp.int32),
            jax.ShapeDtypeStruct((nt,), jnp.int32),
            jax.ShapeDtypeStruct((nt,), jnp.int32),
            jax.ShapeDtypeStruct((1,), jnp.int32),
        ],
        name="moe_plan",
    )(cnt, info_t)


def _sc_worker():
    return lax.axis_index("s") * SC_CORES + lax.axis_index("c")


def _dispatch(hnp, pos1, pos2, n_rows):
    T, W = hnp.shape
    per_w = T // SC_WORKERS
    n_chunks = per_w // SC_ROWS
    mesh = plsc.VectorSubcoreMesh(core_axis_name="c", subcore_axis_name="s")

    @functools.partial(
        pl.kernel, mesh=mesh,
        out_type=jax.ShapeDtypeStruct((n_rows, W), jnp.uint32),
        scratch_types=[
            pltpu.VMEM((SC_ROWS,), jnp.int32),
            pltpu.VMEM((SC_ROWS,), jnp.int32),
            pltpu.VMEM((SC_ROWS, W), jnp.uint32),
        ],
        name="moe_dispatch_sc",
    )
    def scatter(hnp_hbm, p1_hbm, p2_hbm, xs_hbm, i1_v, i2_v, rows_v):
        base0 = _sc_worker() * per_w

        @pl.loop(0, n_chunks)
        def _(j):
            base = pl.multiple_of(base0 + j * SC_ROWS, SC_ROWS)
            pltpu.sync_copy(p1_hbm.at[pl.ds(base, SC_ROWS)], i1_v)
            pltpu.sync_copy(p2_hbm.at[pl.ds(base, SC_ROWS)], i2_v)
            pltpu.sync_copy(hnp_hbm.at[pl.ds(base, SC_ROWS)], rows_v)
            pltpu.sync_copy(rows_v, xs_hbm.at[i1_v])
            pltpu.sync_copy(rows_v, xs_hbm.at[i2_v])

    return scatter(hnp, pos1, pos2)


def _gather_rows(table, idx):
    T = idx.shape[0]
    W = table.shape[1]
    per_w = T // SC_WORKERS
    rows = SC_ROWS
    n_chunks = per_w // rows
    mesh = plsc.VectorSubcoreMesh(core_axis_name="c", subcore_axis_name="s")

    @functools.partial(
        pl.kernel, mesh=mesh,
        out_type=jax.ShapeDtypeStruct((T, W), table.dtype),
        scratch_types=[
            pltpu.VMEM((rows,), jnp.int32),
            pltpu.VMEM((rows, W), table.dtype),
        ],
        name="moe_gather_sc",
    )
    def gather(table_hbm, idx_hbm, out_hbm, i_v, rows_v):
        base0 = _sc_worker() * per_w

        @pl.loop(0, n_chunks)
        def _(j):
            base = pl.multiple_of(base0 + j * rows, rows)
            pltpu.sync_copy(idx_hbm.at[pl.ds(base, rows)], i_v)
            pltpu.sync_copy(table_hbm.at[i_v], rows_v)
            pltpu.sync_copy(rows_v, out_hbm.at[pl.ds(base, rows)])

    return gather(table, idx)


def _experts_kernel(te_ref, nx_ref, hf_ref, nv_ref, xs_ref, w1_hbm, w3_hbm, w2_hbm, ys_ref,
                    w1f, w3f, w2f, w1b, w3b, w2b, sem, *, layer):
    i = pl.program_id(0)

    def fetch(e):
        row = e + layer * MOE_EXPERTS
        return [pltpu.make_async_copy(w1_hbm.at[row], w1f, sem),
                pltpu.make_async_copy(w3_hbm.at[row], w3f, sem),
                pltpu.make_async_copy(w2_hbm.at[row], w2f, sem)]

    @pl.when(i < nv_ref[0])
    def _():
        @pl.when(i == 0)
        def _():
            for cp in fetch(te_ref[0]):
                cp.start()

        @pl.when((i == 0) | (te_ref[i] != te_ref[jnp.maximum(i - 1, 0)]))
        def _():
            for cp in fetch(te_ref[i]):
                cp.wait()
            w1b[...] = w1f[...].astype(BF16)
            w3b[...] = w3f[...].astype(BF16)
            w2b[...] = w2f[...].astype(BF16)

            @pl.when(nx_ref[i] >= 0)
            def _():
                for cp in fetch(nx_ref[i]):
                    cp.start()

        def mlp(rows):
            x = jnp.concatenate(_unpack_bf16_pairs(xs_ref[0:rows, :]), axis=1)
            half = D_MODEL // 2
            d = functools.partial(jnp.dot, preferred_element_type=F32)
            y = None
            fh = w1b.shape[1] // 2
            for f0 in (0, fh):
                h1 = d(x, w1b[:, f0:f0 + fh])
                h3 = d(x, w3b[:, f0:f0 + fh])
                hid = ((h1 * _sigmoid(h1)) * h3).astype(BF16)
                part = d(hid, w2b[f0:f0 + fh, :])
                y = part if y is None else y + part
            ys_ref[0:rows, :] = _pack_bf16_pairs(y[:, :half], y[:, half:])

        @pl.when(hf_ref[i] == 0)
        def _():
            mlp(ROW_TILE)

        @pl.when(hf_ref[i] != 0)
        def _():
            mlp(ROW_TILE // 2)
            ys_ref[ROW_TILE // 2:, :] = jnp.zeros((ROW_TILE // 2, ys_ref.shape[1]), ys_ref.dtype)

    @pl.when(i >= nv_ref[0])
    def _():
        ys_ref[...] = jnp.zeros_like(ys_ref)


def _experts(xs, tile_expert, next_expert, half_tile, n_valid, w1, w3, w2, layer):
    n_rows, W = xs.shape
    nt = n_rows // ROW_TILE
    D, F = w1.shape[1], w1.shape[2]

    def row_map(i, te, nx, hf, nv):
        return (i, 0)

    hbm = pl.BlockSpec(memory_space=pl.ANY)
    return pl.pallas_call(
        functools.partial(_experts_kernel, layer=layer),
        grid_spec=pltpu.PrefetchScalarGridSpec(
            num_scalar_prefetch=4,
            grid=(nt,),
            in_specs=[pl.BlockSpec((ROW_TILE, W), row_map), hbm, hbm, hbm],
            out_specs=pl.BlockSpec((ROW_TILE, D // 2), row_map),
            scratch_shapes=[
                pltpu.VMEM((D, F), F32),
                pltpu.VMEM((D, F), F32),
                pltpu.VMEM((F, D), F32),
                pltpu.VMEM((D, F), BF16),
                pltpu.VMEM((D, F), BF16),
                pltpu.VMEM((F, D), BF16),
                pltpu.SemaphoreType.DMA(()),
            ],
        ),
        out_shape=jax.ShapeDtypeStruct((n_rows, D // 2), jnp.uint32),
        compiler_params=pltpu.CompilerParams(
            dimension_semantics=("arbitrary",), vmem_limit_bytes=VMEM_LIMIT),
        name="moe_experts",
    )(tile_expert, next_expert, half_tile, n_valid, xs, w1, w3, w2)


def _combine_kernel(x_ref, info_ref, fw_ref, y1_ref, y2_ref, out_ref, *, final):
    info = info_ref[...]
    g1 = info[:, _I_G1:_I_G1 + 1]
    g2 = info[:, _I_G2:_I_G2 + 1]
    y1 = jnp.concatenate(_unpack_pairs_f32(y1_ref[...]), axis=1)
    y2 = jnp.concatenate(_unpack_pairs_f32(y2_ref[...]), axis=1)
    y = x_ref[...] + (g1 * y1 + g2 * y2)
    if final:
        y = _rms(y, fw_ref[...])
    out_ref[...] = y


def _combine(x2, part, info, y1, y2, final_w, final):
    T, D = x2.shape
    tp = T // COMBINE_PARTS
    tm = min(TOK_TILE, tp)
    nb = tp // tm
    x_spec = pl.BlockSpec((tm, D), lambda i: (part * nb + i, 0))
    half_spec = pl.BlockSpec((tm, D // 2), lambda i: (i, 0))
    return pl.pallas_call(
        functools.partial(_combine_kernel, final=final),
        grid=(nb,),
        in_specs=[x_spec, pl.BlockSpec((tm, LANES), lambda i: (part * nb + i, 0)),
                  _const_spec((1, D)), half_spec, half_spec],
        out_specs=x_spec,
        out_shape=jax.ShapeDtypeStruct((T, D), F32),
        input_output_aliases={0: 0},
        compiler_params=pltpu.CompilerParams(
            dimension_semantics=("arbitrary",), vmem_limit_bytes=VMEM_LIMIT),
        name="moe_combine",
    )(x2, info, final_w[None, :], y1, y2)


def _hier_moe(x2, norm_w, w_group, b_group, w_expert, b_expert, w1, w3, w2, layer, final_w, final):
    T, D = x2.shape
    nt = (2 * T) // ROW_TILE + MOE_EXPERTS
    hnp, info, info_t, cnt = _router(x2, norm_w, w_group, b_group, w_expert, b_expert)
    pos, tile_expert, next_expert, half_tile, n_valid = _plan(cnt[:, 0].astype(jnp.int32), info_t, nt)
    xs = _dispatch(hnp, pos[0], pos[1], nt * ROW_TILE)
    ys = _experts(xs, tile_expert, next_expert, half_tile, n_valid, w1, w3, w2, layer)
    tp = T // COMBINE_PARTS
    gathered = [(_gather_rows(ys, pos[0, p * tp:(p + 1) * tp]), _gather_rows(ys, pos[1, p * tp:(p + 1) * tp]))
                for p in range(COMBINE_PARTS)]
    out = x2
    for part, (y1, y2) in enumerate(gathered):
        out = _combine(out, part, info, y1, y2, final_w, final)
    return out


def kernel(x, norm_mix, norm_ffn, norm_final, even_w_in, hgrn_lb_logits, hgrn_norm_w, gla_w_up,
           gla_b_up, gla_norm_w, even_w_out, odd_w_in, ret_norm_w, conv_w, conv_b, rglru_w_r,
           rglru_b_r, rglru_w_i, rglru_b_i, rglru_lambda, odd_w_out, moe_w_group, moe_b_group,
           moe_w_expert, moe_b_expert, moe_w1, moe_w3, moe_w2):
    B, L, D = x.shape
    depth = norm_mix.shape[0]
    lb_table = jnp.cumsum(jax.nn.softmax(hgrn_lb_logits.astype(F32), axis=0), axis=0)
    w1 = moe_w1.reshape((-1,) + moe_w1.shape[2:])
    w3 = moe_w3.reshape((-1,) + moe_w3.shape[2:])
    w2 = moe_w2.reshape((-1,) + moe_w2.shape[2:])
    for l in range(depth):
        j = l // 2
        if l % 2 == 0:
            x = _even_mixer(x, norm_mix[l], even_w_in[j], lb_table[l], gla_w_up[j], gla_b_up[j],
                            hgrn_norm_w[j], gla_norm_w[j], even_w_out[j])
        else:
            x = _odd_mixer(x, norm_mix[l], odd_w_in[j], ret_norm_w[j], conv_w[j], conv_b[j],
                           rglru_w_r[j], rglru_b_r[j], rglru_w_i[j], rglru_b_i[j], rglru_lambda[j],
                           odd_w_out[j])
        x = _hier_moe(x.reshape(B * L, D), norm_ffn[l], moe_w_group[l], moe_b_group[l],
                      moe_w_expert[l], moe_b_expert[l], w1, w3, w2, l,
                      norm_final, l == depth - 1).reshape(B, L, D)
    return x
```
